```python
import jax, jax.numpy as jnp
from jax import lax
import numpy as np

D_MODEL = 1024
BATCH = 4
SEQ = 4096
DEPTH = 4
DEC_BATCH = 128
DEC_SEQ = 4
PAST_LEN = 8192
PAGE_SIZE = 128

HEAD_DIM = 64
ATTN_HEADS = 8
KV_HEADS = 2
GQA_GROUP = ATTN_HEADS // KV_HEADS
ATTN_WIDTH = ATTN_HEADS * HEAD_DIM
KV_WIDTH = KV_HEADS * HEAD_DIM
WINDOW = 128
CHUNK = 128
GMLP_HEADS = 4
GMLP_HEAD_DIM = (D_MODEL - ATTN_WIDTH) // GMLP_HEADS
GMLP_WIDTH = GMLP_HEADS * GMLP_HEAD_DIM
MIX_WIDTH = GMLP_WIDTH + ATTN_WIDTH
IN_WIDTH = 2 * GMLP_WIDTH + ATTN_WIDTH + 2 * KV_WIDTH
D_FF = 2816
N_EXPERTS = 8
TOP_K = 2
D_FF_EXPERT = 1408
N_DENSE = (DEPTH + 1) // 2
N_MOE = DEPTH // 2
CACHE_W = min(WINDOW, PAST_LEN)
EPS = 1e-6

kernel_name = "hymba_gmlp_swa_sink_alibi_moe_step"


def rms_norm(x, g):
    xf = x.astype(jnp.float32)
    y = xf * lax.rsqrt(jnp.mean(xf * xf, axis=-1, keepdims=True) + EPS)
    return (y * g.astype(jnp.float32)).astype(x.dtype)


def layer_norm(x, g, b):
    xf = x.astype(jnp.float32)
    xc = xf - jnp.mean(xf, axis=-1, keepdims=True)
    y = xc * lax.rsqrt(jnp.mean(xc * xc, axis=-1, keepdims=True) + EPS)
    return (y * g.astype(jnp.float32) + b.astype(jnp.float32)).astype(x.dtype)


def alibi_slopes():
    h = jnp.arange(1, ATTN_HEADS + 1, dtype=jnp.float32)
    return jnp.exp2(-8.0 * h / ATTN_HEADS).reshape(KV_HEADS, GQA_GROUP)


def in_projection(h, w_in, b_in, ln_g, ln_b):
    n, t = h.shape[:2]
    z = h @ w_in + b_in
    cuts = [GMLP_WIDTH, 2 * GMLP_WIDTH, 2 * GMLP_WIDTH + ATTN_WIDTH, 2 * GMLP_WIDTH + ATTN_WIDTH + KV_WIDTH]
    u, v, q, k, val = jnp.split(z, cuts, axis=-1)
    u = jax.nn.gelu(u, approximate=False)
    v = layer_norm(jax.nn.gelu(v, approximate=False).reshape(n, t, GMLP_HEADS, GMLP_HEAD_DIM), ln_g, ln_b)
    q = q.reshape(n, t, KV_HEADS, GQA_GROUP, HEAD_DIM)
    k = k.reshape(n, t, KV_HEADS, HEAD_DIM)
    val = val.reshape(n, t, KV_HEADS, HEAD_DIM)
    return u, v, q, k, val


def spatial_gate(u, v, w_s, b_s):
    n, t = v.shape[:2]
    c = min(t, CHUNK)
    ws = jnp.tril(w_s[:, :c, :c])
    vc = v.reshape(n, t // c, c, GMLP_HEADS, GMLP_HEAD_DIM)
    mixed = jnp.einsum('hts,ncshd->ncthd', ws, vc) + b_s[:, :c].T[:, :, None]
    return u * mixed.reshape(n, t, GMLP_WIDTH)


def sink_attention(q, k, v, dist, valid, sinks):
    s = jnp.einsum('...qhgd,...khd->...hgqk', q, k).astype(jnp.float32) * (HEAD_DIM ** -0.5)
    s = s - alibi_slopes()[:, :, None, None] * dist[..., None, None, :, :].astype(jnp.float32)
    s = jnp.where(valid[..., None, None, :, :], s, -jnp.inf)
    sink = jnp.broadcast_to(sinks.astype(jnp.float32).reshape(KV_HEADS, GQA_GROUP, 1, 1), s.shape[:-1] + (1,))
    p = jax.nn.softmax(jnp.concatenate([s, sink], axis=-1), axis=-1)[..., :-1]
    return jnp.einsum('...hgqk,...khd->...qhgd', p.astype(v.dtype), v)


def window_attention_prompt(q, k, v, sinks):
    n, t = k.shape[:2]
    nb = t // WINDOW
    qb = q.reshape(n, nb, WINDOW, KV_HEADS, GQA_GROUP, HEAD_DIM)

    def band(x):
        xb = x.reshape(n, nb, WINDOW, KV_HEADS, HEAD_DIM)
        prev = jnp.pad(xb, ((0, 0), (1, 0), (0, 0), (0, 0), (0, 0)))[:, :-1]
        return jnp.concatenate([prev, xb], axis=2)

    kj = jnp.arange(2 * WINDOW)[None, :]
    dist = (jnp.arange(WINDOW)[:, None] + WINDOW) - kj
    first = (jnp.arange(nb) == 0)[:, None, None]
    valid = (dist >= 0) & (dist < WINDOW) & ~(first & (kj < WINDOW))
    o = sink_attention(qb, band(k), band(v), dist[None, None], valid[None], sinks)
    return o.reshape(n, t, ATTN_WIDTH)


def window_attention_sample(q, k, v, cache_k, cache_v, sinks):
    n, t = k.shape[:2]
    cw = cache_k.shape[1]
    keys = jnp.concatenate([cache_k, k], axis=1)
    vals = jnp.concatenate([cache_v, v], axis=1)
    kpos = jnp.concatenate([jnp.arange(cw) - cw, jnp.arange(t)])
    dist = jnp.arange(t)[:, None] - kpos[None, :]
    valid = (dist >= 0) & (dist < WINDOW)
    o = sink_attention(q, keys, vals, dist[None], valid[None], sinks)
    return o.reshape(n, t, ATTN_WIDTH), keys[:, -cw:], vals[:, -cw:]


def token_mixers(x, norm_g, w_in, b_in, ln_g, ln_b, w_s, b_s, sinks, g_a, g_b, w_out, cache=None):
    h = rms_norm(x, norm_g)
    u, v, q, k, val = in_projection(h, w_in, b_in, ln_g, ln_b)
    a = spatial_gate(u, v, w_s, b_s)
    if cache is None:
        b = window_attention_prompt(q, k, val, sinks)
        keep = min(WINDOW, x.shape[1])
        new_k, new_v = k[:, -keep:], val[:, -keep:]
    else:
        b, new_k, new_v = window_attention_sample(q, k, val, cache[0], cache[1], sinks)
    merged = jnp.concatenate([rms_norm(a, g_a), rms_norm(b, g_b)], axis=-1) @ w_out
    return x + merged, new_k, new_v, v


def swiglu(h, wg, wu, wd):
    return (jax.nn.silu(h @ wg) * (h @ wu)) @ wd


def moe_swiglu(h, w_router, wg, wu, wd):
    logits = (h @ w_router).astype(jnp.float32)
    top_v, top_i = lax.top_k(logits, TOP_K)
    top_w = jax.nn.softmax(top_v, axis=-1)
    gates = jnp.sum(jax.nn.one_hot(top_i, N_EXPERTS, dtype=jnp.float32) * top_w[..., None], axis=-2)
    out = jnp.zeros_like(h)
    for e in range(N_EXPERTS):
        out = out + gates[..., e:e + 1].astype(h.dtype) * swiglu(h, wg[e], wu[e], wd[e])
    return out


def setup_inputs(seed: int = 0) -> dict:
    key = jax.random.key(seed)
    ks = jax.random.split(key, 24)
    nrm = jax.random.normal
    f32 = jnp.float32
    return {
        "x_prompt": nrm(ks[0], (BATCH, SEQ, D_MODEL), f32),
        "x_sample": nrm(ks[1], (DEC_BATCH, DEC_SEQ, D_MODEL), f32),
        "cache_win_k": nrm(ks[2], (DEPTH, DEC_BATCH, CACHE_W, KV_HEADS, HEAD_DIM), f32),
        "cache_win_v": nrm(ks[3], (DEPTH, DEC_BATCH, CACHE_W, KV_HEADS, HEAD_DIM), f32),
        "attn_norm_g": 1.0 + 0.02 * nrm(ks[4], (DEPTH, D_MODEL), f32),
        "w_in": nrm(ks[5], (DEPTH, D_MODEL, IN_WIDTH), f32) * D_MODEL ** -0.5,
        "b_in": 0.02 * nrm(ks[6], (DEPTH, IN_WIDTH), f32),
        "chunk_ln_g": 1.0 + 0.02 * nrm(ks[7], (DEPTH, GMLP_HEADS, GMLP_HEAD_DIM), f32),
        "chunk_ln_b": 0.02 * nrm(ks[8], (DEPTH, GMLP_HEADS, GMLP_HEAD_DIM), f32),
        "w_spatial": nrm(ks[9], (DEPTH, GMLP_HEADS, CHUNK, CHUNK), f32) * CHUNK ** -0.5,
        "b_spatial": 1.0 + 0.02 * nrm(ks[10], (DEPTH, GMLP_HEADS, CHUNK), f32),
        "attn_sinks": nrm(ks[11], (DEPTH, ATTN_HEADS), f32),
        "mix_norm_a_g": 1.0 + 0.02 * nrm(ks[12], (DEPTH, GMLP_WIDTH), f32),
        "mix_norm_b_g": 1.0 + 0.02 * nrm(ks[13], (DEPTH, ATTN_WIDTH), f32),
        "w_out": nrm(ks[14], (DEPTH, MIX_WIDTH, D_MODEL), f32) * MIX_WIDTH ** -0.5,
        "ffn_norm_g": 1.0 + 0.02 * nrm(ks[15], (DEPTH, D_MODEL), f32),
        "ffn_w_gate": nrm(ks[16], (N_DENSE, D_MODEL, D_FF), f32) * D_MODEL ** -0.5,
        "ffn_w_up": nrm(ks[17], (N_DENSE, D_MODEL, D_FF), f32) * D_MODEL ** -0.5,
        "ffn_w_down": nrm(ks[18], (N_DENSE, D_FF, D_MODEL), f32) * D_FF ** -0.5,
        "router_w": nrm(ks[19], (N_MOE, D_MODEL, N_EXPERTS), f32) * D_MODEL ** -0.5,
        "expert_w_gate": nrm(ks[20], (N_MOE, N_EXPERTS, D_MODEL, D_FF_EXPERT), f32) * D_MODEL ** -0.5,
        "expert_w_up": nrm(ks[21], (N_MOE, N_EXPERTS, D_MODEL, D_FF_EXPERT), f32) * D_MODEL ** -0.5,
        "expert_w_down": nrm(ks[22], (N_MOE, N_EXPERTS, D_FF_EXPERT, D_MODEL), f32) * D_FF_EXPERT ** -0.5,
        "final_norm_g": 1.0 + 0.02 * nrm(ks[23], (D_MODEL,), f32),
    }


def reference(x_prompt, x_sample, cache_win_k, cache_win_v, attn_norm_g, w_in, b_in, chunk_ln_g,
              chunk_ln_b, w_spatial, b_spatial, attn_sinks, mix_norm_a_g, mix_norm_b_g, w_out,
              ffn_norm_g, ffn_w_gate, ffn_w_up, ffn_w_down, router_w, expert_w_gate, expert_w_up,
              expert_w_down, final_norm_g):
    xp, xs = x_prompt, x_sample
    kp, vp, ksm, vsm, cvs = [], [], [], [], []
    for l in range(DEPTH):
        mix_w = (attn_norm_g[l], w_in[l], b_in[l], chunk_ln_g[l], chunk_ln_b[l], w_spatial[l],
                 b_spatial[l], attn_sinks[l], mix_norm_a_g[l], mix_norm_b_g[l], w_out[l])
        xp, k_p, v_p, _ = token_mixers(xp, *mix_w)
        xs, k_s, v_s, cv_s = token_mixers(xs, *mix_w, cache=(cache_win_k[l], cache_win_v[l]))
        kp.append(k_p)
        vp.append(v_p)
        ksm.append(k_s)
        vsm.append(v_s)
        cvs.append(cv_s)
        i = l // 2
        if l % 2 == 0:
            xp = xp + swiglu(rms_norm(xp, ffn_norm_g[l]), ffn_w_gate[i], ffn_w_up[i], ffn_w_down[i])
            xs = xs + swiglu(rms_norm(xs, ffn_norm_g[l]), ffn_w_gate[i], ffn_w_up[i], ffn_w_down[i])
        else:
            xp = xp + moe_swiglu(rms_norm(xp, ffn_norm_g[l]), router_w[i], expert_w_gate[i], expert_w_up[i], expert_w_down[i])
            xs = xs + moe_swiglu(rms_norm(xs, ffn_norm_g[l]), router_w[i], expert_w_gate[i], expert_w_up[i], expert_w_down[i])
    y_prompt = rms_norm(xp, final_norm_g)
    y_sample = rms_norm(xs, final_norm_g)
    return (y_prompt, y_sample, jnp.stack(kp), jnp.stack(vp), jnp.stack(ksm), jnp.stack(vsm), jnp.stack(cvs))
```

```python
import functools

import numpy as np
import jax
import jax.numpy as jnp
from jax import lax
from jax.experimental import pallas as pl
from jax.experimental.pallas import tpu as pltpu

D_MODEL = 1024
BATCH = 4
SEQ = 4096
DEPTH = 4
DEC_BATCH = 128
DEC_SEQ = 4
HEAD_DIM = 64
ATTN_HEADS = 8
KV_HEADS = 2
GQA_GROUP = ATTN_HEADS // KV_HEADS
ATTN_WIDTH = ATTN_HEADS * HEAD_DIM
KV_WIDTH = KV_HEADS * HEAD_DIM
WINDOW = 128
CHUNK = 128
GMLP_HEADS = 4
GMLP_HEAD_DIM = 128
GMLP_WIDTH = GMLP_HEADS * GMLP_HEAD_DIM
IN_WIDTH = 2 * GMLP_WIDTH + ATTN_WIDTH + 2 * KV_WIDTH
D_FF = 2816
N_EXPERTS = 8
D_FF_EXPERT = 1408
CACHE_W = 128
EPS = 1e-6

LANES = 128
TM = 512
T_PROMPT = BATCH * SEQ
T_SAMPLE = DEC_BATCH * DEC_SEQ
T_ALL = T_PROMPT + T_SAMPLE
NT_PROMPT = T_PROMPT // TM
NT_ALL = T_ALL // TM
TILES_PER_SEQ = SEQ // TM
BLOCKS_PER_TILE = TM // WINDOW
SEQ_GROUP = 16
N_SEQ_GROUPS = DEC_BATCH // SEQ_GROUP
KEYS_PAD = CACHE_W + 8
VMEM_LIMIT = 56 * 1024 * 1024

F32 = jnp.float32
BF16 = jnp.bfloat16
NEG_INF = float("-inf")

Q0 = 2 * GMLP_WIDTH
K0 = Q0 + ATTN_WIDTH
V0 = K0 + KV_WIDTH


def _q_perm():
    new = np.arange(ATTN_WIDTH)
    g, rem = new // LANES, new % LANES
    kv, d = rem // HEAD_DIM, rem % HEAD_DIM
    return kv * (GQA_GROUP * HEAD_DIM) + g * HEAD_DIM + d


def _slope(kv, g):
    return 2.0 ** (-8.0 * (kv * GQA_GROUP + g + 1) / ATTN_HEADS)


def _prompt_bias():
    qi = np.arange(WINDOW)[:, None]
    kj = np.arange(2 * WINDOW)[None, :]
    dist = (qi + WINDOW) - kj
    valid = (dist >= 0) & (dist < WINDOW)
    out = np.zeros((2, ATTN_HEADS, WINDOW, 2 * WINDOW), np.float32)
    for var in range(2):
        ok = valid & ~((var == 1) & (kj < WINDOW))
        for kv in range(KV_HEADS):
            for g in range(GQA_GROUP):
                out[var, kv * GQA_GROUP + g] = np.where(ok, -_slope(kv, g) * dist, NEG_INF)
    return out


def _sample_bias():
    out = np.zeros((2, 2 * GQA_GROUP * 8, KEYS_PAD), np.float32)
    j = np.arange(KEYS_PAD)
    kpos = np.where(j < CACHE_W, j - CACHE_W, j - CACHE_W)
    for s in range(2):
        for kv in range(KV_HEADS):
            for g in range(GQA_GROUP):
                for sp in range(2):
                    for t in range(DEC_SEQ):
                        row = (kv * GQA_GROUP + g) * 8 + sp * DEC_SEQ + t
                        dist = t - kpos
                        ok = (dist >= 0) & (dist < WINDOW) & (j < CACHE_W + DEC_SEQ)
                        if sp == s:
                            out[s, row] = np.where(ok, -_slope(kv, g) * dist, NEG_INF)
    return out


def _rms(x, g):
    return x * lax.rsqrt(jnp.mean(x * x, axis=-1, keepdims=True) + EPS) * g


def _gelu(x):
    return 0.5 * x * (1.0 + lax.erf(x * (2.0 ** -0.5)))


def _gmlp_v(z, lng_ref, lnb_ref):
    out = []
    for hh in range(GMLP_HEADS):
        sl = slice(hh * LANES, (hh + 1) * LANES)
        vh = _gelu(z[:, GMLP_WIDTH + hh * LANES:GMLP_WIDTH + (hh + 1) * LANES])
        vc = vh - jnp.mean(vh, axis=-1, keepdims=True)
        y = vc * lax.rsqrt(jnp.mean(vc * vc, axis=-1, keepdims=True) + EPS)
        out.append(y * lng_ref[:, sl] + lnb_ref[:, sl])
    return out


def _softmax_sink_unnorm(s, sink):
    m = jnp.maximum(jnp.max(s, axis=-1, keepdims=True), sink)
    p = jnp.exp(s - m)
    den = jnp.sum(p, axis=-1, keepdims=True) + jnp.exp(sink - m)
    return p, 1.0 / den


def _dot_nt(a, b):
    return lax.dot_general(a, b, (((1,), (1,)), ((), ())), preferred_element_type=F32)


def _mixer_prompt_kernel(x_ref, ng_ref, win_ref, bin_ref, lng_ref, lnb_ref, ws_ref, bs_ref,
                         sink_ref, bias_ref, ga_ref, gb_ref, wout_ref,
                         xo_ref, ks_ref, vs_ref, kprev_ref, vprev_ref):
    i = pl.program_id(0)

    @pl.when(i == 0)
    def _():
        kprev_ref[...] = jnp.zeros_like(kprev_ref)
        vprev_ref[...] = jnp.zeros_like(vprev_ref)

    x = x_ref[...]
    h = _rms(x, ng_ref[...]).astype(BF16)
    z = jnp.dot(h, win_ref[...], preferred_element_type=F32) + bin_ref[...]

    u = _gelu(z[:, :GMLP_WIDTH])
    v_heads = _gmlp_v(z, lng_ref, lnb_ref)
    row = lax.broadcasted_iota(jnp.int32, (CHUNK, CHUNK), 0)
    col = lax.broadcasted_iota(jnp.int32, (CHUNK, CHUNK), 1)
    a_cols = []
    for hh in range(GMLP_HEADS):
        w = jnp.where(row >= col, ws_ref[hh], 0.0).astype(BF16)
        v_cat = jnp.concatenate(
            [v_heads[hh][c * CHUNK:(c + 1) * CHUNK] for c in range(BLOCKS_PER_TILE)], axis=1)
        mixed = jnp.dot(w, v_cat.astype(BF16), preferred_element_type=F32)
        bias = bs_ref[:, hh * LANES:(hh + 1) * LANES]
        a_cols.append(jnp.concatenate(
            [mixed[:, c * LANES:(c + 1) * LANES] + bias for c in range(BLOCKS_PER_TILE)], axis=0))
    a = u * jnp.concatenate(a_cols, axis=1)
    a_n = _rms(a, ga_ref[...]).astype(BF16)

    q = z[:, Q0:K0] * (HEAD_DIM ** -0.5)
    k = z[:, K0:V0]
    val = z[:, V0:]
    k_bf = k.astype(BF16)
    v_bf = val.astype(BF16)
    lane = lax.broadcasted_iota(jnp.int32, (1, LANES), 1)
    low = lane < HEAD_DIM
    first = jnp.where(i % TILES_PER_SEQ == 0, 1, 0)
    k_prev = kprev_ref[...]
    v_prev = vprev_ref[...]
    b_rows = []
    for c in range(BLOCKS_PER_TILE):
        rows = slice(c * WINDOW, (c + 1) * WINDOW)
        k_band = jnp.concatenate([k_prev, k_bf[rows]], axis=0)
        v_band = jnp.concatenate([v_prev, v_bf[rows]], axis=0)
        o_kv = []
        for kv in range(KV_HEADS):
            keep = low if kv == 0 else jnp.logical_not(low)
            qm = jnp.concatenate(
                [jnp.where(keep, q[rows, g * LANES:(g + 1) * LANES], 0.0) for g in range(GQA_GROUP)],
                axis=0).astype(BF16)
            s_all = _dot_nt(qm, k_band)
            ps, invs = [], []
            for g in range(GQA_GROUP):
                hd = kv * GQA_GROUP + g
                bias = bias_ref[first, hd] if c == 0 else bias_ref[0, hd]
                p, inv = _softmax_sink_unnorm(s_all[g * WINDOW:(g + 1) * WINDOW] + bias, sink_ref[hd])
                ps.append(p.astype(BF16))
                invs.append(inv)
            o = jnp.dot(jnp.concatenate(ps, axis=0), v_band, preferred_element_type=F32)
            o_kv.append(o * jnp.concatenate(invs, axis=0))
        b_rows.append(jnp.concatenate(
            [jnp.where(low, o_kv[0][g * WINDOW:(g + 1) * WINDOW], o_kv[1][g * WINDOW:(g + 1) * WINDOW])
             for g in range(GQA_GROUP)], axis=1))
        k_prev = k_bf[rows]
        v_prev = v_bf[rows]
    b = jnp.concatenate(b_rows, axis=0)
    b_n = _rms(b, gb_ref[...]).astype(BF16)

    merged = jnp.concatenate([a_n, b_n], axis=1)
    xo_ref[...] = x + jnp.dot(merged, wout_ref[...], preferred_element_type=F32)

    kprev_ref[...] = k_prev
    vprev_ref[...] = v_prev
    ks_ref[0] = k[TM - WINDOW:]
    vs_ref[0] = val[TM - WINDOW:]


def _mixer_prompt(x, x_rows, lw):
    const = lambda *shape: pl.BlockSpec(shape, lambda i: (0,) * len(shape))
    return pl.pallas_call(
        _mixer_prompt_kernel,
        grid=(NT_PROMPT,),
        in_specs=[
            pl.BlockSpec((TM, D_MODEL), lambda i: (i, 0)),
            const(1, D_MODEL), const(D_MODEL, IN_WIDTH), const(1, IN_WIDTH),
            const(1, GMLP_WIDTH), const(1, GMLP_WIDTH),
            const(GMLP_HEADS, CHUNK, CHUNK), const(CHUNK, GMLP_WIDTH),
            pl.BlockSpec(memory_space=pltpu.SMEM),
            const(2, ATTN_HEADS, WINDOW, 2 * WINDOW),
            const(1, GMLP_WIDTH), const(1, ATTN_WIDTH), const(D_MODEL, D_MODEL),
        ],
        out_specs=[
            pl.BlockSpec((TM, D_MODEL), lambda i: (i, 0)),
            pl.BlockSpec((1, WINDOW, KV_WIDTH), lambda i: (i // TILES_PER_SEQ, 0, 0)),
            pl.BlockSpec((1, WINDOW, KV_WIDTH), lambda i: (i // TILES_PER_SEQ, 0, 0)),
        ],
        out_shape=[
            jax.ShapeDtypeStruct((T_ALL, D_MODEL), F32),
            jax.ShapeDtypeStruct((BATCH, WINDOW, KV_WIDTH), F32),
            jax.ShapeDtypeStruct((BATCH, WINDOW, KV_WIDTH), F32),
        ],
        scratch_shapes=[pltpu.VMEM((WINDOW, KV_WIDTH), BF16), pltpu.VMEM((WINDOW, KV_WIDTH), BF16)],
        compiler_params=pltpu.CompilerParams(
            dimension_semantics=("arbitrary",), vmem_limit_bytes=VMEM_LIMIT),
        name="mixer_prompt",
    )(x, lw["norm_g"], lw["w_in"], lw["b_in"], lw["ln_g"], lw["ln_b"], lw["w_s"], lw["b_s_prompt"],
      lw["sinks"], lw["bias_prompt"], lw["g_a"], lw["g_b"], lw["w_out"])


def _mixer_sample_kernel(x_ref, xalias_ref, ck_ref, cv_ref, ng_ref, win_ref, bin_ref, lng_ref, lnb_ref,
                         wsd_ref, bsd_ref, sinkcol_ref, bias_ref, ga_ref, gb_ref, wout_ref,
                         xo_ref, ko_ref, vo_ref, cvo_ref,
                         q_s, k_s, v_s, a_s, b_s, kfull, vfull):
    del xalias_ref
    j = pl.program_id(0)

    @pl.when(j == 0)
    def _():
        h = _rms(x_ref[...], ng_ref[...]).astype(BF16)
        z = jnp.dot(h, win_ref[...], preferred_element_type=F32) + bin_ref[...]
        u = _gelu(z[:, :GMLP_WIDTH])
        v = jnp.concatenate(_gmlp_v(z, lng_ref, lnb_ref), axis=1)
        cvo_ref[...] = v
        t_row = lax.broadcasted_iota(jnp.int32, (TM, 1), 0) % DEC_SEQ
        mixed = bsd_ref[...]
        for d in range(DEC_SEQ):
            vd = v if d == 0 else pltpu.roll(v, d, 0)
            mixed = mixed + jnp.where(t_row >= d, wsd_ref[d], 0.0) * vd
        a_s[...] = _rms(u * mixed, ga_ref[...]).astype(BF16)
        q_s[...] = z[:, Q0:K0] * (HEAD_DIM ** -0.5)
        k_s[...] = z[:, K0:V0]
        v_s[...] = z[:, V0:]
        kfull[:, CACHE_W + DEC_SEQ:, :] = jnp.zeros((SEQ_GROUP, KEYS_PAD - CACHE_W - DEC_SEQ, KV_WIDTH), F32)
        vfull[:, CACHE_W + DEC_SEQ:, :] = jnp.zeros((SEQ_GROUP, KEYS_PAD - CACHE_W - DEC_SEQ, KV_WIDTH), F32)

    base = pl.multiple_of(j * (SEQ_GROUP * DEC_SEQ), SEQ_GROUP * DEC_SEQ)
    kfull[:, :CACHE_W, :] = ck_ref[0]
    vfull[:, :CACHE_W, :] = cv_ref[0]
    k_new = k_s[pl.ds(base, SEQ_GROUP * DEC_SEQ), :]
    v_new = v_s[pl.ds(base, SEQ_GROUP * DEC_SEQ), :]
    for s in range(SEQ_GROUP):
        kfull[s, CACHE_W:CACHE_W + DEC_SEQ, :] = k_new[s * DEC_SEQ:(s + 1) * DEC_SEQ]
        vfull[s, CACHE_W:CACHE_W + DEC_SEQ, :] = v_new[s * DEC_SEQ:(s + 1) * DEC_SEQ]
    ko_ref[0] = kfull[:, DEC_SEQ:DEC_SEQ + CACHE_W, :]
    vo_ref[0] = vfull[:, DEC_SEQ:DEC_SEQ + CACHE_W, :]

    lane = lax.broadcasted_iota(jnp.int32, (1, LANES), 1)
    low = lane < HEAD_DIM
    pair_row = lax.broadcasted_iota(jnp.int32, (8, 1), 0) // DEC_SEQ
    sink = sinkcol_ref[...]

    def pair_body(m, carry):
        r0 = pl.multiple_of(base + m * 8, 8)
        q8 = q_s[pl.ds(r0, 8), :]
        qm = jnp.concatenate(
            [jnp.where(low if kv == 0 else jnp.logical_not(low), q8[:, g * LANES:(g + 1) * LANES], 0.0)
             for kv in range(KV_HEADS) for g in range(GQA_GROUP)], axis=0).astype(BF16)
        b8 = [jnp.zeros((8, LANES), F32) for _ in range(GQA_GROUP)]
        for s in range(2):
            kn = kfull[2 * m + s].astype(BF16)
            vn = vfull[2 * m + s].astype(BF16)
            p, inv = _softmax_sink_unnorm(_dot_nt(qm, kn) + bias_ref[s], sink)
            o = jnp.dot(p.astype(BF16), vn, preferred_element_type=F32) * inv
            for g in range(GQA_GROUP):
                og = jnp.where(low, o[g * 8:(g + 1) * 8],
                               o[(GQA_GROUP + g) * 8:(GQA_GROUP + g + 1) * 8])
                b8[g] = jnp.where(pair_row == s, og, b8[g])
        b_s[pl.ds(r0, 8), :] = jnp.concatenate(b8, axis=1)
        return carry

    lax.fori_loop(0, SEQ_GROUP // 2, pair_body, 0)

    @pl.when(j == N_SEQ_GROUPS - 1)
    def _():
        b_n = _rms(b_s[...], gb_ref[...]).astype(BF16)
        merged = jnp.concatenate([a_s[...], b_n], axis=1)
        xo_ref[...] = x_ref[...] + jnp.dot(merged, wout_ref[...], preferred_element_type=F32)


def _mixer_sample(x, x_block, x_buf, cache_k, cache_v, layer, lw):
    const = lambda *shape: pl.BlockSpec(shape, lambda j: (0,) * len(shape))
    cache_spec = pl.BlockSpec((1, SEQ_GROUP, CACHE_W, KV_WIDTH), lambda j: (layer, j, 0, 0))
    state_spec = pl.BlockSpec((1, SEQ_GROUP, CACHE_W, KV_WIDTH), lambda j: (j, 0, 0, 0))
    state_shape = jax.ShapeDtypeStruct((N_SEQ_GROUPS, SEQ_GROUP, CACHE_W, KV_WIDTH), F32)
    return pl.pallas_call(
        _mixer_sample_kernel,
        grid=(N_SEQ_GROUPS,),
        in_specs=[
            pl.BlockSpec((TM, D_MODEL), lambda j: (x_block, 0)),
            pl.BlockSpec(memory_space=pl.ANY),
            cache_spec, cache_spec,
            const(1, D_MODEL), const(D_MODEL, IN_WIDTH), const(1, IN_WIDTH),
            const(1, GMLP_WIDTH), const(1, GMLP_WIDTH),
            const(DEC_SEQ, TM, GMLP_WIDTH), const(TM, GMLP_WIDTH),
            const(2 * GQA_GROUP * 8, 1), const(2, 2 * GQA_GROUP * 8, KEYS_PAD),
            const(1, GMLP_WIDTH), const(1, ATTN_WIDTH), const(D_MODEL, D_MODEL),
        ],
        out_specs=[
            pl.BlockSpec((TM, D_MODEL), lambda j: (NT_PROMPT, 0)),
            state_spec, state_spec,
            const(TM, GMLP_WIDTH),
        ],
        out_shape=[
            jax.ShapeDtypeStruct((T_ALL, D_MODEL), F32),
            state_shape, state_shape,
            jax.ShapeDtypeStruct((TM, GMLP_WIDTH), F32),
        ],
        scratch_shapes=[
            pltpu.VMEM((TM, ATTN_WIDTH), F32), pltpu.VMEM((TM, KV_WIDTH), F32),
            pltpu.VMEM((TM, KV_WIDTH), F32), pltpu.VMEM((TM, GMLP_WIDTH), BF16),
            pltpu.VMEM((TM, ATTN_WIDTH), F32),
            pltpu.VMEM((SEQ_GROUP, KEYS_PAD, KV_WIDTH), F32),
            pltpu.VMEM((SEQ_GROUP, KEYS_PAD, KV_WIDTH), F32),
        ],
        input_output_aliases={1: 0},
        compiler_params=pltpu.CompilerParams(
            dimension_semantics=("arbitrary",), vmem_limit_bytes=VMEM_LIMIT),
        name="mixer_sample",
    )(x, x_buf, cache_k, cache_v, lw["norm_g"], lw["w_in"], lw["b_in"], lw["ln_g"], lw["ln_b"],
      lw["w_s_sample"], lw["b_s_sample"], lw["sink_col"], lw["bias_sample"],
      lw["g_a"], lw["g_b"], lw["w_out"])


def _swiglu_kernel(x_ref, ng_ref, wg_ref, wu_ref, wd_ref, xo_ref):
    x = x_ref[...]
    h = _rms(x, ng_ref[...]).astype(BF16)
    gate = jnp.dot(h, wg_ref[...], preferred_element_type=F32)
    up = jnp.dot(h, wu_ref[...], preferred_element_type=F32)
    act = (jax.nn.silu(gate) * up).astype(BF16)
    xo_ref[...] = x + jnp.dot(act, wd_ref[...], preferred_element_type=F32)


def _swiglu(x, norm_g, wg, wu, wd):
    const = lambda *shape: pl.BlockSpec(shape, lambda i: (0,) * len(shape),
                                        pipeline_mode=pl.Buffered(1))
    return pl.pallas_call(
        _swiglu_kernel,
        grid=(NT_ALL,),
        in_specs=[
            pl.BlockSpec((TM, D_MODEL), lambda i: (i, 0)),
            const(1, D_MODEL), const(D_MODEL, D_FF), const(D_MODEL, D_FF), const(D_FF, D_MODEL),
        ],
        out_specs=pl.BlockSpec((TM, D_MODEL), lambda i: (i, 0)),
        out_shape=jax.ShapeDtypeStruct((T_ALL, D_MODEL), F32),
        compiler_params=pltpu.CompilerParams(
            dimension_semantics=("arbitrary",), vmem_limit_bytes=VMEM_LIMIT),
        name="swiglu",
    )(x, norm_g, wg, wu, wd)


def _moe_kernel(x_ref, ng_ref, wr_ref, wg_ref, wu_ref, wd_ref, xo_ref, h_s, gates_s, acc_s):
    e = pl.program_id(1)
    lane = lax.broadcasted_iota(jnp.int32, (1, LANES), 1)

    @pl.when(e == 0)
    def _():
        x = x_ref[...]
        h = _rms(x, ng_ref[...])
        h_s[...] = h.astype(BF16)
        logits = jnp.dot(h, wr_ref[...], preferred_element_type=F32, precision=lax.Precision.HIGHEST)
        logits = jnp.where(lane < N_EXPERTS, logits, NEG_INF)
        m1 = jnp.max(logits, axis=-1, keepdims=True)
        i1 = jnp.min(jnp.where(logits == m1, lane, LANES), axis=-1, keepdims=True)
        rest = jnp.where(lane == i1, NEG_INF, logits)
        m2 = jnp.max(rest, axis=-1, keepdims=True)
        i2 = jnp.min(jnp.where(rest == m2, lane, LANES), axis=-1, keepdims=True)
        e2 = jnp.exp(m2 - m1)
        den = 1.0 + e2
        gates_s[...] = jnp.where(lane == i1, 1.0 / den, 0.0) + jnp.where(lane == i2, e2 / den, 0.0)
        acc_s[...] = x

    h = h_s[...]
    gate = jnp.dot(h, wg_ref[0], preferred_element_type=F32)
    up = jnp.dot(h, wu_ref[0], preferred_element_type=F32)
    act = (jax.nn.silu(gate) * up).astype(BF16)
    y = jnp.dot(act, wd_ref[0], preferred_element_type=F32)
    g_e = jnp.sum(jnp.where(lane == e, gates_s[...], 0.0), axis=-1, keepdims=True)
    acc_s[...] += g_e * y

    @pl.when(e == N_EXPERTS - 1)
    def _():
        xo_ref[...] = acc_s[...]


def _moe(x, norm_g, w_router, wg, wu, wd):
    return pl.pallas_call(
        _moe_kernel,
        grid=(NT_ALL, N_EXPERTS),
        in_specs=[
            pl.BlockSpec((TM, D_MODEL), lambda i, e: (i, 0)),
            pl.BlockSpec((1, D_MODEL), lambda i, e: (0, 0)),
            pl.BlockSpec((D_MODEL, LANES), lambda i, e: (0, 0)),
            pl.BlockSpec((1, D_MODEL, D_FF_EXPERT), lambda i, e: (e, 0, 0)),
            pl.BlockSpec((1, D_MODEL, D_FF_EXPERT), lambda i, e: (e, 0, 0)),
            pl.BlockSpec((1, D_FF_EXPERT, D_MODEL), lambda i, e: (e, 0, 0)),
        ],
        out_specs=pl.BlockSpec((TM, D_MODEL), lambda i, e: (i, 0)),
        out_shape=jax.ShapeDtypeStruct((T_ALL, D_MODEL), F32),
        scratch_shapes=[pltpu.VMEM((TM, D_MODEL), BF16), pltpu.VMEM((TM, LANES), F32),
                        pltpu.VMEM((TM, D_MODEL), F32)],
        compiler_params=pltpu.CompilerParams(
            dimension_semantics=("arbitrary", "arbitrary"), vmem_limit_bytes=VMEM_LIMIT),
        name="moe",
    )(x, norm_g, w_router, wg, wu, wd)


def _final_norm_kernel(x_ref, g_ref, yp_ref, ys_ref):
    i = pl.program_id(0)
    y = _rms(x_ref[...], g_ref[...])

    @pl.when(i < NT_PROMPT)
    def _():
        yp_ref[...] = y

    @pl.when(i == NT_PROMPT)
    def _():
        ys_ref[...] = y


def _final_norm(x, g):
    return pl.pallas_call(
        _final_norm_kernel,
        grid=(NT_ALL,),
        in_specs=[pl.BlockSpec((TM, D_MODEL), lambda i: (i, 0)),
                  pl.BlockSpec((1, D_MODEL), lambda i: (0, 0))],
        out_specs=[pl.BlockSpec((TM, D_MODEL), lambda i: (jnp.minimum(i, NT_PROMPT - 1), 0)),
                   pl.BlockSpec((TM, D_MODEL), lambda i: (0, 0))],
        out_shape=[jax.ShapeDtypeStruct((T_PROMPT, D_MODEL), F32),
                   jax.ShapeDtypeStruct((T_SAMPLE, D_MODEL), F32)],
        compiler_params=pltpu.CompilerParams(dimension_semantics=("arbitrary",)),
        name="final_norm",
    )(x, g)


def kernel(x_prompt, x_sample, cache_win_k, cache_win_v, attn_norm_g, w_in, b_in, chunk_ln_g, chunk_ln_b, w_spatial, b_spatial, attn_sinks, mix_norm_a_g, mix_norm_b_g, w_out, ffn_norm_g, ffn_w_gate, ffn_w_up, ffn_w_down, router_w, expert_w_gate, expert_w_up, expert_w_down, final_norm_g):
    qp = _q_perm()
    w_in_p = jnp.concatenate([w_in[..., :Q0], w_in[..., Q0:K0][..., qp], w_in[..., K0:]], axis=-1).astype(BF16)
    b_in_p = jnp.concatenate([b_in[..., :Q0], b_in[..., Q0:K0][..., qp], b_in[..., K0:]], axis=-1)
    w_out_p = jnp.concatenate([w_out[:, :GMLP_WIDTH], w_out[:, GMLP_WIDTH:][:, qp]], axis=1).astype(BF16)
    g_b_p = mix_norm_b_g[:, qp]
    bs_prompt = jnp.repeat(jnp.transpose(b_spatial, (0, 2, 1)), GMLP_HEAD_DIM, axis=2)
    bs_sample = jnp.tile(bs_prompt[:, :DEC_SEQ], (1, DEC_BATCH, 1))
    t_idx = np.arange(DEC_SEQ)
    ws_small = jnp.stack([w_spatial[:, :, t_idx, np.maximum(t_idx - d, 0)] for d in range(DEC_SEQ)], axis=1)
    ws_sample = jnp.tile(jnp.repeat(jnp.transpose(ws_small, (0, 1, 3, 2)), GMLP_HEAD_DIM, axis=3),
                         (1, 1, DEC_BATCH, 1))
    sink_col = jnp.repeat(attn_sinks, 8, axis=1)[..., None]
    bias_prompt = jnp.asarray(_prompt_bias())
    bias_sample = jnp.asarray(_sample_bias())
    wr_pad = jnp.pad(router_w, ((0, 0), (0, 0), (0, LANES - N_EXPERTS)))
    ffn_wg, ffn_wu, ffn_wd = ffn_w_gate.astype(BF16), ffn_w_up.astype(BF16), ffn_w_down.astype(BF16)
    ex_wg, ex_wu, ex_wd = expert_w_gate.astype(BF16), expert_w_up.astype(BF16), expert_w_down.astype(BF16)

    ck = cache_win_k.reshape(DEPTH, DEC_BATCH, CACHE_W, KV_WIDTH)
    cv = cache_win_v.reshape(DEPTH, DEC_BATCH, CACHE_W, KV_WIDTH)

    x = None
    kp, vp, ksm, vsm, cvs = [], [], [], [], []
    for l in range(DEPTH):
        lw = dict(
            norm_g=attn_norm_g[l][None], w_in=w_in_p[l], b_in=b_in_p[l][None],
            ln_g=chunk_ln_g[l].reshape(1, GMLP_WIDTH), ln_b=chunk_ln_b[l].reshape(1, GMLP_WIDTH),
            w_s=w_spatial[l], b_s_prompt=bs_prompt[l], w_s_sample=ws_sample[l], b_s_sample=bs_sample[l],
            sinks=attn_sinks[l], sink_col=sink_col[l], bias_prompt=bias_prompt, bias_sample=bias_sample,
            g_a=mix_norm_a_g[l][None], g_b=g_b_p[l][None], w_out=w_out_p[l])
        if l == 0:
            x_new, k_p, v_p = _mixer_prompt(x_prompt.reshape(T_PROMPT, D_MODEL), None, lw)
            x_new, k_s, v_s, cv_s = _mixer_sample(x_sample.reshape(T_SAMPLE, D_MODEL), 0, x_new, ck, cv, l, lw)
        else:
            x_new, k_p, v_p = _mixer_prompt(x, None, lw)
            x_new, k_s, v_s, cv_s = _mixer_sample(x, NT_PROMPT, x_new, ck, cv, l, lw)
        kp.append(k_p)
        vp.append(v_p)
        ksm.append(k_s)
        vsm.append(v_s)
        cvs.append(cv_s)
        i = l // 2
        if l % 2 == 0:
            x = _swiglu(x_new, ffn_norm_g[l][None], ffn_wg[i], ffn_wu[i], ffn_wd[i])
        else:
            x = _moe(x_new, ffn_norm_g[l][None], wr_pad[i], ex_wg[i], ex_wu[i], ex_wd[i])
    y_p, y_s = _final_norm(x, final_norm_g[None])
    win_p = (DEPTH, BATCH, WINDOW, KV_HEADS, HEAD_DIM)
    win_s = (DEPTH, DEC_BATCH, CACHE_W, KV_HEADS, HEAD_DIM)
    return (y_p.reshape(BATCH, SEQ, D_MODEL), y_s.reshape(DEC_BATCH, DEC_SEQ, D_MODEL),
            jnp.stack(kp).reshape(win_p), jnp.stack(vp).reshape(win_p),
            jnp.stack(ksm).reshape(win_s), jnp.stack(vsm).reshape(win_s),
            jnp.stack(cvs).reshape(DEPTH, DEC_BATCH, DEC_SEQ, GMLP_HEADS, GMLP_HEAD_DIM))
```

```python
import functools

import numpy as np
import jax
import jax.numpy as jnp
from jax import lax
from jax.experimental import pallas as pl
from jax.experimental.pallas import tpu as pltpu

D_MODEL = 1024
BATCH = 4
SEQ = 4096
DEPTH = 4
DEC_BATCH = 128
DEC_SEQ = 4
HEAD_DIM = 64
ATTN_HEADS = 8
KV_HEADS = 2
GQA_GROUP = ATTN_HEADS // KV_HEADS
ATTN_WIDTH = ATTN_HEADS * HEAD_DIM
KV_WIDTH = KV_HEADS * HEAD_DIM
WINDOW = 128
CHUNK = 128
GMLP_HEADS = 4
GMLP_HEAD_DIM = 128
GMLP_WIDTH = GMLP_HEADS * GMLP_HEAD_DIM
IN_WIDTH = 2 * GMLP_WIDTH + ATTN_WIDTH + 2 * KV_WIDTH
D_FF = 2816
N_EXPERTS = 8
D_FF_EXPERT = 1408
CACHE_W = 128
EPS = 1e-6

LANES = 128
TM = 512
T_PROMPT = BATCH * SEQ
T_SAMPLE = DEC_BATCH * DEC_SEQ
T_ALL = T_PROMPT + T_SAMPLE
NT_PROMPT = T_PROMPT // TM
NT_ALL = T_ALL // TM
TILES_PER_SEQ = SEQ // TM
BLOCKS_PER_TILE = TM // WINDOW
SEQ_GROUP = 16
N_SEQ_GROUPS = DEC_BATCH // SEQ_GROUP
KEYS_PAD = CACHE_W + 8
PIECE = 16
SORT_ROWS = 2 * TM + N_EXPERTS * PIECE
TILES_PER_SUPER = 11
N_SUPER = NT_ALL // TILES_PER_SUPER
SUPER_ROWS = TILES_PER_SUPER * SORT_ROWS
CHUNK_ROWS = 256
PIECES_PER_CHUNK = CHUNK_ROWS // PIECE
VMEM_LIMIT = 56 * 1024 * 1024

F32 = jnp.float32
BF16 = jnp.bfloat16
NEG_INF = float("-inf")

Q0 = 2 * GMLP_WIDTH
K0 = Q0 + ATTN_WIDTH
V0 = K0 + KV_WIDTH


def _q_perm():
    new = np.arange(ATTN_WIDTH)
    g, rem = new // LANES, new % LANES
    kv, d = rem // HEAD_DIM, rem % HEAD_DIM
    return kv * (GQA_GROUP * HEAD_DIM) + g * HEAD_DIM + d


def _slope(kv, g):
    return 2.0 ** (-8.0 * (kv * GQA_GROUP + g + 1) / ATTN_HEADS)


def _prompt_bias():
    qi = np.arange(WINDOW)[:, None]
    kj = np.arange(2 * WINDOW)[None, :]
    dist = (qi + WINDOW) - kj
    valid = (dist >= 0) & (dist < WINDOW)
    out = np.zeros((2, ATTN_HEADS, WINDOW, 2 * WINDOW), np.float32)
    for var in range(2):
        ok = valid & ~((var == 1) & (kj < WINDOW))
        for kv in range(KV_HEADS):
            for g in range(GQA_GROUP):
                out[var, kv * GQA_GROUP + g] = np.where(ok, -_slope(kv, g) * dist, NEG_INF)
    return out


def _sample_bias():
    out = np.zeros((2, 2 * GQA_GROUP * 8, KEYS_PAD), np.float32)
    j = np.arange(KEYS_PAD)
    kpos = np.where(j < CACHE_W, j - CACHE_W, j - CACHE_W)
    for s in range(2):
        for kv in range(KV_HEADS):
            for g in range(GQA_GROUP):
                for sp in range(2):
                    for t in range(DEC_SEQ):
                        row = (kv * GQA_GROUP + g) * 8 + sp * DEC_SEQ + t
                        dist = t - kpos
                        ok = (dist >= 0) & (dist < WINDOW) & (j < CACHE_W + DEC_SEQ)
                        if sp == s:
                            out[s, row] = np.where(ok, -_slope(kv, g) * dist, NEG_INF)
    return out


def _rms(x, g):
    return x * lax.rsqrt(jnp.mean(x * x, axis=-1, keepdims=True) + EPS) * g


def _gelu(x):
    return 0.5 * x * (1.0 + lax.erf(x * (2.0 ** -0.5)))


def _gmlp_v(z, lng_ref, lnb_ref):
    out = []
    for hh in range(GMLP_HEADS):
        sl = slice(hh * LANES, (hh + 1) * LANES)
        vh = _gelu(z[:, GMLP_WIDTH + hh * LANES:GMLP_WIDTH + (hh + 1) * LANES])
        vc = vh - jnp.mean(vh, axis=-1, keepdims=True)
        y = vc * lax.rsqrt(jnp.mean(vc * vc, axis=-1, keepdims=True) + EPS)
        out.append(y * lng_ref[:, sl] + lnb_ref[:, sl])
    return out


def _softmax_sink_unnorm(s, sink):
    m = jnp.maximum(jnp.max(s, axis=-1, keepdims=True), sink)
    p = jnp.exp(s - m)
    den = jnp.sum(p, axis=-1, keepdims=True) + jnp.exp(sink - m)
    return p, 1.0 / den


def _dot_nt(a, b):
    return lax.dot_general(a, b, (((1,), (1,)), ((), ())), preferred_element_type=F32)


def _mixer_prompt_kernel(*refs):
    i = pl.program_id(0)

    @pl.when(i < NT_PROMPT)
    def _():
        _mixer_prompt_tile(*refs)

    @pl.when(i == NT_PROMPT)
    def _():
        xo_ref = refs[13]
        xo_ref[...] = jnp.zeros_like(xo_ref)


def _mixer_prompt_tile(x_ref, ng_ref, win_ref, bin_ref, lng_ref, lnb_ref, ws_ref, bs_ref,
                       sink_ref, bias_ref, ga_ref, gb_ref, wout_ref,
                       xo_ref, ks_ref, vs_ref, kprev_ref, vprev_ref):
    i = pl.program_id(0)

    @pl.when(i == 0)
    def _():
        kprev_ref[...] = jnp.zeros_like(kprev_ref)
        vprev_ref[...] = jnp.zeros_like(vprev_ref)

    x = x_ref[...]
    h = _rms(x, ng_ref[...]).astype(BF16)
    z = jnp.dot(h, win_ref[...], preferred_element_type=F32) + bin_ref[...]

    u = _gelu(z[:, :GMLP_WIDTH])
    v_heads = _gmlp_v(z, lng_ref, lnb_ref)
    row = lax.broadcasted_iota(jnp.int32, (CHUNK, CHUNK), 0)
    col = lax.broadcasted_iota(jnp.int32, (CHUNK, CHUNK), 1)
    a_cols = []
    for hh in range(GMLP_HEADS):
        w = jnp.where(row >= col, ws_ref[hh], 0.0).astype(BF16)
        v_cat = jnp.concatenate(
            [v_heads[hh][c * CHUNK:(c + 1) * CHUNK] for c in range(BLOCKS_PER_TILE)], axis=1)
        mixed = jnp.dot(w, v_cat.astype(BF16), preferred_element_type=F32)
        bias = bs_ref[:, hh * LANES:(hh + 1) * LANES]
        a_cols.append(jnp.concatenate(
            [mixed[:, c * LANES:(c + 1) * LANES] + bias for c in range(BLOCKS_PER_TILE)], axis=0))
    a = u * jnp.concatenate(a_cols, axis=1)
    a_n = _rms(a, ga_ref[...]).astype(BF16)

    q = z[:, Q0:K0] * (HEAD_DIM ** -0.5)
    k = z[:, K0:V0]
    val = z[:, V0:]
    k_bf = k.astype(BF16)
    v_bf = val.astype(BF16)
    lane = lax.broadcasted_iota(jnp.int32, (1, LANES), 1)
    low = lane < HEAD_DIM
    first = jnp.where(i % TILES_PER_SEQ == 0, 1, 0)
    k_prev = kprev_ref[...]
    v_prev = vprev_ref[...]
    b_rows = []
    for c in range(BLOCKS_PER_TILE):
        rows = slice(c * WINDOW, (c + 1) * WINDOW)
        k_band = jnp.concatenate([k_prev, k_bf[rows]], axis=0)
        v_band = jnp.concatenate([v_prev, v_bf[rows]], axis=0)
        o_kv = []
        for kv in range(KV_HEADS):
            keep = low if kv == 0 else jnp.logical_not(low)
            qm = jnp.concatenate(
                [jnp.where(keep, q[rows, g * LANES:(g + 1) * LANES], 0.0) for g in range(GQA_GROUP)],
                axis=0).astype(BF16)
            s_all = _dot_nt(qm, k_band)
            ps, invs = [], []
            for g in range(GQA_GROUP):
                hd = kv * GQA_GROUP + g
                bias = bias_ref[first, hd] if c == 0 else bias_ref[0, hd]
                p, inv = _softmax_sink_unnorm(s_all[g * WINDOW:(g + 1) * WINDOW] + bias, sink_ref[hd])
                ps.append(p.astype(BF16))
                invs.append(inv)
            o = jnp.dot(jnp.concatenate(ps, axis=0), v_band, preferred_element_type=F32)
            o_kv.append(o * jnp.concatenate(invs, axis=0))
        b_rows.append(jnp.concatenate(
            [jnp.where(low, o_kv[0][g * WINDOW:(g + 1) * WINDOW], o_kv[1][g * WINDOW:(g + 1) * WINDOW])
             for g in range(GQA_GROUP)], axis=1))
        k_prev = k_bf[rows]
        v_prev = v_bf[rows]
    b = jnp.concatenate(b_rows, axis=0)
    b_n = _rms(b, gb_ref[...]).astype(BF16)

    merged = jnp.concatenate([a_n, b_n], axis=1)
    xo_ref[...] = x + jnp.dot(merged, wout_ref[...], preferred_element_type=F32)

    kprev_ref[...] = k_prev
    vprev_ref[...] = v_prev
    ks_ref[0] = k[TM - WINDOW:]
    vs_ref[0] = val[TM - WINDOW:]


def _mixer_prompt(x, lw):
    const = lambda *shape: pl.BlockSpec(shape, lambda i: (0,) * len(shape))
    return pl.pallas_call(
        _mixer_prompt_kernel,
        grid=(NT_ALL,),
        in_specs=[
            pl.BlockSpec((TM, D_MODEL), lambda i: (jnp.minimum(i, NT_PROMPT - 1), 0)),
            const(1, D_MODEL), const(D_MODEL, IN_WIDTH), const(1, IN_WIDTH),
            const(1, GMLP_WIDTH), const(1, GMLP_WIDTH),
            const(GMLP_HEADS, CHUNK, CHUNK), const(CHUNK, GMLP_WIDTH),
            pl.BlockSpec(memory_space=pltpu.SMEM),
            const(2, ATTN_HEADS, WINDOW, 2 * WINDOW),
            const(1, GMLP_WIDTH), const(1, ATTN_WIDTH), const(D_MODEL, D_MODEL),
        ],
        out_specs=[
            pl.BlockSpec((TM, D_MODEL), lambda i: (i, 0)),
            pl.BlockSpec((1, WINDOW, KV_WIDTH), lambda i: (jnp.minimum(i // TILES_PER_SEQ, BATCH - 1), 0, 0)),
            pl.BlockSpec((1, WINDOW, KV_WIDTH), lambda i: (jnp.minimum(i // TILES_PER_SEQ, BATCH - 1), 0, 0)),
        ],
        out_shape=[
            jax.ShapeDtypeStruct((T_ALL, D_MODEL), F32),
            jax.ShapeDtypeStruct((BATCH, WINDOW, KV_WIDTH), F32),
            jax.ShapeDtypeStruct((BATCH, WINDOW, KV_WIDTH), F32),
        ],
        scratch_shapes=[pltpu.VMEM((WINDOW, KV_WIDTH), BF16), pltpu.VMEM((WINDOW, KV_WIDTH), BF16)],
        compiler_params=pltpu.CompilerParams(
            dimension_semantics=("arbitrary",), vmem_limit_bytes=VMEM_LIMIT),
        name="mixer_prompt",
    )(x, lw["norm_g"], lw["w_in"], lw["b_in"], lw["ln_g"], lw["ln_b"], lw["w_s"], lw["b_s_prompt"],
      lw["sinks"], lw["bias_prompt"], lw["g_a"], lw["g_b"], lw["w_out"])


def _mixer_sample_kernel(x_ref, xalias_ref, ck_ref, cv_ref, ng_ref, win_ref, bin_ref, lng_ref, lnb_ref,
                         wsd_ref, bsd_ref, sinkcol_ref, bias_ref, ga_ref, gb_ref, wout_ref,
                         xo_ref, ko_ref, vo_ref, cvo_ref,
                         q_s, k_s, v_s, a_s, b_s, kfull, vfull):
    del xalias_ref
    j = pl.program_id(0)

    @pl.when(j == 0)
    def _():
        h = _rms(x_ref[...], ng_ref[...]).astype(BF16)
        z = jnp.dot(h, win_ref[...], preferred_element_type=F32) + bin_ref[...]
        u = _gelu(z[:, :GMLP_WIDTH])
        v = jnp.concatenate(_gmlp_v(z, lng_ref, lnb_ref), axis=1)
        cvo_ref[...] = v
        t_row = lax.broadcasted_iota(jnp.int32, (TM, 1), 0) % DEC_SEQ
        mixed = bsd_ref[...]
        for d in range(DEC_SEQ):
            vd = v if d == 0 else pltpu.roll(v, d, 0)
            mixed = mixed + jnp.where(t_row >= d, wsd_ref[d], 0.0) * vd
        a_s[...] = _rms(u * mixed, ga_ref[...]).astype(BF16)
        q_s[...] = z[:, Q0:K0] * (HEAD_DIM ** -0.5)
        k_s[...] = z[:, K0:V0]
        v_s[...] = z[:, V0:]
        kfull[:, CACHE_W + DEC_SEQ:, :] = jnp.zeros((SEQ_GROUP, KEYS_PAD - CACHE_W - DEC_SEQ, KV_WIDTH), F32)
        vfull[:, CACHE_W + DEC_SEQ:, :] = jnp.zeros((SEQ_GROUP, KEYS_PAD - CACHE_W - DEC_SEQ, KV_WIDTH), F32)

    base = pl.multiple_of(j * (SEQ_GROUP * DEC_SEQ), SEQ_GROUP * DEC_SEQ)
    kfull[:, :CACHE_W, :] = ck_ref[0]
    vfull[:, :CACHE_W, :] = cv_ref[0]
    k_new = k_s[pl.ds(base, SEQ_GROUP * DEC_SEQ), :]
    v_new = v_s[pl.ds(base, SEQ_GROUP * DEC_SEQ), :]
    for s in range(SEQ_GROUP):
        kfull[s, CACHE_W:CACHE_W + DEC_SEQ, :] = k_new[s * DEC_SEQ:(s + 1) * DEC_SEQ]
        vfull[s, CACHE_W:CACHE_W + DEC_SEQ, :] = v_new[s * DEC_SEQ:(s + 1) * DEC_SEQ]
    ko_ref[0] = kfull[:, DEC_SEQ:DEC_SEQ + CACHE_W, :]
    vo_ref[0] = vfull[:, DEC_SEQ:DEC_SEQ + CACHE_W, :]

    lane = lax.broadcasted_iota(jnp.int32, (1, LANES), 1)
    low = lane < HEAD_DIM
    pair_row = lax.broadcasted_iota(jnp.int32, (8, 1), 0) // DEC_SEQ
    sink = sinkcol_ref[...]

    def pair_body(m, carry):
        r0 = pl.multiple_of(base + m * 8, 8)
        q8 = q_s[pl.ds(r0, 8), :]
        qm = jnp.concatenate(
            [jnp.where(low if kv == 0 else jnp.logical_not(low), q8[:, g * LANES:(g + 1) * LANES], 0.0)
             for kv in range(KV_HEADS) for g in range(GQA_GROUP)], axis=0).astype(BF16)
        b8 = [jnp.zeros((8, LANES), F32) for _ in range(GQA_GROUP)]
        for s in range(2):
            kn = kfull[2 * m + s].astype(BF16)
            vn = vfull[2 * m + s].astype(BF16)
            p, inv = _softmax_sink_unnorm(_dot_nt(qm, kn) + bias_ref[s], sink)
            o = jnp.dot(p.astype(BF16), vn, preferred_element_type=F32) * inv
            for g in range(GQA_GROUP):
                og = jnp.where(low, o[g * 8:(g + 1) * 8],
                               o[(GQA_GROUP + g) * 8:(GQA_GROUP + g + 1) * 8])
                b8[g] = jnp.where(pair_row == s, og, b8[g])
        b_s[pl.ds(r0, 8), :] = jnp.concatenate(b8, axis=1)
        return carry

    lax.fori_loop(0, SEQ_GROUP // 2, pair_body, 0)

    @pl.when(j == N_SEQ_GROUPS - 1)
    def _():
        b_n = _rms(b_s[...], gb_ref[...]).astype(BF16)
        merged = jnp.concatenate([a_s[...], b_n], axis=1)
        xo_ref[...] = x_ref[...] + jnp.dot(merged, wout_ref[...], preferred_element_type=F32)


def _mixer_sample(x, x_block, x_buf, cache_k, cache_v, layer, lw):
    const = lambda *shape: pl.BlockSpec(shape, lambda j: (0,) * len(shape))
    cache_spec = pl.BlockSpec((1, SEQ_GROUP, CACHE_W, KV_WIDTH), lambda j: (layer, j, 0, 0))
    state_spec = pl.BlockSpec((1, SEQ_GROUP, CACHE_W, KV_WIDTH), lambda j: (j, 0, 0, 0))
    state_shape = jax.ShapeDtypeStruct((N_SEQ_GROUPS, SEQ_GROUP, CACHE_W, KV_WIDTH), F32)
    return pl.pallas_call(
        _mixer_sample_kernel,
        grid=(N_SEQ_GROUPS,),
        in_specs=[
            pl.BlockSpec((TM, D_MODEL), lambda j: (x_block, 0)),
            pl.BlockSpec(memory_space=pl.ANY),
            cache_spec, cache_spec,
            const(1, D_MODEL), const(D_MODEL, IN_WIDTH), const(1, IN_WIDTH),
            const(1, GMLP_WIDTH), const(1, GMLP_WIDTH),
            const(DEC_SEQ, TM, GMLP_WIDTH), const(TM, GMLP_WIDTH),
            const(2 * GQA_GROUP * 8, 1), const(2, 2 * GQA_GROUP * 8, KEYS_PAD),
            const(1, GMLP_WIDTH), const(1, ATTN_WIDTH), const(D_MODEL, D_MODEL),
        ],
        out_specs=[
            pl.BlockSpec((TM, D_MODEL), lambda j: (NT_PROMPT, 0)),
            state_spec, state_spec,
            const(TM, GMLP_WIDTH),
        ],
        out_shape=[
            jax.ShapeDtypeStruct((T_ALL, D_MODEL), F32),
            state_shape, state_shape,
            jax.ShapeDtypeStruct((TM, GMLP_WIDTH), F32),
        ],
        scratch_shapes=[
            pltpu.VMEM((TM, ATTN_WIDTH), F32), pltpu.VMEM((TM, KV_WIDTH), F32),
            pltpu.VMEM((TM, KV_WIDTH), F32), pltpu.VMEM((TM, GMLP_WIDTH), BF16),
            pltpu.VMEM((TM, ATTN_WIDTH), F32),
            pltpu.VMEM((SEQ_GROUP, KEYS_PAD, KV_WIDTH), F32),
            pltpu.VMEM((SEQ_GROUP, KEYS_PAD, KV_WIDTH), F32),
        ],
        input_output_aliases={1: 0},
        compiler_params=pltpu.CompilerParams(
            dimension_semantics=("arbitrary",), vmem_limit_bytes=VMEM_LIMIT),
        name="mixer_sample",
    )(x, x_buf, cache_k, cache_v, lw["norm_g"], lw["w_in"], lw["b_in"], lw["ln_g"], lw["ln_b"],
      lw["w_s_sample"], lw["b_s_sample"], lw["sink_col"], lw["bias_sample"],
      lw["g_a"], lw["g_b"], lw["w_out"])


def _swiglu_kernel(x_ref, ng_ref, wg_ref, wu_ref, wd_ref, xo_ref):
    x = x_ref[...]
    h = _rms(x, ng_ref[...]).astype(BF16)
    gate = jnp.dot(h, wg_ref[...], preferred_element_type=F32)
    up = jnp.dot(h, wu_ref[...], preferred_element_type=F32)
    act = (jax.nn.silu(gate) * up).astype(BF16)
    xo_ref[...] = x + jnp.dot(act, wd_ref[...], preferred_element_type=F32)


def _swiglu(x, norm_g, wg, wu, wd):
    const = lambda *shape: pl.BlockSpec(shape, lambda i: (0,) * len(shape),
                                        pipeline_mode=pl.Buffered(1))
    return pl.pallas_call(
        _swiglu_kernel,
        grid=(NT_ALL,),
        in_specs=[
            pl.BlockSpec((TM, D_MODEL), lambda i: (i, 0)),
            const(1, D_MODEL), const(D_MODEL, D_FF), const(D_MODEL, D_FF), const(D_FF, D_MODEL),
        ],
        out_specs=pl.BlockSpec((TM, D_MODEL), lambda i: (i, 0)),
        out_shape=jax.ShapeDtypeStruct((T_ALL, D_MODEL), F32),
        compiler_params=pltpu.CompilerParams(
            dimension_semantics=("arbitrary",), vmem_limit_bytes=VMEM_LIMIT),
        name="swiglu",
    )(x, norm_g, wg, wu, wd)


def _moe_sort_kernel(x_ref, ng_ref, wr_ref, hs_ref, meta_ref, off_ref, cnt_ref):
    lane = lax.broadcasted_iota(jnp.int32, (1, LANES), 1)
    h = _rms(x_ref[...], ng_ref[...])
    logits = jnp.dot(h, wr_ref[...], preferred_element_type=F32, precision=lax.Precision.HIGHEST)
    logits = jnp.where(lane < N_EXPERTS, logits, NEG_INF)
    m1 = jnp.max(logits, axis=-1, keepdims=True)
    i1 = jnp.min(jnp.where(logits == m1, lane, LANES), axis=-1, keepdims=True)
    rest = jnp.where(lane == i1, NEG_INF, logits)
    m2 = jnp.max(rest, axis=-1, keepdims=True)
    i2 = jnp.min(jnp.where(rest == m2, lane, LANES), axis=-1, keepdims=True)
    e2 = jnp.exp(m2 - m1)
    den = 1.0 + e2
    sel0 = lane == i1
    sel1 = lane == i2
    routed = jnp.where(jnp.logical_or(sel0, sel1), 1.0, 0.0)
    r_i = lax.broadcasted_iota(jnp.int32, (TM, TM), 0)
    c_i = lax.broadcasted_iota(jnp.int32, (TM, TM), 1)
    lower = jnp.where(r_i > c_i, 1.0, 0.0).astype(BF16)
    rank = jnp.dot(lower, routed.astype(BF16), preferred_element_type=F32)
    cnt = jnp.broadcast_to(jnp.sum(routed, axis=0, keepdims=True), (8, LANES))
    cnt_pad = jnp.floor((cnt + (PIECE - 1)) * (1.0 / PIECE)) * PIECE
    incl = cnt_pad
    for sh in (1, 2, 4):
        incl = incl + jnp.where(lane >= sh, pltpu.roll(incl, sh, 1), 0.0)
    off = incl - cnt_pad
    pos = off[0:1] + rank
    pos0 = jnp.sum(jnp.where(sel0, pos, 0.0), axis=-1, keepdims=True)
    pos1 = jnp.sum(jnp.where(sel1, pos, 0.0), axis=-1, keepdims=True)
    meta = jnp.where(lane == 0, pos0, jnp.where(lane == 1, pos1,
                     jnp.where(lane == 2, 1.0 / den, jnp.where(lane == 3, e2 / den, 0.0))))
    meta_ref[...] = meta
    meta_t = meta.T
    p_row = lax.broadcasted_iota(jnp.int32, (SORT_ROWS, 1), 0).astype(F32)
    hit = jnp.logical_or(p_row == meta_t[0:1], p_row == meta_t[1:2])
    gather = jnp.where(hit, 1.0, 0.0).astype(BF16)
    hs_ref[...] = jnp.dot(gather, h.astype(BF16), preferred_element_type=F32).astype(BF16)
    off_ref[0] = off.astype(jnp.int32)
    cnt_ref[0] = cnt_pad.astype(jnp.int32)


def _moe_sort(x, norm_g, w_router):
    return pl.pallas_call(
        _moe_sort_kernel,
        grid=(NT_ALL,),
        in_specs=[
            pl.BlockSpec((TM, D_MODEL), lambda i: (i, 0)),
            pl.BlockSpec((1, D_MODEL), lambda i: (0, 0)),
            pl.BlockSpec((D_MODEL, LANES), lambda i: (0, 0)),
        ],
        out_specs=[
            pl.BlockSpec((SORT_ROWS, D_MODEL), lambda i: (i, 0)),
            pl.BlockSpec((TM, LANES), lambda i: (i, 0)),
            pl.BlockSpec((1, 8, LANES), lambda i: (i, 0, 0)),
            pl.BlockSpec((1, 8, LANES), lambda i: (i, 0, 0)),
        ],
        out_shape=[
            jax.ShapeDtypeStruct((NT_ALL * SORT_ROWS, D_MODEL), BF16),
            jax.ShapeDtypeStruct((T_ALL, LANES), F32),
            jax.ShapeDtypeStruct((NT_ALL, 8, LANES), jnp.int32),
            jax.ShapeDtypeStruct((NT_ALL, 8, LANES), jnp.int32),
        ],
        compiler_params=pltpu.CompilerParams(
            dimension_semantics=("arbitrary",), vmem_limit_bytes=VMEM_LIMIT),
        name="moe_sort",
    )(x, norm_g, w_router)


def _moe_experts_kernel(off_ref, cnt_ref, hs_hbm, wg_ref, wu_ref, wd_ref, ys_hbm, buf, lhs, sem):
    s = pl.program_id(0)
    e = pl.program_id(1)
    row0 = pl.multiple_of(s * SUPER_ROWS, PIECE)

    @pl.when(e == 0)
    def _():
        cp = pltpu.make_async_copy(hs_hbm.at[pl.ds(row0, SUPER_ROWS)], buf.at[pl.ds(0, SUPER_ROWS)], sem)
        cp.start()
        cp.wait()

    starts, cums = [], [0]
    for i in range(TILES_PER_SUPER):
        idx = (s * TILES_PER_SUPER + i) * N_EXPERTS + e
        starts.append(i * SORT_ROWS + off_ref[idx])
        cums.append(cums[-1] + cnt_ref[idx] // PIECE)
    n_pieces = cums[-1]

    def chunk_body(q, carry):
        rows = []
        for j in range(PIECES_PER_CHUNK):
            p = q * PIECES_PER_CHUNK + j
            pc = jnp.minimum(p, n_pieces - 1)
            row = starts[0] + PIECE * pc
            for i in range(1, TILES_PER_SUPER):
                row = jnp.where(pc >= cums[i], starts[i] + PIECE * (pc - cums[i]), row)
            rows.append(jnp.where(p < n_pieces, row, SUPER_ROWS))
            lhs[j * PIECE:(j + 1) * PIECE, :] = buf[pl.ds(pl.multiple_of(row, PIECE), PIECE), :]
        xs = lhs[...]
        gate = jnp.dot(xs, wg_ref[0], preferred_element_type=F32)
        up = jnp.dot(xs, wu_ref[0], preferred_element_type=F32)
        act = (jax.nn.silu(gate) * up).astype(BF16)
        y = jnp.dot(act, wd_ref[0], preferred_element_type=F32).astype(BF16)
        for j in range(PIECES_PER_CHUNK):
            buf[pl.ds(pl.multiple_of(rows[j], PIECE), PIECE), :] = y[j * PIECE:(j + 1) * PIECE]
        return carry

    lax.fori_loop(0, (n_pieces + PIECES_PER_CHUNK - 1) // PIECES_PER_CHUNK, chunk_body, 0)

    @pl.when(e == N_EXPERTS - 1)
    def _():
        cp = pltpu.make_async_copy(buf.at[pl.ds(0, SUPER_ROWS)], ys_hbm.at[pl.ds(row0, SUPER_ROWS)], sem)
        cp.start()
        cp.wait()


def _moe_experts(off, cnt, hs, wg, wu, wd):
    grid_spec = pltpu.PrefetchScalarGridSpec(
        num_scalar_prefetch=2,
        grid=(N_SUPER, N_EXPERTS),
        in_specs=[
            pl.BlockSpec(memory_space=pl.ANY),
            pl.BlockSpec((1, D_MODEL, D_FF_EXPERT), lambda s, e, *_: (e, 0, 0)),
            pl.BlockSpec((1, D_MODEL, D_FF_EXPERT), lambda s, e, *_: (e, 0, 0)),
            pl.BlockSpec((1, D_FF_EXPERT, D_MODEL), lambda s, e, *_: (e, 0, 0)),
        ],
        out_specs=pl.BlockSpec(memory_space=pl.ANY),
        scratch_shapes=[pltpu.VMEM((SUPER_ROWS + PIECE, D_MODEL), BF16),
                        pltpu.VMEM((CHUNK_ROWS, D_MODEL), BF16),
                        pltpu.SemaphoreType.DMA],
    )
    return pl.pallas_call(
        _moe_experts_kernel,
        grid_spec=grid_spec,
        out_shape=jax.ShapeDtypeStruct((NT_ALL * SORT_ROWS, D_MODEL), BF16),
        compiler_params=pltpu.CompilerParams(
            dimension_semantics=("arbitrary", "arbitrary"), vmem_limit_bytes=VMEM_LIMIT),
        name="moe_experts",
    )(off, cnt, hs, wg, wu, wd)


def _moe_unsort_kernel(x_ref, ys_ref, meta_ref, xo_ref):
    meta = meta_ref[...]
    p_col = lax.broadcasted_iota(jnp.int32, (1, SORT_ROWS), 1).astype(F32)
    scatter = (jnp.where(p_col == meta[:, 0:1], meta[:, 2:3], 0.0)
               + jnp.where(p_col == meta[:, 1:2], meta[:, 3:4], 0.0)).astype(BF16)
    xo_ref[...] = x_ref[...] + jnp.dot(scatter, ys_ref[...], preferred_element_type=F32)


def _moe_unsort(x, ys, meta):
    return pl.pallas_call(
        _moe_unsort_kernel,
        grid=(NT_ALL,),
        in_specs=[
            pl.BlockSpec((TM, D_MODEL), lambda i: (i, 0)),
            pl.BlockSpec((SORT_ROWS, D_MODEL), lambda i: (i, 0)),
            pl.BlockSpec((TM, LANES), lambda i: (i, 0)),
        ],
        out_specs=pl.BlockSpec((TM, D_MODEL), lambda i: (i, 0)),
        out_shape=jax.ShapeDtypeStruct((T_ALL, D_MODEL), F32),
        compiler_params=pltpu.CompilerParams(
            dimension_semantics=("arbitrary",), vmem_limit_bytes=VMEM_LIMIT),
        name="moe_unsort",
    )(x, ys, meta)


def _moe(x, norm_g, w_router, wg, wu, wd):
    hs, meta, off, cnt = _moe_sort(x, norm_g, w_router)
    off = off[:, 0, :N_EXPERTS].reshape(-1)
    cnt = cnt[:, 0, :N_EXPERTS].reshape(-1)
    ys = _moe_experts(off, cnt, hs, wg, wu, wd)
    return _moe_unsort(x, ys, meta)


def _final_norm_kernel(x_ref, g_ref, yp_ref, ys_ref):
    i = pl.program_id(0)
    y = _rms(x_ref[...], g_ref[...])

    @pl.when(i < NT_PROMPT)
    def _():
        yp_ref[...] = y

    @pl.when(i == NT_PROMPT)
    def _():
        ys_ref[...] = y


def _final_norm(x, g):
    return pl.pallas_call(
        _final_norm_kernel,
        grid=(NT_ALL,),
        in_specs=[pl.BlockSpec((TM, D_MODEL), lambda i: (i, 0)),
                  pl.BlockSpec((1, D_MODEL), lambda i: (0, 0))],
        out_specs=[pl.BlockSpec((TM, D_MODEL), lambda i: (jnp.minimum(i, NT_PROMPT - 1), 0)),
                   pl.BlockSpec((TM, D_MODEL), lambda i: (0, 0))],
        out_shape=[jax.ShapeDtypeStruct((T_PROMPT, D_MODEL), F32),
                   jax.ShapeDtypeStruct((T_SAMPLE, D_MODEL), F32)],
        compiler_params=pltpu.CompilerParams(dimension_semantics=("arbitrary",)),
        name="final_norm",
    )(x, g)


def kernel(x_prompt, x_sample, cache_win_k, cache_win_v, attn_norm_g, w_in, b_in, chunk_ln_g, chunk_ln_b, w_spatial, b_spatial, attn_sinks, mix_norm_a_g, mix_norm_b_g, w_out, ffn_norm_g, ffn_w_gate, ffn_w_up, ffn_w_down, router_w, expert_w_gate, expert_w_up, expert_w_down, final_norm_g):
    qp = _q_perm()
    w_in_p = jnp.concatenate([w_in[..., :Q0], w_in[..., Q0:K0][..., qp], w_in[..., K0:]], axis=-1).astype(BF16)
    b_in_p = jnp.concatenate([b_in[..., :Q0], b_in[..., Q0:K0][..., qp], b_in[..., K0:]], axis=-1)
    w_out_p = jnp.concatenate([w_out[:, :GMLP_WIDTH], w_out[:, GMLP_WIDTH:][:, qp]], axis=1).astype(BF16)
    g_b_p = mix_norm_b_g[:, qp]
    bs_prompt = jnp.repeat(jnp.transpose(b_spatial, (0, 2, 1)), GMLP_HEAD_DIM, axis=2)
    bs_sample = jnp.tile(bs_prompt[:, :DEC_SEQ], (1, DEC_BATCH, 1))
    t_idx = np.arange(DEC_SEQ)
    ws_small = jnp.stack([w_spatial[:, :, t_idx, np.maximum(t_idx - d, 0)] for d in range(DEC_SEQ)], axis=1)
    ws_sample = jnp.tile(jnp.repeat(jnp.transpose(ws_small, (0, 1, 3, 2)), GMLP_HEAD_DIM, axis=3),
                         (1, 1, DEC_BATCH, 1))
    sink_col = jnp.repeat(attn_sinks, 8, axis=1)[..., None]
    bias_prompt = jnp.asarray(_prompt_bias())
    bias_sample = jnp.asarray(_sample_bias())
    wr_pad = jnp.pad(router_w, ((0, 0), (0, 0), (0, LANES - N_EXPERTS)))
    ffn_wg, ffn_wu, ffn_wd = ffn_w_gate.astype(BF16), ffn_w_up.astype(BF16), ffn_w_down.astype(BF16)
    ex_wg, ex_wu, ex_wd = expert_w_gate.astype(BF16), expert_w_up.astype(BF16), expert_w_down.astype(BF16)

    ck = cache_win_k.reshape(DEPTH, DEC_BATCH, CACHE_W, KV_WIDTH)
    cv = cache_win_v.reshape(DEPTH, DEC_BATCH, CACHE_W, KV_WIDTH)

    x = None
    kp, vp, ksm, vsm, cvs = [], [], [], [], []
    for l in range(DEPTH):
        lw = dict(
            norm_g=attn_norm_g[l][None], w_in=w_in_p[l], b_in=b_in_p[l][None],
            ln_g=chunk_ln_g[l].reshape(1, GMLP_WIDTH), ln_b=chunk_ln_b[l].reshape(1, GMLP_WIDTH),
            w_s=w_spatial[l], b_s_prompt=bs_prompt[l], w_s_sample=ws_sample[l], b_s_sample=bs_sample[l],
            sinks=attn_sinks[l], sink_col=sink_col[l], bias_prompt=bias_prompt, bias_sample=bias_sample,
            g_a=mix_norm_a_g[l][None], g_b=g_b_p[l][None], w_out=w_out_p[l])
        if l == 0:
            x_new, k_p, v_p = _mixer_prompt(x_prompt.reshape(T_PROMPT, D_MODEL), lw)
            x_new, k_s, v_s, cv_s = _mixer_sample(x_sample.reshape(T_SAMPLE, D_MODEL), 0, x_new, ck, cv, l, lw)
        else:
            x_new, k_p, v_p = _mixer_prompt(x, lw)
            x_new, k_s, v_s, cv_s = _mixer_sample(x, NT_PROMPT, x_new, ck, cv, l, lw)
        kp.append(k_p)
        vp.append(v_p)
        ksm.append(k_s)
        vsm.append(v_s)
        cvs.append(cv_s)
        i = l // 2
        if l % 2 == 0:
            x = _swiglu(x_new, ffn_norm_g[l][None], ffn_wg[i], ffn_wu[i], ffn_wd[i])
        else:
            x = _moe(x_new, ffn_norm_g[l][None], wr_pad[i], ex_wg[i], ex_wu[i], ex_wd[i])
    y_p, y_s = _final_norm(x, final_norm_g[None])
    win_p = (DEPTH, BATCH, WINDOW, KV_HEADS, HEAD_DIM)
    win_s = (DEPTH, DEC_BATCH, CACHE_W, KV_HEADS, HEAD_DIM)
    return (y_p.reshape(BATCH, SEQ, D_MODEL), y_s.reshape(DEC_BATCH, DEC_SEQ, D_MODEL),
            jnp.stack(kp).reshape(win_p), jnp.stack(vp).reshape(win_p),
            jnp.stack(ksm).reshape(win_s), jnp.stack(vsm).reshape(win_s),
            jnp.stack(cvs).reshape(DEPTH, DEC_BATCH, DEC_SEQ, GMLP_HEADS, GMLP_HEAD_DIM))
```

```python
import functools

import numpy as np
import jax
import jax.numpy as jnp
from jax import lax
from jax.experimental import pallas as pl
from jax.experimental.pallas import tpu as pltpu

D_MODEL = 1024
BATCH = 4
SEQ = 4096
DEPTH = 4
DEC_BATCH = 128
DEC_SEQ = 4
HEAD_DIM = 64
ATTN_HEADS = 8
KV_HEADS = 2
GQA_GROUP = ATTN_HEADS // KV_HEADS
ATTN_WIDTH = ATTN_HEADS * HEAD_DIM
KV_WIDTH = KV_HEADS * HEAD_DIM
WINDOW = 128
CHUNK = 128
GMLP_HEADS = 4
GMLP_HEAD_DIM = 128
GMLP_WIDTH = GMLP_HEADS * GMLP_HEAD_DIM
IN_WIDTH = 2 * GMLP_WIDTH + ATTN_WIDTH + 2 * KV_WIDTH
D_FF = 2816
N_EXPERTS = 8
D_FF_EXPERT = 1408
CACHE_W = 128
EPS = 1e-6

LANES = 128
TM = 512
T_PROMPT = BATCH * SEQ
T_SAMPLE = DEC_BATCH * DEC_SEQ
T_ALL = T_PROMPT + T_SAMPLE
NT_PROMPT = T_PROMPT // TM
NT_ALL = T_ALL // TM
TILES_PER_SEQ = SEQ // TM
BLOCKS_PER_TILE = TM // WINDOW
SEQ_GROUP = 16
N_SEQ_GROUPS = DEC_BATCH // SEQ_GROUP
KEYS_PAD = CACHE_W + 8
PIECE = 16
SORT_ROWS = 2 * TM + N_EXPERTS * PIECE
TILES_PER_SUPER = 11
N_SUPER = NT_ALL // TILES_PER_SUPER
SUPER_ROWS = TILES_PER_SUPER * SORT_ROWS
CHUNK_ROWS = 256
PIECES_PER_CHUNK = CHUNK_ROWS // PIECE
VMEM_LIMIT = 56 * 1024 * 1024

F32 = jnp.float32
BF16 = jnp.bfloat16
NEG_INF = float("-inf")

Q0 = 2 * GMLP_WIDTH
K0 = Q0 + ATTN_WIDTH
V0 = K0 + KV_WIDTH


def _g_major(a, axis):
    axis = axis % a.ndim
    shape = a.shape
    a = a.reshape(shape[:axis] + (KV_HEADS, GQA_GROUP, HEAD_DIM) + shape[axis + 1:])
    return jnp.swapaxes(a, axis, axis + 1).reshape(shape)


def _slope(kv, g):
    return 2.0 ** (-8.0 * (kv * GQA_GROUP + g + 1) / ATTN_HEADS)


def _prompt_bias():
    c = np.arange(WINDOW)[:, None]
    qi = np.arange(WINDOW)[None, :]
    dist = np.where(c > qi, qi + WINDOW - c, qi - c)
    out = np.zeros((2, KV_HEADS, WINDOW, GQA_GROUP * WINDOW), np.float32)
    for var in range(2):
        ok = ~((var == 1) & (c > qi))
        for kv in range(KV_HEADS):
            for g in range(GQA_GROUP):
                out[var, kv, :, g * WINDOW:(g + 1) * WINDOW] = np.where(ok, -_slope(kv, g) * dist, NEG_INF)
    return out


def _sample_bias():
    out = np.zeros((2, 2 * GQA_GROUP * 8, KEYS_PAD), np.float32)
    j = np.arange(KEYS_PAD)
    kpos = np.where(j < CACHE_W, j - CACHE_W, j - CACHE_W)
    for s in range(2):
        for kv in range(KV_HEADS):
            for g in range(GQA_GROUP):
                for sp in range(2):
                    for t in range(DEC_SEQ):
                        row = (kv * GQA_GROUP + g) * 8 + sp * DEC_SEQ + t
                        dist = t - kpos
                        ok = (dist >= 0) & (dist < WINDOW) & (j < CACHE_W + DEC_SEQ)
                        if sp == s:
                            out[s, row] = np.where(ok, -_slope(kv, g) * dist, NEG_INF)
    return out


def _rms(x, g):
    return x * lax.rsqrt(jnp.mean(x * x, axis=-1, keepdims=True) + EPS) * g


def _gelu(x):
    return 0.5 * x * (1.0 + lax.erf(x * (2.0 ** -0.5)))


def _gmlp_v(z, lng_ref, lnb_ref):
    out = []
    for hh in range(GMLP_HEADS):
        sl = slice(hh * LANES, (hh + 1) * LANES)
        vh = _gelu(z[:, GMLP_WIDTH + hh * LANES:GMLP_WIDTH + (hh + 1) * LANES])
        vc = vh - jnp.mean(vh, axis=-1, keepdims=True)
        y = vc * lax.rsqrt(jnp.mean(vc * vc, axis=-1, keepdims=True) + EPS)
        out.append(y * lng_ref[:, sl] + lnb_ref[:, sl])
    return out


def _softmax_sink_unnorm(s, sink):
    m = jnp.maximum(jnp.max(s, axis=-1, keepdims=True), sink)
    p = jnp.exp(s - m)
    den = jnp.sum(p, axis=-1, keepdims=True) + jnp.exp(sink - m)
    return p, 1.0 / den


def _dot_nt(a, b):
    return lax.dot_general(a, b, (((1,), (1,)), ((), ())), preferred_element_type=F32)


def _mixer_prompt_kernel(*refs):
    i = pl.program_id(0)

    @pl.when(i < NT_PROMPT)
    def _():
        _mixer_prompt_tile(*refs)

    @pl.when(i == NT_PROMPT)
    def _():
        xo_ref = refs[13]
        xo_ref[...] = jnp.zeros_like(xo_ref)


def _mixer_prompt_tile(x_ref, ng_ref, win_ref, bin_ref, lng_ref, lnb_ref, ws_ref, bs_ref,
                       sink_ref, bias_ref, ga_ref, gb_ref, wout_ref,
                       xo_ref, ks_ref, vs_ref, kprev_ref, vprev_ref):
    i = pl.program_id(0)

    @pl.when(i == 0)
    def _():
        kprev_ref[...] = jnp.zeros_like(kprev_ref)
        vprev_ref[...] = jnp.zeros_like(vprev_ref)

    x = x_ref[...]
    h = _rms(x, ng_ref[...]).astype(BF16)
    z = jnp.dot(h, win_ref[...], preferred_element_type=F32) + bin_ref[...]

    u = _gelu(z[:, :GMLP_WIDTH])
    v_heads = _gmlp_v(z, lng_ref, lnb_ref)
    row = lax.broadcasted_iota(jnp.int32, (CHUNK, CHUNK), 0)
    col = lax.broadcasted_iota(jnp.int32, (CHUNK, CHUNK), 1)
    a_cols = []
    for hh in range(GMLP_HEADS):
        w = jnp.where(row >= col, ws_ref[hh], 0.0).astype(BF16)
        v_cat = jnp.concatenate(
            [v_heads[hh][c * CHUNK:(c + 1) * CHUNK] for c in range(BLOCKS_PER_TILE)], axis=1)
        mixed = jnp.dot(w, v_cat.astype(BF16), preferred_element_type=F32)
        bias = bs_ref[:, hh * LANES:(hh + 1) * LANES]
        a_cols.append(jnp.concatenate(
            [mixed[:, c * LANES:(c + 1) * LANES] + bias for c in range(BLOCKS_PER_TILE)], axis=0))
    a = u * jnp.concatenate(a_cols, axis=1)
    a_n = _rms(a, ga_ref[...]).astype(BF16)

    q = z[:, Q0:K0] * (HEAD_DIM ** -0.5)
    k = z[:, K0:V0]
    val = z[:, V0:]
    k_bf = k.astype(BF16)
    lane = lax.broadcasted_iota(jnp.int32, (1, LANES), 1)
    low = lane < HEAD_DIM
    first = jnp.where(i % TILES_PER_SEQ == 0, 1, 0)
    c_i = lax.broadcasted_iota(jnp.int32, (WINDOW, WINDOW), 0)
    q_i = lax.broadcasted_iota(jnp.int32, (WINDOW, WINDOW), 1)
    from_prev = jnp.concatenate([c_i > q_i] * GQA_GROUP, axis=1)
    g_lane = lax.broadcasted_iota(jnp.int32, (1, GQA_GROUP * WINDOW), 1) // WINDOW
    top = lax.broadcasted_iota(jnp.int32, (KV_WIDTH, 1), 0) < HEAD_DIM
    one = jnp.ones((), BF16)
    vt_bf = val.T.astype(BF16)
    vt_all = jnp.concatenate([vprev_ref[...], vt_bf], axis=1)
    vt_aug = [jnp.where(top, vt_all, one), jnp.where(top, one, vt_all)]
    k_all = jnp.concatenate([kprev_ref[...], k_bf], axis=0)

    b_rows = []
    for c in range(BLOCKS_PER_TILE):
        rows = slice(c * WINDOW, (c + 1) * WINDOW)
        band = slice(c * WINDOW, (c + 2) * WINDOW)
        o_t = []
        for kv in range(KV_HEADS):
            keep = low if kv == 0 else jnp.logical_not(low)
            qm = jnp.concatenate(
                [jnp.where(keep, q[rows, g * LANES:(g + 1) * LANES], 0.0) for g in range(GQA_GROUP)],
                axis=0).astype(BF16)
            sink = jnp.full((1, GQA_GROUP * WINDOW), sink_ref[kv * GQA_GROUP], F32)
            for g in range(1, GQA_GROUP):
                sink = jnp.where(g_lane == g, sink_ref[kv * GQA_GROUP + g], sink)
            s2 = _dot_nt(k_all[band], qm)
            bias = bias_ref[first, kv] if c == 0 else bias_ref[0, kv]
            s = jnp.where(from_prev, s2[:WINDOW], s2[WINDOW:]) + bias
            m = jnp.maximum(jnp.max(s, axis=0, keepdims=True), sink)
            p = jnp.exp(s - m)
            p2 = jnp.concatenate([jnp.where(from_prev, p, 0.0), jnp.where(from_prev, 0.0, p)],
                                 axis=0).astype(BF16)
            o = jnp.dot(vt_aug[kv][:, band], p2, preferred_element_type=F32)
            own, ones_row = (o[:HEAD_DIM], o[HEAD_DIM:HEAD_DIM + 1]) if kv == 0 else (o[HEAD_DIM:], o[0:1])
            o_t.append(own / (ones_row + jnp.exp(sink - m)))
        bt = jnp.concatenate(o_t, axis=0)
        b_rows.append(jnp.concatenate(
            [bt[:, g * WINDOW:(g + 1) * WINDOW].T for g in range(GQA_GROUP)], axis=1))
    b = jnp.concatenate(b_rows, axis=0)
    b_n = _rms(b, gb_ref[...]).astype(BF16)

    merged = jnp.concatenate([a_n, b_n], axis=1)
    xo_ref[...] = x + jnp.dot(merged, wout_ref[...], preferred_element_type=F32)

    kprev_ref[...] = k_bf[TM - WINDOW:]
    vprev_ref[...] = vt_bf[:, TM - WINDOW:]
    ks_ref[0] = k[TM - WINDOW:]
    vs_ref[0] = val[TM - WINDOW:]


def _mixer_prompt(x, lw):
    const = lambda *shape: pl.BlockSpec(shape, lambda i: (0,) * len(shape))
    return pl.pallas_call(
        _mixer_prompt_kernel,
        grid=(NT_ALL,),
        in_specs=[
            pl.BlockSpec((TM, D_MODEL), lambda i: (jnp.minimum(i, NT_PROMPT - 1), 0)),
            const(1, D_MODEL), const(D_MODEL, IN_WIDTH), const(1, IN_WIDTH),
            const(1, GMLP_WIDTH), const(1, GMLP_WIDTH),
            const(GMLP_HEADS, CHUNK, CHUNK), const(CHUNK, GMLP_WIDTH),
            pl.BlockSpec(memory_space=pltpu.SMEM),
            const(2, KV_HEADS, WINDOW, GQA_GROUP * WINDOW),
            const(1, GMLP_WIDTH), const(1, ATTN_WIDTH), const(D_MODEL, D_MODEL),
        ],
        out_specs=[
            pl.BlockSpec((TM, D_MODEL), lambda i: (i, 0)),
            pl.BlockSpec((1, WINDOW, KV_WIDTH), lambda i: (jnp.minimum(i // TILES_PER_SEQ, BATCH - 1), 0, 0)),
            pl.BlockSpec((1, WINDOW, KV_WIDTH), lambda i: (jnp.minimum(i // TILES_PER_SEQ, BATCH - 1), 0, 0)),
        ],
        out_shape=[
            jax.ShapeDtypeStruct((T_ALL, D_MODEL), F32),
            jax.ShapeDtypeStruct((BATCH, WINDOW, KV_WIDTH), F32),
            jax.ShapeDtypeStruct((BATCH, WINDOW, KV_WIDTH), F32),
        ],
        scratch_shapes=[pltpu.VMEM((WINDOW, KV_WIDTH), BF16), pltpu.VMEM((WINDOW, KV_WIDTH), BF16)],
        compiler_params=pltpu.CompilerParams(
            dimension_semantics=("arbitrary",), vmem_limit_bytes=VMEM_LIMIT),
        name="mixer_prompt",
    )(x, lw["norm_g"], lw["w_in"], lw["b_in"], lw["ln_g"], lw["ln_b"], lw["w_s"], lw["b_s_prompt"],
      lw["sinks"], lw["bias_prompt"], lw["g_a"], lw["g_b"], lw["w_out"])


def _mixer_sample_kernel(x_ref, xalias_ref, ck_ref, cv_ref, ng_ref, win_ref, bin_ref, lng_ref, lnb_ref,
                         wsd_ref, bsd_ref, sinkcol_ref, bias_ref, ga_ref, gb_ref, wout_ref,
                         xo_ref, ko_ref, vo_ref, cvo_ref,
                         q_s, k_s, v_s, a_s, b_s, kfull, vfull):
    del xalias_ref
    j = pl.program_id(0)

    @pl.when(j == 0)
    def _():
        h = _rms(x_ref[...], ng_ref[...]).astype(BF16)
        z = jnp.dot(h, win_ref[...], preferred_element_type=F32) + bin_ref[...]
        u = _gelu(z[:, :GMLP_WIDTH])
        v = jnp.concatenate(_gmlp_v(z, lng_ref, lnb_ref), axis=1)
        cvo_ref[...] = v
        t_row = lax.broadcasted_iota(jnp.int32, (8, 1), 0) % DEC_SEQ

        def tile_rows(pat):
            return jnp.broadcast_to(pat[None], (TM // 8,) + pat.shape).reshape(TM, pat.shape[-1])

        mixed = tile_rows(bsd_ref[...])
        for d in range(DEC_SEQ):
            vd = v if d == 0 else pltpu.roll(v, d, 0)
            mixed = mixed + tile_rows(jnp.where(t_row >= d, wsd_ref[d], 0.0)) * vd
        a_s[...] = _rms(u * mixed, ga_ref[...]).astype(BF16)
        q_s[...] = z[:, Q0:K0] * (HEAD_DIM ** -0.5)
        k_s[...] = z[:, K0:V0]
        v_s[...] = z[:, V0:]
        kfull[:, CACHE_W + DEC_SEQ:, :] = jnp.zeros((SEQ_GROUP, KEYS_PAD - CACHE_W - DEC_SEQ, KV_WIDTH), F32)
        vfull[:, CACHE_W + DEC_SEQ:, :] = jnp.zeros((SEQ_GROUP, KEYS_PAD - CACHE_W - DEC_SEQ, KV_WIDTH), F32)

    base = pl.multiple_of(j * (SEQ_GROUP * DEC_SEQ), SEQ_GROUP * DEC_SEQ)
    kfull[:, :CACHE_W, :] = ck_ref[0]
    vfull[:, :CACHE_W, :] = cv_ref[0]
    k_new = k_s[pl.ds(base, SEQ_GROUP * DEC_SEQ), :]
    v_new = v_s[pl.ds(base, SEQ_GROUP * DEC_SEQ), :]
    for s in range(SEQ_GROUP):
        kfull[s, CACHE_W:CACHE_W + DEC_SEQ, :] = k_new[s * DEC_SEQ:(s + 1) * DEC_SEQ]
        vfull[s, CACHE_W:CACHE_W + DEC_SEQ, :] = v_new[s * DEC_SEQ:(s + 1) * DEC_SEQ]
    ko_ref[0] = kfull[:, DEC_SEQ:DEC_SEQ + CACHE_W, :]
    vo_ref[0] = vfull[:, DEC_SEQ:DEC_SEQ + CACHE_W, :]

    lane = lax.broadcasted_iota(jnp.int32, (1, LANES), 1)
    low = lane < HEAD_DIM
    pair_row = lax.broadcasted_iota(jnp.int32, (8, 1), 0) // DEC_SEQ
    sink = sinkcol_ref[...]

    def pair_body(m, carry):
        r0 = pl.multiple_of(base + m * 8, 8)
        q8 = q_s[pl.ds(r0, 8), :]
        qm = jnp.concatenate(
            [jnp.where(low if kv == 0 else jnp.logical_not(low), q8[:, g * LANES:(g + 1) * LANES], 0.0)
             for kv in range(KV_HEADS) for g in range(GQA_GROUP)], axis=0).astype(BF16)
        b8 = [jnp.zeros((8, LANES), F32) for _ in range(GQA_GROUP)]
        for s in range(2):
            kn = kfull[2 * m + s].astype(BF16)
            vn = vfull[2 * m + s].astype(BF16)
            p, inv = _softmax_sink_unnorm(_dot_nt(qm, kn) + bias_ref[s], sink)
            o = jnp.dot(p.astype(BF16), vn, preferred_element_type=F32) * inv
            for g in range(GQA_GROUP):
                og = jnp.where(low, o[g * 8:(g + 1) * 8],
                               o[(GQA_GROUP + g) * 8:(GQA_GROUP + g + 1) * 8])
                b8[g] = jnp.where(pair_row == s, og, b8[g])
        b_s[pl.ds(r0, 8), :] = jnp.concatenate(b8, axis=1)
        return carry

    lax.fori_loop(0, SEQ_GROUP // 2, pair_body, 0)

    @pl.when(j == N_SEQ_GROUPS - 1)
    def _():
        b_n = _rms(b_s[...], gb_ref[...]).astype(BF16)
        merged = jnp.concatenate([a_s[...], b_n], axis=1)
        xo_ref[...] = x_ref[...] + jnp.dot(merged, wout_ref[...], preferred_element_type=F32)


def _mixer_sample(x, x_block, x_buf, cache_k, cache_v, layer, lw):
    const = lambda *shape: pl.BlockSpec(shape, lambda j: (0,) * len(shape))
    cache_spec = pl.BlockSpec((1, SEQ_GROUP, CACHE_W, KV_WIDTH), lambda j: (layer, j, 0, 0))
    state_spec = pl.BlockSpec((1, SEQ_GROUP, CACHE_W, KV_WIDTH), lambda j: (j, 0, 0, 0))
    state_shape = jax.ShapeDtypeStruct((N_SEQ_GROUPS, SEQ_GROUP, CACHE_W, KV_WIDTH), F32)
    return pl.pallas_call(
        _mixer_sample_kernel,
        grid=(N_SEQ_GROUPS,),
        in_specs=[
            pl.BlockSpec((TM, D_MODEL), lambda j: (x_block, 0)),
            pl.BlockSpec(memory_space=pl.ANY),
            cache_spec, cache_spec,
            const(1, D_MODEL), const(D_MODEL, IN_WIDTH), const(1, IN_WIDTH),
            const(1, GMLP_WIDTH), const(1, GMLP_WIDTH),
            const(DEC_SEQ, 8, GMLP_WIDTH), const(8, GMLP_WIDTH),
            const(2 * GQA_GROUP * 8, 1), const(2, 2 * GQA_GROUP * 8, KEYS_PAD),
            const(1, GMLP_WIDTH), const(1, ATTN_WIDTH), const(D_MODEL, D_MODEL),
        ],
        out_specs=[
            pl.BlockSpec((TM, D_MODEL), lambda j: (NT_PROMPT, 0)),
            state_spec, state_spec,
            const(TM, GMLP_WIDTH),
        ],
        out_shape=[
            jax.ShapeDtypeStruct((T_ALL, D_MODEL), F32),
            state_shape, state_shape,
            jax.ShapeDtypeStruct((TM, GMLP_WIDTH), F32),
        ],
        scratch_shapes=[
            pltpu.VMEM((TM, ATTN_WIDTH), F32), pltpu.VMEM((TM, KV_WIDTH), F32),
            pltpu.VMEM((TM, KV_WIDTH), F32), pltpu.VMEM((TM, GMLP_WIDTH), BF16),
            pltpu.VMEM((TM, ATTN_WIDTH), F32),
            pltpu.VMEM((SEQ_GROUP, KEYS_PAD, KV_WIDTH), F32),
            pltpu.VMEM((SEQ_GROUP, KEYS_PAD, KV_WIDTH), F32),
        ],
        input_output_aliases={1: 0},
        compiler_params=pltpu.CompilerParams(
            dimension_semantics=("arbitrary",), vmem_limit_bytes=VMEM_LIMIT),
        name="mixer_sample",
    )(x, x_buf, cache_k, cache_v, lw["norm_g"], lw["w_in"], lw["b_in"], lw["ln_g"], lw["ln_b"],
      lw["w_s_sample"], lw["b_s_sample"], lw["sink_col"], lw["bias_sample"],
      lw["g_a"], lw["g_b"], lw["w_out"])


def _swiglu_kernel(x_ref, ng_ref, wg_ref, wu_ref, wd_ref, xo_ref):
    x = x_ref[...]
    h = _rms(x, ng_ref[...]).astype(BF16)
    gate = jnp.dot(h, wg_ref[...], preferred_element_type=F32)
    up = jnp.dot(h, wu_ref[...], preferred_element_type=F32)
    act = (jax.nn.silu(gate) * up).astype(BF16)
    xo_ref[...] = x + jnp.dot(act, wd_ref[...], preferred_element_type=F32)


def _swiglu(x, norm_g, wg, wu, wd):
    const = lambda *shape: pl.BlockSpec(shape, lambda i: (0,) * len(shape),
                                        pipeline_mode=pl.Buffered(1))
    return pl.pallas_call(
        _swiglu_kernel,
        grid=(NT_ALL,),
        in_specs=[
            pl.BlockSpec((TM, D_MODEL), lambda i: (i, 0)),
            const(1, D_MODEL), const(D_MODEL, D_FF), const(D_MODEL, D_FF), const(D_FF, D_MODEL),
        ],
        out_specs=pl.BlockSpec((TM, D_MODEL), lambda i: (i, 0)),
        out_shape=jax.ShapeDtypeStruct((T_ALL, D_MODEL), F32),
        compiler_params=pltpu.CompilerParams(
            dimension_semantics=("arbitrary",), vmem_limit_bytes=VMEM_LIMIT),
        name="swiglu",
    )(x, norm_g, wg, wu, wd)


def _moe_sort_kernel(x_ref, ng_ref, wrh_ref, wrl_ref, hs_ref, meta_ref, off_ref, cnt_ref):
    lane = lax.broadcasted_iota(jnp.int32, (1, LANES), 1)
    h = _rms(x_ref[...], ng_ref[...])
    h_hi = h.astype(BF16)
    h_lo = (h - h_hi.astype(F32)).astype(BF16)
    logits = (jnp.dot(h_hi, wrh_ref[...], preferred_element_type=F32)
              + jnp.dot(h_lo, wrh_ref[...], preferred_element_type=F32)
              + jnp.dot(h_hi, wrl_ref[...], preferred_element_type=F32))
    logits = jnp.where(lane < N_EXPERTS, logits, NEG_INF)
    m1 = jnp.max(logits, axis=-1, keepdims=True)
    i1 = jnp.min(jnp.where(logits == m1, lane, LANES), axis=-1, keepdims=True)
    rest = jnp.where(lane == i1, NEG_INF, logits)
    m2 = jnp.max(rest, axis=-1, keepdims=True)
    i2 = jnp.min(jnp.where(rest == m2, lane, LANES), axis=-1, keepdims=True)
    e2 = jnp.exp(m2 - m1)
    den = 1.0 + e2
    sel0 = lane == i1
    sel1 = lane == i2
    routed = jnp.where(jnp.logical_or(sel0, sel1), 1.0, 0.0)
    r_i = lax.broadcasted_iota(jnp.int32, (TM, TM), 0)
    c_i = lax.broadcasted_iota(jnp.int32, (TM, TM), 1)
    lower = jnp.where(r_i > c_i, 1.0, 0.0).astype(BF16)
    rank = jnp.dot(lower, routed.astype(BF16), preferred_element_type=F32)
    cnt = jnp.broadcast_to(jnp.sum(routed, axis=0, keepdims=True), (8, LANES))
    cnt_pad = jnp.floor((cnt + (PIECE - 1)) * (1.0 / PIECE)) * PIECE
    incl = cnt_pad
    for sh in (1, 2, 4):
        incl = incl + jnp.where(lane >= sh, pltpu.roll(incl, sh, 1), 0.0)
    off = incl - cnt_pad
    pos = off[0:1] + rank
    pos0 = jnp.sum(jnp.where(sel0, pos, 0.0), axis=-1, keepdims=True)
    pos1 = jnp.sum(jnp.where(sel1, pos, 0.0), axis=-1, keepdims=True)
    meta = jnp.where(lane == 0, pos0, jnp.where(lane == 1, pos1,
                     jnp.where(lane == 2, 1.0 / den, jnp.where(lane == 3, e2 / den, 0.0))))
    meta_ref[...] = meta
    meta_t = meta.T
    p_row = lax.broadcasted_iota(jnp.int32, (SORT_ROWS, 1), 0).astype(F32)
    hit = jnp.logical_or(p_row == meta_t[0:1], p_row == meta_t[1:2])
    gather = jnp.where(hit, 1.0, 0.0).astype(BF16)
    hs_ref[...] = jnp.dot(gather, h_hi, preferred_element_type=F32).astype(BF16)
    off_ref[0] = off.astype(jnp.int32)
    cnt_ref[0] = cnt_pad.astype(jnp.int32)


def _moe_sort(x, norm_g, wr_hi, wr_lo):
    return pl.pallas_call(
        _moe_sort_kernel,
        grid=(NT_ALL,),
        in_specs=[
            pl.BlockSpec((TM, D_MODEL), lambda i: (i, 0)),
            pl.BlockSpec((1, D_MODEL), lambda i: (0, 0)),
            pl.BlockSpec((D_MODEL, LANES), lambda i: (0, 0)),
            pl.BlockSpec((D_MODEL, LANES), lambda i: (0, 0)),
        ],
        out_specs=[
            pl.BlockSpec((SORT_ROWS, D_MODEL), lambda i: (i, 0)),
            pl.BlockSpec((TM, LANES), lambda i: (i, 0)),
            pl.BlockSpec((1, 8, LANES), lambda i: (i, 0, 0)),
            pl.BlockSpec((1, 8, LANES), lambda i: (i, 0, 0)),
        ],
        out_shape=[
            jax.ShapeDtypeStruct((NT_ALL * SORT_ROWS, D_MODEL), BF16),
            jax.ShapeDtypeStruct((T_ALL, LANES), F32),
            jax.ShapeDtypeStruct((NT_ALL, 8, LANES), jnp.int32),
            jax.ShapeDtypeStruct((NT_ALL, 8, LANES), jnp.int32),
        ],
        compiler_params=pltpu.CompilerParams(
            dimension_semantics=("arbitrary",), vmem_limit_bytes=VMEM_LIMIT),
        name="moe_sort",
    )(x, norm_g, wr_hi, wr_lo)


def _moe_experts_kernel(off_ref, cnt_ref, hs_hbm, wg_ref, wu_ref, wd_ref, ys_hbm, buf, lhs, sem):
    s = pl.program_id(0)
    e = pl.program_id(1)
    row0 = pl.multiple_of(s * SUPER_ROWS, PIECE)

    @pl.when(e == 0)
    def _():
        cp = pltpu.make_async_copy(hs_hbm.at[pl.ds(row0, SUPER_ROWS)], buf.at[pl.ds(0, SUPER_ROWS)], sem)
        cp.start()
        cp.wait()

    starts, cums = [], [0]
    for i in range(TILES_PER_SUPER):
        idx = (s * TILES_PER_SUPER + i) * N_EXPERTS + e
        starts.append(i * SORT_ROWS + off_ref[idx])
        cums.append(cums[-1] + cnt_ref[idx] // PIECE)
    n_pieces = cums[-1]

    def chunk_body(q, carry):
        rows = []
        for j in range(PIECES_PER_CHUNK):
            p = q * PIECES_PER_CHUNK + j
            pc = jnp.minimum(p, n_pieces - 1)
            row = starts[0] + PIECE * pc
            for i in range(1, TILES_PER_SUPER):
                row = jnp.where(pc >= cums[i], starts[i] + PIECE * (pc - cums[i]), row)
            rows.append(jnp.where(p < n_pieces, row, SUPER_ROWS))
            lhs[j * PIECE:(j + 1) * PIECE, :] = buf[pl.ds(pl.multiple_of(row, PIECE), PIECE), :]
        xs = lhs[...]
        gate = jnp.dot(xs, wg_ref[0], preferred_element_type=F32)
        up = jnp.dot(xs, wu_ref[0], preferred_element_type=F32)
        act = (jax.nn.silu(gate) * up).astype(BF16)
        y = jnp.dot(act, wd_ref[0], preferred_element_type=F32).astype(BF16)
        for j in range(PIECES_PER_CHUNK):
            buf[pl.ds(pl.multiple_of(rows[j], PIECE), PIECE), :] = y[j * PIECE:(j + 1) * PIECE]
        return carry

    lax.fori_loop(0, (n_pieces + PIECES_PER_CHUNK - 1) // PIECES_PER_CHUNK, chunk_body, 0)

    @pl.when(e == N_EXPERTS - 1)
    def _():
        cp = pltpu.make_async_copy(buf.at[pl.ds(0, SUPER_ROWS)], ys_hbm.at[pl.ds(row0, SUPER_ROWS)], sem)
        cp.start()
        cp.wait()


def _moe_experts(off, cnt, hs, wg, wu, wd):
    grid_spec = pltpu.PrefetchScalarGridSpec(
        num_scalar_prefetch=2,
        grid=(N_SUPER, N_EXPERTS),
        in_specs=[
            pl.BlockSpec(memory_space=pl.ANY),
            pl.BlockSpec((1, D_MODEL, D_FF_EXPERT), lambda s, e, *_: (e, 0, 0)),
            pl.BlockSpec((1, D_MODEL, D_FF_EXPERT), lambda s, e, *_: (e, 0, 0)),
            pl.BlockSpec((1, D_FF_EXPERT, D_MODEL), lambda s, e, *_: (e, 0, 0)),
        ],
        out_specs=pl.BlockSpec(memory_space=pl.ANY),
        scratch_shapes=[pltpu.VMEM((SUPER_ROWS + PIECE, D_MODEL), BF16),
                        pltpu.VMEM((CHUNK_ROWS, D_MODEL), BF16),
                        pltpu.SemaphoreType.DMA],
    )
    return pl.pallas_call(
        _moe_experts_kernel,
        grid_spec=grid_spec,
        out_shape=jax.ShapeDtypeStruct((NT_ALL * SORT_ROWS, D_MODEL), BF16),
        compiler_params=pltpu.CompilerParams(
            dimension_semantics=("arbitrary", "arbitrary"), vmem_limit_bytes=VMEM_LIMIT),
        name="moe_experts",
    )(off, cnt, hs, wg, wu, wd)


def _unsorted_residual(x_ref, ys_ref, meta_ref):
    meta = meta_ref[...]
    p_col = lax.broadcasted_iota(jnp.int32, (1, SORT_ROWS), 1).astype(F32)
    scatter = (jnp.where(p_col == meta[:, 0:1], meta[:, 2:3], 0.0)
               + jnp.where(p_col == meta[:, 1:2], meta[:, 3:4], 0.0)).astype(BF16)
    return x_ref[...] + jnp.dot(scatter, ys_ref[...], preferred_element_type=F32)


def _moe_unsort_kernel(x_ref, ys_ref, meta_ref, xo_ref):
    xo_ref[...] = _unsorted_residual(x_ref, ys_ref, meta_ref)


def _moe_unsort_final_kernel(x_ref, ys_ref, meta_ref, g_ref, yp_ref, ysm_ref):
    i = pl.program_id(0)
    y = _rms(_unsorted_residual(x_ref, ys_ref, meta_ref), g_ref[...])

    @pl.when(i < NT_PROMPT)
    def _():
        yp_ref[...] = y

    @pl.when(i == NT_PROMPT)
    def _():
        ysm_ref[...] = y


def _moe_unsort(x, ys, meta, final_g=None):
    in_specs = [
        pl.BlockSpec((TM, D_MODEL), lambda i: (i, 0)),
        pl.BlockSpec((SORT_ROWS, D_MODEL), lambda i: (i, 0)),
        pl.BlockSpec((TM, LANES), lambda i: (i, 0)),
    ]
    params = pltpu.CompilerParams(dimension_semantics=("arbitrary",), vmem_limit_bytes=VMEM_LIMIT)
    if final_g is None:
        return pl.pallas_call(
            _moe_unsort_kernel,
            grid=(NT_ALL,),
            in_specs=in_specs,
            out_specs=pl.BlockSpec((TM, D_MODEL), lambda i: (i, 0)),
            out_shape=jax.ShapeDtypeStruct((T_ALL, D_MODEL), F32),
            compiler_params=params,
            name="moe_unsort",
        )(x, ys, meta)
    return pl.pallas_call(
        _moe_unsort_final_kernel,
        grid=(NT_ALL,),
        in_specs=in_specs + [pl.BlockSpec((1, D_MODEL), lambda i: (0, 0))],
        out_specs=[pl.BlockSpec((TM, D_MODEL), lambda i: (jnp.minimum(i, NT_PROMPT - 1), 0)),
                   pl.BlockSpec((TM, D_MODEL), lambda i: (0, 0))],
        out_shape=[jax.ShapeDtypeStruct((T_PROMPT, D_MODEL), F32),
                   jax.ShapeDtypeStruct((T_SAMPLE, D_MODEL), F32)],
        compiler_params=params,
        name="moe_unsort_final",
    )(x, ys, meta, final_g)


def _moe(x, norm_g, wr_hi, wr_lo, wg, wu, wd, final_g=None):
    hs, meta, off, cnt = _moe_sort(x, norm_g, wr_hi, wr_lo)
    off = off[:, 0, :N_EXPERTS].reshape(-1)
    cnt = cnt[:, 0, :N_EXPERTS].reshape(-1)
    ys = _moe_experts(off, cnt, hs, wg, wu, wd)
    return _moe_unsort(x, ys, meta, final_g)


def kernel(x_prompt, x_sample, cache_win_k, cache_win_v, attn_norm_g, w_in, b_in, chunk_ln_g, chunk_ln_b, w_spatial, b_spatial, attn_sinks, mix_norm_a_g, mix_norm_b_g, w_out, ffn_norm_g, ffn_w_gate, ffn_w_up, ffn_w_down, router_w, expert_w_gate, expert_w_up, expert_w_down, final_norm_g):
    w_in_p = jnp.concatenate([w_in[..., :Q0], _g_major(w_in[..., Q0:K0], -1), w_in[..., K0:]], axis=-1).astype(BF16)
    b_in_p = jnp.concatenate([b_in[..., :Q0], _g_major(b_in[..., Q0:K0], -1), b_in[..., K0:]], axis=-1)
    w_out_p = jnp.concatenate([w_out[:, :GMLP_WIDTH], _g_major(w_out[:, GMLP_WIDTH:], 1)], axis=1).astype(BF16)
    g_b_p = _g_major(mix_norm_b_g, 1)
    bs_prompt = jnp.repeat(jnp.transpose(b_spatial, (0, 2, 1)), GMLP_HEAD_DIM, axis=2)
    bs_sample = jnp.tile(bs_prompt[:, :DEC_SEQ], (1, 2, 1))
    t_idx = np.arange(DEC_SEQ)
    ws_small = jnp.stack([w_spatial[:, :, t_idx, np.maximum(t_idx - d, 0)] for d in range(DEC_SEQ)], axis=1)
    ws_sample = jnp.tile(jnp.repeat(jnp.transpose(ws_small, (0, 1, 3, 2)), GMLP_HEAD_DIM, axis=3),
                         (1, 1, 2, 1))
    sink_col = jnp.repeat(attn_sinks, 8, axis=1)[..., None]
    bias_prompt = jnp.asarray(_prompt_bias())
    bias_sample = jnp.asarray(_sample_bias())
    wr_pad = jnp.pad(router_w, ((0, 0), (0, 0), (0, LANES - N_EXPERTS)))
    wr_hi = wr_pad.astype(BF16)
    wr_lo = (wr_pad - wr_hi.astype(F32)).astype(BF16)
    ffn_wg, ffn_wu, ffn_wd = ffn_w_gate.astype(BF16), ffn_w_up.astype(BF16), ffn_w_down.astype(BF16)
    ex_wg, ex_wu, ex_wd = expert_w_gate.astype(BF16), expert_w_up.astype(BF16), expert_w_down.astype(BF16)

    ck = cache_win_k.reshape(DEPTH, DEC_BATCH, CACHE_W, KV_WIDTH)
    cv = cache_win_v.reshape(DEPTH, DEC_BATCH, CACHE_W, KV_WIDTH)

    x = None
    kp, vp, ksm, vsm, cvs = [], [], [], [], []
    for l in range(DEPTH):
        lw = dict(
            norm_g=attn_norm_g[l][None], w_in=w_in_p[l], b_in=b_in_p[l][None],
            ln_g=chunk_ln_g[l].reshape(1, GMLP_WIDTH), ln_b=chunk_ln_b[l].reshape(1, GMLP_WIDTH),
            w_s=w_spatial[l], b_s_prompt=bs_prompt[l], w_s_sample=ws_sample[l], b_s_sample=bs_sample[l],
            sinks=attn_sinks[l], sink_col=sink_col[l], bias_prompt=bias_prompt, bias_sample=bias_sample,
            g_a=mix_norm_a_g[l][None], g_b=g_b_p[l][None], w_out=w_out_p[l])
        if l == 0:
            x_new, k_p, v_p = _mixer_prompt(x_prompt.reshape(T_PROMPT, D_MODEL), lw)
            x_new, k_s, v_s, cv_s = _mixer_sample(x_sample.reshape(T_SAMPLE, D_MODEL), 0, x_new, ck, cv, l, lw)
        else:
            x_new, k_p, v_p = _mixer_prompt(x, lw)
            x_new, k_s, v_s, cv_s = _mixer_sample(x, NT_PROMPT, x_new, ck, cv, l, lw)
        kp.append(k_p)
        vp.append(v_p)
        ksm.append(k_s)
        vsm.append(v_s)
        cvs.append(cv_s)
        i = l // 2
        if l % 2 == 0:
            x = _swiglu(x_new, ffn_norm_g[l][None], ffn_wg[i], ffn_wu[i], ffn_wd[i])
        else:
            final_g = final_norm_g[None] if l == DEPTH - 1 else None
            x = _moe(x_new, ffn_norm_g[l][None], wr_hi[i], wr_lo[i], ex_wg[i], ex_wu[i], ex_wd[i], final_g)
    y_p, y_s = x
    win_p = (DEPTH, BATCH, WINDOW, KV_HEADS, HEAD_DIM)
    win_s = (DEPTH, DEC_BATCH, CACHE_W, KV_HEADS, HEAD_DIM)
    return (y_p.reshape(BATCH, SEQ, D_MODEL), y_s.reshape(DEC_BATCH, DEC_SEQ, D_MODEL),
            jnp.stack(kp).reshape(win_p), jnp.stack(vp).reshape(win_p),
            jnp.stack(ksm).reshape(win_s), jnp.stack(vsm).reshape(win_s),
            jnp.stack(cvs).reshape(DEPTH, DEC_BATCH, DEC_SEQ, GMLP_HEADS, GMLP_HEAD_DIM))
```

```python
import functools

import numpy as np
import jax
import jax.numpy as jnp
from jax import lax
from jax.experimental import pallas as pl
from jax.experimental.pallas import tpu as pltpu

D_MODEL = 1024
BATCH = 4
SEQ = 4096
DEPTH = 4
DEC_BATCH = 128
DEC_SEQ = 4
HEAD_DIM = 64
ATTN_HEADS = 8
KV_HEADS = 2
GQA_GROUP = ATTN_HEADS // KV_HEADS
ATTN_WIDTH = ATTN_HEADS * HEAD_DIM
KV_WIDTH = KV_HEADS * HEAD_DIM
WINDOW = 128
CHUNK = 128
GMLP_HEADS = 4
GMLP_HEAD_DIM = 128
GMLP_WIDTH = GMLP_HEADS * GMLP_HEAD_DIM
IN_WIDTH = 2 * GMLP_WIDTH + ATTN_WIDTH + 2 * KV_WIDTH
D_FF = 2816
N_EXPERTS = 8
D_FF_EXPERT = 1408
CACHE_W = 128
EPS = 1e-6

LANES = 128
TM = 512
T_PROMPT = BATCH * SEQ
T_SAMPLE = DEC_BATCH * DEC_SEQ
T_ALL = T_PROMPT + T_SAMPLE
NT_PROMPT = T_PROMPT // TM
NT_ALL = T_ALL // TM
TILES_PER_SEQ = SEQ // TM
BLOCKS_PER_TILE = TM // WINDOW
SEQ_GROUP = 16
N_SEQ_GROUPS = DEC_BATCH // SEQ_GROUP
KEYS_PAD = CACHE_W + 8
PIECE = 16
SORT_ROWS = 2 * TM + N_EXPERTS * PIECE
TILES_PER_SUPER = 11
N_SUPER = NT_ALL // TILES_PER_SUPER
SUPER_ROWS = TILES_PER_SUPER * SORT_ROWS
SORT_TILES_PER_STEP = 3
CHUNK_ROWS = 256
PIECES_PER_CHUNK = CHUNK_ROWS // PIECE
VMEM_LIMIT = 56 * 1024 * 1024

F32 = jnp.float32
BF16 = jnp.bfloat16
NEG_INF = float("-inf")

Q0 = 2 * GMLP_WIDTH
K0 = Q0 + ATTN_WIDTH
V0 = K0 + KV_WIDTH


def _g_major(a, axis):
    axis = axis % a.ndim
    shape = a.shape
    a = a.reshape(shape[:axis] + (KV_HEADS, GQA_GROUP, HEAD_DIM) + shape[axis + 1:])
    return jnp.swapaxes(a, axis, axis + 1).reshape(shape)


def _slope(kv, g):
    return 2.0 ** (-8.0 * (kv * GQA_GROUP + g + 1) / ATTN_HEADS)


def _prompt_bias():
    c = np.arange(WINDOW)[:, None]
    qi = np.arange(WINDOW)[None, :]
    dist = np.where(c > qi, qi + WINDOW - c, qi - c)
    out = np.zeros((2, KV_HEADS, WINDOW, GQA_GROUP * WINDOW), np.float32)
    for var in range(2):
        ok = ~((var == 1) & (c > qi))
        for kv in range(KV_HEADS):
            for g in range(GQA_GROUP):
                out[var, kv, :, g * WINDOW:(g + 1) * WINDOW] = np.where(ok, -_slope(kv, g) * dist, NEG_INF)
    return out


def _sample_bias():
    out = np.zeros((2, 2 * GQA_GROUP * 8, KEYS_PAD), np.float32)
    j = np.arange(KEYS_PAD)
    kpos = np.where(j < CACHE_W, j - CACHE_W, j - CACHE_W)
    for s in range(2):
        for kv in range(KV_HEADS):
            for g in range(GQA_GROUP):
                for sp in range(2):
                    for t in range(DEC_SEQ):
                        row = (kv * GQA_GROUP + g) * 8 + sp * DEC_SEQ + t
                        dist = t - kpos
                        ok = (dist >= 0) & (dist < WINDOW) & (j < CACHE_W + DEC_SEQ)
                        if sp == s:
                            out[s, row] = np.where(ok, -_slope(kv, g) * dist, NEG_INF)
    return out


def _rms(x, g):
    return x * lax.rsqrt(jnp.mean(x * x, axis=-1, keepdims=True) + EPS) * g


def _gelu(x):
    return 0.5 * x * (1.0 + lax.erf(x * (2.0 ** -0.5)))


def _gmlp_v(z, lng_ref, lnb_ref):
    out = []
    for hh in range(GMLP_HEADS):
        sl = slice(hh * LANES, (hh + 1) * LANES)
        vh = _gelu(z[:, GMLP_WIDTH + hh * LANES:GMLP_WIDTH + (hh + 1) * LANES])
        vc = vh - jnp.mean(vh, axis=-1, keepdims=True)
        y = vc * lax.rsqrt(jnp.mean(vc * vc, axis=-1, keepdims=True) + EPS)
        out.append(y * lng_ref[:, sl] + lnb_ref[:, sl])
    return out


def _softmax_sink_unnorm(s, sink):
    m = jnp.maximum(jnp.max(s, axis=-1, keepdims=True), sink)
    p = jnp.exp(s - m)
    den = jnp.sum(p, axis=-1, keepdims=True) + jnp.exp(sink - m)
    return p, 1.0 / den


def _dot_nt(a, b):
    return lax.dot_general(a, b, (((1,), (1,)), ((), ())), preferred_element_type=F32)


def _mixer_prompt_kernel(*refs):
    i = pl.program_id(0)

    @pl.when(i < NT_PROMPT)
    def _():
        _mixer_prompt_tile(*refs)

    @pl.when(i == NT_PROMPT)
    def _():
        xo_ref = refs[13]
        xo_ref[...] = jnp.zeros_like(xo_ref)


def _mixer_prompt_tile(x_ref, ng_ref, win_ref, bin_ref, lng_ref, lnb_ref, ws_ref, bs_ref,
                       sink_ref, bias_ref, ga_ref, gb_ref, wout_ref,
                       xo_ref, ks_ref, vs_ref, kprev_ref, vprev_ref):
    i = pl.program_id(0)

    @pl.when(i == 0)
    def _():
        kprev_ref[...] = jnp.zeros_like(kprev_ref)
        vprev_ref[...] = jnp.zeros_like(vprev_ref)

    x = x_ref[...]
    h = _rms(x, ng_ref[...]).astype(BF16)
    z = jnp.dot(h, win_ref[...], preferred_element_type=F32) + bin_ref[...]

    u = _gelu(z[:, :GMLP_WIDTH])
    v_heads = _gmlp_v(z, lng_ref, lnb_ref)
    row = lax.broadcasted_iota(jnp.int32, (CHUNK, CHUNK), 0)
    col = lax.broadcasted_iota(jnp.int32, (CHUNK, CHUNK), 1)
    a_cols = []
    for hh in range(GMLP_HEADS):
        w = jnp.where(row >= col, ws_ref[hh], 0.0).astype(BF16)
        v_cat = jnp.concatenate(
            [v_heads[hh][c * CHUNK:(c + 1) * CHUNK] for c in range(BLOCKS_PER_TILE)], axis=1)
        mixed = jnp.dot(w, v_cat.astype(BF16), preferred_element_type=F32)
        bias = bs_ref[:, hh * LANES:(hh + 1) * LANES]
        a_cols.append(jnp.concatenate(
            [mixed[:, c * LANES:(c + 1) * LANES] + bias for c in range(BLOCKS_PER_TILE)], axis=0))
    a = u * jnp.concatenate(a_cols, axis=1)
    a_n = _rms(a, ga_ref[...]).astype(BF16)

    q = z[:, Q0:K0] * (HEAD_DIM ** -0.5)
    k = z[:, K0:V0]
    val = z[:, V0:]
    k_bf = k.astype(BF16)
    lane = lax.broadcasted_iota(jnp.int32, (1, LANES), 1)
    low = lane < HEAD_DIM
    first = jnp.where(i % TILES_PER_SEQ == 0, 1, 0)
    c_i = lax.broadcasted_iota(jnp.int32, (WINDOW, WINDOW), 0)
    q_i = lax.broadcasted_iota(jnp.int32, (WINDOW, WINDOW), 1)
    from_prev = jnp.concatenate([c_i > q_i] * GQA_GROUP, axis=1)
    g_lane = lax.broadcasted_iota(jnp.int32, (1, GQA_GROUP * WINDOW), 1) // WINDOW
    top = lax.broadcasted_iota(jnp.int32, (KV_WIDTH, 1), 0) < HEAD_DIM
    one = jnp.ones((), BF16)
    vt_bf = val.T.astype(BF16)
    vt_all = jnp.concatenate([vprev_ref[...], vt_bf], axis=1)
    vt_aug = [jnp.where(top, vt_all, one), jnp.where(top, one, vt_all)]
    k_all = jnp.concatenate([kprev_ref[...], k_bf], axis=0)

    b_rows = []
    for c in range(BLOCKS_PER_TILE):
        rows = slice(c * WINDOW, (c + 1) * WINDOW)
        band = slice(c * WINDOW, (c + 2) * WINDOW)
        o_t = []
        for kv in range(KV_HEADS):
            keep = low if kv == 0 else jnp.logical_not(low)
            qm = jnp.concatenate(
                [jnp.where(keep, q[rows, g * LANES:(g + 1) * LANES], 0.0) for g in range(GQA_GROUP)],
                axis=0).astype(BF16)
            sink = jnp.full((1, GQA_GROUP * WINDOW), sink_ref[kv * GQA_GROUP], F32)
            for g in range(1, GQA_GROUP):
                sink = jnp.where(g_lane == g, sink_ref[kv * GQA_GROUP + g], sink)
            s2 = _dot_nt(k_all[band], qm)
            bias = bias_ref[first, kv] if c == 0 else bias_ref[0, kv]
            s = jnp.where(from_prev, s2[:WINDOW], s2[WINDOW:]) + bias
            m = jnp.maximum(jnp.max(s, axis=0, keepdims=True), sink)
            p = jnp.exp(s - m)
            p2 = jnp.concatenate([jnp.where(from_prev, p, 0.0), jnp.where(from_prev, 0.0, p)],
                                 axis=0).astype(BF16)
            o = jnp.dot(vt_aug[kv][:, band], p2, preferred_element_type=F32)
            own, ones_row = (o[:HEAD_DIM], o[HEAD_DIM:HEAD_DIM + 1]) if kv == 0 else (o[HEAD_DIM:], o[0:1])
            o_t.append(own / (ones_row + jnp.exp(sink - m)))
        bt = jnp.concatenate(o_t, axis=0)
        b_rows.append(jnp.concatenate(
            [bt[:, g * WINDOW:(g + 1) * WINDOW].T for g in range(GQA_GROUP)], axis=1))
    b = jnp.concatenate(b_rows, axis=0)
    b_n = _rms(b, gb_ref[...]).astype(BF16)

    merged = jnp.concatenate([a_n, b_n], axis=1)
    xo_ref[...] = x + jnp.dot(merged, wout_ref[...], preferred_element_type=F32)

    kprev_ref[...] = k_bf[TM - WINDOW:]
    vprev_ref[...] = vt_bf[:, TM - WINDOW:]
    ks_ref[0] = k[TM - WINDOW:]
    vs_ref[0] = val[TM - WINDOW:]


def _mixer_prompt(x, lw):
    const = lambda *shape: pl.BlockSpec(shape, lambda i: (0,) * len(shape))
    return pl.pallas_call(
        _mixer_prompt_kernel,
        grid=(NT_ALL,),
        in_specs=[
            pl.BlockSpec((TM, D_MODEL), lambda i: (jnp.minimum(i, NT_PROMPT - 1), 0)),
            const(1, D_MODEL), const(D_MODEL, IN_WIDTH), const(1, IN_WIDTH),
            const(1, GMLP_WIDTH), const(1, GMLP_WIDTH),
            const(GMLP_HEADS, CHUNK, CHUNK), const(CHUNK, GMLP_WIDTH),
            pl.BlockSpec(memory_space=pltpu.SMEM),
            const(2, KV_HEADS, WINDOW, GQA_GROUP * WINDOW),
            const(1, GMLP_WIDTH), const(1, ATTN_WIDTH), const(D_MODEL, D_MODEL),
        ],
        out_specs=[
            pl.BlockSpec((TM, D_MODEL), lambda i: (i, 0)),
            pl.BlockSpec((1, WINDOW, KV_WIDTH), lambda i: (jnp.minimum(i // TILES_PER_SEQ, BATCH - 1), 0, 0)),
            pl.BlockSpec((1, WINDOW, KV_WIDTH), lambda i: (jnp.minimum(i // TILES_PER_SEQ, BATCH - 1), 0, 0)),
        ],
        out_shape=[
            jax.ShapeDtypeStruct((T_ALL, D_MODEL), F32),
            jax.ShapeDtypeStruct((BATCH, WINDOW, KV_WIDTH), F32),
            jax.ShapeDtypeStruct((BATCH, WINDOW, KV_WIDTH), F32),
        ],
        scratch_shapes=[pltpu.VMEM((WINDOW, KV_WIDTH), BF16), pltpu.VMEM((WINDOW, KV_WIDTH), BF16)],
        compiler_params=pltpu.CompilerParams(
            dimension_semantics=("arbitrary",), vmem_limit_bytes=VMEM_LIMIT),
        name="mixer_prompt",
    )(x, lw["norm_g"], lw["w_in"], lw["b_in"], lw["ln_g"], lw["ln_b"], lw["w_s"], lw["b_s_prompt"],
      lw["sinks"], lw["bias_prompt"], lw["g_a"], lw["g_b"], lw["w_out"])


def _mixer_sample_kernel(x_ref, xalias_ref, ck_ref, cv_ref, ng_ref, win_ref, bin_ref, lng_ref, lnb_ref,
                         wsd_ref, bsd_ref, sinkcol_ref, bias_ref, ga_ref, gb_ref, wout_ref,
                         xo_ref, ko_ref, vo_ref, cvo_ref,
                         q_s, k_s, v_s, a_s, b_s, kfull, vfull):
    del xalias_ref
    j = pl.program_id(0)

    @pl.when(j == 0)
    def _():
        h = _rms(x_ref[...], ng_ref[...]).astype(BF16)
        z = jnp.dot(h, win_ref[...], preferred_element_type=F32) + bin_ref[...]
        u = _gelu(z[:, :GMLP_WIDTH])
        v = jnp.concatenate(_gmlp_v(z, lng_ref, lnb_ref), axis=1)
        cvo_ref[...] = v
        t_row = lax.broadcasted_iota(jnp.int32, (8, 1), 0) % DEC_SEQ

        def tile_rows(pat):
            return jnp.broadcast_to(pat[None], (TM // 8,) + pat.shape).reshape(TM, pat.shape[-1])

        mixed = tile_rows(bsd_ref[...])
        for d in range(DEC_SEQ):
            vd = v if d == 0 else pltpu.roll(v, d, 0)
            mixed = mixed + tile_rows(jnp.where(t_row >= d, wsd_ref[d], 0.0)) * vd
        a_s[...] = _rms(u * mixed, ga_ref[...]).astype(BF16)
        q_s[...] = z[:, Q0:K0] * (HEAD_DIM ** -0.5)
        k_s[...] = z[:, K0:V0]
        v_s[...] = z[:, V0:]
        kfull[:, CACHE_W + DEC_SEQ:, :] = jnp.zeros((SEQ_GROUP, KEYS_PAD - CACHE_W - DEC_SEQ, KV_WIDTH), F32)
        vfull[:, CACHE_W + DEC_SEQ:, :] = jnp.zeros((SEQ_GROUP, KEYS_PAD - CACHE_W - DEC_SEQ, KV_WIDTH), F32)

    base = pl.multiple_of(j * (SEQ_GROUP * DEC_SEQ), SEQ_GROUP * DEC_SEQ)
    kfull[:, :CACHE_W, :] = ck_ref[0]
    vfull[:, :CACHE_W, :] = cv_ref[0]
    k_new = k_s[pl.ds(base, SEQ_GROUP * DEC_SEQ), :]
    v_new = v_s[pl.ds(base, SEQ_GROUP * DEC_SEQ), :]
    for s in range(SEQ_GROUP):
        kfull[s, CACHE_W:CACHE_W + DEC_SEQ, :] = k_new[s * DEC_SEQ:(s + 1) * DEC_SEQ]
        vfull[s, CACHE_W:CACHE_W + DEC_SEQ, :] = v_new[s * DEC_SEQ:(s + 1) * DEC_SEQ]
    ko_ref[0] = kfull[:, DEC_SEQ:DEC_SEQ + CACHE_W, :]
    vo_ref[0] = vfull[:, DEC_SEQ:DEC_SEQ + CACHE_W, :]

    lane = lax.broadcasted_iota(jnp.int32, (1, LANES), 1)
    low = lane < HEAD_DIM
    pair_row = lax.broadcasted_iota(jnp.int32, (8, 1), 0) // DEC_SEQ
    sink = sinkcol_ref[...]

    def pair_body(m, carry):
        r0 = pl.multiple_of(base + m * 8, 8)
        q8 = q_s[pl.ds(r0, 8), :]
        qm = jnp.concatenate(
            [jnp.where(low if kv == 0 else jnp.logical_not(low), q8[:, g * LANES:(g + 1) * LANES], 0.0)
             for kv in range(KV_HEADS) for g in range(GQA_GROUP)], axis=0).astype(BF16)
        b8 = [jnp.zeros((8, LANES), F32) for _ in range(GQA_GROUP)]
        for s in range(2):
            kn = kfull[2 * m + s].astype(BF16)
            vn = vfull[2 * m + s].astype(BF16)
            p, inv = _softmax_sink_unnorm(_dot_nt(qm, kn) + bias_ref[s], sink)
            o = jnp.dot(p.astype(BF16), vn, preferred_element_type=F32) * inv
            for g in range(GQA_GROUP):
                og = jnp.where(low, o[g * 8:(g + 1) * 8],
                               o[(GQA_GROUP + g) * 8:(GQA_GROUP + g + 1) * 8])
                b8[g] = jnp.where(pair_row == s, og, b8[g])
        b_s[pl.ds(r0, 8), :] = jnp.concatenate(b8, axis=1)
        return carry

    lax.fori_loop(0, SEQ_GROUP // 2, pair_body, 0)

    @pl.when(j == N_SEQ_GROUPS - 1)
    def _():
        b_n = _rms(b_s[...], gb_ref[...]).astype(BF16)
        merged = jnp.concatenate([a_s[...], b_n], axis=1)
        xo_ref[...] = x_ref[...] + jnp.dot(merged, wout_ref[...], preferred_element_type=F32)


def _mixer_sample(x, x_block, x_buf, cache_k, cache_v, layer, lw):
    const = lambda *shape: pl.BlockSpec(shape, lambda j: (0,) * len(shape))
    cache_spec = pl.BlockSpec((1, SEQ_GROUP, CACHE_W, KV_WIDTH), lambda j: (layer, j, 0, 0))
    state_spec = pl.BlockSpec((1, SEQ_GROUP, CACHE_W, KV_WIDTH), lambda j: (j, 0, 0, 0))
    state_shape = jax.ShapeDtypeStruct((N_SEQ_GROUPS, SEQ_GROUP, CACHE_W, KV_WIDTH), F32)
    return pl.pallas_call(
        _mixer_sample_kernel,
        grid=(N_SEQ_GROUPS,),
        in_specs=[
            pl.BlockSpec((TM, D_MODEL), lambda j: (x_block, 0)),
            pl.BlockSpec(memory_space=pl.ANY),
            cache_spec, cache_spec,
            const(1, D_MODEL), const(D_MODEL, IN_WIDTH), const(1, IN_WIDTH),
            const(1, GMLP_WIDTH), const(1, GMLP_WIDTH),
            const(DEC_SEQ, 8, GMLP_WIDTH), const(8, GMLP_WIDTH),
            const(2 * GQA_GROUP * 8, 1), const(2, 2 * GQA_GROUP * 8, KEYS_PAD),
            const(1, GMLP_WIDTH), const(1, ATTN_WIDTH), const(D_MODEL, D_MODEL),
        ],
        out_specs=[
            pl.BlockSpec((TM, D_MODEL), lambda j: (NT_PROMPT, 0)),
            state_spec, state_spec,
            const(TM, GMLP_WIDTH),
        ],
        out_shape=[
            jax.ShapeDtypeStruct((T_ALL, D_MODEL), F32),
            state_shape, state_shape,
            jax.ShapeDtypeStruct((TM, GMLP_WIDTH), F32),
        ],
        scratch_shapes=[
            pltpu.VMEM((TM, ATTN_WIDTH), F32), pltpu.VMEM((TM, KV_WIDTH), F32),
            pltpu.VMEM((TM, KV_WIDTH), F32), pltpu.VMEM((TM, GMLP_WIDTH), BF16),
            pltpu.VMEM((TM, ATTN_WIDTH), F32),
            pltpu.VMEM((SEQ_GROUP, KEYS_PAD, KV_WIDTH), F32),
            pltpu.VMEM((SEQ_GROUP, KEYS_PAD, KV_WIDTH), F32),
        ],
        input_output_aliases={1: 0},
        compiler_params=pltpu.CompilerParams(
            dimension_semantics=("arbitrary",), vmem_limit_bytes=VMEM_LIMIT),
        name="mixer_sample",
    )(x, x_buf, cache_k, cache_v, lw["norm_g"], lw["w_in"], lw["b_in"], lw["ln_g"], lw["ln_b"],
      lw["w_s_sample"], lw["b_s_sample"], lw["sink_col"], lw["bias_sample"],
      lw["g_a"], lw["g_b"], lw["w_out"])


def _swiglu_kernel(x_ref, ng_ref, wg_ref, wu_ref, wd_ref, xo_ref):
    x = x_ref[...]
    h = _rms(x, ng_ref[...]).astype(BF16)
    gate = jnp.dot(h, wg_ref[...], preferred_element_type=F32)
    up = jnp.dot(h, wu_ref[...], preferred_element_type=F32)
    act = (jax.nn.silu(gate) * up).astype(BF16)
    xo_ref[...] = x + jnp.dot(act, wd_ref[...], preferred_element_type=F32)


def _swiglu(x, norm_g, wg, wu, wd):
    const = lambda *shape: pl.BlockSpec(shape, lambda i: (0,) * len(shape),
                                        pipeline_mode=pl.Buffered(1))
    return pl.pallas_call(
        _swiglu_kernel,
        grid=(NT_ALL,),
        in_specs=[
            pl.BlockSpec((TM, D_MODEL), lambda i: (i, 0)),
            const(1, D_MODEL), const(D_MODEL, D_FF), const(D_MODEL, D_FF), const(D_FF, D_MODEL),
        ],
        out_specs=pl.BlockSpec((TM, D_MODEL), lambda i: (i, 0)),
        out_shape=jax.ShapeDtypeStruct((T_ALL, D_MODEL), F32),
        compiler_params=pltpu.CompilerParams(
            dimension_semantics=("arbitrary",), vmem_limit_bytes=VMEM_LIMIT),
        name="swiglu",
    )(x, norm_g, wg, wu, wd)


def _moe_sort_kernel(x_ref, ng_ref, wrh_ref, wrl_ref, hs_ref, meta_ref, off_ref, cnt_ref):
    n = SORT_TILES_PER_STEP
    logits, routes = {}, {}
    for step in range(n + 2):
        if step < n:
            logits[step] = _sort_logits(x_ref[step * TM:(step + 1) * TM, :], ng_ref, wrh_ref, wrl_ref)
        j = step - 1
        if 0 <= j < n:
            h_hi, lg = logits.pop(j)
            meta, off, cnt_pad = _sort_route(lg)
            meta_ref[j * TM:(j + 1) * TM, :] = meta
            off_ref[j] = off.astype(jnp.int32)
            cnt_ref[j] = cnt_pad.astype(jnp.int32)
            routes[j] = (h_hi, meta)
        j = step - 2
        if 0 <= j < n:
            hs_ref[j * SORT_ROWS:(j + 1) * SORT_ROWS, :] = _sort_gather(*routes.pop(j))


def _sort_logits(x, ng_ref, wrh_ref, wrl_ref):
    h = _rms(x, ng_ref[...])
    h_hi = h.astype(BF16)
    h_lo = (h - h_hi.astype(F32)).astype(BF16)
    logits = (jnp.dot(h_hi, wrh_ref[...], preferred_element_type=F32)
              + jnp.dot(h_lo, wrh_ref[...], preferred_element_type=F32)
              + jnp.dot(h_hi, wrl_ref[...], preferred_element_type=F32))
    return h_hi, logits


def _sort_route(logits):
    lane = lax.broadcasted_iota(jnp.int32, (1, LANES), 1)
    logits = jnp.where(lane < N_EXPERTS, logits, NEG_INF)
    m1 = jnp.max(logits, axis=-1, keepdims=True)
    i1 = jnp.min(jnp.where(logits == m1, lane, LANES), axis=-1, keepdims=True)
    rest = jnp.where(lane == i1, NEG_INF, logits)
    m2 = jnp.max(rest, axis=-1, keepdims=True)
    i2 = jnp.min(jnp.where(rest == m2, lane, LANES), axis=-1, keepdims=True)
    e2 = jnp.exp(m2 - m1)
    den = 1.0 + e2
    sel0 = lane == i1
    sel1 = lane == i2
    routed = jnp.where(jnp.logical_or(sel0, sel1), 1.0, 0.0)
    r_i = lax.broadcasted_iota(jnp.int32, (TM, TM), 0)
    c_i = lax.broadcasted_iota(jnp.int32, (TM, TM), 1)
    lower = jnp.where(r_i > c_i, 1.0, 0.0).astype(BF16)
    rank = jnp.dot(lower, routed.astype(BF16), preferred_element_type=F32)
    cnt = jnp.broadcast_to(jnp.sum(routed, axis=0, keepdims=True), (8, LANES))
    cnt_pad = jnp.floor((cnt + (PIECE - 1)) * (1.0 / PIECE)) * PIECE
    incl = cnt_pad
    for sh in (1, 2, 4):
        incl = incl + jnp.where(lane >= sh, pltpu.roll(incl, sh, 1), 0.0)
    off = incl - cnt_pad
    pos = off[0:1] + rank
    pos0 = jnp.sum(jnp.where(sel0, pos, 0.0), axis=-1, keepdims=True)
    pos1 = jnp.sum(jnp.where(sel1, pos, 0.0), axis=-1, keepdims=True)
    meta = jnp.where(lane == 0, pos0, jnp.where(lane == 1, pos1,
                     jnp.where(lane == 2, 1.0 / den, jnp.where(lane == 3, e2 / den, 0.0))))
    return meta, off, cnt_pad


def _sort_gather(h_hi, meta):
    meta_t = meta.T
    p_row = lax.broadcasted_iota(jnp.int32, (SORT_ROWS, 1), 0).astype(F32)
    hit = jnp.logical_or(p_row == meta_t[0:1], p_row == meta_t[1:2])
    gather = jnp.where(hit, 1.0, 0.0).astype(BF16)
    return jnp.dot(gather, h_hi, preferred_element_type=F32).astype(BF16)


def _moe_sort(x, norm_g, wr_hi, wr_lo):
    return pl.pallas_call(
        _moe_sort_kernel,
        grid=(NT_ALL // SORT_TILES_PER_STEP,),
        in_specs=[
            pl.BlockSpec((SORT_TILES_PER_STEP * TM, D_MODEL), lambda i: (i, 0)),
            pl.BlockSpec((1, D_MODEL), lambda i: (0, 0)),
            pl.BlockSpec((D_MODEL, LANES), lambda i: (0, 0)),
            pl.BlockSpec((D_MODEL, LANES), lambda i: (0, 0)),
        ],
        out_specs=[
            pl.BlockSpec((SORT_TILES_PER_STEP * SORT_ROWS, D_MODEL), lambda i: (i, 0)),
            pl.BlockSpec((SORT_TILES_PER_STEP * TM, LANES), lambda i: (i, 0)),
            pl.BlockSpec((SORT_TILES_PER_STEP, 8, LANES), lambda i: (i, 0, 0)),
            pl.BlockSpec((SORT_TILES_PER_STEP, 8, LANES), lambda i: (i, 0, 0)),
        ],
        out_shape=[
            jax.ShapeDtypeStruct((NT_ALL * SORT_ROWS, D_MODEL), BF16),
            jax.ShapeDtypeStruct((T_ALL, LANES), F32),
            jax.ShapeDtypeStruct((NT_ALL, 8, LANES), jnp.int32),
            jax.ShapeDtypeStruct((NT_ALL, 8, LANES), jnp.int32),
        ],
        compiler_params=pltpu.CompilerParams(
            dimension_semantics=("arbitrary",), vmem_limit_bytes=VMEM_LIMIT),
        name="moe_sort",
    )(x, norm_g, wr_hi, wr_lo)


def _moe_experts_kernel(off_ref, cnt_ref, hs_hbm, wgu_ref, wd_ref, ys_hbm, buf, lhs, sem):
    s = pl.program_id(0)
    e = pl.program_id(1)
    row0 = pl.multiple_of(s * SUPER_ROWS, PIECE)

    @pl.when(e == 0)
    def _():
        cp = pltpu.make_async_copy(hs_hbm.at[pl.ds(row0, SUPER_ROWS)], buf.at[pl.ds(0, SUPER_ROWS)], sem)
        cp.start()
        cp.wait()

    starts, cums = [], [0]
    for i in range(TILES_PER_SUPER):
        idx = (s * TILES_PER_SUPER + i) * N_EXPERTS + e
        starts.append(i * SORT_ROWS + off_ref[idx])
        cums.append(cums[-1] + cnt_ref[idx] // PIECE)
    n_pieces = cums[-1]

    def piece_rows(q):
        src, dst = [], []
        for j in range(PIECES_PER_CHUNK):
            p = q * PIECES_PER_CHUNK + j
            pc = jnp.minimum(p, n_pieces - 1)
            row = starts[0] + PIECE * pc
            for i in range(1, TILES_PER_SUPER):
                row = jnp.where(pc >= cums[i], starts[i] + PIECE * (pc - cums[i]), row)
            src.append(row)
            dst.append(jnp.where(p < n_pieces, row, SUPER_ROWS))
        return tuple(src), tuple(dst)

    def chunk_body(q, carry):
        src, dst = carry
        for j in range(PIECES_PER_CHUNK):
            lhs[j * PIECE:(j + 1) * PIECE, :] = buf[pl.ds(pl.multiple_of(src[j], PIECE), PIECE), :]
        xs = lhs[...]
        gate_up = jnp.dot(xs, wgu_ref[0], preferred_element_type=F32)
        nxt = piece_rows(q + 1)
        act = (jax.nn.silu(gate_up[:, :D_FF_EXPERT]) * gate_up[:, D_FF_EXPERT:]).astype(BF16)
        y = jnp.dot(act, wd_ref[0], preferred_element_type=F32).astype(BF16)
        for j in range(PIECES_PER_CHUNK):
            buf[pl.ds(pl.multiple_of(dst[j], PIECE), PIECE), :] = y[j * PIECE:(j + 1) * PIECE]
        return nxt

    lax.fori_loop(0, (n_pieces + PIECES_PER_CHUNK - 1) // PIECES_PER_CHUNK, chunk_body, piece_rows(0))

    @pl.when(e == N_EXPERTS - 1)
    def _():
        cp = pltpu.make_async_copy(buf.at[pl.ds(0, SUPER_ROWS)], ys_hbm.at[pl.ds(row0, SUPER_ROWS)], sem)
        cp.start()
        cp.wait()


def _moe_experts(off, cnt, hs, wgu, wd):
    grid_spec = pltpu.PrefetchScalarGridSpec(
        num_scalar_prefetch=2,
        grid=(N_SUPER, N_EXPERTS),
        in_specs=[
            pl.BlockSpec(memory_space=pl.ANY),
            pl.BlockSpec((1, D_MODEL, 2 * D_FF_EXPERT), lambda s, e, *_: (e, 0, 0)),
            pl.BlockSpec((1, D_FF_EXPERT, D_MODEL), lambda s, e, *_: (e, 0, 0)),
        ],
        out_specs=pl.BlockSpec(memory_space=pl.ANY),
        scratch_shapes=[pltpu.VMEM((SUPER_ROWS + PIECE, D_MODEL), BF16),
                        pltpu.VMEM((CHUNK_ROWS, D_MODEL), BF16),
                        pltpu.SemaphoreType.DMA],
    )
    return pl.pallas_call(
        _moe_experts_kernel,
        grid_spec=grid_spec,
        out_shape=jax.ShapeDtypeStruct((NT_ALL * SORT_ROWS, D_MODEL), BF16),
        compiler_params=pltpu.CompilerParams(
            dimension_semantics=("arbitrary", "arbitrary"), vmem_limit_bytes=VMEM_LIMIT),
        name="moe_experts",
    )(off, cnt, hs, wgu, wd)


def _unsorted_residual(x_ref, ys_ref, meta_ref):
    meta = meta_ref[...]
    p_col = lax.broadcasted_iota(jnp.int32, (1, SORT_ROWS), 1).astype(F32)
    scatter = (jnp.where(p_col == meta[:, 0:1], meta[:, 2:3], 0.0)
               + jnp.where(p_col == meta[:, 1:2], meta[:, 3:4], 0.0)).astype(BF16)
    return x_ref[...] + jnp.dot(scatter, ys_ref[...], preferred_element_type=F32)


def _moe_unsort_kernel(x_ref, ys_ref, meta_ref, xo_ref):
    xo_ref[...] = _unsorted_residual(x_ref, ys_ref, meta_ref)


def _moe_unsort_final_kernel(x_ref, ys_ref, meta_ref, g_ref, yp_ref, ysm_ref):
    i = pl.program_id(0)
    y = _rms(_unsorted_residual(x_ref, ys_ref, meta_ref), g_ref[...])

    @pl.when(i < NT_PROMPT)
    def _():
        yp_ref[...] = y

    @pl.when(i == NT_PROMPT)
    def _():
        ysm_ref[...] = y


def _moe_unsort(x, ys, meta, final_g=None):
    in_specs = [
        pl.BlockSpec((TM, D_MODEL), lambda i: (i, 0)),
        pl.BlockSpec((SORT_ROWS, D_MODEL), lambda i: (i, 0)),
        pl.BlockSpec((TM, LANES), lambda i: (i, 0)),
    ]
    params = pltpu.CompilerParams(dimension_semantics=("arbitrary",), vmem_limit_bytes=VMEM_LIMIT)
    if final_g is None:
        return pl.pallas_call(
            _moe_unsort_kernel,
            grid=(NT_ALL,),
            in_specs=in_specs,
            out_specs=pl.BlockSpec((TM, D_MODEL), lambda i: (i, 0)),
            out_shape=jax.ShapeDtypeStruct((T_ALL, D_MODEL), F32),
            compiler_params=params,
            name="moe_unsort",
        )(x, ys, meta)
    return pl.pallas_call(
        _moe_unsort_final_kernel,
        grid=(NT_ALL,),
        in_specs=in_specs + [pl.BlockSpec((1, D_MODEL), lambda i: (0, 0))],
        out_specs=[pl.BlockSpec((TM, D_MODEL), lambda i: (jnp.minimum(i, NT_PROMPT - 1), 0)),
                   pl.BlockSpec((TM, D_MODEL), lambda i: (0, 0))],
        out_shape=[jax.ShapeDtypeStruct((T_PROMPT, D_MODEL), F32),
                   jax.ShapeDtypeStruct((T_SAMPLE, D_MODEL), F32)],
        compiler_params=params,
        name="moe_unsort_final",
    )(x, ys, meta, final_g)


def _moe(x, norm_g, wr_hi, wr_lo, wgu, wd, final_g=None):
    hs, meta, off, cnt = _moe_sort(x, norm_g, wr_hi, wr_lo)
    off = off[:, 0, :N_EXPERTS].reshape(-1)
    cnt = cnt[:, 0, :N_EXPERTS].reshape(-1)
    ys = _moe_experts(off, cnt, hs, wgu, wd)
    return _moe_unsort(x, ys, meta, final_g)


def kernel(x_prompt, x_sample, cache_win_k, cache_win_v, attn_norm_g, w_in, b_in, chunk_ln_g, chunk_ln_b, w_spatial, b_spatial, attn_sinks, mix_norm_a_g, mix_norm_b_g, w_out, ffn_norm_g, ffn_w_gate, ffn_w_up, ffn_w_down, router_w, expert_w_gate, expert_w_up, expert_w_down, final_norm_g):
    w_in_p = jnp.concatenate([w_in[..., :Q0], _g_major(w_in[..., Q0:K0], -1), w_in[..., K0:]], axis=-1).astype(BF16)
    b_in_p = jnp.concatenate([b_in[..., :Q0], _g_major(b_in[..., Q0:K0], -1), b_in[..., K0:]], axis=-1)
    w_out_p = jnp.concatenate([w_out[:, :GMLP_WIDTH], _g_major(w_out[:, GMLP_WIDTH:], 1)], axis=1).astype(BF16)
    g_b_p = _g_major(mix_norm_b_g, 1)
    bs_prompt = jnp.repeat(jnp.transpose(b_spatial, (0, 2, 1)), GMLP_HEAD_DIM, axis=2)
    bs_sample = jnp.tile(bs_prompt[:, :DEC_SEQ], (1, 2, 1))
    t_idx = np.arange(DEC_SEQ)
    ws_small = jnp.stack([w_spatial[:, :, t_idx, np.maximum(t_idx - d, 0)] for d in range(DEC_SEQ)], axis=1)
    ws_sample = jnp.tile(jnp.repeat(jnp.transpose(ws_small, (0, 1, 3, 2)), GMLP_HEAD_DIM, axis=3),
                         (1, 1, 2, 1))
    sink_col = jnp.repeat(attn_sinks, 8, axis=1)[..., None]
    bias_prompt = jnp.asarray(_prompt_bias())
    bias_sample = jnp.asarray(_sample_bias())
    wr_pad = jnp.pad(router_w, ((0, 0), (0, 0), (0, LANES - N_EXPERTS)))
    wr_hi = wr_pad.astype(BF16)
    wr_lo = (wr_pad - wr_hi.astype(F32)).astype(BF16)
    ffn_wg, ffn_wu, ffn_wd = ffn_w_gate.astype(BF16), ffn_w_up.astype(BF16), ffn_w_down.astype(BF16)
    ex_wgu = jnp.concatenate([expert_w_gate, expert_w_up], axis=-1).astype(BF16)
    ex_wd = expert_w_down.astype(BF16)

    ck = cache_win_k.reshape(DEPTH, DEC_BATCH, CACHE_W, KV_WIDTH)
    cv = cache_win_v.reshape(DEPTH, DEC_BATCH, CACHE_W, KV_WIDTH)

    x = None
    kp, vp, ksm, vsm, cvs = [], [], [], [], []
    for l in range(DEPTH):
        lw = dict(
            norm_g=attn_norm_g[l][None], w_in=w_in_p[l], b_in=b_in_p[l][None],
            ln_g=chunk_ln_g[l].reshape(1, GMLP_WIDTH), ln_b=chunk_ln_b[l].reshape(1, GMLP_WIDTH),
            w_s=w_spatial[l], b_s_prompt=bs_prompt[l], w_s_sample=ws_sample[l], b_s_sample=bs_sample[l],
            sinks=attn_sinks[l], sink_col=sink_col[l], bias_prompt=bias_prompt, bias_sample=bias_sample,
            g_a=mix_norm_a_g[l][None], g_b=g_b_p[l][None], w_out=w_out_p[l])
        if l == 0:
            x_new, k_p, v_p = _mixer_prompt(x_prompt.reshape(T_PROMPT, D_MODEL), lw)
            x_new, k_s, v_s, cv_s = _mixer_sample(x_sample.reshape(T_SAMPLE, D_MODEL), 0, x_new, ck, cv, l, lw)
        else:
            x_new, k_p, v_p = _mixer_prompt(x, lw)
            x_new, k_s, v_s, cv_s = _mixer_sample(x, NT_PROMPT, x_new, ck, cv, l, lw)
        kp.append(k_p)
        vp.append(v_p)
        ksm.append(k_s)
        vsm.append(v_s)
        cvs.append(cv_s)
        i = l // 2
        if l % 2 == 0:
            x = _swiglu(x_new, ffn_norm_g[l][None], ffn_wg[i], ffn_wu[i], ffn_wd[i])
        else:
            final_g = final_norm_g[None] if l == DEPTH - 1 else None
            x = _moe(x_new, ffn_norm_g[l][None], wr_hi[i], wr_lo[i], ex_wgu[i], ex_wd[i], final_g)
    y_p, y_s = x
    win_p = (DEPTH, BATCH, WINDOW, KV_HEADS, HEAD_DIM)
    win_s = (DEPTH, DEC_BATCH, CACHE_W, KV_HEADS, HEAD_DIM)
    return (y_p.reshape(BATCH, SEQ, D_MODEL), y_s.reshape(DEC_BATCH, DEC_SEQ, D_MODEL),
            jnp.stack(kp).reshape(win_p), jnp.stack(vp).reshape(win_p),
            jnp.stack(ksm).reshape(win_s), jnp.stack(vsm).reshape(win_s),
            jnp.stack(cvs).reshape(DEPTH, DEC_BATCH, DEC_SEQ, GMLP_HEADS, GMLP_HEAD_DIM))
```

```python
import functools

import numpy as np
import jax
import jax.numpy as jnp
from jax import lax
from jax.experimental import pallas as pl
from jax.experimental.pallas import tpu as pltpu

D_MODEL = 1024
BATCH = 4
SEQ = 4096
DEPTH = 4
DEC_BATCH = 128
DEC_SEQ = 4
HEAD_DIM = 64
ATTN_HEADS = 8
KV_HEADS = 2
GQA_GROUP = ATTN_HEADS // KV_HEADS
ATTN_WIDTH = ATTN_HEADS * HEAD_DIM
KV_WIDTH = KV_HEADS * HEAD_DIM
WINDOW = 128
CHUNK = 128
GMLP_HEADS = 4
GMLP_HEAD_DIM = 128
GMLP_WIDTH = GMLP_HEADS * GMLP_HEAD_DIM
IN_WIDTH = 2 * GMLP_WIDTH + ATTN_WIDTH + 2 * KV_WIDTH
D_FF = 2816
N_EXPERTS = 8
D_FF_EXPERT = 1408
CACHE_W = 128
EPS = 1e-6

LANES = 128
TM = 512
T_PROMPT = BATCH * SEQ
T_SAMPLE = DEC_BATCH * DEC_SEQ
T_ALL = T_PROMPT + T_SAMPLE
NT_PROMPT = T_PROMPT // TM
NT_ALL = T_ALL // TM
TILES_PER_SEQ = SEQ // TM
BLOCKS_PER_TILE = TM // WINDOW
SEQ_GROUP = 16
N_SEQ_GROUPS = DEC_BATCH // SEQ_GROUP
KEYS_PAD = CACHE_W + 8
PIECE = 16
SORT_ROWS = 2 * TM + N_EXPERTS * PIECE
TILES_PER_SUPER = 11
N_SUPER = NT_ALL // TILES_PER_SUPER
SUPER_ROWS = TILES_PER_SUPER * SORT_ROWS
SORT_TILES_PER_STEP = 3
CHUNK_ROWS = 256
PIECES_PER_CHUNK = CHUNK_ROWS // PIECE
VMEM_LIMIT = 56 * 1024 * 1024

F32 = jnp.float32
BF16 = jnp.bfloat16
NEG_INF = float("-inf")

Q0 = 2 * GMLP_WIDTH
K0 = Q0 + ATTN_WIDTH
V0 = K0 + KV_WIDTH


def _g_major(a, axis):
    axis = axis % a.ndim
    shape = a.shape
    a = a.reshape(shape[:axis] + (KV_HEADS, GQA_GROUP, HEAD_DIM) + shape[axis + 1:])
    return jnp.swapaxes(a, axis, axis + 1).reshape(shape)


def _slope(kv, g):
    return 2.0 ** (-8.0 * (kv * GQA_GROUP + g + 1) / ATTN_HEADS)


def _prompt_bias():
    c = np.arange(WINDOW)[:, None]
    qi = np.arange(WINDOW)[None, :]
    dist = np.where(c > qi, qi + WINDOW - c, qi - c)
    out = np.zeros((2, KV_HEADS, WINDOW, GQA_GROUP * WINDOW), np.float32)
    for var in range(2):
        ok = ~((var == 1) & (c > qi))
        for kv in range(KV_HEADS):
            for g in range(GQA_GROUP):
                out[var, kv, :, g * WINDOW:(g + 1) * WINDOW] = np.where(ok, -_slope(kv, g) * dist, NEG_INF)
    return out


def _sample_bias():
    out = np.zeros((2, 2 * GQA_GROUP * 8, KEYS_PAD), np.float32)
    j = np.arange(KEYS_PAD)
    kpos = np.where(j < CACHE_W, j - CACHE_W, j - CACHE_W)
    for s in range(2):
        for kv in range(KV_HEADS):
            for g in range(GQA_GROUP):
                for sp in range(2):
                    for t in range(DEC_SEQ):
                        row = (kv * GQA_GROUP + g) * 8 + sp * DEC_SEQ + t
                        dist = t - kpos
                        ok = (dist >= 0) & (dist < WINDOW) & (j < CACHE_W + DEC_SEQ)
                        if sp == s:
                            out[s, row] = np.where(ok, -_slope(kv, g) * dist, NEG_INF)
    return out


def _rms(x, g):
    return x * lax.rsqrt(jnp.mean(x * x, axis=-1, keepdims=True) + EPS) * g


def _gelu(x):
    return 0.5 * x * (1.0 + lax.erf(x * (2.0 ** -0.5)))


def _gmlp_v(z_v, lng_ref, lnb_ref):
    out = []
    for hh in range(GMLP_HEADS):
        sl = slice(hh * LANES, (hh + 1) * LANES)
        vh = _gelu(z_v[:, sl])
        vc = vh - jnp.mean(vh, axis=-1, keepdims=True)
        y = vc * lax.rsqrt(jnp.mean(vc * vc, axis=-1, keepdims=True) + EPS)
        out.append(y * lng_ref[:, sl] + lnb_ref[:, sl])
    return out


def _softmax_sink_unnorm(s, sink):
    m = jnp.maximum(jnp.max(s, axis=-1, keepdims=True), sink)
    p = jnp.exp(s - m)
    den = jnp.sum(p, axis=-1, keepdims=True) + jnp.exp(sink - m)
    return p, 1.0 / den


def _dot_nt(a, b):
    return lax.dot_general(a, b, (((1,), (1,)), ((), ())), preferred_element_type=F32)


def _mixer_prompt_kernel(*refs):
    i = pl.program_id(0)

    @pl.when(i < NT_PROMPT)
    def _():
        _mixer_prompt_tile(*refs)

    @pl.when(i == NT_PROMPT)
    def _():
        xo_ref = refs[13]
        xo_ref[...] = jnp.zeros_like(xo_ref)


def _mixer_prompt_tile(x_ref, ng_ref, win_ref, bin_ref, lng_ref, lnb_ref, ws_ref, bs_ref,
                       sink_ref, bias_ref, ga_ref, gb_ref, wout_ref,
                       xo_ref, ks_ref, vs_ref, kprev_ref, vprev_ref):
    i = pl.program_id(0)

    @pl.when(i == 0)
    def _():
        kprev_ref[...] = jnp.zeros_like(kprev_ref)
        vprev_ref[...] = jnp.zeros_like(vprev_ref)

    x = x_ref[...]
    h = _rms(x, ng_ref[...]).astype(BF16)

    def in_proj(lo, hi):
        return jnp.dot(h, win_ref[:, lo:hi], preferred_element_type=F32) + bin_ref[:, lo:hi]

    z_att = in_proj(Q0, IN_WIDTH)
    q = (z_att[:, :ATTN_WIDTH] * (HEAD_DIM ** -0.5)).astype(BF16)
    k = z_att[:, ATTN_WIDTH:ATTN_WIDTH + KV_WIDTH]
    val = z_att[:, ATTN_WIDTH + KV_WIDTH:]
    k_bf = k.astype(BF16)
    lane = lax.broadcasted_iota(jnp.int32, (1, LANES), 1)
    low = lane < HEAD_DIM
    first = jnp.where(i % TILES_PER_SEQ == 0, 1, 0)
    c_i = lax.broadcasted_iota(jnp.int32, (WINDOW, WINDOW), 0)
    q_i = lax.broadcasted_iota(jnp.int32, (WINDOW, WINDOW), 1)
    from_prev = jnp.concatenate([c_i > q_i] * GQA_GROUP, axis=1)
    g_lane = lax.broadcasted_iota(jnp.int32, (1, GQA_GROUP * WINDOW), 1) // WINDOW
    top = lax.broadcasted_iota(jnp.int32, (KV_WIDTH, 1), 0) < HEAD_DIM
    one = jnp.ones((), BF16)
    vt_bf = val.T.astype(BF16)
    vt_all = jnp.concatenate([vprev_ref[...], vt_bf], axis=1)
    vt_aug = [jnp.where(top, vt_all, one), jnp.where(top, one, vt_all)]
    k_all = jnp.concatenate([kprev_ref[...], k_bf], axis=0)

    zero = jnp.zeros((), BF16)
    sinks = []
    for kv in range(KV_HEADS):
        sink = jnp.full((1, GQA_GROUP * WINDOW), sink_ref[kv * GQA_GROUP], F32)
        for g in range(1, GQA_GROUP):
            sink = jnp.where(g_lane == g, sink_ref[kv * GQA_GROUP + g], sink)
        sinks.append(sink)

    def attention_block(c):
        rows = slice(c * WINDOW, (c + 1) * WINDOW)
        band = slice(c * WINDOW, (c + 2) * WINDOW)
        o_t = []
        for kv in range(KV_HEADS):
            keep = low if kv == 0 else jnp.logical_not(low)
            qm = jnp.concatenate(
                [jnp.where(keep, q[rows, g * LANES:(g + 1) * LANES], zero) for g in range(GQA_GROUP)],
                axis=0)
            s2 = _dot_nt(k_all[band], qm)
            bias = bias_ref[first, kv] if c == 0 else bias_ref[0, kv]
            s = jnp.where(from_prev, s2[:WINDOW], s2[WINDOW:]) + bias
            m = jnp.maximum(jnp.max(s, axis=0, keepdims=True), sinks[kv])
            p = jnp.exp(s - m)
            p2 = jnp.concatenate([jnp.where(from_prev, p, 0.0), jnp.where(from_prev, 0.0, p)],
                                 axis=0).astype(BF16)
            o = jnp.dot(vt_aug[kv][:, band], p2, preferred_element_type=F32)
            own, ones_row = (o[:HEAD_DIM], o[HEAD_DIM:HEAD_DIM + 1]) if kv == 0 else (o[HEAD_DIM:], o[0:1])
            o_t.append(own / (ones_row + jnp.exp(sinks[kv] - m)))
        bt = jnp.concatenate(o_t, axis=0)
        return jnp.concatenate(
            [bt[:, g * WINDOW:(g + 1) * WINDOW].T for g in range(GQA_GROUP)], axis=1)

    z_u = in_proj(0, GMLP_WIDTH)
    b_rows = [attention_block(0)]
    z_v = in_proj(GMLP_WIDTH, Q0)
    b_rows.append(attention_block(1))
    u = _gelu(z_u)
    b_rows.append(attention_block(2))
    v_heads = _gmlp_v(z_v, lng_ref, lnb_ref)
    row = lax.broadcasted_iota(jnp.int32, (CHUNK, CHUNK), 0)
    col = lax.broadcasted_iota(jnp.int32, (CHUNK, CHUNK), 1)
    a_cols = []
    for hh in range(GMLP_HEADS):
        w = jnp.where(row >= col, ws_ref[hh], 0.0).astype(BF16)
        v_cat = jnp.concatenate(
            [v_heads[hh][c * CHUNK:(c + 1) * CHUNK] for c in range(BLOCKS_PER_TILE)], axis=1)
        mixed = jnp.dot(w, v_cat.astype(BF16), preferred_element_type=F32)
        bias = bs_ref[:, hh * LANES:(hh + 1) * LANES]
        a_cols.append(jnp.concatenate(
            [mixed[:, c * LANES:(c + 1) * LANES] + bias for c in range(BLOCKS_PER_TILE)], axis=0))
    a = u * jnp.concatenate(a_cols, axis=1)
    a_n = _rms(a, ga_ref[...]).astype(BF16)
    b_rows.append(attention_block(3))
    out = x + jnp.dot(a_n, wout_ref[:GMLP_WIDTH, :], preferred_element_type=F32)
    b_n = _rms(jnp.concatenate(b_rows, axis=0), gb_ref[...]).astype(BF16)
    xo_ref[...] = out + jnp.dot(b_n, wout_ref[GMLP_WIDTH:, :], preferred_element_type=F32)

    kprev_ref[...] = k_bf[TM - WINDOW:]
    vprev_ref[...] = vt_bf[:, TM - WINDOW:]
    ks_ref[0] = k[TM - WINDOW:]
    vs_ref[0] = val[TM - WINDOW:]


def _mixer_prompt(x, lw):
    const = lambda *shape: pl.BlockSpec(shape, lambda i: (0,) * len(shape))
    return pl.pallas_call(
        _mixer_prompt_kernel,
        grid=(NT_ALL,),
        in_specs=[
            pl.BlockSpec((TM, D_MODEL), lambda i: (jnp.minimum(i, NT_PROMPT - 1), 0)),
            const(1, D_MODEL), const(D_MODEL, IN_WIDTH), const(1, IN_WIDTH),
            const(1, GMLP_WIDTH), const(1, GMLP_WIDTH),
            const(GMLP_HEADS, CHUNK, CHUNK), const(CHUNK, GMLP_WIDTH),
            pl.BlockSpec(memory_space=pltpu.SMEM),
            const(2, KV_HEADS, WINDOW, GQA_GROUP * WINDOW),
            const(1, GMLP_WIDTH), const(1, ATTN_WIDTH), const(D_MODEL, D_MODEL),
        ],
        out_specs=[
            pl.BlockSpec((TM, D_MODEL), lambda i: (i, 0)),
            pl.BlockSpec((1, WINDOW, KV_WIDTH), lambda i: (jnp.minimum(i // TILES_PER_SEQ, BATCH - 1), 0, 0)),
            pl.BlockSpec((1, WINDOW, KV_WIDTH), lambda i: (jnp.minimum(i // TILES_PER_SEQ, BATCH - 1), 0, 0)),
        ],
        out_shape=[
            jax.ShapeDtypeStruct((T_ALL, D_MODEL), F32),
            jax.ShapeDtypeStruct((BATCH, WINDOW, KV_WIDTH), F32),
            jax.ShapeDtypeStruct((BATCH, WINDOW, KV_WIDTH), F32),
        ],
        scratch_shapes=[pltpu.VMEM((WINDOW, KV_WIDTH), BF16), pltpu.VMEM((WINDOW, KV_WIDTH), BF16)],
        compiler_params=pltpu.CompilerParams(
            dimension_semantics=("arbitrary",), vmem_limit_bytes=VMEM_LIMIT),
        name="mixer_prompt",
    )(x, lw["norm_g"], lw["w_in"], lw["b_in"], lw["ln_g"], lw["ln_b"], lw["w_s"], lw["b_s_prompt"],
      lw["sinks"], lw["bias_prompt"], lw["g_a"], lw["g_b"], lw["w_out"])


def _mixer_sample_kernel(x_ref, xalias_ref, ck_ref, cv_ref, ng_ref, win_ref, bin_ref, lng_ref, lnb_ref,
                         wsd_ref, bsd_ref, sinkcol_ref, bias_ref, ga_ref, gb_ref, wout_ref,
                         xo_ref, ko_ref, vo_ref, cvo_ref,
                         q_s, k_s, v_s, a_s, b_s, kfull, vfull):
    del xalias_ref
    j = pl.program_id(0)

    @pl.when(j == 0)
    def _():
        h = _rms(x_ref[...], ng_ref[...]).astype(BF16)
        z = jnp.dot(h, win_ref[...], preferred_element_type=F32) + bin_ref[...]
        u = _gelu(z[:, :GMLP_WIDTH])
        v = jnp.concatenate(_gmlp_v(z[:, GMLP_WIDTH:Q0], lng_ref, lnb_ref), axis=1)
        cvo_ref[...] = v
        t_row = lax.broadcasted_iota(jnp.int32, (8, 1), 0) % DEC_SEQ

        def tile_rows(pat):
            return jnp.broadcast_to(pat[None], (TM // 8,) + pat.shape).reshape(TM, pat.shape[-1])

        mixed = tile_rows(bsd_ref[...])
        for d in range(DEC_SEQ):
            vd = v if d == 0 else pltpu.roll(v, d, 0)
            mixed = mixed + tile_rows(jnp.where(t_row >= d, wsd_ref[d], 0.0)) * vd
        a_s[...] = _rms(u * mixed, ga_ref[...]).astype(BF16)
        q_s[...] = z[:, Q0:K0] * (HEAD_DIM ** -0.5)
        k_s[...] = z[:, K0:V0]
        v_s[...] = z[:, V0:]
        kfull[:, CACHE_W + DEC_SEQ:, :] = jnp.zeros((SEQ_GROUP, KEYS_PAD - CACHE_W - DEC_SEQ, KV_WIDTH), F32)
        vfull[:, CACHE_W + DEC_SEQ:, :] = jnp.zeros((SEQ_GROUP, KEYS_PAD - CACHE_W - DEC_SEQ, KV_WIDTH), F32)

    base = pl.multiple_of(j * (SEQ_GROUP * DEC_SEQ), SEQ_GROUP * DEC_SEQ)
    kfull[:, :CACHE_W, :] = ck_ref[0]
    vfull[:, :CACHE_W, :] = cv_ref[0]
    k_new = k_s[pl.ds(base, SEQ_GROUP * DEC_SEQ), :]
    v_new = v_s[pl.ds(base, SEQ_GROUP * DEC_SEQ), :]
    for s in range(SEQ_GROUP):
        kfull[s, CACHE_W:CACHE_W + DEC_SEQ, :] = k_new[s * DEC_SEQ:(s + 1) * DEC_SEQ]
        vfull[s, CACHE_W:CACHE_W + DEC_SEQ, :] = v_new[s * DEC_SEQ:(s + 1) * DEC_SEQ]
    ko_ref[0] = kfull[:, DEC_SEQ:DEC_SEQ + CACHE_W, :]
    vo_ref[0] = vfull[:, DEC_SEQ:DEC_SEQ + CACHE_W, :]

    lane = lax.broadcasted_iota(jnp.int32, (1, LANES), 1)
    low = lane < HEAD_DIM
    pair_row = lax.broadcasted_iota(jnp.int32, (8, 1), 0) // DEC_SEQ
    sink = sinkcol_ref[...]

    def pair_body(m, carry):
        r0 = pl.multiple_of(base + m * 8, 8)
        q8 = q_s[pl.ds(r0, 8), :]
        qm = jnp.concatenate(
            [jnp.where(low if kv == 0 else jnp.logical_not(low), q8[:, g * LANES:(g + 1) * LANES], 0.0)
             for kv in range(KV_HEADS) for g in range(GQA_GROUP)], axis=0).astype(BF16)
        b8 = [jnp.zeros((8, LANES), F32) for _ in range(GQA_GROUP)]
        for s in range(2):
            kn = kfull[2 * m + s].astype(BF16)
            vn = vfull[2 * m + s].astype(BF16)
            p, inv = _softmax_sink_unnorm(_dot_nt(qm, kn) + bias_ref[s], sink)
            o = jnp.dot(p.astype(BF16), vn, preferred_element_type=F32) * inv
            for g in range(GQA_GROUP):
                og = jnp.where(low, o[g * 8:(g + 1) * 8],
                               o[(GQA_GROUP + g) * 8:(GQA_GROUP + g + 1) * 8])
                b8[g] = jnp.where(pair_row == s, og, b8[g])
        b_s[pl.ds(r0, 8), :] = jnp.concatenate(b8, axis=1)
        return carry

    lax.fori_loop(0, SEQ_GROUP // 2, pair_body, 0)

    @pl.when(j == N_SEQ_GROUPS - 1)
    def _():
        b_n = _rms(b_s[...], gb_ref[...]).astype(BF16)
        merged = jnp.concatenate([a_s[...], b_n], axis=1)
        xo_ref[...] = x_ref[...] + jnp.dot(merged, wout_ref[...], preferred_element_type=F32)


def _mixer_sample(x, x_block, x_buf, cache_k, cache_v, layer, lw):
    const = lambda *shape: pl.BlockSpec(shape, lambda j: (0,) * len(shape))
    cache_spec = pl.BlockSpec((1, SEQ_GROUP, CACHE_W, KV_WIDTH), lambda j: (layer, j, 0, 0))
    state_spec = pl.BlockSpec((1, SEQ_GROUP, CACHE_W, KV_WIDTH), lambda j: (j, 0, 0, 0))
    state_shape = jax.ShapeDtypeStruct((N_SEQ_GROUPS, SEQ_GROUP, CACHE_W, KV_WIDTH), F32)
    return pl.pallas_call(
        _mixer_sample_kernel,
        grid=(N_SEQ_GROUPS,),
        in_specs=[
            pl.BlockSpec((TM, D_MODEL), lambda j: (x_block, 0)),
            pl.BlockSpec(memory_space=pl.ANY),
            cache_spec, cache_spec,
            const(1, D_MODEL), const(D_MODEL, IN_WIDTH), const(1, IN_WIDTH),
            const(1, GMLP_WIDTH), const(1, GMLP_WIDTH),
            const(DEC_SEQ, 8, GMLP_WIDTH), const(8, GMLP_WIDTH),
            const(2 * GQA_GROUP * 8, 1), const(2, 2 * GQA_GROUP * 8, KEYS_PAD),
            const(1, GMLP_WIDTH), const(1, ATTN_WIDTH), const(D_MODEL, D_MODEL),
        ],
        out_specs=[
            pl.BlockSpec((TM, D_MODEL), lambda j: (NT_PROMPT, 0)),
            state_spec, state_spec,
            const(TM, GMLP_WIDTH),
        ],
        out_shape=[
            jax.ShapeDtypeStruct((T_ALL, D_MODEL), F32),
            state_shape, state_shape,
            jax.ShapeDtypeStruct((TM, GMLP_WIDTH), F32),
        ],
        scratch_shapes=[
            pltpu.VMEM((TM, ATTN_WIDTH), F32), pltpu.VMEM((TM, KV_WIDTH), F32),
            pltpu.VMEM((TM, KV_WIDTH), F32), pltpu.VMEM((TM, GMLP_WIDTH), BF16),
            pltpu.VMEM((TM, ATTN_WIDTH), F32),
            pltpu.VMEM((SEQ_GROUP, KEYS_PAD, KV_WIDTH), F32),
            pltpu.VMEM((SEQ_GROUP, KEYS_PAD, KV_WIDTH), F32),
        ],
        input_output_aliases={1: 0},
        compiler_params=pltpu.CompilerParams(
            dimension_semantics=("arbitrary",), vmem_limit_bytes=VMEM_LIMIT),
        name="mixer_sample",
    )(x, x_buf, cache_k, cache_v, lw["norm_g"], lw["w_in"], lw["b_in"], lw["ln_g"], lw["ln_b"],
      lw["w_s_sample"], lw["b_s_sample"], lw["sink_col"], lw["bias_sample"],
      lw["g_a"], lw["g_b"], lw["w_out"])


def _swiglu_kernel(x_ref, ng_ref, wg_ref, wu_ref, wd_ref, xo_ref):
    x = x_ref[...]
    h = _rms(x, ng_ref[...]).astype(BF16)
    gate = jnp.dot(h, wg_ref[...], preferred_element_type=F32)
    up = jnp.dot(h, wu_ref[...], preferred_element_type=F32)
    act = (jax.nn.silu(gate) * up).astype(BF16)
    xo_ref[...] = x + jnp.dot(act, wd_ref[...], preferred_element_type=F32)


def _swiglu(x, norm_g, wg, wu, wd):
    const = lambda *shape: pl.BlockSpec(shape, lambda i: (0,) * len(shape),
                                        pipeline_mode=pl.Buffered(1))
    return pl.pallas_call(
        _swiglu_kernel,
        grid=(NT_ALL,),
        in_specs=[
            pl.BlockSpec((TM, D_MODEL), lambda i: (i, 0)),
            const(1, D_MODEL), const(D_MODEL, D_FF), const(D_MODEL, D_FF), const(D_FF, D_MODEL),
        ],
        out_specs=pl.BlockSpec((TM, D_MODEL), lambda i: (i, 0)),
        out_shape=jax.ShapeDtypeStruct((T_ALL, D_MODEL), F32),
        compiler_params=pltpu.CompilerParams(
            dimension_semantics=("arbitrary",), vmem_limit_bytes=VMEM_LIMIT),
        name="swiglu",
    )(x, norm_g, wg, wu, wd)


def _moe_sort_kernel(x_ref, ng_ref, wrh_ref, wrl_ref, hs_ref, meta_ref, off_ref, cnt_ref):
    n = SORT_TILES_PER_STEP
    logits, routes = {}, {}
    for step in range(n + 2):
        if step < n:
            logits[step] = _sort_logits(x_ref[step * TM:(step + 1) * TM, :], ng_ref, wrh_ref, wrl_ref)
        j = step - 1
        if 0 <= j < n:
            h_hi, lg = logits.pop(j)
            meta, off, cnt_pad = _sort_route(lg)
            meta_ref[j * TM:(j + 1) * TM, :] = meta
            off_ref[j] = off.astype(jnp.int32)
            cnt_ref[j] = cnt_pad.astype(jnp.int32)
            routes[j] = (h_hi, meta)
        j = step - 2
        if 0 <= j < n:
            hs_ref[j * SORT_ROWS:(j + 1) * SORT_ROWS, :] = _sort_gather(*routes.pop(j))


def _sort_logits(x, ng_ref, wrh_ref, wrl_ref):
    h = _rms(x, ng_ref[...])
    h_hi = h.astype(BF16)
    h_lo = (h - h_hi.astype(F32)).astype(BF16)
    logits = (jnp.dot(h_hi, wrh_ref[...], preferred_element_type=F32)
              + jnp.dot(h_lo, wrh_ref[...], preferred_element_type=F32)
              + jnp.dot(h_hi, wrl_ref[...], preferred_element_type=F32))
    return h_hi, logits


def _sort_route(logits):
    lane = lax.broadcasted_iota(jnp.int32, (1, LANES), 1)
    logits = jnp.where(lane < N_EXPERTS, logits, NEG_INF)
    m1 = jnp.max(logits, axis=-1, keepdims=True)
    i1 = jnp.min(jnp.where(logits == m1, lane, LANES), axis=-1, keepdims=True)
    rest = jnp.where(lane == i1, NEG_INF, logits)
    m2 = jnp.max(rest, axis=-1, keepdims=True)
    i2 = jnp.min(jnp.where(rest == m2, lane, LANES), axis=-1, keepdims=True)
    e2 = jnp.exp(m2 - m1)
    den = 1.0 + e2
    sel0 = lane == i1
    sel1 = lane == i2
    routed = jnp.where(jnp.logical_or(sel0, sel1), 1.0, 0.0)
    r_i = lax.broadcasted_iota(jnp.int32, (TM, TM), 0)
    c_i = lax.broadcasted_iota(jnp.int32, (TM, TM), 1)
    lower = jnp.where(r_i > c_i, 1.0, 0.0).astype(BF16)
    rank = jnp.dot(lower, routed.astype(BF16), preferred_element_type=F32)
    cnt = jnp.broadcast_to(jnp.sum(routed, axis=0, keepdims=True), (8, LANES))
    cnt_pad = jnp.floor((cnt + (PIECE - 1)) * (1.0 / PIECE)) * PIECE
    incl = cnt_pad
    for sh in (1, 2, 4):
        incl = incl + jnp.where(lane >= sh, pltpu.roll(incl, sh, 1), 0.0)
    off = incl - cnt_pad
    pos = off[0:1] + rank
    pos0 = jnp.sum(jnp.where(sel0, pos, 0.0), axis=-1, keepdims=True)
    pos1 = jnp.sum(jnp.where(sel1, pos, 0.0), axis=-1, keepdims=True)
    meta = jnp.where(lane == 0, pos0, jnp.where(lane == 1, pos1,
                     jnp.where(lane == 2, 1.0 / den, jnp.where(lane == 3, e2 / den, 0.0))))
    return meta, off, cnt_pad


def _sort_gather(h_hi, meta):
    meta_t = meta.T
    p_row = lax.broadcasted_iota(jnp.int32, (SORT_ROWS, 1), 0).astype(F32)
    hit = jnp.logical_or(p_row == meta_t[0:1], p_row == meta_t[1:2])
    gather = jnp.where(hit, 1.0, 0.0).astype(BF16)
    return jnp.dot(gather, h_hi, preferred_element_type=F32).astype(BF16)


def _moe_sort(x, norm_g, wr_hi, wr_lo):
    return pl.pallas_call(
        _moe_sort_kernel,
        grid=(NT_ALL // SORT_TILES_PER_STEP,),
        in_specs=[
            pl.BlockSpec((SORT_TILES_PER_STEP * TM, D_MODEL), lambda i: (i, 0)),
            pl.BlockSpec((1, D_MODEL), lambda i: (0, 0)),
            pl.BlockSpec((D_MODEL, LANES), lambda i: (0, 0)),
            pl.BlockSpec((D_MODEL, LANES), lambda i: (0, 0)),
        ],
        out_specs=[
            pl.BlockSpec((SORT_TILES_PER_STEP * SORT_ROWS, D_MODEL), lambda i: (i, 0)),
            pl.BlockSpec((SORT_TILES_PER_STEP * TM, LANES), lambda i: (i, 0)),
            pl.BlockSpec((SORT_TILES_PER_STEP, 8, LANES), lambda i: (i, 0, 0)),
            pl.BlockSpec((SORT_TILES_PER_STEP, 8, LANES), lambda i: (i, 0, 0)),
        ],
        out_shape=[
            jax.ShapeDtypeStruct((NT_ALL * SORT_ROWS, D_MODEL), BF16),
            jax.ShapeDtypeStruct((T_ALL, LANES), F32),
            jax.ShapeDtypeStruct((NT_ALL, 8, LANES), jnp.int32),
            jax.ShapeDtypeStruct((NT_ALL, 8, LANES), jnp.int32),
        ],
        compiler_params=pltpu.CompilerParams(
            dimension_semantics=("arbitrary",), vmem_limit_bytes=VMEM_LIMIT),
        name="moe_sort",
    )(x, norm_g, wr_hi, wr_lo)


def _moe_experts_kernel(off_ref, cnt_ref, hs_hbm, wgu_ref, wd_ref, ys_hbm, buf, lhs, sem_in, sem_out):
    s = pl.program_id(0)
    e = pl.program_id(1)
    row0 = pl.multiple_of(s * SUPER_ROWS, PIECE)
    is_first = e == 0
    is_last = e == N_EXPERTS - 1

    def copy_in(t):
        r = pl.multiple_of(t * SORT_ROWS, PIECE)
        return pltpu.make_async_copy(hs_hbm.at[pl.ds(row0 + r, SORT_ROWS)], buf.at[pl.ds(r, SORT_ROWS)],
                                     sem_in.at[t])

    def copy_out(t):
        r = pl.multiple_of(t * SORT_ROWS, PIECE)
        return pltpu.make_async_copy(buf.at[pl.ds(r, SORT_ROWS)], ys_hbm.at[pl.ds(row0 + r, SORT_ROWS)],
                                     sem_out.at[t])

    def for_tiles(lo, hi, fn):
        hi = jnp.maximum(lo, hi)

        def body(t, carry):
            fn(t)
            return carry

        lax.fori_loop(lo, hi, body, 0)
        return hi

    @pl.when(is_first)
    def _():
        for t in range(TILES_PER_SUPER):
            copy_in(t).start()

    starts, cums = [], [0]
    for i in range(TILES_PER_SUPER):
        idx = (s * TILES_PER_SUPER + i) * N_EXPERTS + e
        starts.append(i * SORT_ROWS + off_ref[idx])
        cums.append(cums[-1] + cnt_ref[idx] // PIECE)
    n_pieces = cums[-1]

    def locate(p):
        pc = jnp.minimum(p, n_pieces - 1)
        row = starts[0] + PIECE * pc
        tile = jnp.int32(0)
        for i in range(1, TILES_PER_SUPER):
            inside = pc >= cums[i]
            row = jnp.where(inside, starts[i] + PIECE * (pc - cums[i]), row)
            tile = jnp.where(inside, i, tile)
        return row, tile

    def piece_rows(q):
        src, dst, tiles = [], [], []
        for j in range(PIECES_PER_CHUNK):
            p = q * PIECES_PER_CHUNK + j
            row, tile = locate(p)
            src.append(row)
            dst.append(jnp.where(p < n_pieces, row, SUPER_ROWS))
            tiles.append(tile)
        return tuple(src), tuple(dst), tiles[0], tiles[-1]

    def chunk_body(q, carry):
        (src, dst, _, last_tile), arrived, sent = carry
        arrived = for_tiles(arrived, jnp.where(is_first, last_tile + 1, 0), lambda t: copy_in(t).wait())
        for j in range(PIECES_PER_CHUNK):
            lhs[j * PIECE:(j + 1) * PIECE, :] = buf[pl.ds(pl.multiple_of(src[j], PIECE), PIECE), :]
        xs = lhs[...]
        gate_up = jnp.dot(xs, wgu_ref[0], preferred_element_type=F32)
        nxt = piece_rows(q + 1)
        act = (jax.nn.silu(gate_up[:, :D_FF_EXPERT]) * gate_up[:, D_FF_EXPERT:]).astype(BF16)
        y = jnp.dot(act, wd_ref[0], preferred_element_type=F32).astype(BF16)
        for j in range(PIECES_PER_CHUNK):
            buf[pl.ds(pl.multiple_of(dst[j], PIECE), PIECE), :] = y[j * PIECE:(j + 1) * PIECE]
        sent = for_tiles(sent, jnp.where(is_last, nxt[2], 0), lambda t: copy_out(t).start())
        return nxt, arrived, sent

    n_chunks = (n_pieces + PIECES_PER_CHUNK - 1) // PIECES_PER_CHUNK
    _, arrived, sent = lax.fori_loop(0, n_chunks, chunk_body, (piece_rows(0), jnp.int32(0), jnp.int32(0)))
    for_tiles(arrived, jnp.where(is_first, TILES_PER_SUPER, 0), lambda t: copy_in(t).wait())
    for_tiles(sent, jnp.where(is_last, TILES_PER_SUPER, 0), lambda t: copy_out(t).start())
    for_tiles(0, jnp.where(is_last, TILES_PER_SUPER, 0), lambda t: copy_out(t).wait())


def _moe_experts(off, cnt, hs, wgu, wd):
    grid_spec = pltpu.PrefetchScalarGridSpec(
        num_scalar_prefetch=2,
        grid=(N_SUPER, N_EXPERTS),
        in_specs=[
            pl.BlockSpec(memory_space=pl.ANY),
            pl.BlockSpec((1, D_MODEL, 2 * D_FF_EXPERT), lambda s, e, *_: (e, 0, 0)),
            pl.BlockSpec((1, D_FF_EXPERT, D_MODEL), lambda s, e, *_: (e, 0, 0)),
        ],
        out_specs=pl.BlockSpec(memory_space=pl.ANY),
        scratch_shapes=[pltpu.VMEM((SUPER_ROWS + PIECE, D_MODEL), BF16),
                        pltpu.VMEM((CHUNK_ROWS, D_MODEL), BF16),
                        pltpu.SemaphoreType.DMA((TILES_PER_SUPER,)),
                        pltpu.SemaphoreType.DMA((TILES_PER_SUPER,))],
    )
    return pl.pallas_call(
        _moe_experts_kernel,
        grid_spec=grid_spec,
        out_shape=jax.ShapeDtypeStruct((NT_ALL * SORT_ROWS, D_MODEL), BF16),
        compiler_params=pltpu.CompilerParams(
            dimension_semantics=("arbitrary", "arbitrary"), vmem_limit_bytes=VMEM_LIMIT),
        name="moe_experts",
    )(off, cnt, hs, wgu, wd)


def _unsorted_residual(x_ref, ys_ref, meta_ref):
    meta = meta_ref[...]
    p_col = lax.broadcasted_iota(jnp.int32, (1, SORT_ROWS), 1).astype(F32)
    scatter = (jnp.where(p_col == meta[:, 0:1], meta[:, 2:3], 0.0)
               + jnp.where(p_col == meta[:, 1:2], meta[:, 3:4], 0.0)).astype(BF16)
    return x_ref[...] + jnp.dot(scatter, ys_ref[...], preferred_element_type=F32)


def _moe_unsort_kernel(x_ref, ys_ref, meta_ref, xo_ref):
    xo_ref[...] = _unsorted_residual(x_ref, ys_ref, meta_ref)


def _moe_unsort_final_kernel(x_ref, ys_ref, meta_ref, g_ref, yp_ref, ysm_ref):
    i = pl.program_id(0)
    y = _rms(_unsorted_residual(x_ref, ys_ref, meta_ref), g_ref[...])

    @pl.when(i < NT_PROMPT)
    def _():
        yp_ref[...] = y

    @pl.when(i == NT_PROMPT)
    def _():
        ysm_ref[...] = y


def _moe_unsort(x, ys, meta, final_g=None):
    in_specs = [
        pl.BlockSpec((TM, D_MODEL), lambda i: (i, 0)),
        pl.BlockSpec((SORT_ROWS, D_MODEL), lambda i: (i, 0)),
        pl.BlockSpec((TM, LANES), lambda i: (i, 0)),
    ]
    params = pltpu.CompilerParams(dimension_semantics=("arbitrary",), vmem_limit_bytes=VMEM_LIMIT)
    if final_g is None:
        return pl.pallas_call(
            _moe_unsort_kernel,
            grid=(NT_ALL,),
            in_specs=in_specs,
            out_specs=pl.BlockSpec((TM, D_MODEL), lambda i: (i, 0)),
            out_shape=jax.ShapeDtypeStruct((T_ALL, D_MODEL), F32),
            compiler_params=params,
            name="moe_unsort",
        )(x, ys, meta)
    return pl.pallas_call(
        _moe_unsort_final_kernel,
        grid=(NT_ALL,),
        in_specs=in_specs + [pl.BlockSpec((1, D_MODEL), lambda i: (0, 0))],
        out_specs=[pl.BlockSpec((TM, D_MODEL), lambda i: (jnp.minimum(i, NT_PROMPT - 1), 0)),
                   pl.BlockSpec((TM, D_MODEL), lambda i: (0, 0))],
        out_shape=[jax.ShapeDtypeStruct((T_PROMPT, D_MODEL), F32),
                   jax.ShapeDtypeStruct((T_SAMPLE, D_MODEL), F32)],
        compiler_params=params,
        name="moe_unsort_final",
    )(x, ys, meta, final_g)


def _moe(x, norm_g, wr_hi, wr_lo, wgu, wd, final_g=None):
    hs, meta, off, cnt = _moe_sort(x, norm_g, wr_hi, wr_lo)
    off = off[:, 0, :N_EXPERTS].reshape(-1)
    cnt = cnt[:, 0, :N_EXPERTS].reshape(-1)
    ys = _moe_experts(off, cnt, hs, wgu, wd)
    return _moe_unsort(x, ys, meta, final_g)


def kernel(x_prompt, x_sample, cache_win_k, cache_win_v, attn_norm_g, w_in, b_in, chunk_ln_g, chunk_ln_b, w_spatial, b_spatial, attn_sinks, mix_norm_a_g, mix_norm_b_g, w_out, ffn_norm_g, ffn_w_gate, ffn_w_up, ffn_w_down, router_w, expert_w_gate, expert_w_up, expert_w_down, final_norm_g):
    w_in_p = jnp.concatenate([w_in[..., :Q0], _g_major(w_in[..., Q0:K0], -1), w_in[..., K0:]], axis=-1).astype(BF16)
    b_in_p = jnp.concatenate([b_in[..., :Q0], _g_major(b_in[..., Q0:K0], -1), b_in[..., K0:]], axis=-1)
    w_out_p = jnp.concatenate([w_out[:, :GMLP_WIDTH], _g_major(w_out[:, GMLP_WIDTH:], 1)], axis=1).astype(BF16)
    g_b_p = _g_major(mix_norm_b_g, 1)
    bs_prompt = jnp.repeat(jnp.transpose(b_spatial, (0, 2, 1)), GMLP_HEAD_DIM, axis=2)
    bs_sample = jnp.tile(bs_prompt[:, :DEC_SEQ], (1, 2, 1))
    t_idx = np.arange(DEC_SEQ)
    ws_small = jnp.stack([w_spatial[:, :, t_idx, np.maximum(t_idx - d, 0)] for d in range(DEC_SEQ)], axis=1)
    ws_sample = jnp.tile(jnp.repeat(jnp.transpose(ws_small, (0, 1, 3, 2)), GMLP_HEAD_DIM, axis=3),
                         (1, 1, 2, 1))
    sink_col = jnp.repeat(attn_sinks, 8, axis=1)[..., None]
    bias_prompt = jnp.asarray(_prompt_bias())
    bias_sample = jnp.asarray(_sample_bias())
    wr_pad = jnp.pad(router_w, ((0, 0), (0, 0), (0, LANES - N_EXPERTS)))
    wr_hi = wr_pad.astype(BF16)
    wr_lo = (wr_pad - wr_hi.astype(F32)).astype(BF16)
    ffn_wg, ffn_wu, ffn_wd = ffn_w_gate.astype(BF16), ffn_w_up.astype(BF16), ffn_w_down.astype(BF16)
    ex_wgu = jnp.concatenate([expert_w_gate.astype(BF16), expert_w_up.astype(BF16)], axis=-1)
    ex_wd = expert_w_down.astype(BF16)

    ck = cache_win_k.reshape(DEPTH, DEC_BATCH, CACHE_W, KV_WIDTH)
    cv = cache_win_v.reshape(DEPTH, DEC_BATCH, CACHE_W, KV_WIDTH)

    x = None
    kp, vp, ksm, vsm, cvs = [], [], [], [], []
    for l in range(DEPTH):
        lw = dict(
            norm_g=attn_norm_g[l][None], w_in=w_in_p[l], b_in=b_in_p[l][None],
            ln_g=chunk_ln_g[l].reshape(1, GMLP_WIDTH), ln_b=chunk_ln_b[l].reshape(1, GMLP_WIDTH),
            w_s=w_spatial[l], b_s_prompt=bs_prompt[l], w_s_sample=ws_sample[l], b_s_sample=bs_sample[l],
            sinks=attn_sinks[l], sink_col=sink_col[l], bias_prompt=bias_prompt, bias_sample=bias_sample,
            g_a=mix_norm_a_g[l][None], g_b=g_b_p[l][None], w_out=w_out_p[l])
        if l == 0:
            x_new, k_p, v_p = _mixer_prompt(x_prompt.reshape(T_PROMPT, D_MODEL), lw)
            x_new, k_s, v_s, cv_s = _mixer_sample(x_sample.reshape(T_SAMPLE, D_MODEL), 0, x_new, ck, cv, l, lw)
        else:
            x_new, k_p, v_p = _mixer_prompt(x, lw)
            x_new, k_s, v_s, cv_s = _mixer_sample(x, NT_PROMPT, x_new, ck, cv, l, lw)
        kp.append(k_p)
        vp.append(v_p)
        ksm.append(k_s)
        vsm.append(v_s)
        cvs.append(cv_s)
        i = l // 2
        if l % 2 == 0:
            x = _swiglu(x_new, ffn_norm_g[l][None], ffn_wg[i], ffn_wu[i], ffn_wd[i])
        else:
            final_g = final_norm_g[None] if l == DEPTH - 1 else None
            x = _moe(x_new, ffn_norm_g[l][None], wr_hi[i], wr_lo[i], ex_wgu[i], ex_wd[i], final_g)
    y_p, y_s = x
    win_p = (DEPTH, BATCH, WINDOW, KV_HEADS, HEAD_DIM)
    win_s = (DEPTH, DEC_BATCH, CACHE_W, KV_HEADS, HEAD_DIM)
    return (y_p.reshape(BATCH, SEQ, D_MODEL), y_s.reshape(DEC_BATCH, DEC_SEQ, D_MODEL),
            jnp.stack(kp).reshape(win_p), jnp.stack(vp).reshape(win_p),
            jnp.stack(ksm).reshape(win_s), jnp.stack(vsm).reshape(win_s),
            jnp.stack(cvs).reshape(DEPTH, DEC_BATCH, DEC_SEQ, GMLP_HEADS, GMLP_HEAD_DIM))
```

```python
import functools

import numpy as np
import jax
import jax.numpy as jnp
from jax import lax
from jax.experimental import pallas as pl
from jax.experimental.pallas import tpu as pltpu

D_MODEL = 1024
BATCH = 4
SEQ = 4096
DEPTH = 4
DEC_BATCH = 128
DEC_SEQ = 4
HEAD_DIM = 64
ATTN_HEADS = 8
KV_HEADS = 2
GQA_GROUP = ATTN_HEADS // KV_HEADS
ATTN_WIDTH = ATTN_HEADS * HEAD_DIM
KV_WIDTH = KV_HEADS * HEAD_DIM
WINDOW = 128
CHUNK = 128
GMLP_HEADS = 4
GMLP_HEAD_DIM = 128
GMLP_WIDTH = GMLP_HEADS * GMLP_HEAD_DIM
IN_WIDTH = 2 * GMLP_WIDTH + ATTN_WIDTH + 2 * KV_WIDTH
D_FF = 2816
N_EXPERTS = 8
D_FF_EXPERT = 1408
CACHE_W = 128
EPS = 1e-6

LANES = 128
TM = 512
T_PROMPT = BATCH * SEQ
T_SAMPLE = DEC_BATCH * DEC_SEQ
T_ALL = T_PROMPT + T_SAMPLE
NT_PROMPT = T_PROMPT // TM
NT_ALL = T_ALL // TM
TILES_PER_SEQ = SEQ // TM
BLOCKS_PER_TILE = TM // WINDOW
SEQ_GROUP = 16
SEQS_PER_TILE = 4
N_SEQ_GROUPS = DEC_BATCH // SEQ_GROUP
KEYS_PAD = CACHE_W + 8
PIECE = 16
SORT_ROWS = 2 * TM + N_EXPERTS * PIECE
TILES_PER_SUPER = 11
N_SUPER = NT_ALL // TILES_PER_SUPER
SUPER_ROWS = TILES_PER_SUPER * SORT_ROWS
SORT_TILES_PER_STEP = 3
CHUNK_ROWS = 256
FF_REM = D_FF_EXPERT % 256
FF_MAIN = D_FF_EXPERT - FF_REM
PIECES_PER_CHUNK = CHUNK_ROWS // PIECE
VMEM_LIMIT = 56 * 1024 * 1024

F32 = jnp.float32
BF16 = jnp.bfloat16
NEG_INF = float("-inf")

Q0 = 2 * GMLP_WIDTH
K0 = Q0 + ATTN_WIDTH
V0 = K0 + KV_WIDTH


def _g_major(a, axis):
    axis = axis % a.ndim
    shape = a.shape
    a = a.reshape(shape[:axis] + (KV_HEADS, GQA_GROUP, HEAD_DIM) + shape[axis + 1:])
    return jnp.swapaxes(a, axis, axis + 1).reshape(shape)


def _slope(kv, g):
    return 2.0 ** (-8.0 * (kv * GQA_GROUP + g + 1) / ATTN_HEADS)


def _prompt_bias():
    c = np.arange(WINDOW)[:, None]
    qi = np.arange(WINDOW)[None, :]
    dist = np.where(c > qi, qi + WINDOW - c, qi - c)
    out = np.zeros((2, KV_HEADS, WINDOW, GQA_GROUP * WINDOW), np.float32)
    for var in range(2):
        ok = ~((var == 1) & (c > qi))
        for kv in range(KV_HEADS):
            for g in range(GQA_GROUP):
                out[var, kv, :, g * WINDOW:(g + 1) * WINDOW] = np.where(ok, -_slope(kv, g) * dist, NEG_INF)
    return out


def _sample_bias():
    out = np.zeros((KEYS_PAD, LANES), np.float32)
    j = np.arange(KEYS_PAD)
    kpos = j - CACHE_W
    for kv in range(KV_HEADS):
        for g in range(GQA_GROUP):
            for s in range(SEQS_PER_TILE):
                for t in range(DEC_SEQ):
                    col = ((kv * GQA_GROUP + g) * SEQS_PER_TILE + s) * DEC_SEQ + t
                    dist = t - kpos
                    ok = (dist >= 0) & (dist < WINDOW) & (j < CACHE_W + DEC_SEQ)
                    out[:, col] = np.where(ok, -_slope(kv, g) * dist, NEG_INF)
    return out


def _rms(x, g):
    return x * lax.rsqrt(jnp.mean(x * x, axis=-1, keepdims=True) + EPS) * g


def _gelu(x):
    return 0.5 * x * (1.0 + lax.erf(x * (2.0 ** -0.5)))


def _gmlp_v(z_v, lng_ref, lnb_ref):
    out = []
    for hh in range(GMLP_HEADS):
        sl = slice(hh * LANES, (hh + 1) * LANES)
        vh = _gelu(z_v[:, sl])
        vc = vh - jnp.mean(vh, axis=-1, keepdims=True)
        y = vc * lax.rsqrt(jnp.mean(vc * vc, axis=-1, keepdims=True) + EPS)
        out.append(y * lng_ref[:, sl] + lnb_ref[:, sl])
    return out


def _dot_nt(a, b):
    return lax.dot_general(a, b, (((1,), (1,)), ((), ())), preferred_element_type=F32)


def _mixer_prompt_kernel(*refs):
    i = pl.program_id(0)

    @pl.when(i < NT_PROMPT)
    def _():
        _mixer_prompt_tile(*refs)

    @pl.when(i == NT_PROMPT)
    def _():
        xo_ref = refs[13]
        xo_ref[...] = jnp.zeros_like(xo_ref)


def _mixer_prompt_tile(x_ref, ng_ref, win_ref, bin_ref, lng_ref, lnb_ref, ws_ref, bs_ref,
                       sink_ref, bias_ref, ga_ref, gb_ref, wout_ref,
                       xo_ref, ks_ref, vs_ref, kprev_ref, vprev_ref):
    i = pl.program_id(0)

    @pl.when(i == 0)
    def _():
        kprev_ref[...] = jnp.zeros_like(kprev_ref)
        vprev_ref[...] = jnp.zeros_like(vprev_ref)

    x = x_ref[...]
    h = _rms(x, ng_ref[...]).astype(BF16)

    def in_proj(lo, hi):
        return jnp.dot(h, win_ref[:, lo:hi], preferred_element_type=F32) + bin_ref[:, lo:hi]

    z_att = in_proj(Q0, IN_WIDTH)
    q = (z_att[:, :ATTN_WIDTH] * (HEAD_DIM ** -0.5)).astype(BF16)
    k = z_att[:, ATTN_WIDTH:ATTN_WIDTH + KV_WIDTH]
    val = z_att[:, ATTN_WIDTH + KV_WIDTH:]
    k_bf = k.astype(BF16)
    lane = lax.broadcasted_iota(jnp.int32, (1, LANES), 1)
    low = lane < HEAD_DIM
    first = jnp.where(i % TILES_PER_SEQ == 0, 1, 0)
    c_i = lax.broadcasted_iota(jnp.int32, (WINDOW, WINDOW), 0)
    q_i = lax.broadcasted_iota(jnp.int32, (WINDOW, WINDOW), 1)
    from_prev = jnp.concatenate([c_i > q_i] * GQA_GROUP, axis=1)
    g_lane = lax.broadcasted_iota(jnp.int32, (1, GQA_GROUP * WINDOW), 1) // WINDOW
    top = lax.broadcasted_iota(jnp.int32, (KV_WIDTH, 1), 0) < HEAD_DIM
    one = jnp.ones((), BF16)
    vt_bf = val.T.astype(BF16)
    vt_all = jnp.concatenate([vprev_ref[...], vt_bf], axis=1)
    vt_aug = [jnp.where(top, vt_all, one), jnp.where(top, one, vt_all)]
    k_all = jnp.concatenate([kprev_ref[...], k_bf], axis=0)

    zero = jnp.zeros((), BF16)
    sinks = []
    for kv in range(KV_HEADS):
        sink = jnp.full((1, GQA_GROUP * WINDOW), sink_ref[kv * GQA_GROUP], F32)
        for g in range(1, GQA_GROUP):
            sink = jnp.where(g_lane == g, sink_ref[kv * GQA_GROUP + g], sink)
        sinks.append(sink)

    def attention_block(c):
        rows = slice(c * WINDOW, (c + 1) * WINDOW)
        band = slice(c * WINDOW, (c + 2) * WINDOW)
        o_t = []
        for kv in range(KV_HEADS):
            keep = low if kv == 0 else jnp.logical_not(low)
            qm = jnp.concatenate(
                [jnp.where(keep, q[rows, g * LANES:(g + 1) * LANES], zero) for g in range(GQA_GROUP)],
                axis=0)
            s2 = _dot_nt(k_all[band], qm)
            bias = bias_ref[first, kv] if c == 0 else bias_ref[0, kv]
            s = jnp.where(from_prev, s2[:WINDOW], s2[WINDOW:]) + bias
            m = jnp.maximum(jnp.max(s, axis=0, keepdims=True), sinks[kv])
            p = jnp.exp(s - m)
            p2 = jnp.concatenate([jnp.where(from_prev, p, 0.0), jnp.where(from_prev, 0.0, p)],
                                 axis=0).astype(BF16)
            o = jnp.dot(vt_aug[kv][:, band], p2, preferred_element_type=F32)
            own, ones_row = (o[:HEAD_DIM], o[HEAD_DIM:HEAD_DIM + 1]) if kv == 0 else (o[HEAD_DIM:], o[0:1])
            o_t.append(own / (ones_row + jnp.exp(sinks[kv] - m)))
        bt = jnp.concatenate(o_t, axis=0)
        return jnp.concatenate(
            [bt[:, g * WINDOW:(g + 1) * WINDOW].T for g in range(GQA_GROUP)], axis=1)

    z_u = in_proj(0, GMLP_WIDTH)
    b_rows = [attention_block(0)]
    z_v = in_proj(GMLP_WIDTH, Q0)
    b_rows.append(attention_block(1))
    u = _gelu(z_u)
    b_rows.append(attention_block(2))
    v_heads = _gmlp_v(z_v, lng_ref, lnb_ref)
    row = lax.broadcasted_iota(jnp.int32, (CHUNK, CHUNK), 0)
    col = lax.broadcasted_iota(jnp.int32, (CHUNK, CHUNK), 1)
    a_cols = []
    for hh in range(GMLP_HEADS):
        w = jnp.where(row >= col, ws_ref[hh], 0.0).astype(BF16)
        v_cat = jnp.concatenate(
            [v_heads[hh][c * CHUNK:(c + 1) * CHUNK] for c in range(BLOCKS_PER_TILE)], axis=1)
        mixed = jnp.dot(w, v_cat.astype(BF16), preferred_element_type=F32)
        bias = bs_ref[:, hh * LANES:(hh + 1) * LANES]
        a_cols.append(jnp.concatenate(
            [mixed[:, c * LANES:(c + 1) * LANES] + bias for c in range(BLOCKS_PER_TILE)], axis=0))
    a = u * jnp.concatenate(a_cols, axis=1)
    a_n = _rms(a, ga_ref[...]).astype(BF16)
    b_rows.append(attention_block(3))
    out = x + jnp.dot(a_n, wout_ref[:GMLP_WIDTH, :], preferred_element_type=F32)
    b_n = _rms(jnp.concatenate(b_rows, axis=0), gb_ref[...]).astype(BF16)
    xo_ref[...] = out + jnp.dot(b_n, wout_ref[GMLP_WIDTH:, :], preferred_element_type=F32)

    kprev_ref[...] = k_bf[TM - WINDOW:]
    vprev_ref[...] = vt_bf[:, TM - WINDOW:]
    ks_ref[0] = k[TM - WINDOW:]
    vs_ref[0] = val[TM - WINDOW:]


def _mixer_prompt(x, lw):
    const = lambda *shape: pl.BlockSpec(shape, lambda i: (0,) * len(shape))
    return pl.pallas_call(
        _mixer_prompt_kernel,
        grid=(NT_ALL,),
        in_specs=[
            pl.BlockSpec((TM, D_MODEL), lambda i: (jnp.minimum(i, NT_PROMPT - 1), 0)),
            const(1, D_MODEL), const(D_MODEL, IN_WIDTH), const(1, IN_WIDTH),
            const(1, GMLP_WIDTH), const(1, GMLP_WIDTH),
            const(GMLP_HEADS, CHUNK, CHUNK), const(CHUNK, GMLP_WIDTH),
            pl.BlockSpec(memory_space=pltpu.SMEM),
            const(2, KV_HEADS, WINDOW, GQA_GROUP * WINDOW),
            const(1, GMLP_WIDTH), const(1, ATTN_WIDTH), const(D_MODEL, D_MODEL),
        ],
        out_specs=[
            pl.BlockSpec((TM, D_MODEL), lambda i: (i, 0)),
            pl.BlockSpec((1, WINDOW, KV_WIDTH), lambda i: (jnp.minimum(i // TILES_PER_SEQ, BATCH - 1), 0, 0)),
            pl.BlockSpec((1, WINDOW, KV_WIDTH), lambda i: (jnp.minimum(i // TILES_PER_SEQ, BATCH - 1), 0, 0)),
        ],
        out_shape=[
            jax.ShapeDtypeStruct((T_ALL, D_MODEL), F32),
            jax.ShapeDtypeStruct((BATCH, WINDOW, KV_WIDTH), F32),
            jax.ShapeDtypeStruct((BATCH, WINDOW, KV_WIDTH), F32),
        ],
        scratch_shapes=[pltpu.VMEM((WINDOW, KV_WIDTH), BF16), pltpu.VMEM((WINDOW, KV_WIDTH), BF16)],
        compiler_params=pltpu.CompilerParams(
            dimension_semantics=("arbitrary",), vmem_limit_bytes=VMEM_LIMIT),
        name="mixer_prompt",
    )(x, lw["norm_g"], lw["w_in"], lw["b_in"], lw["ln_g"], lw["ln_b"], lw["w_s"], lw["b_s_prompt"],
      lw["sinks"], lw["bias_prompt"], lw["g_a"], lw["g_b"], lw["w_out"])


def _mixer_sample_kernel(x_ref, xalias_ref, ck_ref, cv_ref, ng_ref, win_ref, bin_ref, lng_ref, lnb_ref,
                         wsd_ref, bsd_ref, sinkrow_ref, bias_ref, ga_ref, gb_ref, wout_ref,
                         xo_ref, ko_ref, vo_ref, cvo_ref,
                         q_s, k_s, v_s, a_s, b_s, kfull, vfull):
    del xalias_ref
    j = pl.program_id(0)

    @pl.when(j == 0)
    def _():
        h = _rms(x_ref[...], ng_ref[...]).astype(BF16)
        z = jnp.dot(h, win_ref[...], preferred_element_type=F32) + bin_ref[...]
        u = _gelu(z[:, :GMLP_WIDTH])
        v = jnp.concatenate(_gmlp_v(z[:, GMLP_WIDTH:Q0], lng_ref, lnb_ref), axis=1)
        cvo_ref[...] = v
        t_row = lax.broadcasted_iota(jnp.int32, (8, 1), 0) % DEC_SEQ

        def tile_rows(pat):
            return jnp.broadcast_to(pat[None], (TM // 8,) + pat.shape).reshape(TM, pat.shape[-1])

        mixed = tile_rows(bsd_ref[...])
        for d in range(DEC_SEQ):
            vd = v if d == 0 else pltpu.roll(v, d, 0)
            mixed = mixed + tile_rows(jnp.where(t_row >= d, wsd_ref[d], 0.0)) * vd
        a_s[...] = _rms(u * mixed, ga_ref[...]).astype(BF16)
        q_s[...] = z[:, Q0:K0] * (HEAD_DIM ** -0.5)
        k_s[...] = z[:, K0:V0]
        v_s[...] = z[:, V0:]
        kfull[:, CACHE_W + DEC_SEQ:, :] = jnp.zeros((SEQ_GROUP, KEYS_PAD - CACHE_W - DEC_SEQ, KV_WIDTH), F32)
        vfull[:, CACHE_W + DEC_SEQ:, :] = jnp.zeros((SEQ_GROUP, KEYS_PAD - CACHE_W - DEC_SEQ, KV_WIDTH), F32)

    base = pl.multiple_of(j * (SEQ_GROUP * DEC_SEQ), SEQ_GROUP * DEC_SEQ)
    kfull[:, :CACHE_W, :] = ck_ref[0]
    vfull[:, :CACHE_W, :] = cv_ref[0]
    k_new = k_s[pl.ds(base, SEQ_GROUP * DEC_SEQ), :]
    v_new = v_s[pl.ds(base, SEQ_GROUP * DEC_SEQ), :]
    for s in range(SEQ_GROUP):
        kfull[s, CACHE_W:CACHE_W + DEC_SEQ, :] = k_new[s * DEC_SEQ:(s + 1) * DEC_SEQ]
        vfull[s, CACHE_W:CACHE_W + DEC_SEQ, :] = v_new[s * DEC_SEQ:(s + 1) * DEC_SEQ]
    ko_ref[0] = kfull[:, DEC_SEQ:DEC_SEQ + CACHE_W, :]
    vo_ref[0] = vfull[:, DEC_SEQ:DEC_SEQ + CACHE_W, :]

    lane = lax.broadcasted_iota(jnp.int32, (1, LANES), 1)
    low = lane < HEAD_DIM
    col_seq = (lane // DEC_SEQ) % SEQS_PER_TILE
    row_seq = (lax.broadcasted_iota(jnp.int32, (LANES, 1), 0) // DEC_SEQ) % SEQS_PER_TILE
    sink = sinkrow_ref[...]
    zero = jnp.zeros((), BF16)
    quad_rows = SEQS_PER_TILE * DEC_SEQ
    for m in range(SEQ_GROUP // SEQS_PER_TILE):
        r0 = pl.multiple_of(base + m * quad_rows, quad_rows)
        q16 = q_s[pl.ds(r0, quad_rows), :]
        qt = jnp.concatenate(
            [jnp.where(low if kv == 0 else jnp.logical_not(low), q16[:, g * LANES:(g + 1) * LANES], 0.0)
             for kv in range(KV_HEADS) for g in range(GQA_GROUP)], axis=0).astype(BF16)
        s = bias_ref[...]
        for n in range(SEQS_PER_TILE):
            kn = kfull[SEQS_PER_TILE * m + n].astype(BF16)
            s = s + _dot_nt(kn, jnp.where(row_seq == n, qt, zero))
        mx = jnp.maximum(jnp.max(s, axis=0, keepdims=True), sink)
        p = jnp.exp(s - mx)
        p = (p / (jnp.sum(p, axis=0, keepdims=True) + jnp.exp(sink - mx))).astype(BF16)
        o = jnp.zeros((LANES, KV_WIDTH), F32)
        for n in range(SEQS_PER_TILE):
            vn = vfull[SEQS_PER_TILE * m + n].astype(BF16)
            o = o + lax.dot_general(jnp.where(col_seq == n, p, zero), vn, (((0,), (0,)), ((), ())),
                                    preferred_element_type=F32)
        b_s[pl.ds(r0, quad_rows), :] = jnp.concatenate(
            [jnp.where(low, o[g * quad_rows:(g + 1) * quad_rows],
                       o[(GQA_GROUP + g) * quad_rows:(GQA_GROUP + g + 1) * quad_rows])
             for g in range(GQA_GROUP)], axis=1)

    @pl.when(j == N_SEQ_GROUPS - 1)
    def _():
        b_n = _rms(b_s[...], gb_ref[...]).astype(BF16)
        merged = jnp.concatenate([a_s[...], b_n], axis=1)
        xo_ref[...] = x_ref[...] + jnp.dot(merged, wout_ref[...], preferred_element_type=F32)


def _mixer_sample(x, x_block, x_buf, cache_k, cache_v, layer, lw):
    const = lambda *shape: pl.BlockSpec(shape, lambda j: (0,) * len(shape))
    cache_spec = pl.BlockSpec((1, SEQ_GROUP, CACHE_W, KV_WIDTH), lambda j: (layer, j, 0, 0))
    state_spec = pl.BlockSpec((1, SEQ_GROUP, CACHE_W, KV_WIDTH), lambda j: (j, 0, 0, 0))
    state_shape = jax.ShapeDtypeStruct((N_SEQ_GROUPS, SEQ_GROUP, CACHE_W, KV_WIDTH), F32)
    return pl.pallas_call(
        _mixer_sample_kernel,
        grid=(N_SEQ_GROUPS,),
        in_specs=[
            pl.BlockSpec((TM, D_MODEL), lambda j: (x_block, 0)),
            pl.BlockSpec(memory_space=pl.ANY),
            cache_spec, cache_spec,
            const(1, D_MODEL), const(D_MODEL, IN_WIDTH), const(1, IN_WIDTH),
            const(1, GMLP_WIDTH), const(1, GMLP_WIDTH),
            const(DEC_SEQ, 8, GMLP_WIDTH), const(8, GMLP_WIDTH),
            const(1, LANES), const(KEYS_PAD, LANES),
            const(1, GMLP_WIDTH), const(1, ATTN_WIDTH), const(D_MODEL, D_MODEL),
        ],
        out_specs=[
            pl.BlockSpec((TM, D_MODEL), lambda j: (NT_PROMPT, 0)),
            state_spec, state_spec,
            const(TM, GMLP_WIDTH),
        ],
        out_shape=[
            jax.ShapeDtypeStruct((T_ALL, D_MODEL), F32),
            state_shape, state_shape,
            jax.ShapeDtypeStruct((TM, GMLP_WIDTH), F32),
        ],
        scratch_shapes=[
            pltpu.VMEM((TM, ATTN_WIDTH), F32), pltpu.VMEM((TM, KV_WIDTH), F32),
            pltpu.VMEM((TM, KV_WIDTH), F32), pltpu.VMEM((TM, GMLP_WIDTH), BF16),
            pltpu.VMEM((TM, ATTN_WIDTH), F32),
            pltpu.VMEM((SEQ_GROUP, KEYS_PAD, KV_WIDTH), F32),
            pltpu.VMEM((SEQ_GROUP, KEYS_PAD, KV_WIDTH), F32),
        ],
        input_output_aliases={1: 0},
        compiler_params=pltpu.CompilerParams(
            dimension_semantics=("arbitrary",), vmem_limit_bytes=VMEM_LIMIT),
        name="mixer_sample",
    )(x, x_buf, cache_k, cache_v, lw["norm_g"], lw["w_in"], lw["b_in"], lw["ln_g"], lw["ln_b"],
      lw["w_s_sample"], lw["b_s_sample"], lw["sink_row"], lw["bias_sample"],
      lw["g_a"], lw["g_b"], lw["w_out"])


def _swiglu_kernel(x_ref, ng_ref, wg_ref, wu_ref, wd_ref, xo_ref):
    x = x_ref[...]
    h = _rms(x, ng_ref[...]).astype(BF16)
    gate = jnp.dot(h, wg_ref[...], preferred_element_type=F32)
    up = jnp.dot(h, wu_ref[...], preferred_element_type=F32)
    act = (jax.nn.silu(gate) * up).astype(BF16)
    xo_ref[...] = x + jnp.dot(act, wd_ref[...], preferred_element_type=F32)


def _swiglu(x, norm_g, wg, wu, wd):
    const = lambda *shape: pl.BlockSpec(shape, lambda i: (0,) * len(shape),
                                        pipeline_mode=pl.Buffered(1))
    return pl.pallas_call(
        _swiglu_kernel,
        grid=(NT_ALL,),
        in_specs=[
            pl.BlockSpec((TM, D_MODEL), lambda i: (i, 0)),
            const(1, D_MODEL), const(D_MODEL, D_FF), const(D_MODEL, D_FF), const(D_FF, D_MODEL),
        ],
        out_specs=pl.BlockSpec((TM, D_MODEL), lambda i: (i, 0)),
        out_shape=jax.ShapeDtypeStruct((T_ALL, D_MODEL), F32),
        compiler_params=pltpu.CompilerParams(
            dimension_semantics=("arbitrary",), vmem_limit_bytes=VMEM_LIMIT),
        name="swiglu",
    )(x, norm_g, wg, wu, wd)


def _moe_sort_kernel(x_ref, ng_ref, wrh_ref, wrl_ref, hs_ref, meta_ref, off_ref, cnt_ref):
    n = SORT_TILES_PER_STEP
    logits, routes = {}, {}
    for step in range(n + 2):
        if step < n:
            logits[step] = _sort_logits(x_ref[step * TM:(step + 1) * TM, :], ng_ref, wrh_ref, wrl_ref)
        j = step - 1
        if 0 <= j < n:
            h_hi, lg = logits.pop(j)
            meta, off, cnt_pad = _sort_route(lg)
            meta_ref[j * TM:(j + 1) * TM, :] = meta
            off_ref[j] = off.astype(jnp.int32)
            cnt_ref[j] = cnt_pad.astype(jnp.int32)
            routes[j] = (h_hi, meta)
        j = step - 2
        if 0 <= j < n:
            hs_ref[j * SORT_ROWS:(j + 1) * SORT_ROWS, :] = _sort_gather(*routes.pop(j))


def _sort_logits(x, ng_ref, wrh_ref, wrl_ref):
    h = _rms(x, ng_ref[...])
    h_hi = h.astype(BF16)
    h_lo = (h - h_hi.astype(F32)).astype(BF16)
    logits = (jnp.dot(h_hi, wrh_ref[...], preferred_element_type=F32)
              + jnp.dot(h_lo, wrh_ref[...], preferred_element_type=F32)
              + jnp.dot(h_hi, wrl_ref[...], preferred_element_type=F32))
    return h_hi, logits


def _sort_route(logits):
    lane = lax.broadcasted_iota(jnp.int32, (1, LANES), 1)
    logits = jnp.where(lane < N_EXPERTS, logits, NEG_INF)
    m1 = jnp.max(logits, axis=-1, keepdims=True)
    i1 = jnp.min(jnp.where(logits == m1, lane, LANES), axis=-1, keepdims=True)
    rest = jnp.where(lane == i1, NEG_INF, logits)
    m2 = jnp.max(rest, axis=-1, keepdims=True)
    i2 = jnp.min(jnp.where(rest == m2, lane, LANES), axis=-1, keepdims=True)
    e2 = jnp.exp(m2 - m1)
    den = 1.0 + e2
    sel0 = lane == i1
    sel1 = lane == i2
    routed = jnp.where(jnp.logical_or(sel0, sel1), 1.0, 0.0)
    r_i = lax.broadcasted_iota(jnp.int32, (TM, TM), 0)
    c_i = lax.broadcasted_iota(jnp.int32, (TM, TM), 1)
    lower = jnp.where(r_i > c_i, 1.0, 0.0).astype(BF16)
    rank = jnp.dot(lower, routed.astype(BF16), preferred_element_type=F32)
    cnt = jnp.broadcast_to(jnp.sum(routed, axis=0, keepdims=True), (8, LANES))
    cnt_pad = jnp.floor((cnt + (PIECE - 1)) * (1.0 / PIECE)) * PIECE
    incl = cnt_pad
    for sh in (1, 2, 4):
        incl = incl + jnp.where(lane >= sh, pltpu.roll(incl, sh, 1), 0.0)
    off = incl - cnt_pad
    pos = off[0:1] + rank
    pos0 = jnp.sum(jnp.where(sel0, pos, 0.0), axis=-1, keepdims=True)
    pos1 = jnp.sum(jnp.where(sel1, pos, 0.0), axis=-1, keepdims=True)
    meta = jnp.where(lane == 0, pos0, jnp.where(lane == 1, pos1,
                     jnp.where(lane == 2, 1.0 / den, jnp.where(lane == 3, e2 / den, 0.0))))
    return meta, off, cnt_pad


def _sort_gather(h_hi, meta):
    meta_t = meta.T
    p_row = lax.broadcasted_iota(jnp.int32, (SORT_ROWS, 1), 0).astype(F32)
    hit = jnp.logical_or(p_row == meta_t[0:1], p_row == meta_t[1:2])
    gather = jnp.where(hit, 1.0, 0.0).astype(BF16)
    return jnp.dot(gather, h_hi, preferred_element_type=F32).astype(BF16)


def _moe_sort(x, norm_g, wr_hi, wr_lo):
    return pl.pallas_call(
        _moe_sort_kernel,
        grid=(NT_ALL // SORT_TILES_PER_STEP,),
        in_specs=[
            pl.BlockSpec((SORT_TILES_PER_STEP * TM, D_MODEL), lambda i: (i, 0)),
            pl.BlockSpec((1, D_MODEL), lambda i: (0, 0)),
            pl.BlockSpec((D_MODEL, LANES), lambda i: (0, 0)),
            pl.BlockSpec((D_MODEL, LANES), lambda i: (0, 0)),
        ],
        out_specs=[
            pl.BlockSpec((SORT_TILES_PER_STEP * SORT_ROWS, D_MODEL), lambda i: (i, 0)),
            pl.BlockSpec((SORT_TILES_PER_STEP * TM, LANES), lambda i: (i, 0)),
            pl.BlockSpec((SORT_TILES_PER_STEP, 8, LANES), lambda i: (i, 0, 0)),
            pl.BlockSpec((SORT_TILES_PER_STEP, 8, LANES), lambda i: (i, 0, 0)),
        ],
        out_shape=[
            jax.ShapeDtypeStruct((NT_ALL * SORT_ROWS, D_MODEL), BF16),
            jax.ShapeDtypeStruct((T_ALL, LANES), F32),
            jax.ShapeDtypeStruct((NT_ALL, 8, LANES), jnp.int32),
            jax.ShapeDtypeStruct((NT_ALL, 8, LANES), jnp.int32),
        ],
        compiler_params=pltpu.CompilerParams(
            dimension_semantics=("arbitrary",), vmem_limit_bytes=VMEM_LIMIT),
        name="moe_sort",
    )(x, norm_g, wr_hi, wr_lo)


def _moe_experts_kernel(off_ref, cnt_ref, hs_hbm, wg_ref, wu_ref, wrem_ref, wd_ref, ys_hbm,
                        buf, lhs, sem_in, sem_out):
    s = pl.program_id(0)
    e = pl.program_id(1)
    row0 = pl.multiple_of(s * SUPER_ROWS, PIECE)
    is_first = e == 0
    is_last = e == N_EXPERTS - 1

    def copy_in(t):
        r = pl.multiple_of(t * SORT_ROWS, PIECE)
        return pltpu.make_async_copy(hs_hbm.at[pl.ds(row0 + r, SORT_ROWS)], buf.at[pl.ds(r, SORT_ROWS)],
                                     sem_in.at[t])

    def copy_out(t):
        r = pl.multiple_of(t * SORT_ROWS, PIECE)
        return pltpu.make_async_copy(buf.at[pl.ds(r, SORT_ROWS)], ys_hbm.at[pl.ds(row0 + r, SORT_ROWS)],
                                     sem_out.at[t])

    def for_tiles(lo, hi, fn):
        hi = jnp.maximum(lo, hi)

        def body(t, carry):
            fn(t)
            return carry

        lax.fori_loop(lo, hi, body, 0)
        return hi

    @pl.when(is_first)
    def _():
        for t in range(TILES_PER_SUPER):
            copy_in(t).start()

    starts, cums = [], [0]
    for i in range(TILES_PER_SUPER):
        idx = (s * TILES_PER_SUPER + i) * N_EXPERTS + e
        starts.append(i * SORT_ROWS + off_ref[idx])
        cums.append(cums[-1] + cnt_ref[idx] // PIECE)
    n_pieces = cums[-1]

    def locate(p):
        pc = jnp.minimum(p, n_pieces - 1)
        row = starts[0] + PIECE * pc
        tile = jnp.int32(0)
        for i in range(1, TILES_PER_SUPER):
            inside = pc >= cums[i]
            row = jnp.where(inside, starts[i] + PIECE * (pc - cums[i]), row)
            tile = jnp.where(inside, i, tile)
        return row, tile

    def piece_rows(q):
        src, dst, tiles = [], [], []
        for j in range(PIECES_PER_CHUNK):
            p = q * PIECES_PER_CHUNK + j
            row, tile = locate(p)
            src.append(row)
            dst.append(jnp.where(p < n_pieces, row, SUPER_ROWS))
            tiles.append(tile)
        return tuple(src), tuple(dst), tiles[0], tiles[-1]

    def chunk_body(q, carry):
        (src, dst, _, last_tile), arrived, sent = carry
        arrived = for_tiles(arrived, jnp.where(is_first, last_tile + 1, 0), lambda t: copy_in(t).wait())
        for j in range(PIECES_PER_CHUNK):
            lhs[j * PIECE:(j + 1) * PIECE, :] = buf[pl.ds(pl.multiple_of(src[j], PIECE), PIECE), :]
        xs = lhs[...]
        gate = jnp.dot(xs, wg_ref[0, :, :FF_MAIN], preferred_element_type=F32)
        up = jnp.dot(xs, wu_ref[0, :, :FF_MAIN], preferred_element_type=F32)
        rem = jnp.dot(xs, wrem_ref[0], preferred_element_type=F32)
        nxt = piece_rows(q + 1)
        act = jnp.concatenate([jax.nn.silu(gate) * up,
                               jax.nn.silu(rem[:, :FF_REM]) * rem[:, FF_REM:]], axis=1).astype(BF16)
        y = jnp.dot(act, wd_ref[0], preferred_element_type=F32).astype(BF16)
        for j in range(PIECES_PER_CHUNK):
            buf[pl.ds(pl.multiple_of(dst[j], PIECE), PIECE), :] = y[j * PIECE:(j + 1) * PIECE]
        sent = for_tiles(sent, jnp.where(is_last, nxt[2], 0), lambda t: copy_out(t).start())
        return nxt, arrived, sent

    n_chunks = (n_pieces + PIECES_PER_CHUNK - 1) // PIECES_PER_CHUNK
    _, arrived, sent = lax.fori_loop(0, n_chunks, chunk_body, (piece_rows(0), jnp.int32(0), jnp.int32(0)))
    for_tiles(arrived, jnp.where(is_first, TILES_PER_SUPER, 0), lambda t: copy_in(t).wait())
    for_tiles(sent, jnp.where(is_last, TILES_PER_SUPER, 0), lambda t: copy_out(t).start())
    for_tiles(0, jnp.where(is_last, TILES_PER_SUPER, 0), lambda t: copy_out(t).wait())


def _moe_experts(off, cnt, hs, wg, wu, wrem, wd):
    grid_spec = pltpu.PrefetchScalarGridSpec(
        num_scalar_prefetch=2,
        grid=(N_SUPER, N_EXPERTS),
        in_specs=[
            pl.BlockSpec(memory_space=pl.ANY),
            pl.BlockSpec((1, D_MODEL, D_FF_EXPERT), lambda s, e, *_: (e, 0, 0)),
            pl.BlockSpec((1, D_MODEL, D_FF_EXPERT), lambda s, e, *_: (e, 0, 0)),
            pl.BlockSpec((1, D_MODEL, 2 * FF_REM), lambda s, e, *_: (e, 0, 0)),
            pl.BlockSpec((1, D_FF_EXPERT, D_MODEL), lambda s, e, *_: (e, 0, 0)),
        ],
        out_specs=pl.BlockSpec(memory_space=pl.ANY),
        scratch_shapes=[pltpu.VMEM((SUPER_ROWS + PIECE, D_MODEL), BF16),
                        pltpu.VMEM((CHUNK_ROWS, D_MODEL), BF16),
                        pltpu.SemaphoreType.DMA((TILES_PER_SUPER,)),
                        pltpu.SemaphoreType.DMA((TILES_PER_SUPER,))],
    )
    return pl.pallas_call(
        _moe_experts_kernel,
        grid_spec=grid_spec,
        out_shape=jax.ShapeDtypeStruct((NT_ALL * SORT_ROWS, D_MODEL), BF16),
        compiler_params=pltpu.CompilerParams(
            dimension_semantics=("arbitrary", "arbitrary"), vmem_limit_bytes=VMEM_LIMIT),
        name="moe_experts",
    )(off, cnt, hs, wg, wu, wrem, wd)


def _unsorted_residual(x_ref, ys_ref, meta_ref):
    meta = meta_ref[...]
    p_col = lax.broadcasted_iota(jnp.int32, (1, SORT_ROWS), 1).astype(F32)
    scatter = (jnp.where(p_col == meta[:, 0:1], meta[:, 2:3], 0.0)
               + jnp.where(p_col == meta[:, 1:2], meta[:, 3:4], 0.0)).astype(BF16)
    return x_ref[...] + jnp.dot(scatter, ys_ref[...], preferred_element_type=F32)


def _moe_unsort_kernel(x_ref, ys_ref, meta_ref, xo_ref):
    xo_ref[...] = _unsorted_residual(x_ref, ys_ref, meta_ref)


def _moe_unsort_final_kernel(x_ref, ys_ref, meta_ref, g_ref, yp_ref, ysm_ref):
    i = pl.program_id(0)
    y = _rms(_unsorted_residual(x_ref, ys_ref, meta_ref), g_ref[...])

    @pl.when(i < NT_PROMPT)
    def _():
        yp_ref[...] = y

    @pl.when(i == NT_PROMPT)
    def _():
        ysm_ref[...] = y


def _moe_unsort(x, ys, meta, final_g=None):
    in_specs = [
        pl.BlockSpec((TM, D_MODEL), lambda i: (i, 0)),
        pl.BlockSpec((SORT_ROWS, D_MODEL), lambda i: (i, 0)),
        pl.BlockSpec((TM, LANES), lambda i: (i, 0)),
    ]
    params = pltpu.CompilerParams(dimension_semantics=("arbitrary",), vmem_limit_bytes=VMEM_LIMIT)
    if final_g is None:
        return pl.pallas_call(
            _moe_unsort_kernel,
            grid=(NT_ALL,),
            in_specs=in_specs,
            out_specs=pl.BlockSpec((TM, D_MODEL), lambda i: (i, 0)),
            out_shape=jax.ShapeDtypeStruct((T_ALL, D_MODEL), F32),
            compiler_params=params,
            name="moe_unsort",
        )(x, ys, meta)
    return pl.pallas_call(
        _moe_unsort_final_kernel,
        grid=(NT_ALL,),
        in_specs=in_specs + [pl.BlockSpec((1, D_MODEL), lambda i: (0, 0))],
        out_specs=[pl.BlockSpec((TM, D_MODEL), lambda i: (jnp.minimum(i, NT_PROMPT - 1), 0)),
                   pl.BlockSpec((TM, D_MODEL), lambda i: (0, 0))],
        out_shape=[jax.ShapeDtypeStruct((T_PROMPT, D_MODEL), F32),
                   jax.ShapeDtypeStruct((T_SAMPLE, D_MODEL), F32)],
        compiler_params=params,
        name="moe_unsort_final",
    )(x, ys, meta, final_g)


def _moe(x, norm_g, wr_hi, wr_lo, wg, wu, wrem, wd, final_g=None):
    hs, meta, off, cnt = _moe_sort(x, norm_g, wr_hi, wr_lo)
    off = off[:, 0, :N_EXPERTS].reshape(-1)
    cnt = cnt[:, 0, :N_EXPERTS].reshape(-1)
    ys = _moe_experts(off, cnt, hs, wg, wu, wrem, wd)
    return _moe_unsort(x, ys, meta, final_g)


def kernel(x_prompt, x_sample, cache_win_k, cache_win_v, attn_norm_g, w_in, b_in, chunk_ln_g, chunk_ln_b, w_spatial, b_spatial, attn_sinks, mix_norm_a_g, mix_norm_b_g, w_out, ffn_norm_g, ffn_w_gate, ffn_w_up, ffn_w_down, router_w, expert_w_gate, expert_w_up, expert_w_down, final_norm_g):
    w_in_p = jnp.concatenate([w_in[..., :Q0], _g_major(w_in[..., Q0:K0], -1), w_in[..., K0:]], axis=-1).astype(BF16)
    b_in_p = jnp.concatenate([b_in[..., :Q0], _g_major(b_in[..., Q0:K0], -1), b_in[..., K0:]], axis=-1)
    w_out_p = jnp.concatenate([w_out[:, :GMLP_WIDTH], _g_major(w_out[:, GMLP_WIDTH:], 1)], axis=1).astype(BF16)
    g_b_p = _g_major(mix_norm_b_g, 1)
    bs_prompt = jnp.repeat(jnp.transpose(b_spatial, (0, 2, 1)), GMLP_HEAD_DIM, axis=2)
    bs_sample = jnp.tile(bs_prompt[:, :DEC_SEQ], (1, 2, 1))
    t_idx = np.arange(DEC_SEQ)
    ws_small = jnp.stack([w_spatial[:, :, t_idx, np.maximum(t_idx - d, 0)] for d in range(DEC_SEQ)], axis=1)
    ws_sample = jnp.tile(jnp.repeat(jnp.transpose(ws_small, (0, 1, 3, 2)), GMLP_HEAD_DIM, axis=3),
                         (1, 1, 2, 1))
    sink_row = jnp.repeat(attn_sinks, SEQS_PER_TILE * DEC_SEQ, axis=1)[:, None, :]
    bias_prompt = jnp.asarray(_prompt_bias())
    bias_sample = jnp.asarray(_sample_bias())
    wr_pad = jnp.pad(router_w, ((0, 0), (0, 0), (0, LANES - N_EXPERTS)))
    wr_hi = wr_pad.astype(BF16)
    wr_lo = (wr_pad - wr_hi.astype(F32)).astype(BF16)
    ffn_wg, ffn_wu, ffn_wd = ffn_w_gate.astype(BF16), ffn_w_up.astype(BF16), ffn_w_down.astype(BF16)
    ex_wg, ex_wu = expert_w_gate.astype(BF16), expert_w_up.astype(BF16)
    ex_wrem = jnp.concatenate([ex_wg[..., FF_MAIN:], ex_wu[..., FF_MAIN:]], axis=-1)
    ex_wd = expert_w_down.astype(BF16)

    ck = cache_win_k.reshape(DEPTH, DEC_BATCH, CACHE_W, KV_WIDTH)
    cv = cache_win_v.reshape(DEPTH, DEC_BATCH, CACHE_W, KV_WIDTH)

    x = None
    kp, vp, ksm, vsm, cvs = [], [], [], [], []
    for l in range(DEPTH):
        lw = dict(
            norm_g=attn_norm_g[l][None], w_in=w_in_p[l], b_in=b_in_p[l][None],
            ln_g=chunk_ln_g[l].reshape(1, GMLP_WIDTH), ln_b=chunk_ln_b[l].reshape(1, GMLP_WIDTH),
            w_s=w_spatial[l], b_s_prompt=bs_prompt[l], w_s_sample=ws_sample[l], b_s_sample=bs_sample[l],
            sinks=attn_sinks[l], sink_row=sink_row[l], bias_prompt=bias_prompt, bias_sample=bias_sample,
            g_a=mix_norm_a_g[l][None], g_b=g_b_p[l][None], w_out=w_out_p[l])
        if l == 0:
            x_new, k_p, v_p = _mixer_prompt(x_prompt.reshape(T_PROMPT, D_MODEL), lw)
            x_new, k_s, v_s, cv_s = _mixer_sample(x_sample.reshape(T_SAMPLE, D_MODEL), 0, x_new, ck, cv, l, lw)
        else:
            x_new, k_p, v_p = _mixer_prompt(x, lw)
            x_new, k_s, v_s, cv_s = _mixer_sample(x, NT_PROMPT, x_new, ck, cv, l, lw)
        kp.append(k_p)
        vp.append(v_p)
        ksm.append(k_s)
        vsm.append(v_s)
        cvs.append(cv_s)
        i = l // 2
        if l % 2 == 0:
            x = _swiglu(x_new, ffn_norm_g[l][None], ffn_wg[i], ffn_wu[i], ffn_wd[i])
        else:
            final_g = final_norm_g[None] if l == DEPTH - 1 else None
            x = _moe(x_new, ffn_norm_g[l][None], wr_hi[i], wr_lo[i], ex_wg[i], ex_wu[i], ex_wrem[i], ex_wd[i], final_g)
    y_p, y_s = x
    win_p = (DEPTH, BATCH, WINDOW, KV_HEADS, HEAD_DIM)
    win_s = (DEPTH, DEC_BATCH, CACHE_W, KV_HEADS, HEAD_DIM)
    return (y_p.reshape(BATCH, SEQ, D_MODEL), y_s.reshape(DEC_BATCH, DEC_SEQ, D_MODEL),
            jnp.stack(kp).reshape(win_p), jnp.stack(vp).reshape(win_p),
            jnp.stack(ksm).reshape(win_s), jnp.stack(vsm).reshape(win_s),
            jnp.stack(cvs).reshape(DEPTH, DEC_BATCH, DEC_SEQ, GMLP_HEADS, GMLP_HEAD_DIM))
```

```python
import functools

import numpy as np
import jax
import jax.numpy as jnp
from jax import lax
from jax.experimental import pallas as pl
from jax.experimental.pallas import tpu as pltpu

D_MODEL = 1024
BATCH = 4
SEQ = 4096
DEPTH = 4
DEC_BATCH = 128
DEC_SEQ = 4
HEAD_DIM = 64
ATTN_HEADS = 8
KV_HEADS = 2
GQA_GROUP = ATTN_HEADS // KV_HEADS
ATTN_WIDTH = ATTN_HEADS * HEAD_DIM
KV_WIDTH = KV_HEADS * HEAD_DIM
WINDOW = 128
CHUNK = 128
GMLP_HEADS = 4
GMLP_HEAD_DIM = 128
GMLP_WIDTH = GMLP_HEADS * GMLP_HEAD_DIM
IN_WIDTH = 2 * GMLP_WIDTH + ATTN_WIDTH + 2 * KV_WIDTH
D_FF = 2816
N_EXPERTS = 8
D_FF_EXPERT = 1408
CACHE_W = 128
EPS = 1e-6

LANES = 128
TM = 512
T_PROMPT = BATCH * SEQ
T_SAMPLE = DEC_BATCH * DEC_SEQ
T_ALL = T_PROMPT + T_SAMPLE
NT_PROMPT = T_PROMPT // TM
NT_ALL = T_ALL // TM
TILES_PER_SEQ = SEQ // TM
BLOCKS_PER_TILE = TM // WINDOW
SEQ_GROUP = 16
SEQS_PER_TILE = 4
N_SEQ_GROUPS = DEC_BATCH // SEQ_GROUP
KEYS_PAD = CACHE_W + 8
PIECE = 16
SORT_ROWS = 2 * TM + N_EXPERTS * PIECE
TILES_PER_SUPER = 11
N_SUPER = NT_ALL // TILES_PER_SUPER
SUPER_ROWS = TILES_PER_SUPER * SORT_ROWS
SORT_TILES_PER_STEP = 3
CHUNK_ROWS = 256
FF_REM = D_FF_EXPERT % 256
FF_MAIN = D_FF_EXPERT - FF_REM
PIECES_PER_CHUNK = CHUNK_ROWS // PIECE
VMEM_LIMIT = 56 * 1024 * 1024

F32 = jnp.float32
BF16 = jnp.bfloat16
NEG_INF = float("-inf")

Q0 = 2 * GMLP_WIDTH
K0 = Q0 + ATTN_WIDTH
V0 = K0 + KV_WIDTH


def _g_major(a, axis):
    axis = axis % a.ndim
    shape = a.shape
    a = a.reshape(shape[:axis] + (KV_HEADS, GQA_GROUP, HEAD_DIM) + shape[axis + 1:])
    return jnp.swapaxes(a, axis, axis + 1).reshape(shape)


def _slope(kv, g):
    return 2.0 ** (-8.0 * (kv * GQA_GROUP + g + 1) / ATTN_HEADS)


def _prompt_bias():
    c = np.arange(WINDOW)[:, None]
    qi = np.arange(WINDOW)[None, :]
    dist = np.where(c > qi, qi + WINDOW - c, qi - c)
    out = np.zeros((2, KV_HEADS, WINDOW, GQA_GROUP * WINDOW), np.float32)
    for var in range(2):
        ok = ~((var == 1) & (c > qi))
        for kv in range(KV_HEADS):
            for g in range(GQA_GROUP):
                out[var, kv, :, g * WINDOW:(g + 1) * WINDOW] = np.where(ok, -_slope(kv, g) * dist, NEG_INF)
    return out


def _sample_bias():
    out = np.zeros((KEYS_PAD, LANES), np.float32)
    j = np.arange(KEYS_PAD)
    kpos = j - CACHE_W
    for kv in range(KV_HEADS):
        for g in range(GQA_GROUP):
            for s in range(SEQS_PER_TILE):
                for t in range(DEC_SEQ):
                    col = ((kv * GQA_GROUP + g) * SEQS_PER_TILE + s) * DEC_SEQ + t
                    dist = t - kpos
                    ok = (dist >= 0) & (dist < WINDOW) & (j < CACHE_W + DEC_SEQ)
                    out[:, col] = np.where(ok, -_slope(kv, g) * dist, NEG_INF)
    return out


def _rms(x, g):
    return x * lax.rsqrt(jnp.mean(x * x, axis=-1, keepdims=True) + EPS) * g


def _gelu(x):
    return 0.5 * x * (1.0 + lax.erf(x * (2.0 ** -0.5)))


def _gmlp_v(z_v, lng_ref, lnb_ref):
    out = []
    for hh in range(GMLP_HEADS):
        sl = slice(hh * LANES, (hh + 1) * LANES)
        vh = _gelu(z_v[:, sl])
        vc = vh - jnp.mean(vh, axis=-1, keepdims=True)
        y = vc * lax.rsqrt(jnp.mean(vc * vc, axis=-1, keepdims=True) + EPS)
        out.append(y * lng_ref[:, sl] + lnb_ref[:, sl])
    return out


def _dot_nt(a, b):
    return lax.dot_general(a, b, (((1,), (1,)), ((), ())), preferred_element_type=F32)


def _mixer_prompt_kernel(*refs):
    i = pl.program_id(0)

    @pl.when(i < NT_PROMPT)
    def _():
        _mixer_prompt_tile(*refs)

    @pl.when(i == NT_PROMPT)
    def _():
        xo_ref = refs[13]
        xo_ref[...] = jnp.zeros_like(xo_ref)


def _mixer_prompt_tile(x_ref, ng_ref, win_ref, bin_ref, lng_ref, lnb_ref, ws_ref, bs_ref,
                       sink_ref, bias_ref, ga_ref, gb_ref, wout_ref,
                       xo_ref, ks_ref, vs_ref, kprev_ref, vprev_ref):
    i = pl.program_id(0)

    @pl.when(i == 0)
    def _():
        kprev_ref[...] = jnp.zeros_like(kprev_ref)
        vprev_ref[...] = jnp.zeros_like(vprev_ref)

    x = x_ref[...]
    h = _rms(x, ng_ref[...]).astype(BF16)

    def in_proj(lo, hi):
        return jnp.dot(h, win_ref[:, lo:hi], preferred_element_type=F32) + bin_ref[:, lo:hi]

    z_att = in_proj(Q0, IN_WIDTH)
    q = (z_att[:, :ATTN_WIDTH] * (HEAD_DIM ** -0.5)).astype(BF16)
    k = z_att[:, ATTN_WIDTH:ATTN_WIDTH + KV_WIDTH]
    val = z_att[:, ATTN_WIDTH + KV_WIDTH:]
    k_bf = k.astype(BF16)
    lane = lax.broadcasted_iota(jnp.int32, (1, LANES), 1)
    low = lane < HEAD_DIM
    first = jnp.where(i % TILES_PER_SEQ == 0, 1, 0)
    c_i = lax.broadcasted_iota(jnp.int32, (WINDOW, WINDOW), 0)
    q_i = lax.broadcasted_iota(jnp.int32, (WINDOW, WINDOW), 1)
    from_prev = jnp.concatenate([c_i > q_i] * GQA_GROUP, axis=1)
    g_lane = lax.broadcasted_iota(jnp.int32, (1, GQA_GROUP * WINDOW), 1) // WINDOW
    top = lax.broadcasted_iota(jnp.int32, (KV_WIDTH, 1), 0) < HEAD_DIM
    one = jnp.ones((), BF16)
    vt_bf = val.T.astype(BF16)
    vt_all = jnp.concatenate([vprev_ref[...], vt_bf], axis=1)
    vt_aug = [jnp.where(top, vt_all, one), jnp.where(top, one, vt_all)]
    k_all = jnp.concatenate([kprev_ref[...], k_bf], axis=0)

    zero = jnp.zeros((), BF16)
    sinks = []
    for kv in range(KV_HEADS):
        sink = jnp.full((1, GQA_GROUP * WINDOW), sink_ref[kv * GQA_GROUP], F32)
        for g in range(1, GQA_GROUP):
            sink = jnp.where(g_lane == g, sink_ref[kv * GQA_GROUP + g], sink)
        sinks.append(sink)

    def attention_block(c):
        rows = slice(c * WINDOW, (c + 1) * WINDOW)
        band = slice(c * WINDOW, (c + 2) * WINDOW)
        o_t = []
        for kv in range(KV_HEADS):
            keep = low if kv == 0 else jnp.logical_not(low)
            qm = jnp.concatenate(
                [jnp.where(keep, q[rows, g * LANES:(g + 1) * LANES], zero) for g in range(GQA_GROUP)],
                axis=0)
            s2 = _dot_nt(k_all[band], qm)
            bias = bias_ref[first, kv] if c == 0 else bias_ref[0, kv]
            s = jnp.where(from_prev, s2[:WINDOW], s2[WINDOW:]) + bias
            m = jnp.maximum(jnp.max(s, axis=0, keepdims=True), sinks[kv])
            p = jnp.exp(s - m)
            p2 = jnp.concatenate([jnp.where(from_prev, p, 0.0), jnp.where(from_prev, 0.0, p)],
                                 axis=0).astype(BF16)
            o = jnp.dot(vt_aug[kv][:, band], p2, preferred_element_type=F32)
            own, ones_row = (o[:HEAD_DIM], o[HEAD_DIM:HEAD_DIM + 1]) if kv == 0 else (o[HEAD_DIM:], o[0:1])
            o_t.append(own / (ones_row + jnp.exp(sinks[kv] - m)))
        bt = jnp.concatenate(o_t, axis=0)
        return jnp.concatenate(
            [bt[:, g * WINDOW:(g + 1) * WINDOW].T for g in range(GQA_GROUP)], axis=1)

    z_u = in_proj(0, GMLP_WIDTH)
    b_rows = [attention_block(0)]
    z_v = in_proj(GMLP_WIDTH, Q0)
    b_rows.append(attention_block(1))
    u = _gelu(z_u)
    b_rows.append(attention_block(2))
    v_heads = _gmlp_v(z_v, lng_ref, lnb_ref)
    row = lax.broadcasted_iota(jnp.int32, (CHUNK, CHUNK), 0)
    col = lax.broadcasted_iota(jnp.int32, (CHUNK, CHUNK), 1)
    a_cols = []
    for hh in range(GMLP_HEADS):
        w = jnp.where(row >= col, ws_ref[hh], 0.0).astype(BF16)
        v_cat = jnp.concatenate(
            [v_heads[hh][c * CHUNK:(c + 1) * CHUNK] for c in range(BLOCKS_PER_TILE)], axis=1)
        mixed = jnp.dot(w, v_cat.astype(BF16), preferred_element_type=F32)
        bias = bs_ref[:, hh * LANES:(hh + 1) * LANES]
        a_cols.append(jnp.concatenate(
            [mixed[:, c * LANES:(c + 1) * LANES] + bias for c in range(BLOCKS_PER_TILE)], axis=0))
    a = u * jnp.concatenate(a_cols, axis=1)
    a_n = _rms(a, ga_ref[...]).astype(BF16)
    b_rows.append(attention_block(3))
    out = x + jnp.dot(a_n, wout_ref[:GMLP_WIDTH, :], preferred_element_type=F32)
    b_n = _rms(jnp.concatenate(b_rows, axis=0), gb_ref[...]).astype(BF16)
    xo_ref[...] = out + jnp.dot(b_n, wout_ref[GMLP_WIDTH:, :], preferred_element_type=F32)

    kprev_ref[...] = k_bf[TM - WINDOW:]
    vprev_ref[...] = vt_bf[:, TM - WINDOW:]
    ks_ref[0] = k[TM - WINDOW:]
    vs_ref[0] = val[TM - WINDOW:]


def _layer_block(layer, *shape, **kwargs):
    return pl.BlockSpec((None,) + shape, lambda *_: (layer,) + (0,) * len(shape), **kwargs)


def _mixer_prompt(x, lw, layer):
    const = lambda *shape: pl.BlockSpec(shape, lambda i: (0,) * len(shape))
    per_layer = functools.partial(_layer_block, layer)
    return pl.pallas_call(
        _mixer_prompt_kernel,
        grid=(NT_ALL,),
        in_specs=[
            pl.BlockSpec((TM, D_MODEL), lambda i: (jnp.minimum(i, NT_PROMPT - 1), 0)),
            per_layer(1, D_MODEL), per_layer(D_MODEL, IN_WIDTH), per_layer(1, IN_WIDTH),
            per_layer(1, GMLP_WIDTH), per_layer(1, GMLP_WIDTH),
            per_layer(GMLP_HEADS, CHUNK, CHUNK), per_layer(CHUNK, GMLP_WIDTH),
            pl.BlockSpec(memory_space=pltpu.SMEM),
            const(2, KV_HEADS, WINDOW, GQA_GROUP * WINDOW),
            per_layer(1, GMLP_WIDTH), per_layer(1, ATTN_WIDTH), per_layer(D_MODEL, D_MODEL),
        ],
        out_specs=[
            pl.BlockSpec((TM, D_MODEL), lambda i: (i, 0)),
            pl.BlockSpec((1, WINDOW, KV_WIDTH), lambda i: (jnp.minimum(i // TILES_PER_SEQ, BATCH - 1), 0, 0)),
            pl.BlockSpec((1, WINDOW, KV_WIDTH), lambda i: (jnp.minimum(i // TILES_PER_SEQ, BATCH - 1), 0, 0)),
        ],
        out_shape=[
            jax.ShapeDtypeStruct((T_ALL, D_MODEL), F32),
            jax.ShapeDtypeStruct((BATCH, WINDOW, KV_WIDTH), F32),
            jax.ShapeDtypeStruct((BATCH, WINDOW, KV_WIDTH), F32),
        ],
        scratch_shapes=[pltpu.VMEM((WINDOW, KV_WIDTH), BF16), pltpu.VMEM((WINDOW, KV_WIDTH), BF16)],
        compiler_params=pltpu.CompilerParams(
            dimension_semantics=("arbitrary",), vmem_limit_bytes=VMEM_LIMIT),
        name="mixer_prompt",
    )(x, lw["norm_g"], lw["w_in"], lw["b_in"], lw["ln_g"], lw["ln_b"], lw["w_s"], lw["b_s_prompt"],
      lw["sinks"], lw["bias_prompt"], lw["g_a"], lw["g_b"], lw["w_out"])


def _mixer_sample_kernel(x_ref, xalias_ref, ck_ref, cv_ref, ng_ref, win_ref, bin_ref, lng_ref, lnb_ref,
                         wsd_ref, bsd_ref, sinkrow_ref, bias_ref, ga_ref, gb_ref, wout_ref,
                         xo_ref, ko_ref, vo_ref, cvo_ref,
                         q_s, k_s, v_s, a_s, b_s, kfull, vfull):
    del xalias_ref
    j = pl.program_id(0)

    @pl.when(j == 0)
    def _():
        h = _rms(x_ref[...], ng_ref[...]).astype(BF16)
        z = jnp.dot(h, win_ref[...], preferred_element_type=F32) + bin_ref[...]
        u = _gelu(z[:, :GMLP_WIDTH])
        v = jnp.concatenate(_gmlp_v(z[:, GMLP_WIDTH:Q0], lng_ref, lnb_ref), axis=1)
        cvo_ref[...] = v
        t_row = lax.broadcasted_iota(jnp.int32, (8, 1), 0) % DEC_SEQ

        def tile_rows(pat):
            return jnp.broadcast_to(pat[None], (TM // 8,) + pat.shape).reshape(TM, pat.shape[-1])

        mixed = tile_rows(bsd_ref[...])
        for d in range(DEC_SEQ):
            vd = v if d == 0 else pltpu.roll(v, d, 0)
            mixed = mixed + tile_rows(jnp.where(t_row >= d, wsd_ref[d], 0.0)) * vd
        a_s[...] = _rms(u * mixed, ga_ref[...]).astype(BF16)
        q_s[...] = z[:, Q0:K0] * (HEAD_DIM ** -0.5)
        k_s[...] = z[:, K0:V0]
        v_s[...] = z[:, V0:]
        kfull[:, CACHE_W + DEC_SEQ:, :] = jnp.zeros((SEQ_GROUP, KEYS_PAD - CACHE_W - DEC_SEQ, KV_WIDTH), F32)
        vfull[:, CACHE_W + DEC_SEQ:, :] = jnp.zeros((SEQ_GROUP, KEYS_PAD - CACHE_W - DEC_SEQ, KV_WIDTH), F32)

    base = pl.multiple_of(j * (SEQ_GROUP * DEC_SEQ), SEQ_GROUP * DEC_SEQ)
    for kv in range(KV_HEADS):
        head = slice(kv * HEAD_DIM, (kv + 1) * HEAD_DIM)
        kfull[:, :CACHE_W, head] = ck_ref[0, :, pl.ds(kv, CACHE_W, stride=KV_HEADS), :]
        vfull[:, :CACHE_W, head] = cv_ref[0, :, pl.ds(kv, CACHE_W, stride=KV_HEADS), :]
    k_new = k_s[pl.ds(base, SEQ_GROUP * DEC_SEQ), :]
    v_new = v_s[pl.ds(base, SEQ_GROUP * DEC_SEQ), :]
    for s in range(SEQ_GROUP):
        kfull[s, CACHE_W:CACHE_W + DEC_SEQ, :] = k_new[s * DEC_SEQ:(s + 1) * DEC_SEQ]
        vfull[s, CACHE_W:CACHE_W + DEC_SEQ, :] = v_new[s * DEC_SEQ:(s + 1) * DEC_SEQ]
    k_win = kfull[:, DEC_SEQ:DEC_SEQ + CACHE_W, :]
    v_win = vfull[:, DEC_SEQ:DEC_SEQ + CACHE_W, :]
    for kv in range(KV_HEADS):
        head = slice(kv * HEAD_DIM, (kv + 1) * HEAD_DIM)
        ko_ref[0, :, pl.ds(kv, CACHE_W, stride=KV_HEADS), :] = k_win[:, :, head]
        vo_ref[0, :, pl.ds(kv, CACHE_W, stride=KV_HEADS), :] = v_win[:, :, head]

    lane = lax.broadcasted_iota(jnp.int32, (1, LANES), 1)
    low = lane < HEAD_DIM
    col_seq = (lane // DEC_SEQ) % SEQS_PER_TILE
    row_seq = (lax.broadcasted_iota(jnp.int32, (LANES, 1), 0) // DEC_SEQ) % SEQS_PER_TILE
    sink = sinkrow_ref[...]
    zero = jnp.zeros((), BF16)
    quad_rows = SEQS_PER_TILE * DEC_SEQ
    for m in range(SEQ_GROUP // SEQS_PER_TILE):
        r0 = pl.multiple_of(base + m * quad_rows, quad_rows)
        q16 = q_s[pl.ds(r0, quad_rows), :]
        qt = jnp.concatenate(
            [jnp.where(low if kv == 0 else jnp.logical_not(low), q16[:, g * LANES:(g + 1) * LANES], 0.0)
             for kv in range(KV_HEADS) for g in range(GQA_GROUP)], axis=0).astype(BF16)
        s = bias_ref[...]
        for n in range(SEQS_PER_TILE):
            kn = kfull[SEQS_PER_TILE * m + n].astype(BF16)
            s = s + _dot_nt(kn, jnp.where(row_seq == n, qt, zero))
        mx = jnp.maximum(jnp.max(s, axis=0, keepdims=True), sink)
        p = jnp.exp(s - mx)
        p = (p / (jnp.sum(p, axis=0, keepdims=True) + jnp.exp(sink - mx))).astype(BF16)
        o = jnp.zeros((LANES, KV_WIDTH), F32)
        for n in range(SEQS_PER_TILE):
            vn = vfull[SEQS_PER_TILE * m + n].astype(BF16)
            o = o + lax.dot_general(jnp.where(col_seq == n, p, zero), vn, (((0,), (0,)), ((), ())),
                                    preferred_element_type=F32)
        b_s[pl.ds(r0, quad_rows), :] = jnp.concatenate(
            [jnp.where(low, o[g * quad_rows:(g + 1) * quad_rows],
                       o[(GQA_GROUP + g) * quad_rows:(GQA_GROUP + g + 1) * quad_rows])
             for g in range(GQA_GROUP)], axis=1)

    @pl.when(j == N_SEQ_GROUPS - 1)
    def _():
        b_n = _rms(b_s[...], gb_ref[...]).astype(BF16)
        merged = jnp.concatenate([a_s[...], b_n], axis=1)
        xo_ref[...] = x_ref[...] + jnp.dot(merged, wout_ref[...], preferred_element_type=F32)


def _mixer_sample(x, x_block, x_buf, cache_k, cache_v, layer, lw):
    const = lambda *shape: pl.BlockSpec(shape, lambda j: (0,) * len(shape))
    per_layer = functools.partial(_layer_block, layer)
    cache_spec = pl.BlockSpec((1, SEQ_GROUP, KV_HEADS * CACHE_W, HEAD_DIM), lambda j: (layer, j, 0, 0))
    state_spec = pl.BlockSpec((1, SEQ_GROUP, KV_HEADS * CACHE_W, HEAD_DIM), lambda j: (j, 0, 0, 0))
    state_shape = jax.ShapeDtypeStruct((N_SEQ_GROUPS, SEQ_GROUP, KV_HEADS * CACHE_W, HEAD_DIM), F32)
    return pl.pallas_call(
        _mixer_sample_kernel,
        grid=(N_SEQ_GROUPS,),
        in_specs=[
            pl.BlockSpec((TM, D_MODEL), lambda j: (x_block, 0)),
            pl.BlockSpec(memory_space=pl.ANY),
            cache_spec, cache_spec,
            per_layer(1, D_MODEL), per_layer(D_MODEL, IN_WIDTH), per_layer(1, IN_WIDTH),
            per_layer(1, GMLP_WIDTH), per_layer(1, GMLP_WIDTH),
            per_layer(DEC_SEQ, 8, GMLP_WIDTH), per_layer(8, GMLP_WIDTH),
            per_layer(1, LANES), const(KEYS_PAD, LANES),
            per_layer(1, GMLP_WIDTH), per_layer(1, ATTN_WIDTH), per_layer(D_MODEL, D_MODEL),
        ],
        out_specs=[
            pl.BlockSpec((TM, D_MODEL), lambda j: (NT_PROMPT, 0)),
            state_spec, state_spec,
            const(TM, GMLP_WIDTH),
        ],
        out_shape=[
            jax.ShapeDtypeStruct((T_ALL, D_MODEL), F32),
            state_shape, state_shape,
            jax.ShapeDtypeStruct((TM, GMLP_WIDTH), F32),
        ],
        scratch_shapes=[
            pltpu.VMEM((TM, ATTN_WIDTH), F32), pltpu.VMEM((TM, KV_WIDTH), F32),
            pltpu.VMEM((TM, KV_WIDTH), F32), pltpu.VMEM((TM, GMLP_WIDTH), BF16),
            pltpu.VMEM((TM, ATTN_WIDTH), F32),
            pltpu.VMEM((SEQ_GROUP, KEYS_PAD, KV_WIDTH), F32),
            pltpu.VMEM((SEQ_GROUP, KEYS_PAD, KV_WIDTH), F32),
        ],
        input_output_aliases={1: 0},
        compiler_params=pltpu.CompilerParams(
            dimension_semantics=("arbitrary",), vmem_limit_bytes=VMEM_LIMIT),
        name="mixer_sample",
    )(x, x_buf, cache_k, cache_v, lw["norm_g"], lw["w_in"], lw["b_in"], lw["ln_g"], lw["ln_b"],
      lw["w_s_sample"], lw["b_s_sample"], lw["sink_row"], lw["bias_sample"],
      lw["g_a"], lw["g_b"], lw["w_out"])


def _swiglu_kernel(x_ref, ng_ref, wg_ref, wu_ref, wd_ref, xo_ref):
    x = x_ref[...]
    h = _rms(x, ng_ref[...]).astype(BF16)
    gate = jnp.dot(h, wg_ref[...], preferred_element_type=F32)
    up = jnp.dot(h, wu_ref[...], preferred_element_type=F32)
    act = (jax.nn.silu(gate) * up).astype(BF16)
    xo_ref[...] = x + jnp.dot(act, wd_ref[...], preferred_element_type=F32)


def _swiglu(x, norm_g, wg, wu, wd, layer, ffn_index):
    weights = functools.partial(_layer_block, ffn_index, pipeline_mode=pl.Buffered(1))
    return pl.pallas_call(
        _swiglu_kernel,
        grid=(NT_ALL,),
        in_specs=[
            pl.BlockSpec((TM, D_MODEL), lambda i: (i, 0)),
            _layer_block(layer, 1, D_MODEL),
            weights(D_MODEL, D_FF), weights(D_MODEL, D_FF), weights(D_FF, D_MODEL),
        ],
        out_specs=pl.BlockSpec((TM, D_MODEL), lambda i: (i, 0)),
        out_shape=jax.ShapeDtypeStruct((T_ALL, D_MODEL), F32),
        compiler_params=pltpu.CompilerParams(
            dimension_semantics=("arbitrary",), vmem_limit_bytes=VMEM_LIMIT),
        name="swiglu",
    )(x, norm_g, wg, wu, wd)


def _moe_sort_kernel(x_ref, ng_ref, wrh_ref, wrl_ref, hs_ref, meta_ref, off_ref, cnt_ref):
    n = SORT_TILES_PER_STEP
    logits, routes = {}, {}
    for step in range(n + 2):
        if step < n:
            logits[step] = _sort_logits(x_ref[step * TM:(step + 1) * TM, :], ng_ref, wrh_ref, wrl_ref)
        j = step - 1
        if 0 <= j < n:
            h_hi, lg = logits.pop(j)
            meta, off, cnt_pad = _sort_route(lg)
            meta_ref[j * TM:(j + 1) * TM, :] = meta
            off_ref[j] = off.astype(jnp.int32)
            cnt_ref[j] = cnt_pad.astype(jnp.int32)
            routes[j] = (h_hi, meta)
        j = step - 2
        if 0 <= j < n:
            hs_ref[j * SORT_ROWS:(j + 1) * SORT_ROWS, :] = _sort_gather(*routes.pop(j))


def _sort_logits(x, ng_ref, wrh_ref, wrl_ref):
    h = _rms(x, ng_ref[...])
    h_hi = h.astype(BF16)
    h_lo = (h - h_hi.astype(F32)).astype(BF16)
    logits = (jnp.dot(h_hi, wrh_ref[...], preferred_element_type=F32)
              + jnp.dot(h_lo, wrh_ref[...], preferred_element_type=F32)
              + jnp.dot(h_hi, wrl_ref[...], preferred_element_type=F32))
    return h_hi, logits


def _sort_route(logits):
    lane = lax.broadcasted_iota(jnp.int32, (1, LANES), 1)
    logits = jnp.where(lane < N_EXPERTS, logits, NEG_INF)
    m1 = jnp.max(logits, axis=-1, keepdims=True)
    i1 = jnp.min(jnp.where(logits == m1, lane, LANES), axis=-1, keepdims=True)
    rest = jnp.where(lane == i1, NEG_INF, logits)
    m2 = jnp.max(rest, axis=-1, keepdims=True)
    i2 = jnp.min(jnp.where(rest == m2, lane, LANES), axis=-1, keepdims=True)
    e2 = jnp.exp(m2 - m1)
    den = 1.0 + e2
    sel0 = lane == i1
    sel1 = lane == i2
    routed = jnp.where(jnp.logical_or(sel0, sel1), 1.0, 0.0)
    r_i = lax.broadcasted_iota(jnp.int32, (TM, TM), 0)
    c_i = lax.broadcasted_iota(jnp.int32, (TM, TM), 1)
    lower = jnp.where(r_i > c_i, 1.0, 0.0).astype(BF16)
    rank = jnp.dot(lower, routed.astype(BF16), preferred_element_type=F32)
    cnt = jnp.broadcast_to(jnp.sum(routed, axis=0, keepdims=True), (8, LANES))
    cnt_pad = jnp.floor((cnt + (PIECE - 1)) * (1.0 / PIECE)) * PIECE
    incl = cnt_pad
    for sh in (1, 2, 4):
        incl = incl + jnp.where(lane >= sh, pltpu.roll(incl, sh, 1), 0.0)
    off = incl - cnt_pad
    pos = off[0:1] + rank
    pos0 = jnp.sum(jnp.where(sel0, pos, 0.0), axis=-1, keepdims=True)
    pos1 = jnp.sum(jnp.where(sel1, pos, 0.0), axis=-1, keepdims=True)
    meta = jnp.where(lane == 0, pos0, jnp.where(lane == 1, pos1,
                     jnp.where(lane == 2, 1.0 / den, jnp.where(lane == 3, e2 / den, 0.0))))
    return meta, off, cnt_pad


def _sort_gather(h_hi, meta):
    meta_t = meta.T
    p_row = lax.broadcasted_iota(jnp.int32, (SORT_ROWS, 1), 0).astype(F32)
    hit = jnp.logical_or(p_row == meta_t[0:1], p_row == meta_t[1:2])
    gather = jnp.where(hit, 1.0, 0.0).astype(BF16)
    return jnp.dot(gather, h_hi, preferred_element_type=F32).astype(BF16)


def _moe_sort(x, norm_g, wr_hi, wr_lo, layer, moe_index):
    return pl.pallas_call(
        _moe_sort_kernel,
        grid=(NT_ALL // SORT_TILES_PER_STEP,),
        in_specs=[
            pl.BlockSpec((SORT_TILES_PER_STEP * TM, D_MODEL), lambda i: (i, 0)),
            _layer_block(layer, 1, D_MODEL),
            _layer_block(moe_index, D_MODEL, LANES),
            _layer_block(moe_index, D_MODEL, LANES),
        ],
        out_specs=[
            pl.BlockSpec((SORT_TILES_PER_STEP * SORT_ROWS, D_MODEL), lambda i: (i, 0)),
            pl.BlockSpec((SORT_TILES_PER_STEP * TM, LANES), lambda i: (i, 0)),
            pl.BlockSpec((SORT_TILES_PER_STEP, 8, LANES), lambda i: (i, 0, 0)),
            pl.BlockSpec((SORT_TILES_PER_STEP, 8, LANES), lambda i: (i, 0, 0)),
        ],
        out_shape=[
            jax.ShapeDtypeStruct((NT_ALL * SORT_ROWS, D_MODEL), BF16),
            jax.ShapeDtypeStruct((T_ALL, LANES), F32),
            jax.ShapeDtypeStruct((NT_ALL, 8, LANES), jnp.int32),
            jax.ShapeDtypeStruct((NT_ALL, 8, LANES), jnp.int32),
        ],
        compiler_params=pltpu.CompilerParams(
            dimension_semantics=("arbitrary",), vmem_limit_bytes=VMEM_LIMIT),
        name="moe_sort",
    )(x, norm_g, wr_hi, wr_lo)


def _moe_experts_kernel(off_ref, cnt_ref, hs_hbm, wg_ref, wu_ref, wrem_ref, wd_ref, ys_hbm,
                        buf, lhs, sem_in, sem_out):
    s = pl.program_id(0)
    e = pl.program_id(1)
    row0 = pl.multiple_of(s * SUPER_ROWS, PIECE)
    is_first = e == 0
    is_last = e == N_EXPERTS - 1

    def copy_in(t):
        r = pl.multiple_of(t * SORT_ROWS, PIECE)
        return pltpu.make_async_copy(hs_hbm.at[pl.ds(row0 + r, SORT_ROWS)], buf.at[pl.ds(r, SORT_ROWS)],
                                     sem_in.at[t])

    def copy_out(t):
        r = pl.multiple_of(t * SORT_ROWS, PIECE)
        return pltpu.make_async_copy(buf.at[pl.ds(r, SORT_ROWS)], ys_hbm.at[pl.ds(row0 + r, SORT_ROWS)],
                                     sem_out.at[t])

    def for_tiles(lo, hi, fn):
        hi = jnp.maximum(lo, hi)

        def body(t, carry):
            fn(t)
            return carry

        lax.fori_loop(lo, hi, body, 0)
        return hi

    @pl.when(is_first)
    def _():
        for t in range(TILES_PER_SUPER):
            copy_in(t).start()

    starts, cums = [], [0]
    for i in range(TILES_PER_SUPER):
        idx = (s * TILES_PER_SUPER + i) * N_EXPERTS + e
        starts.append(i * SORT_ROWS + off_ref[idx])
        cums.append(cums[-1] + cnt_ref[idx] // PIECE)
    n_pieces = cums[-1]

    def locate(p):
        pc = jnp.minimum(p, n_pieces - 1)
        row = starts[0] + PIECE * pc
        tile = jnp.int32(0)
        for i in range(1, TILES_PER_SUPER):
            inside = pc >= cums[i]
            row = jnp.where(inside, starts[i] + PIECE * (pc - cums[i]), row)
            tile = jnp.where(inside, i, tile)
        return row, tile

    def piece_rows(q):
        src, dst, tiles = [], [], []
        for j in range(PIECES_PER_CHUNK):
            p = q * PIECES_PER_CHUNK + j
            row, tile = locate(p)
            src.append(row)
            dst.append(jnp.where(p < n_pieces, row, SUPER_ROWS))
            tiles.append(tile)
        return tuple(src), tuple(dst), tiles[0], tiles[-1]

    def chunk_body(q, carry):
        (src, dst, _, last_tile), arrived, sent = carry
        arrived = for_tiles(arrived, jnp.where(is_first, last_tile + 1, 0), lambda t: copy_in(t).wait())
        for j in range(PIECES_PER_CHUNK):
            lhs[j * PIECE:(j + 1) * PIECE, :] = buf[pl.ds(pl.multiple_of(src[j], PIECE), PIECE), :]
        xs = lhs[...]
        gate = jnp.dot(xs, wg_ref[0, :, :FF_MAIN], preferred_element_type=F32)
        up = jnp.dot(xs, wu_ref[0, :, :FF_MAIN], preferred_element_type=F32)
        rem = jnp.dot(xs, wrem_ref[0], preferred_element_type=F32)
        nxt = piece_rows(q + 1)
        act = jnp.concatenate([jax.nn.silu(gate) * up,
                               jax.nn.silu(rem[:, :FF_REM]) * rem[:, FF_REM:]], axis=1).astype(BF16)
        y = jnp.dot(act, wd_ref[0], preferred_element_type=F32).astype(BF16)
        for j in range(PIECES_PER_CHUNK):
            buf[pl.ds(pl.multiple_of(dst[j], PIECE), PIECE), :] = y[j * PIECE:(j + 1) * PIECE]
        sent = for_tiles(sent, jnp.where(is_last, nxt[2], 0), lambda t: copy_out(t).start())
        return nxt, arrived, sent

    n_chunks = (n_pieces + PIECES_PER_CHUNK - 1) // PIECES_PER_CHUNK
    _, arrived, sent = lax.fori_loop(0, n_chunks, chunk_body, (piece_rows(0), jnp.int32(0), jnp.int32(0)))
    for_tiles(arrived, jnp.where(is_first, TILES_PER_SUPER, 0), lambda t: copy_in(t).wait())
    for_tiles(sent, jnp.where(is_last, TILES_PER_SUPER, 0), lambda t: copy_out(t).start())
    for_tiles(0, jnp.where(is_last, TILES_PER_SUPER, 0), lambda t: copy_out(t).wait())


def _moe_experts(off, cnt, hs, wg, wu, wrem, wd, moe_index):
    expert = lambda *shape: pl.BlockSpec((None, 1) + shape, lambda s, e, *_: (moe_index, e, 0, 0))
    grid_spec = pltpu.PrefetchScalarGridSpec(
        num_scalar_prefetch=2,
        grid=(N_SUPER, N_EXPERTS),
        in_specs=[
            pl.BlockSpec(memory_space=pl.ANY),
            expert(D_MODEL, D_FF_EXPERT), expert(D_MODEL, D_FF_EXPERT), expert(D_MODEL, 2 * FF_REM),
            expert(D_FF_EXPERT, D_MODEL),
        ],
        out_specs=pl.BlockSpec(memory_space=pl.ANY),
        scratch_shapes=[pltpu.VMEM((SUPER_ROWS + PIECE, D_MODEL), BF16),
                        pltpu.VMEM((CHUNK_ROWS, D_MODEL), BF16),
                        pltpu.SemaphoreType.DMA((TILES_PER_SUPER,)),
                        pltpu.SemaphoreType.DMA((TILES_PER_SUPER,))],
    )
    return pl.pallas_call(
        _moe_experts_kernel,
        grid_spec=grid_spec,
        out_shape=jax.ShapeDtypeStruct((NT_ALL * SORT_ROWS, D_MODEL), BF16),
        compiler_params=pltpu.CompilerParams(
            dimension_semantics=("arbitrary", "arbitrary"), vmem_limit_bytes=VMEM_LIMIT),
        name="moe_experts",
    )(off, cnt, hs, wg, wu, wrem, wd)


def _unsorted_residual(x_ref, ys_ref, meta_ref):
    meta = meta_ref[...]
    p_col = lax.broadcasted_iota(jnp.int32, (1, SORT_ROWS), 1).astype(F32)
    scatter = (jnp.where(p_col == meta[:, 0:1], meta[:, 2:3], 0.0)
               + jnp.where(p_col == meta[:, 1:2], meta[:, 3:4], 0.0)).astype(BF16)
    return x_ref[...] + jnp.dot(scatter, ys_ref[...], preferred_element_type=F32)


def _moe_unsort_kernel(x_ref, ys_ref, meta_ref, xo_ref):
    xo_ref[...] = _unsorted_residual(x_ref, ys_ref, meta_ref)


def _moe_unsort_final_kernel(x_ref, ys_ref, meta_ref, g_ref, yp_ref, ysm_ref):
    i = pl.program_id(0)
    y = _rms(_unsorted_residual(x_ref, ys_ref, meta_ref), g_ref[...])

    @pl.when(i < NT_PROMPT)
    def _():
        yp_ref[...] = y

    @pl.when(i == NT_PROMPT)
    def _():
        ysm_ref[...] = y


def _moe_unsort(x, ys, meta, final_g=None):
    in_specs = [
        pl.BlockSpec((TM, D_MODEL), lambda i: (i, 0)),
        pl.BlockSpec((SORT_ROWS, D_MODEL), lambda i: (i, 0)),
        pl.BlockSpec((TM, LANES), lambda i: (i, 0)),
    ]
    params = pltpu.CompilerParams(dimension_semantics=("arbitrary",), vmem_limit_bytes=VMEM_LIMIT)
    if final_g is None:
        return pl.pallas_call(
            _moe_unsort_kernel,
            grid=(NT_ALL,),
            in_specs=in_specs,
            out_specs=pl.BlockSpec((TM, D_MODEL), lambda i: (i, 0)),
            out_shape=jax.ShapeDtypeStruct((T_ALL, D_MODEL), F32),
            compiler_params=params,
            name="moe_unsort",
        )(x, ys, meta)
    return pl.pallas_call(
        _moe_unsort_final_kernel,
        grid=(NT_ALL,),
        in_specs=in_specs + [pl.BlockSpec((1, D_MODEL), lambda i: (0, 0))],
        out_specs=[pl.BlockSpec((TM, D_MODEL), lambda i: (jnp.minimum(i, NT_PROMPT - 1), 0)),
                   pl.BlockSpec((TM, D_MODEL), lambda i: (0, 0))],
        out_shape=[jax.ShapeDtypeStruct((T_PROMPT, D_MODEL), F32),
                   jax.ShapeDtypeStruct((T_SAMPLE, D_MODEL), F32)],
        compiler_params=params,
        name="moe_unsort_final",
    )(x, ys, meta, final_g)


def _moe(x, norm_g, wr_hi, wr_lo, wg, wu, wrem, wd, layer, moe_index, final_g=None):
    hs, meta, off, cnt = _moe_sort(x, norm_g, wr_hi, wr_lo, layer, moe_index)
    off = off[:, 0, :N_EXPERTS].reshape(-1)
    cnt = cnt[:, 0, :N_EXPERTS].reshape(-1)
    ys = _moe_experts(off, cnt, hs, wg, wu, wrem, wd, moe_index)
    return _moe_unsort(x, ys, meta, final_g)


def kernel(x_prompt, x_sample, cache_win_k, cache_win_v, attn_norm_g, w_in, b_in, chunk_ln_g, chunk_ln_b, w_spatial, b_spatial, attn_sinks, mix_norm_a_g, mix_norm_b_g, w_out, ffn_norm_g, ffn_w_gate, ffn_w_up, ffn_w_down, router_w, expert_w_gate, expert_w_up, expert_w_down, final_norm_g):
    w_in_p = jnp.concatenate([w_in[..., :Q0], _g_major(w_in[..., Q0:K0], -1), w_in[..., K0:]], axis=-1).astype(BF16)
    b_in_p = jnp.concatenate([b_in[..., :Q0], _g_major(b_in[..., Q0:K0], -1), b_in[..., K0:]], axis=-1)
    w_out_p = jnp.concatenate([w_out[:, :GMLP_WIDTH], _g_major(w_out[:, GMLP_WIDTH:], 1)], axis=1).astype(BF16)
    g_b_p = _g_major(mix_norm_b_g, 1)
    bs_prompt = jnp.repeat(jnp.transpose(b_spatial, (0, 2, 1)), GMLP_HEAD_DIM, axis=2)
    bs_sample = jnp.tile(bs_prompt[:, :DEC_SEQ], (1, 2, 1))
    t_idx = np.arange(DEC_SEQ)
    ws_small = jnp.stack([w_spatial[:, :, t_idx, np.maximum(t_idx - d, 0)] for d in range(DEC_SEQ)], axis=1)
    ws_sample = jnp.tile(jnp.repeat(jnp.transpose(ws_small, (0, 1, 3, 2)), GMLP_HEAD_DIM, axis=3),
                         (1, 1, 2, 1))
    sink_row = jnp.repeat(attn_sinks, SEQS_PER_TILE * DEC_SEQ, axis=1)[:, None, :]
    bias_prompt = jnp.asarray(_prompt_bias())
    bias_sample = jnp.asarray(_sample_bias())
    wr_pad = jnp.pad(router_w, ((0, 0), (0, 0), (0, LANES - N_EXPERTS)))
    wr_hi = wr_pad.astype(BF16)
    wr_lo = (wr_pad - wr_hi.astype(F32)).astype(BF16)
    ffn_wg, ffn_wu, ffn_wd = ffn_w_gate.astype(BF16), ffn_w_up.astype(BF16), ffn_w_down.astype(BF16)
    ex_wg, ex_wu = expert_w_gate.astype(BF16), expert_w_up.astype(BF16)
    ex_wrem = jnp.concatenate([ex_wg[..., FF_MAIN:], ex_wu[..., FF_MAIN:]], axis=-1)
    ex_wd = expert_w_down.astype(BF16)

    ck = cache_win_k.reshape(DEPTH, DEC_BATCH, CACHE_W * KV_HEADS, HEAD_DIM)
    cv = cache_win_v.reshape(DEPTH, DEC_BATCH, CACHE_W * KV_HEADS, HEAD_DIM)

    lw = dict(
        norm_g=attn_norm_g[:, None, :], w_in=w_in_p, b_in=b_in_p[:, None, :],
        ln_g=chunk_ln_g.reshape(DEPTH, 1, GMLP_WIDTH), ln_b=chunk_ln_b.reshape(DEPTH, 1, GMLP_WIDTH),
        w_s=w_spatial, b_s_prompt=bs_prompt, w_s_sample=ws_sample, b_s_sample=bs_sample,
        sink_row=sink_row, bias_prompt=bias_prompt, bias_sample=bias_sample,
        g_a=mix_norm_a_g[:, None, :], g_b=g_b_p[:, None, :], w_out=w_out_p)
    ffn_g = ffn_norm_g[:, None, :]

    x = None
    kp, vp, ksm, vsm, cvs = [], [], [], [], []
    for l in range(DEPTH):
        lw["sinks"] = attn_sinks[l]
        if l == 0:
            x_new, k_p, v_p = _mixer_prompt(x_prompt.reshape(T_PROMPT, D_MODEL), lw, l)
            x_new, k_s, v_s, cv_s = _mixer_sample(x_sample.reshape(T_SAMPLE, D_MODEL), 0, x_new, ck, cv, l, lw)
        else:
            x_new, k_p, v_p = _mixer_prompt(x, lw, l)
            x_new, k_s, v_s, cv_s = _mixer_sample(x, NT_PROMPT, x_new, ck, cv, l, lw)
        kp.append(k_p)
        vp.append(v_p)
        ksm.append(k_s)
        vsm.append(v_s)
        cvs.append(cv_s)
        i = l // 2
        if l % 2 == 0:
            x = _swiglu(x_new, ffn_g, ffn_wg, ffn_wu, ffn_wd, l, i)
        else:
            final_g = final_norm_g[None] if l == DEPTH - 1 else None
            x = _moe(x_new, ffn_g, wr_hi, wr_lo, ex_wg, ex_wu, ex_wrem, ex_wd, l, i, final_g)
    y_p, y_s = x
    win_p = (DEPTH, BATCH, WINDOW, KV_HEADS, HEAD_DIM)
    win_s = (DEPTH, DEC_BATCH, CACHE_W, KV_HEADS, HEAD_DIM)
    return (y_p.reshape(BATCH, SEQ, D_MODEL), y_s.reshape(DEC_BATCH, DEC_SEQ, D_MODEL),
            jnp.stack(kp).reshape(win_p), jnp.stack(vp).reshape(win_p),
            jnp.stack(ksm).reshape(win_s), jnp.stack(vsm).reshape(win_s),
            jnp.stack(cvs).reshape(DEPTH, DEC_BATCH, DEC_SEQ, GMLP_HEADS, GMLP_HEAD_DIM))
```

```python
import functools

import numpy as np
import jax
import jax.numpy as jnp
from jax import lax
from jax.experimental import pallas as pl
from jax.experimental.pallas import tpu as pltpu

D_MODEL = 1024
BATCH = 4
SEQ = 4096
DEPTH = 4
DEC_BATCH = 128
DEC_SEQ = 4
HEAD_DIM = 64
ATTN_HEADS = 8
KV_HEADS = 2
GQA_GROUP = ATTN_HEADS // KV_HEADS
ATTN_WIDTH = ATTN_HEADS * HEAD_DIM
KV_WIDTH = KV_HEADS * HEAD_DIM
WINDOW = 128
CHUNK = 128
GMLP_HEADS = 4
GMLP_HEAD_DIM = 128
GMLP_WIDTH = GMLP_HEADS * GMLP_HEAD_DIM
IN_WIDTH = 2 * GMLP_WIDTH + ATTN_WIDTH + 2 * KV_WIDTH
D_FF = 2816
N_EXPERTS = 8
D_FF_EXPERT = 1408
CACHE_W = 128
EPS = 1e-6

LANES = 128
TM = 512
T_PROMPT = BATCH * SEQ
T_SAMPLE = DEC_BATCH * DEC_SEQ
T_ALL = T_PROMPT + T_SAMPLE
NT_PROMPT = T_PROMPT // TM
NT_ALL = T_ALL // TM
TILES_PER_SEQ = SEQ // TM
BLOCKS_PER_TILE = TM // WINDOW
SEQ_GROUP = 16
SEQS_PER_TILE = 4
N_SEQ_GROUPS = DEC_BATCH // SEQ_GROUP
KEYS_PAD = CACHE_W + 8
PIECE = 16
SORT_ROWS = 2 * TM + N_EXPERTS * PIECE
TILES_PER_SUPER = 11
N_SUPER = NT_ALL // TILES_PER_SUPER
SUPER_ROWS = TILES_PER_SUPER * SORT_ROWS
SORT_TILES_PER_STEP = 3
CHUNK_ROWS = 256
FF_REM = D_FF_EXPERT % 256
FF_MAIN = D_FF_EXPERT - FF_REM
PIECES_PER_CHUNK = CHUNK_ROWS // PIECE
VMEM_LIMIT = 56 * 1024 * 1024

F32 = jnp.float32
BF16 = jnp.bfloat16
NEG_INF = float("-inf")

Q0 = 2 * GMLP_WIDTH
K0 = Q0 + ATTN_WIDTH
V0 = K0 + KV_WIDTH


def _g_major(a, axis):
    axis = axis % a.ndim
    shape = a.shape
    a = a.reshape(shape[:axis] + (KV_HEADS, GQA_GROUP, HEAD_DIM) + shape[axis + 1:])
    return jnp.swapaxes(a, axis, axis + 1).reshape(shape)


def _slope(kv, g):
    return 2.0 ** (-8.0 * (kv * GQA_GROUP + g + 1) / ATTN_HEADS)


def _prompt_bias():
    c = np.arange(WINDOW)[:, None]
    qi = np.arange(WINDOW)[None, :]
    dist = np.where(c > qi, qi + WINDOW - c, qi - c)
    out = np.zeros((2, KV_HEADS, WINDOW, GQA_GROUP * WINDOW), np.float32)
    for var in range(2):
        ok = ~((var == 1) & (c > qi))
        for kv in range(KV_HEADS):
            for g in range(GQA_GROUP):
                out[var, kv, :, g * WINDOW:(g + 1) * WINDOW] = np.where(ok, -_slope(kv, g) * dist, NEG_INF)
    return out


def _sample_bias():
    out = np.zeros((KEYS_PAD, LANES), np.float32)
    j = np.arange(KEYS_PAD)
    kpos = j - CACHE_W
    for kv in range(KV_HEADS):
        for g in range(GQA_GROUP):
            for s in range(SEQS_PER_TILE):
                for t in range(DEC_SEQ):
                    col = ((kv * GQA_GROUP + g) * SEQS_PER_TILE + s) * DEC_SEQ + t
                    dist = t - kpos
                    ok = (dist >= 0) & (dist < WINDOW) & (j < CACHE_W + DEC_SEQ)
                    out[:, col] = np.where(ok, -_slope(kv, g) * dist, NEG_INF)
    return out


def _rms(x, g):
    return x * lax.rsqrt(jnp.mean(x * x, axis=-1, keepdims=True) + EPS) * g


def _gelu(x):
    return 0.5 * x * (1.0 + lax.erf(x * (2.0 ** -0.5)))


def _gmlp_v(z_v, lng_ref, lnb_ref):
    out = []
    for hh in range(GMLP_HEADS):
        sl = slice(hh * LANES, (hh + 1) * LANES)
        vh = _gelu(z_v[:, sl])
        vc = vh - jnp.mean(vh, axis=-1, keepdims=True)
        y = vc * lax.rsqrt(jnp.mean(vc * vc, axis=-1, keepdims=True) + EPS)
        out.append(y * lng_ref[:, sl] + lnb_ref[:, sl])
    return out


def _dot_nt(a, b):
    return lax.dot_general(a, b, (((1,), (1,)), ((), ())), preferred_element_type=F32)


def _mixer_prompt_kernel(*refs):
    i = pl.program_id(0)

    @pl.when(i < NT_PROMPT)
    def _():
        _mixer_prompt_tile(*refs)

    @pl.when(i == NT_PROMPT)
    def _():
        xo_ref = refs[13]
        xo_ref[...] = jnp.zeros_like(xo_ref)


def _mixer_prompt_tile(x_ref, ng_ref, win_ref, bin_ref, lng_ref, lnb_ref, ws_ref, bs_ref,
                       sink_ref, bias_ref, ga_ref, gb_ref, wout_ref,
                       xo_ref, ks_ref, vs_ref, kprev_ref, vprev_ref):
    i = pl.program_id(0)

    @pl.when(i == 0)
    def _():
        kprev_ref[...] = jnp.zeros_like(kprev_ref)
        vprev_ref[...] = jnp.zeros_like(vprev_ref)

    x = x_ref[...]
    h = _rms(x, ng_ref[...]).astype(BF16)

    def in_proj(lo, hi):
        return jnp.dot(h, win_ref[:, lo:hi], preferred_element_type=F32) + bin_ref[:, lo:hi]

    z_att = in_proj(Q0, IN_WIDTH)
    q = (z_att[:, :ATTN_WIDTH] * (HEAD_DIM ** -0.5)).astype(BF16)
    k = z_att[:, ATTN_WIDTH:ATTN_WIDTH + KV_WIDTH]
    val = z_att[:, ATTN_WIDTH + KV_WIDTH:]
    k_bf = k.astype(BF16)
    lane = lax.broadcasted_iota(jnp.int32, (1, LANES), 1)
    low = lane < HEAD_DIM
    first = jnp.where(i % TILES_PER_SEQ == 0, 1, 0)
    c_i = lax.broadcasted_iota(jnp.int32, (WINDOW, WINDOW), 0)
    q_i = lax.broadcasted_iota(jnp.int32, (WINDOW, WINDOW), 1)
    from_prev = jnp.concatenate([c_i > q_i] * GQA_GROUP, axis=1)
    g_lane = lax.broadcasted_iota(jnp.int32, (1, GQA_GROUP * WINDOW), 1) // WINDOW
    top = lax.broadcasted_iota(jnp.int32, (KV_WIDTH, 1), 0) < HEAD_DIM
    one = jnp.ones((), BF16)
    vt_bf = val.T.astype(BF16)
    vt_all = jnp.concatenate([vprev_ref[...], vt_bf], axis=1)
    vt_aug = [jnp.where(top, vt_all, one), jnp.where(top, one, vt_all)]
    k_all = jnp.concatenate([kprev_ref[...], k_bf], axis=0)

    zero = jnp.zeros((), BF16)
    sinks = []
    for kv in range(KV_HEADS):
        sink = jnp.full((1, GQA_GROUP * WINDOW), sink_ref[kv * GQA_GROUP], F32)
        for g in range(1, GQA_GROUP):
            sink = jnp.where(g_lane == g, sink_ref[kv * GQA_GROUP + g], sink)
        sinks.append(sink)

    def attention_block(c):
        rows = slice(c * WINDOW, (c + 1) * WINDOW)
        band = slice(c * WINDOW, (c + 2) * WINDOW)
        o_t = []
        for kv in range(KV_HEADS):
            keep = low if kv == 0 else jnp.logical_not(low)
            qm = jnp.concatenate(
                [jnp.where(keep, q[rows, g * LANES:(g + 1) * LANES], zero) for g in range(GQA_GROUP)],
                axis=0)
            s2 = _dot_nt(k_all[band], qm)
            bias = bias_ref[first, kv] if c == 0 else bias_ref[0, kv]
            s = jnp.where(from_prev, s2[:WINDOW], s2[WINDOW:]) + bias
            m = jnp.maximum(jnp.max(s, axis=0, keepdims=True), sinks[kv])
            p = jnp.exp(s - m)
            p2 = jnp.concatenate([jnp.where(from_prev, p, 0.0), jnp.where(from_prev, 0.0, p)],
                                 axis=0).astype(BF16)
            o = jnp.dot(vt_aug[kv][:, band], p2, preferred_element_type=F32)
            own, ones_row = (o[:HEAD_DIM], o[HEAD_DIM:HEAD_DIM + 1]) if kv == 0 else (o[HEAD_DIM:], o[0:1])
            o_t.append(own / (ones_row + jnp.exp(sinks[kv] - m)))
        bt = jnp.concatenate(o_t, axis=0)
        return jnp.concatenate(
            [bt[:, g * WINDOW:(g + 1) * WINDOW].T for g in range(GQA_GROUP)], axis=1)

    z_u = in_proj(0, GMLP_WIDTH)
    b_rows = [attention_block(0)]
    z_v = in_proj(GMLP_WIDTH, Q0)
    b_rows.append(attention_block(1))
    u = _gelu(z_u)
    b_rows.append(attention_block(2))
    v_heads = _gmlp_v(z_v, lng_ref, lnb_ref)
    row = lax.broadcasted_iota(jnp.int32, (CHUNK, CHUNK), 0)
    col = lax.broadcasted_iota(jnp.int32, (CHUNK, CHUNK), 1)
    a_cols = []
    for hh in range(GMLP_HEADS):
        w = jnp.where(row >= col, ws_ref[hh], 0.0).astype(BF16)
        v_cat = jnp.concatenate(
            [v_heads[hh][c * CHUNK:(c + 1) * CHUNK] for c in range(BLOCKS_PER_TILE)], axis=1)
        mixed = jnp.dot(w, v_cat.astype(BF16), preferred_element_type=F32)
        bias = bs_ref[:, hh * LANES:(hh + 1) * LANES]
        a_cols.append(jnp.concatenate(
            [mixed[:, c * LANES:(c + 1) * LANES] + bias for c in range(BLOCKS_PER_TILE)], axis=0))
    a = u * jnp.concatenate(a_cols, axis=1)
    a_n = _rms(a, ga_ref[...]).astype(BF16)
    b_rows.append(attention_block(3))
    out = x + jnp.dot(a_n, wout_ref[:GMLP_WIDTH, :], preferred_element_type=F32)
    b_n = _rms(jnp.concatenate(b_rows, axis=0), gb_ref[...]).astype(BF16)
    xo_ref[...] = out + jnp.dot(b_n, wout_ref[GMLP_WIDTH:, :], preferred_element_type=F32)

    kprev_ref[...] = k_bf[TM - WINDOW:]
    vprev_ref[...] = vt_bf[:, TM - WINDOW:]
    ks_ref[0] = k[TM - WINDOW:]
    vs_ref[0] = val[TM - WINDOW:]


def _layer_block(layer, *shape, **kwargs):
    return pl.BlockSpec((None,) + shape, lambda *_: (layer,) + (0,) * len(shape), **kwargs)


def _mixer_prompt(x, lw, layer):
    const = lambda *shape: pl.BlockSpec(shape, lambda i: (0,) * len(shape))
    per_layer = functools.partial(_layer_block, layer)
    return pl.pallas_call(
        _mixer_prompt_kernel,
        grid=(NT_ALL,),
        in_specs=[
            pl.BlockSpec((TM, D_MODEL), lambda i: (jnp.minimum(i, NT_PROMPT - 1), 0)),
            per_layer(1, D_MODEL), per_layer(D_MODEL, IN_WIDTH), per_layer(1, IN_WIDTH),
            per_layer(1, GMLP_WIDTH), per_layer(1, GMLP_WIDTH),
            per_layer(GMLP_HEADS, CHUNK, CHUNK), per_layer(CHUNK, GMLP_WIDTH),
            pl.BlockSpec(memory_space=pltpu.SMEM),
            const(2, KV_HEADS, WINDOW, GQA_GROUP * WINDOW),
            per_layer(1, GMLP_WIDTH), per_layer(1, ATTN_WIDTH), per_layer(D_MODEL, D_MODEL),
        ],
        out_specs=[
            pl.BlockSpec((TM, D_MODEL), lambda i: (i, 0)),
            pl.BlockSpec((1, WINDOW, KV_WIDTH), lambda i: (jnp.minimum(i // TILES_PER_SEQ, BATCH - 1), 0, 0)),
            pl.BlockSpec((1, WINDOW, KV_WIDTH), lambda i: (jnp.minimum(i // TILES_PER_SEQ, BATCH - 1), 0, 0)),
        ],
        out_shape=[
            jax.ShapeDtypeStruct((T_ALL, D_MODEL), F32),
            jax.ShapeDtypeStruct((BATCH, WINDOW, KV_WIDTH), F32),
            jax.ShapeDtypeStruct((BATCH, WINDOW, KV_WIDTH), F32),
        ],
        scratch_shapes=[pltpu.VMEM((WINDOW, KV_WIDTH), BF16), pltpu.VMEM((WINDOW, KV_WIDTH), BF16)],
        compiler_params=pltpu.CompilerParams(
            dimension_semantics=("arbitrary",), vmem_limit_bytes=VMEM_LIMIT),
        name="mixer_prompt",
    )(x, lw["norm_g"], lw["w_in"], lw["b_in"], lw["ln_g"], lw["ln_b"], lw["w_s"], lw["b_s_prompt"],
      lw["sinks"], lw["bias_prompt"], lw["g_a"], lw["g_b"], lw["w_out"])


def _mixer_sample_kernel(x_ref, xalias_ref, ck_ref, cv_ref, ng_ref, win_ref, bin_ref, lng_ref, lnb_ref,
                         wsd_ref, bsd_ref, sinkrow_ref, bias_ref, ga_ref, gb_ref, wout_ref,
                         xo_ref, ko_ref, vo_ref, cvo_ref,
                         q_s, k_s, v_s, a_s, b_s, kfull, vfull):
    del xalias_ref
    j = pl.program_id(0)

    @pl.when(j == 0)
    def _():
        h = _rms(x_ref[...], ng_ref[...]).astype(BF16)
        z = jnp.dot(h, win_ref[...], preferred_element_type=F32) + bin_ref[...]
        u = _gelu(z[:, :GMLP_WIDTH])
        v = jnp.concatenate(_gmlp_v(z[:, GMLP_WIDTH:Q0], lng_ref, lnb_ref), axis=1)
        cvo_ref[...] = v
        t_row = lax.broadcasted_iota(jnp.int32, (8, 1), 0) % DEC_SEQ

        def tile_rows(pat):
            return jnp.broadcast_to(pat[None], (TM // 8,) + pat.shape).reshape(TM, pat.shape[-1])

        mixed = tile_rows(bsd_ref[...])
        for d in range(DEC_SEQ):
            vd = v if d == 0 else pltpu.roll(v, d, 0)
            mixed = mixed + tile_rows(jnp.where(t_row >= d, wsd_ref[d], 0.0)) * vd
        a_s[...] = _rms(u * mixed, ga_ref[...]).astype(BF16)
        q_s[...] = z[:, Q0:K0] * (HEAD_DIM ** -0.5)
        k_s[...] = z[:, K0:V0]
        v_s[...] = z[:, V0:]
        kfull[:, CACHE_W + DEC_SEQ:, :] = jnp.zeros((SEQ_GROUP, KEYS_PAD - CACHE_W - DEC_SEQ, KV_WIDTH), F32)
        vfull[:, CACHE_W + DEC_SEQ:, :] = jnp.zeros((SEQ_GROUP, KEYS_PAD - CACHE_W - DEC_SEQ, KV_WIDTH), F32)

    base = pl.multiple_of(j * (SEQ_GROUP * DEC_SEQ), SEQ_GROUP * DEC_SEQ)
    for kv in range(KV_HEADS):
        head = slice(kv * HEAD_DIM, (kv + 1) * HEAD_DIM)
        kfull[:, :CACHE_W, head] = ck_ref[0, :, pl.ds(kv, CACHE_W, stride=KV_HEADS), :]
        vfull[:, :CACHE_W, head] = cv_ref[0, :, pl.ds(kv, CACHE_W, stride=KV_HEADS), :]
    k_new = k_s[pl.ds(base, SEQ_GROUP * DEC_SEQ), :]
    v_new = v_s[pl.ds(base, SEQ_GROUP * DEC_SEQ), :]
    for s in range(SEQ_GROUP):
        kfull[s, CACHE_W:CACHE_W + DEC_SEQ, :] = k_new[s * DEC_SEQ:(s + 1) * DEC_SEQ]
        vfull[s, CACHE_W:CACHE_W + DEC_SEQ, :] = v_new[s * DEC_SEQ:(s + 1) * DEC_SEQ]
    ko_ref[0] = kfull[:, DEC_SEQ:DEC_SEQ + CACHE_W, :]
    vo_ref[0] = vfull[:, DEC_SEQ:DEC_SEQ + CACHE_W, :]

    lane = lax.broadcasted_iota(jnp.int32, (1, LANES), 1)
    low = lane < HEAD_DIM
    col_seq = (lane // DEC_SEQ) % SEQS_PER_TILE
    row_seq = (lax.broadcasted_iota(jnp.int32, (LANES, 1), 0) // DEC_SEQ) % SEQS_PER_TILE
    sink = sinkrow_ref[...]
    zero = jnp.zeros((), BF16)
    quad_rows = SEQS_PER_TILE * DEC_SEQ
    for m in range(SEQ_GROUP // SEQS_PER_TILE):
        r0 = pl.multiple_of(base + m * quad_rows, quad_rows)
        q16 = q_s[pl.ds(r0, quad_rows), :]
        qt = jnp.concatenate(
            [jnp.where(low if kv == 0 else jnp.logical_not(low), q16[:, g * LANES:(g + 1) * LANES], 0.0)
             for kv in range(KV_HEADS) for g in range(GQA_GROUP)], axis=0).astype(BF16)
        s = bias_ref[...]
        for n in range(SEQS_PER_TILE):
            kn = kfull[SEQS_PER_TILE * m + n].astype(BF16)
            s = s + _dot_nt(kn, jnp.where(row_seq == n, qt, zero))
        mx = jnp.maximum(jnp.max(s, axis=0, keepdims=True), sink)
        p = jnp.exp(s - mx)
        p = (p / (jnp.sum(p, axis=0, keepdims=True) + jnp.exp(sink - mx))).astype(BF16)
        o = jnp.zeros((LANES, KV_WIDTH), F32)
        for n in range(SEQS_PER_TILE):
            vn = vfull[SEQS_PER_TILE * m + n].astype(BF16)
            o = o + lax.dot_general(jnp.where(col_seq == n, p, zero), vn, (((0,), (0,)), ((), ())),
                                    preferred_element_type=F32)
        b_s[pl.ds(r0, quad_rows), :] = jnp.concatenate(
            [jnp.where(low, o[g * quad_rows:(g + 1) * quad_rows],
                       o[(GQA_GROUP + g) * quad_rows:(GQA_GROUP + g + 1) * quad_rows])
             for g in range(GQA_GROUP)], axis=1)

    @pl.when(j == N_SEQ_GROUPS - 1)
    def _():
        b_n = _rms(b_s[...], gb_ref[...]).astype(BF16)
        merged = jnp.concatenate([a_s[...], b_n], axis=1)
        xo_ref[...] = x_ref[...] + jnp.dot(merged, wout_ref[...], preferred_element_type=F32)


def _mixer_sample(x, x_block, x_buf, cache_k, cache_v, layer, lw):
    const = lambda *shape: pl.BlockSpec(shape, lambda j: (0,) * len(shape))
    per_layer = functools.partial(_layer_block, layer)
    cache_spec = pl.BlockSpec((1, SEQ_GROUP, KV_HEADS * CACHE_W, HEAD_DIM), lambda j: (layer, j, 0, 0))
    state_spec = pl.BlockSpec((1, SEQ_GROUP, CACHE_W, KV_WIDTH), lambda j: (j, 0, 0, 0))
    state_shape = jax.ShapeDtypeStruct((N_SEQ_GROUPS, SEQ_GROUP, CACHE_W, KV_WIDTH), F32)
    return pl.pallas_call(
        _mixer_sample_kernel,
        grid=(N_SEQ_GROUPS,),
        in_specs=[
            pl.BlockSpec((TM, D_MODEL), lambda j: (x_block, 0)),
            pl.BlockSpec(memory_space=pl.ANY),
            cache_spec, cache_spec,
            per_layer(1, D_MODEL), per_layer(D_MODEL, IN_WIDTH), per_layer(1, IN_WIDTH),
            per_layer(1, GMLP_WIDTH), per_layer(1, GMLP_WIDTH),
            per_layer(DEC_SEQ, 8, GMLP_WIDTH), per_layer(8, GMLP_WIDTH),
            per_layer(1, LANES), const(KEYS_PAD, LANES),
            per_layer(1, GMLP_WIDTH), per_layer(1, ATTN_WIDTH), per_layer(D_MODEL, D_MODEL),
        ],
        out_specs=[
            pl.BlockSpec((TM, D_MODEL), lambda j: (NT_PROMPT, 0)),
            state_spec, state_spec,
            const(TM, GMLP_WIDTH),
        ],
        out_shape=[
            jax.ShapeDtypeStruct((T_ALL, D_MODEL), F32),
            state_shape, state_shape,
            jax.ShapeDtypeStruct((TM, GMLP_WIDTH), F32),
        ],
        scratch_shapes=[
            pltpu.VMEM((TM, ATTN_WIDTH), F32), pltpu.VMEM((TM, KV_WIDTH), F32),
            pltpu.VMEM((TM, KV_WIDTH), F32), pltpu.VMEM((TM, GMLP_WIDTH), BF16),
            pltpu.VMEM((TM, ATTN_WIDTH), F32),
            pltpu.VMEM((SEQ_GROUP, KEYS_PAD, KV_WIDTH), F32),
            pltpu.VMEM((SEQ_GROUP, KEYS_PAD, KV_WIDTH), F32),
        ],
        input_output_aliases={1: 0},
        compiler_params=pltpu.CompilerParams(
            dimension_semantics=("arbitrary",), vmem_limit_bytes=VMEM_LIMIT),
        name="mixer_sample",
    )(x, x_buf, cache_k, cache_v, lw["norm_g"], lw["w_in"], lw["b_in"], lw["ln_g"], lw["ln_b"],
      lw["w_s_sample"], lw["b_s_sample"], lw["sink_row"], lw["bias_sample"],
      lw["g_a"], lw["g_b"], lw["w_out"])


def _swiglu_kernel(x_ref, ng_ref, wg_ref, wu_ref, wd_ref, xo_ref):
    x = x_ref[...]
    h = _rms(x, ng_ref[...]).astype(BF16)
    gate = jnp.dot(h, wg_ref[...], preferred_element_type=F32)
    up = jnp.dot(h, wu_ref[...], preferred_element_type=F32)
    act = (jax.nn.silu(gate) * up).astype(BF16)
    xo_ref[...] = x + jnp.dot(act, wd_ref[...], preferred_element_type=F32)


def _swiglu(x, norm_g, wg, wu, wd, layer, ffn_index):
    weights = functools.partial(_layer_block, ffn_index, pipeline_mode=pl.Buffered(1))
    return pl.pallas_call(
        _swiglu_kernel,
        grid=(NT_ALL,),
        in_specs=[
            pl.BlockSpec((TM, D_MODEL), lambda i: (i, 0)),
            _layer_block(layer, 1, D_MODEL),
            weights(D_MODEL, D_FF), weights(D_MODEL, D_FF), weights(D_FF, D_MODEL),
        ],
        out_specs=pl.BlockSpec((TM, D_MODEL), lambda i: (i, 0)),
        out_shape=jax.ShapeDtypeStruct((T_ALL, D_MODEL), F32),
        compiler_params=pltpu.CompilerParams(
            dimension_semantics=("arbitrary",), vmem_limit_bytes=VMEM_LIMIT),
        name="swiglu",
    )(x, norm_g, wg, wu, wd)


def _moe_sort_kernel(x_ref, ng_ref, wrh_ref, wrl_ref, hs_ref, meta_ref, off_ref, cnt_ref):
    n = SORT_TILES_PER_STEP
    logits, routes = {}, {}
    for step in range(n + 2):
        if step < n:
            logits[step] = _sort_logits(x_ref[step * TM:(step + 1) * TM, :], ng_ref, wrh_ref, wrl_ref)
        j = step - 1
        if 0 <= j < n:
            h_hi, lg = logits.pop(j)
            meta, off, cnt_pad = _sort_route(lg)
            meta_ref[j * TM:(j + 1) * TM, :] = meta
            off_ref[j] = off.astype(jnp.int32)
            cnt_ref[j] = cnt_pad.astype(jnp.int32)
            routes[j] = (h_hi, meta)
        j = step - 2
        if 0 <= j < n:
            hs_ref[j * SORT_ROWS:(j + 1) * SORT_ROWS, :] = _sort_gather(*routes.pop(j))


def _sort_logits(x, ng_ref, wrh_ref, wrl_ref):
    h = _rms(x, ng_ref[...])
    h_hi = h.astype(BF16)
    h_lo = (h - h_hi.astype(F32)).astype(BF16)
    logits = (jnp.dot(h_hi, wrh_ref[...], preferred_element_type=F32)
              + jnp.dot(h_lo, wrh_ref[...], preferred_element_type=F32)
              + jnp.dot(h_hi, wrl_ref[...], preferred_element_type=F32))
    return h_hi, logits


def _sort_route(logits):
    lane = lax.broadcasted_iota(jnp.int32, (1, LANES), 1)
    logits = jnp.where(lane < N_EXPERTS, logits, NEG_INF)
    m1 = jnp.max(logits, axis=-1, keepdims=True)
    i1 = jnp.min(jnp.where(logits == m1, lane, LANES), axis=-1, keepdims=True)
    rest = jnp.where(lane == i1, NEG_INF, logits)
    m2 = jnp.max(rest, axis=-1, keepdims=True)
    i2 = jnp.min(jnp.where(rest == m2, lane, LANES), axis=-1, keepdims=True)
    e2 = jnp.exp(m2 - m1)
    den = 1.0 + e2
    sel0 = lane == i1
    sel1 = lane == i2
    routed = jnp.where(jnp.logical_or(sel0, sel1), 1.0, 0.0)
    r_i = lax.broadcasted_iota(jnp.int32, (TM, TM), 0)
    c_i = lax.broadcasted_iota(jnp.int32, (TM, TM), 1)
    lower = jnp.where(r_i > c_i, 1.0, 0.0).astype(BF16)
    rank = jnp.dot(lower, routed.astype(BF16), preferred_element_type=F32)
    cnt = jnp.broadcast_to(jnp.sum(routed, axis=0, keepdims=True), (8, LANES))
    cnt_pad = jnp.floor((cnt + (PIECE - 1)) * (1.0 / PIECE)) * PIECE
    incl = cnt_pad
    for sh in (1, 2, 4):
        incl = incl + jnp.where(lane >= sh, pltpu.roll(incl, sh, 1), 0.0)
    off = incl - cnt_pad
    pos = off[0:1] + rank
    pos0 = jnp.sum(jnp.where(sel0, pos, 0.0), axis=-1, keepdims=True)
    pos1 = jnp.sum(jnp.where(sel1, pos, 0.0), axis=-1, keepdims=True)
    meta = jnp.where(lane == 0, pos0, jnp.where(lane == 1, pos1,
                     jnp.where(lane == 2, 1.0 / den, jnp.where(lane == 3, e2 / den, 0.0))))
    return meta, off, cnt_pad


def _sort_gather(h_hi, meta):
    meta_t = meta.T
    p_row = lax.broadcasted_iota(jnp.int32, (SORT_ROWS, 1), 0).astype(F32)
    hit = jnp.logical_or(p_row == meta_t[0:1], p_row == meta_t[1:2])
    gather = jnp.where(hit, 1.0, 0.0).astype(BF16)
    return jnp.dot(gather, h_hi, preferred_element_type=F32).astype(BF16)


def _moe_sort(x, norm_g, wr_hi, wr_lo, layer, moe_index):
    return pl.pallas_call(
        _moe_sort_kernel,
        grid=(NT_ALL // SORT_TILES_PER_STEP,),
        in_specs=[
            pl.BlockSpec((SORT_TILES_PER_STEP * TM, D_MODEL), lambda i: (i, 0)),
            _layer_block(layer, 1, D_MODEL),
            _layer_block(moe_index, D_MODEL, LANES),
            _layer_block(moe_index, D_MODEL, LANES),
        ],
        out_specs=[
            pl.BlockSpec((SORT_TILES_PER_STEP * SORT_ROWS, D_MODEL), lambda i: (i, 0)),
            pl.BlockSpec((SORT_TILES_PER_STEP * TM, LANES), lambda i: (i, 0)),
            pl.BlockSpec((SORT_TILES_PER_STEP, 8, LANES), lambda i: (i, 0, 0)),
            pl.BlockSpec((SORT_TILES_PER_STEP, 8, LANES), lambda i: (i, 0, 0)),
        ],
        out_shape=[
            jax.ShapeDtypeStruct((NT_ALL * SORT_ROWS, D_MODEL), BF16),
            jax.ShapeDtypeStruct((T_ALL, LANES), F32),
            jax.ShapeDtypeStruct((NT_ALL, 8, LANES), jnp.int32),
            jax.ShapeDtypeStruct((NT_ALL, 8, LANES), jnp.int32),
        ],
        compiler_params=pltpu.CompilerParams(
            dimension_semantics=("arbitrary",), vmem_limit_bytes=VMEM_LIMIT),
        name="moe_sort",
    )(x, norm_g, wr_hi, wr_lo)


def _moe_experts_kernel(off_ref, cnt_ref, hs_hbm, wg_ref, wu_ref, wrem_ref, wd_ref, ys_hbm,
                        buf, lhs, sem_in, sem_out):
    s = pl.program_id(0)
    e = pl.program_id(1)
    row0 = pl.multiple_of(s * SUPER_ROWS, PIECE)
    is_first = e == 0
    is_last = e == N_EXPERTS - 1

    def copy_in(t):
        r = pl.multiple_of(t * SORT_ROWS, PIECE)
        return pltpu.make_async_copy(hs_hbm.at[pl.ds(row0 + r, SORT_ROWS)], buf.at[pl.ds(r, SORT_ROWS)],
                                     sem_in.at[t])

    def copy_out(t):
        r = pl.multiple_of(t * SORT_ROWS, PIECE)
        return pltpu.make_async_copy(buf.at[pl.ds(r, SORT_ROWS)], ys_hbm.at[pl.ds(row0 + r, SORT_ROWS)],
                                     sem_out.at[t])

    def for_tiles(lo, hi, fn):
        hi = jnp.maximum(lo, hi)

        def body(t, carry):
            fn(t)
            return carry

        lax.fori_loop(lo, hi, body, 0)
        return hi

    @pl.when(is_first)
    def _():
        for t in range(TILES_PER_SUPER):
            copy_in(t).start()

    starts, cums = [], [0]
    for i in range(TILES_PER_SUPER):
        idx = (s * TILES_PER_SUPER + i) * N_EXPERTS + e
        starts.append(i * SORT_ROWS + off_ref[idx])
        cums.append(cums[-1] + cnt_ref[idx] // PIECE)
    n_pieces = cums[-1]

    def locate(p):
        pc = jnp.minimum(p, n_pieces - 1)
        row = starts[0] + PIECE * pc
        tile = jnp.int32(0)
        for i in range(1, TILES_PER_SUPER):
            inside = pc >= cums[i]
            row = jnp.where(inside, starts[i] + PIECE * (pc - cums[i]), row)
            tile = jnp.where(inside, i, tile)
        return row, tile

    def piece_rows(q):
        src, dst, tiles = [], [], []
        for j in range(PIECES_PER_CHUNK):
            p = q * PIECES_PER_CHUNK + j
            row, tile = locate(p)
            src.append(row)
            dst.append(jnp.where(p < n_pieces, row, SUPER_ROWS))
            tiles.append(tile)
        return tuple(src), tuple(dst), tiles[0], tiles[-1]

    def chunk_body(q, carry):
        (src, dst, _, last_tile), arrived, sent = carry
        arrived = for_tiles(arrived, jnp.where(is_first, last_tile + 1, 0), lambda t: copy_in(t).wait())
        for j in range(PIECES_PER_CHUNK):
            lhs[j * PIECE:(j + 1) * PIECE, :] = buf[pl.ds(pl.multiple_of(src[j], PIECE), PIECE), :]
        xs = lhs[...]
        gate = jnp.dot(xs, wg_ref[0, :, :FF_MAIN], preferred_element_type=F32)
        up = jnp.dot(xs, wu_ref[0, :, :FF_MAIN], preferred_element_type=F32)
        rem = jnp.dot(xs, wrem_ref[0], preferred_element_type=F32)
        nxt = piece_rows(q + 1)
        act = jnp.concatenate([jax.nn.silu(gate) * up,
                               jax.nn.silu(rem[:, :FF_REM]) * rem[:, FF_REM:]], axis=1).astype(BF16)
        y = jnp.dot(act, wd_ref[0], preferred_element_type=F32).astype(BF16)
        for j in range(PIECES_PER_CHUNK):
            buf[pl.ds(pl.multiple_of(dst[j], PIECE), PIECE), :] = y[j * PIECE:(j + 1) * PIECE]
        sent = for_tiles(sent, jnp.where(is_last, nxt[2], 0), lambda t: copy_out(t).start())
        return nxt, arrived, sent

    n_chunks = (n_pieces + PIECES_PER_CHUNK - 1) // PIECES_PER_CHUNK
    _, arrived, sent = lax.fori_loop(0, n_chunks, chunk_body, (piece_rows(0), jnp.int32(0), jnp.int32(0)))
    for_tiles(arrived, jnp.where(is_first, TILES_PER_SUPER, 0), lambda t: copy_in(t).wait())
    for_tiles(sent, jnp.where(is_last, TILES_PER_SUPER, 0), lambda t: copy_out(t).start())
    for_tiles(0, jnp.where(is_last, TILES_PER_SUPER, 0), lambda t: copy_out(t).wait())


def _moe_experts(off, cnt, hs, wg, wu, wrem, wd, moe_index):
    expert = lambda *shape: pl.BlockSpec((None, 1) + shape, lambda s, e, *_: (moe_index, e, 0, 0))
    grid_spec = pltpu.PrefetchScalarGridSpec(
        num_scalar_prefetch=2,
        grid=(N_SUPER, N_EXPERTS),
        in_specs=[
            pl.BlockSpec(memory_space=pl.ANY),
            expert(D_MODEL, D_FF_EXPERT), expert(D_MODEL, D_FF_EXPERT), expert(D_MODEL, 2 * FF_REM),
            expert(D_FF_EXPERT, D_MODEL),
        ],
        out_specs=pl.BlockSpec(memory_space=pl.ANY),
        scratch_shapes=[pltpu.VMEM((SUPER_ROWS + PIECE, D_MODEL), BF16),
                        pltpu.VMEM((CHUNK_ROWS, D_MODEL), BF16),
                        pltpu.SemaphoreType.DMA((TILES_PER_SUPER,)),
                        pltpu.SemaphoreType.DMA((TILES_PER_SUPER,))],
    )
    return pl.pallas_call(
        _moe_experts_kernel,
        grid_spec=grid_spec,
        out_shape=jax.ShapeDtypeStruct((NT_ALL * SORT_ROWS, D_MODEL), BF16),
        compiler_params=pltpu.CompilerParams(
            dimension_semantics=("arbitrary", "arbitrary"), vmem_limit_bytes=VMEM_LIMIT),
        name="moe_experts",
    )(off, cnt, hs, wg, wu, wrem, wd)


def _unsorted_residual(x_ref, ys_ref, meta_ref):
    meta = meta_ref[...]
    p_col = lax.broadcasted_iota(jnp.int32, (1, SORT_ROWS), 1).astype(F32)
    scatter = (jnp.where(p_col == meta[:, 0:1], meta[:, 2:3], 0.0)
               + jnp.where(p_col == meta[:, 1:2], meta[:, 3:4], 0.0)).astype(BF16)
    return x_ref[...] + jnp.dot(scatter, ys_ref[...], preferred_element_type=F32)


def _moe_unsort_kernel(x_ref, ys_ref, meta_ref, xo_ref):
    xo_ref[...] = _unsorted_residual(x_ref, ys_ref, meta_ref)


def _moe_unsort_final_kernel(x_ref, ys_ref, meta_ref, g_ref, yp_ref, ysm_ref):
    i = pl.program_id(0)
    y = _rms(_unsorted_residual(x_ref, ys_ref, meta_ref), g_ref[...])

    @pl.when(i < NT_PROMPT)
    def _():
        yp_ref[...] = y

    @pl.when(i == NT_PROMPT)
    def _():
        ysm_ref[...] = y


def _moe_unsort(x, ys, meta, final_g=None):
    in_specs = [
        pl.BlockSpec((TM, D_MODEL), lambda i: (i, 0)),
        pl.BlockSpec((SORT_ROWS, D_MODEL), lambda i: (i, 0)),
        pl.BlockSpec((TM, LANES), lambda i: (i, 0)),
    ]
    params = pltpu.CompilerParams(dimension_semantics=("arbitrary",), vmem_limit_bytes=VMEM_LIMIT)
    if final_g is None:
        return pl.pallas_call(
            _moe_unsort_kernel,
            grid=(NT_ALL,),
            in_specs=in_specs,
            out_specs=pl.BlockSpec((TM, D_MODEL), lambda i: (i, 0)),
            out_shape=jax.ShapeDtypeStruct((T_ALL, D_MODEL), F32),
            compiler_params=params,
            name="moe_unsort",
        )(x, ys, meta)
    return pl.pallas_call(
        _moe_unsort_final_kernel,
        grid=(NT_ALL,),
        in_specs=in_specs + [pl.BlockSpec((1, D_MODEL), lambda i: (0, 0))],
        out_specs=[pl.BlockSpec((TM, D_MODEL), lambda i: (jnp.minimum(i, NT_PROMPT - 1), 0)),
                   pl.BlockSpec((TM, D_MODEL), lambda i: (0, 0))],
        out_shape=[jax.ShapeDtypeStruct((T_PROMPT, D_MODEL), F32),
                   jax.ShapeDtypeStruct((T_SAMPLE, D_MODEL), F32)],
        compiler_params=params,
        name="moe_unsort_final",
    )(x, ys, meta, final_g)


def _moe(x, norm_g, wr_hi, wr_lo, wg, wu, wrem, wd, layer, moe_index, final_g=None):
    hs, meta, off, cnt = _moe_sort(x, norm_g, wr_hi, wr_lo, layer, moe_index)
    off = off[:, 0, :N_EXPERTS].reshape(-1)
    cnt = cnt[:, 0, :N_EXPERTS].reshape(-1)
    ys = _moe_experts(off, cnt, hs, wg, wu, wrem, wd, moe_index)
    return _moe_unsort(x, ys, meta, final_g)


def kernel(x_prompt, x_sample, cache_win_k, cache_win_v, attn_norm_g, w_in, b_in, chunk_ln_g, chunk_ln_b, w_spatial, b_spatial, attn_sinks, mix_norm_a_g, mix_norm_b_g, w_out, ffn_norm_g, ffn_w_gate, ffn_w_up, ffn_w_down, router_w, expert_w_gate, expert_w_up, expert_w_down, final_norm_g):
    w_in_p = jnp.concatenate([w_in[..., :Q0], _g_major(w_in[..., Q0:K0], -1), w_in[..., K0:]], axis=-1).astype(BF16)
    b_in_p = jnp.concatenate([b_in[..., :Q0], _g_major(b_in[..., Q0:K0], -1), b_in[..., K0:]], axis=-1)
    w_out_p = jnp.concatenate([w_out[:, :GMLP_WIDTH], _g_major(w_out[:, GMLP_WIDTH:], 1)], axis=1).astype(BF16)
    g_b_p = _g_major(mix_norm_b_g, 1)
    bs_prompt = jnp.repeat(jnp.transpose(b_spatial, (0, 2, 1)), GMLP_HEAD_DIM, axis=2)
    bs_sample = jnp.tile(bs_prompt[:, :DEC_SEQ], (1, 2, 1))
    t_idx = np.arange(DEC_SEQ)
    ws_small = jnp.stack([w_spatial[:, :, t_idx, np.maximum(t_idx - d, 0)] for d in range(DEC_SEQ)], axis=1)
    ws_sample = jnp.tile(jnp.repeat(jnp.transpose(ws_small, (0, 1, 3, 2)), GMLP_HEAD_DIM, axis=3),
                         (1, 1, 2, 1))
    sink_row = jnp.repeat(attn_sinks, SEQS_PER_TILE * DEC_SEQ, axis=1)[:, None, :]
    bias_prompt = jnp.asarray(_prompt_bias())
    bias_sample = jnp.asarray(_sample_bias())
    wr_pad = jnp.pad(router_w, ((0, 0), (0, 0), (0, LANES - N_EXPERTS)))
    wr_hi = wr_pad.astype(BF16)
    wr_lo = (wr_pad - wr_hi.astype(F32)).astype(BF16)
    ffn_wg, ffn_wu, ffn_wd = ffn_w_gate.astype(BF16), ffn_w_up.astype(BF16), ffn_w_down.astype(BF16)
    ex_wg, ex_wu = expert_w_gate.astype(BF16), expert_w_up.astype(BF16)
    ex_wrem = jnp.concatenate([ex_wg[..., FF_MAIN:], ex_wu[..., FF_MAIN:]], axis=-1)
    ex_wd = expert_w_down.astype(BF16)

    ck = cache_win_k.reshape(DEPTH, DEC_BATCH, CACHE_W * KV_HEADS, HEAD_DIM)
    cv = cache_win_v.reshape(DEPTH, DEC_BATCH, CACHE_W * KV_HEADS, HEAD_DIM)

    lw = dict(
        norm_g=attn_norm_g[:, None, :], w_in=w_in_p, b_in=b_in_p[:, None, :],
        ln_g=chunk_ln_g.reshape(DEPTH, 1, GMLP_WIDTH), ln_b=chunk_ln_b.reshape(DEPTH, 1, GMLP_WIDTH),
        w_s=w_spatial, b_s_prompt=bs_prompt, w_s_sample=ws_sample, b_s_sample=bs_sample,
        sink_row=sink_row, bias_prompt=bias_prompt, bias_sample=bias_sample,
        g_a=mix_norm_a_g[:, None, :], g_b=g_b_p[:, None, :], w_out=w_out_p)
    ffn_g = ffn_norm_g[:, None, :]

    x = None
    kp, vp, ksm, vsm, cvs = [], [], [], [], []
    for l in range(DEPTH):
        lw["sinks"] = attn_sinks[l]
        if l == 0:
            x_new, k_p, v_p = _mixer_prompt(x_prompt.reshape(T_PROMPT, D_MODEL), lw, l)
            x_new, k_s, v_s, cv_s = _mixer_sample(x_sample.reshape(T_SAMPLE, D_MODEL), 0, x_new, ck, cv, l, lw)
        else:
            x_new, k_p, v_p = _mixer_prompt(x, lw, l)
            x_new, k_s, v_s, cv_s = _mixer_sample(x, NT_PROMPT, x_new, ck, cv, l, lw)
        kp.append(k_p)
        vp.append(v_p)
        ksm.append(k_s)
        vsm.append(v_s)
        cvs.append(cv_s)
        i = l // 2
        if l % 2 == 0:
            x = _swiglu(x_new, ffn_g, ffn_wg, ffn_wu, ffn_wd, l, i)
        else:
            final_g = final_norm_g[None] if l == DEPTH - 1 else None
            x = _moe(x_new, ffn_g, wr_hi, wr_lo, ex_wg, ex_wu, ex_wrem, ex_wd, l, i, final_g)
    y_p, y_s = x
    win_p = (DEPTH, BATCH, WINDOW, KV_HEADS, HEAD_DIM)
    win_s = (DEPTH, DEC_BATCH, CACHE_W, KV_HEADS, HEAD_DIM)
    return (y_p.reshape(BATCH, SEQ, D_MODEL), y_s.reshape(DEC_BATCH, DEC_SEQ, D_MODEL),
            jnp.stack(kp).reshape(win_p), jnp.stack(vp).reshape(win_p),
            jnp.stack(ksm).reshape(win_s), jnp.stack(vsm).reshape(win_s),
            jnp.stack(cvs).reshape(DEPTH, DEC_BATCH, DEC_SEQ, GMLP_HEADS, GMLP_HEAD_DIM))
```

```python
import functools

import numpy as np
import jax
import jax.numpy as jnp
from jax import lax
from jax.experimental import pallas as pl
from jax.experimental.pallas import tpu as pltpu

D_MODEL = 1024
BATCH = 4
SEQ = 4096
DEPTH = 4
DEC_BATCH = 128
DEC_SEQ = 4
HEAD_DIM = 64
ATTN_HEADS = 8
KV_HEADS = 2
GQA_GROUP = ATTN_HEADS // KV_HEADS
ATTN_WIDTH = ATTN_HEADS * HEAD_DIM
KV_WIDTH = KV_HEADS * HEAD_DIM
WINDOW = 128
CHUNK = 128
GMLP_HEADS = 4
GMLP_HEAD_DIM = 128
GMLP_WIDTH = GMLP_HEADS * GMLP_HEAD_DIM
IN_WIDTH = 2 * GMLP_WIDTH + ATTN_WIDTH + 2 * KV_WIDTH
D_FF = 2816
N_EXPERTS = 8
D_FF_EXPERT = 1408
CACHE_W = 128
EPS = 1e-6

LANES = 128
TM = 512
T_PROMPT = BATCH * SEQ
T_SAMPLE = DEC_BATCH * DEC_SEQ
T_ALL = T_PROMPT + T_SAMPLE
NT_PROMPT = T_PROMPT // TM
NT_ALL = T_ALL // TM
TILES_PER_SEQ = SEQ // TM
BLOCKS_PER_TILE = TM // WINDOW
SEQ_GROUP = 16
SEQS_PER_TILE = 4
N_SEQ_GROUPS = DEC_BATCH // SEQ_GROUP
KEYS_PAD = CACHE_W + 8
PIECE = 16
SORT_ROWS = 2 * TM + N_EXPERTS * PIECE
TILES_PER_SUPER = 11
N_SUPER = NT_ALL // TILES_PER_SUPER
SUPER_ROWS = TILES_PER_SUPER * SORT_ROWS
SORT_TILES_PER_STEP = 3
CHUNK_ROWS = 256
FF_REM = D_FF_EXPERT % 256
FF_MAIN = D_FF_EXPERT - FF_REM
PIECES_PER_CHUNK = CHUNK_ROWS // PIECE
VMEM_LIMIT = 56 * 1024 * 1024

F32 = jnp.float32
BF16 = jnp.bfloat16
NEG_INF = float("-inf")

Q0 = 2 * GMLP_WIDTH
K0 = Q0 + ATTN_WIDTH
V0 = K0 + KV_WIDTH


def _g_major(a, axis):
    axis = axis % a.ndim
    shape = a.shape
    a = a.reshape(shape[:axis] + (KV_HEADS, GQA_GROUP, HEAD_DIM) + shape[axis + 1:])
    return jnp.swapaxes(a, axis, axis + 1).reshape(shape)


def _slope(kv, g):
    return 2.0 ** (-8.0 * (kv * GQA_GROUP + g + 1) / ATTN_HEADS)


def _prompt_bias():
    c = np.arange(WINDOW)[:, None]
    qi = np.arange(WINDOW)[None, :]
    dist = np.where(c > qi, qi + WINDOW - c, qi - c)
    out = np.zeros((2, KV_HEADS, WINDOW, GQA_GROUP * WINDOW), np.float32)
    for var in range(2):
        ok = ~((var == 1) & (c > qi))
        for kv in range(KV_HEADS):
            for g in range(GQA_GROUP):
                out[var, kv, :, g * WINDOW:(g + 1) * WINDOW] = np.where(ok, -_slope(kv, g) * dist, NEG_INF)
    return out


def _sample_bias():
    out = np.zeros((KEYS_PAD, LANES), np.float32)
    j = np.arange(KEYS_PAD)
    kpos = j - CACHE_W
    for kv in range(KV_HEADS):
        for g in range(GQA_GROUP):
            for s in range(SEQS_PER_TILE):
                for t in range(DEC_SEQ):
                    col = ((kv * GQA_GROUP + g) * SEQS_PER_TILE + s) * DEC_SEQ + t
                    dist = t - kpos
                    ok = (dist >= 0) & (dist < WINDOW) & (j < CACHE_W + DEC_SEQ)
                    out[:, col] = np.where(ok, -_slope(kv, g) * dist, NEG_INF)
    return out


def _rms(x, g):
    return x * lax.rsqrt(jnp.mean(x * x, axis=-1, keepdims=True) + EPS) * g


def _gelu(x):
    return 0.5 * x * (1.0 + lax.erf(x * (2.0 ** -0.5)))


def _gmlp_v(z_v, lng_ref, lnb_ref):
    out = []
    for hh in range(GMLP_HEADS):
        sl = slice(hh * LANES, (hh + 1) * LANES)
        vh = _gelu(z_v[:, sl])
        vc = vh - jnp.mean(vh, axis=-1, keepdims=True)
        y = vc * lax.rsqrt(jnp.mean(vc * vc, axis=-1, keepdims=True) + EPS)
        out.append(y * lng_ref[:, sl] + lnb_ref[:, sl])
    return out


def _dot_nt(a, b):
    return lax.dot_general(a, b, (((1,), (1,)), ((), ())), preferred_element_type=F32)


def _mixer_prompt_kernel(*refs):
    i = pl.program_id(0)

    @pl.when(i < NT_PROMPT)
    def _():
        _mixer_prompt_tile(*refs)

    @pl.when(i == NT_PROMPT)
    def _():
        xo_ref = refs[13]
        xo_ref[...] = jnp.zeros_like(xo_ref)


def _mixer_prompt_tile(x_ref, ng_ref, win_ref, bin_ref, lng_ref, lnb_ref, ws_ref, bs_ref,
                       sink_ref, bias_ref, ga_ref, gb_ref, wout_ref,
                       xo_ref, ks_ref, vs_ref, kprev_ref, vprev_ref):
    i = pl.program_id(0)

    @pl.when(i == 0)
    def _():
        kprev_ref[...] = jnp.zeros_like(kprev_ref)
        vprev_ref[...] = jnp.zeros_like(vprev_ref)

    x = x_ref[...]
    h = _rms(x, ng_ref[...]).astype(BF16)

    def in_proj(lo, hi):
        return jnp.dot(h, win_ref[:, lo:hi], preferred_element_type=F32) + bin_ref[:, lo:hi]

    z_att = in_proj(Q0, IN_WIDTH)
    q = (z_att[:, :ATTN_WIDTH] * (HEAD_DIM ** -0.5)).astype(BF16)
    k = z_att[:, ATTN_WIDTH:ATTN_WIDTH + KV_WIDTH]
    val = z_att[:, ATTN_WIDTH + KV_WIDTH:]
    k_bf = k.astype(BF16)
    lane = lax.broadcasted_iota(jnp.int32, (1, LANES), 1)
    low = lane < HEAD_DIM
    first = jnp.where(i % TILES_PER_SEQ == 0, 1, 0)
    c_i = lax.broadcasted_iota(jnp.int32, (WINDOW, WINDOW), 0)
    q_i = lax.broadcasted_iota(jnp.int32, (WINDOW, WINDOW), 1)
    from_prev = jnp.concatenate([c_i > q_i] * GQA_GROUP, axis=1)
    g_lane = lax.broadcasted_iota(jnp.int32, (1, GQA_GROUP * WINDOW), 1) // WINDOW
    top = lax.broadcasted_iota(jnp.int32, (KV_WIDTH, 1), 0) < HEAD_DIM
    one = jnp.ones((), BF16)
    vt_bf = val.T.astype(BF16)
    vt_all = jnp.concatenate([vprev_ref[...], vt_bf], axis=1)
    vt_aug = [jnp.where(top, vt_all, one), jnp.where(top, one, vt_all)]
    k_all = jnp.concatenate([kprev_ref[...], k_bf], axis=0)

    zero = jnp.zeros((), BF16)
    sinks = []
    for kv in range(KV_HEADS):
        sink = jnp.full((1, GQA_GROUP * WINDOW), sink_ref[kv * GQA_GROUP], F32)
        for g in range(1, GQA_GROUP):
            sink = jnp.where(g_lane == g, sink_ref[kv * GQA_GROUP + g], sink)
        sinks.append(sink)

    def attention_block(c):
        rows = slice(c * WINDOW, (c + 1) * WINDOW)
        band = slice(c * WINDOW, (c + 2) * WINDOW)
        o_t = []
        for kv in range(KV_HEADS):
            keep = low if kv == 0 else jnp.logical_not(low)
            qm = jnp.concatenate(
                [jnp.where(keep, q[rows, g * LANES:(g + 1) * LANES], zero) for g in range(GQA_GROUP)],
                axis=0)
            s2 = _dot_nt(k_all[band], qm)
            bias = bias_ref[first, kv] if c == 0 else bias_ref[0, kv]
            s = jnp.where(from_prev, s2[:WINDOW], s2[WINDOW:]) + bias
            m = jnp.maximum(jnp.max(s, axis=0, keepdims=True), sinks[kv])
            p = jnp.exp(s - m)
            p2 = jnp.concatenate([jnp.where(from_prev, p, 0.0), jnp.where(from_prev, 0.0, p)],
                                 axis=0).astype(BF16)
            o = jnp.dot(vt_aug[kv][:, band], p2, preferred_element_type=F32)
            own, ones_row = (o[:HEAD_DIM], o[HEAD_DIM:HEAD_DIM + 1]) if kv == 0 else (o[HEAD_DIM:], o[0:1])
            o_t.append(own / (ones_row + jnp.exp(sinks[kv] - m)))
        bt = jnp.concatenate(o_t, axis=0)
        return jnp.concatenate(
            [bt[:, g * WINDOW:(g + 1) * WINDOW].T for g in range(GQA_GROUP)], axis=1)

    z_u = in_proj(0, GMLP_WIDTH)
    b_rows = [attention_block(0)]
    z_v = in_proj(GMLP_WIDTH, Q0)
    b_rows.append(attention_block(1))
    u = _gelu(z_u)
    b_rows.append(attention_block(2))
    v_heads = _gmlp_v(z_v, lng_ref, lnb_ref)
    row = lax.broadcasted_iota(jnp.int32, (CHUNK, CHUNK), 0)
    col = lax.broadcasted_iota(jnp.int32, (CHUNK, CHUNK), 1)
    a_cols = []
    for hh in range(GMLP_HEADS):
        w = jnp.where(row >= col, ws_ref[hh], 0.0).astype(BF16)
        v_cat = jnp.concatenate(
            [v_heads[hh][c * CHUNK:(c + 1) * CHUNK] for c in range(BLOCKS_PER_TILE)], axis=1)
        mixed = jnp.dot(w, v_cat.astype(BF16), preferred_element_type=F32)
        bias = bs_ref[:, hh * LANES:(hh + 1) * LANES]
        a_cols.append(jnp.concatenate(
            [mixed[:, c * LANES:(c + 1) * LANES] + bias for c in range(BLOCKS_PER_TILE)], axis=0))
    a = u * jnp.concatenate(a_cols, axis=1)
    a_n = _rms(a, ga_ref[...]).astype(BF16)
    b_rows.append(attention_block(3))
    out = x + jnp.dot(a_n, wout_ref[:GMLP_WIDTH, :], preferred_element_type=F32)
    b_n = _rms(jnp.concatenate(b_rows, axis=0), gb_ref[...]).astype(BF16)
    xo_ref[...] = out + jnp.dot(b_n, wout_ref[GMLP_WIDTH:, :], preferred_element_type=F32)

    kprev_ref[...] = k_bf[TM - WINDOW:]
    vprev_ref[...] = vt_bf[:, TM - WINDOW:]
    ks_ref[0] = k[TM - WINDOW:]
    vs_ref[0] = val[TM - WINDOW:]


def _layer_block(layer, *shape, **kwargs):
    return pl.BlockSpec((None,) + shape, lambda *_: (layer,) + (0,) * len(shape), **kwargs)


def _mixer_prompt(x, lw, layer):
    const = lambda *shape: pl.BlockSpec(shape, lambda i: (0,) * len(shape))
    per_layer = functools.partial(_layer_block, layer)
    return pl.pallas_call(
        _mixer_prompt_kernel,
        grid=(NT_ALL,),
        in_specs=[
            pl.BlockSpec((TM, D_MODEL), lambda i: (jnp.minimum(i, NT_PROMPT - 1), 0)),
            per_layer(1, D_MODEL), per_layer(D_MODEL, IN_WIDTH), per_layer(1, IN_WIDTH),
            per_layer(1, GMLP_WIDTH), per_layer(1, GMLP_WIDTH),
            per_layer(GMLP_HEADS, CHUNK, CHUNK), per_layer(CHUNK, GMLP_WIDTH),
            pl.BlockSpec(memory_space=pltpu.SMEM),
            const(2, KV_HEADS, WINDOW, GQA_GROUP * WINDOW),
            per_layer(1, GMLP_WIDTH), per_layer(1, ATTN_WIDTH), per_layer(D_MODEL, D_MODEL),
        ],
        out_specs=[
            pl.BlockSpec((TM, D_MODEL), lambda i: (i, 0)),
            pl.BlockSpec((1, WINDOW, KV_WIDTH), lambda i: (jnp.minimum(i // TILES_PER_SEQ, BATCH - 1), 0, 0)),
            pl.BlockSpec((1, WINDOW, KV_WIDTH), lambda i: (jnp.minimum(i // TILES_PER_SEQ, BATCH - 1), 0, 0)),
        ],
        out_shape=[
            jax.ShapeDtypeStruct((T_ALL, D_MODEL), F32),
            jax.ShapeDtypeStruct((BATCH, WINDOW, KV_WIDTH), F32),
            jax.ShapeDtypeStruct((BATCH, WINDOW, KV_WIDTH), F32),
        ],
        scratch_shapes=[pltpu.VMEM((WINDOW, KV_WIDTH), BF16), pltpu.VMEM((WINDOW, KV_WIDTH), BF16)],
        compiler_params=pltpu.CompilerParams(
            dimension_semantics=("arbitrary",), vmem_limit_bytes=VMEM_LIMIT),
        name="mixer_prompt",
    )(x, lw["norm_g"], lw["w_in"], lw["b_in"], lw["ln_g"], lw["ln_b"], lw["w_s"], lw["b_s_prompt"],
      lw["sinks"], lw["bias_prompt"], lw["g_a"], lw["g_b"], lw["w_out"])


def _mixer_sample_kernel(x_ref, xalias_ref, ck_ref, cv_ref, ng_ref, win_ref, bin_ref, lng_ref, lnb_ref,
                         wsd_ref, bsd_ref, sinkrow_ref, bias_ref, ga_ref, gb_ref, wout_ref,
                         xo_ref, ko_ref, vo_ref, cvo_ref,
                         q_s, k_s, v_s, a_s, b_s, kfull, vfull):
    del xalias_ref
    j = pl.program_id(0)

    @pl.when(j == 0)
    def _():
        h = _rms(x_ref[...], ng_ref[...]).astype(BF16)
        z = jnp.dot(h, win_ref[...], preferred_element_type=F32) + bin_ref[...]
        u = _gelu(z[:, :GMLP_WIDTH])
        v = jnp.concatenate(_gmlp_v(z[:, GMLP_WIDTH:Q0], lng_ref, lnb_ref), axis=1)
        cvo_ref[...] = v
        t_row = lax.broadcasted_iota(jnp.int32, (8, 1), 0) % DEC_SEQ

        def tile_rows(pat):
            return jnp.broadcast_to(pat[None], (TM // 8,) + pat.shape).reshape(TM, pat.shape[-1])

        mixed = tile_rows(bsd_ref[...])
        for d in range(DEC_SEQ):
            vd = v if d == 0 else pltpu.roll(v, d, 0)
            mixed = mixed + tile_rows(jnp.where(t_row >= d, wsd_ref[d], 0.0)) * vd
        a_s[...] = _rms(u * mixed, ga_ref[...]).astype(BF16)
        q_s[...] = z[:, Q0:K0] * (HEAD_DIM ** -0.5)
        k_s[...] = z[:, K0:V0]
        v_s[...] = z[:, V0:]
        kfull[:, CACHE_W + DEC_SEQ:, :] = jnp.zeros((SEQ_GROUP, KEYS_PAD - CACHE_W - DEC_SEQ, KV_WIDTH), F32)
        vfull[:, CACHE_W + DEC_SEQ:, :] = jnp.zeros((SEQ_GROUP, KEYS_PAD - CACHE_W - DEC_SEQ, KV_WIDTH), F32)

    base = pl.multiple_of(j * (SEQ_GROUP * DEC_SEQ), SEQ_GROUP * DEC_SEQ)
    kfull[:, :CACHE_W, :] = ck_ref[0]
    vfull[:, :CACHE_W, :] = cv_ref[0]
    k_new = k_s[pl.ds(base, SEQ_GROUP * DEC_SEQ), :]
    v_new = v_s[pl.ds(base, SEQ_GROUP * DEC_SEQ), :]
    for s in range(SEQ_GROUP):
        kfull[s, CACHE_W:CACHE_W + DEC_SEQ, :] = k_new[s * DEC_SEQ:(s + 1) * DEC_SEQ]
        vfull[s, CACHE_W:CACHE_W + DEC_SEQ, :] = v_new[s * DEC_SEQ:(s + 1) * DEC_SEQ]
    ko_ref[0] = kfull[:, DEC_SEQ:DEC_SEQ + CACHE_W, :]
    vo_ref[0] = vfull[:, DEC_SEQ:DEC_SEQ + CACHE_W, :]

    lane = lax.broadcasted_iota(jnp.int32, (1, LANES), 1)
    low = lane < HEAD_DIM
    col_seq = (lane // DEC_SEQ) % SEQS_PER_TILE
    row_seq = (lax.broadcasted_iota(jnp.int32, (LANES, 1), 0) // DEC_SEQ) % SEQS_PER_TILE
    sink = sinkrow_ref[...]
    zero = jnp.zeros((), BF16)
    quad_rows = SEQS_PER_TILE * DEC_SEQ
    for m in range(SEQ_GROUP // SEQS_PER_TILE):
        r0 = pl.multiple_of(base + m * quad_rows, quad_rows)
        q16 = q_s[pl.ds(r0, quad_rows), :]
        qt = jnp.concatenate(
            [jnp.where(low if kv == 0 else jnp.logical_not(low), q16[:, g * LANES:(g + 1) * LANES], 0.0)
             for kv in range(KV_HEADS) for g in range(GQA_GROUP)], axis=0).astype(BF16)
        s = bias_ref[...]
        for n in range(SEQS_PER_TILE):
            kn = kfull[SEQS_PER_TILE * m + n].astype(BF16)
            s = s + _dot_nt(kn, jnp.where(row_seq == n, qt, zero))
        mx = jnp.maximum(jnp.max(s, axis=0, keepdims=True), sink)
        p = jnp.exp(s - mx)
        p = (p / (jnp.sum(p, axis=0, keepdims=True) + jnp.exp(sink - mx))).astype(BF16)
        o = jnp.zeros((LANES, KV_WIDTH), F32)
        for n in range(SEQS_PER_TILE):
            vn = vfull[SEQS_PER_TILE * m + n].astype(BF16)
            o = o + lax.dot_general(jnp.where(col_seq == n, p, zero), vn, (((0,), (0,)), ((), ())),
                                    preferred_element_type=F32)
        b_s[pl.ds(r0, quad_rows), :] = jnp.concatenate(
            [jnp.where(low, o[g * quad_rows:(g + 1) * quad_rows],
                       o[(GQA_GROUP + g) * quad_rows:(GQA_GROUP + g + 1) * quad_rows])
             for g in range(GQA_GROUP)], axis=1)

    @pl.when(j == N_SEQ_GROUPS - 1)
    def _():
        b_n = _rms(b_s[...], gb_ref[...]).astype(BF16)
        merged = jnp.concatenate([a_s[...], b_n], axis=1)
        xo_ref[...] = x_ref[...] + jnp.dot(merged, wout_ref[...], preferred_element_type=F32)


def _mixer_sample(x, x_block, x_buf, cache_k, cache_v, layer, lw):
    const = lambda *shape: pl.BlockSpec(shape, lambda j: (0,) * len(shape))
    per_layer = functools.partial(_layer_block, layer)
    cache_spec = pl.BlockSpec((1, SEQ_GROUP, CACHE_W, KV_WIDTH), lambda j: (layer, j, 0, 0))
    state_spec = pl.BlockSpec((1, SEQ_GROUP, CACHE_W, KV_WIDTH), lambda j: (j, 0, 0, 0))
    state_shape = jax.ShapeDtypeStruct((N_SEQ_GROUPS, SEQ_GROUP, CACHE_W, KV_WIDTH), F32)
    return pl.pallas_call(
        _mixer_sample_kernel,
        grid=(N_SEQ_GROUPS,),
        in_specs=[
            pl.BlockSpec((TM, D_MODEL), lambda j: (x_block, 0)),
            pl.BlockSpec(memory_space=pl.ANY),
            cache_spec, cache_spec,
            per_layer(1, D_MODEL), per_layer(D_MODEL, IN_WIDTH), per_layer(1, IN_WIDTH),
            per_layer(1, GMLP_WIDTH), per_layer(1, GMLP_WIDTH),
            per_layer(DEC_SEQ, 8, GMLP_WIDTH), per_layer(8, GMLP_WIDTH),
            per_layer(1, LANES), const(KEYS_PAD, LANES),
            per_layer(1, GMLP_WIDTH), per_layer(1, ATTN_WIDTH), per_layer(D_MODEL, D_MODEL),
        ],
        out_specs=[
            pl.BlockSpec((TM, D_MODEL), lambda j: (NT_PROMPT, 0)),
            state_spec, state_spec,
            const(TM, GMLP_WIDTH),
        ],
        out_shape=[
            jax.ShapeDtypeStruct((T_ALL, D_MODEL), F32),
            state_shape, state_shape,
            jax.ShapeDtypeStruct((TM, GMLP_WIDTH), F32),
        ],
        scratch_shapes=[
            pltpu.VMEM((TM, ATTN_WIDTH), F32), pltpu.VMEM((TM, KV_WIDTH), F32),
            pltpu.VMEM((TM, KV_WIDTH), F32), pltpu.VMEM((TM, GMLP_WIDTH), BF16),
            pltpu.VMEM((TM, ATTN_WIDTH), F32),
            pltpu.VMEM((SEQ_GROUP, KEYS_PAD, KV_WIDTH), F32),
            pltpu.VMEM((SEQ_GROUP, KEYS_PAD, KV_WIDTH), F32),
        ],
        input_output_aliases={1: 0},
        compiler_params=pltpu.CompilerParams(
            dimension_semantics=("arbitrary",), vmem_limit_bytes=VMEM_LIMIT),
        name="mixer_sample",
    )(x, x_buf, cache_k, cache_v, lw["norm_g"], lw["w_in"], lw["b_in"], lw["ln_g"], lw["ln_b"],
      lw["w_s_sample"], lw["b_s_sample"], lw["sink_row"], lw["bias_sample"],
      lw["g_a"], lw["g_b"], lw["w_out"])


def _swiglu_kernel(x_ref, ng_ref, wg_ref, wu_ref, wd_ref, eg_ref, eu_ref, ed_ref,
                   xo_ref, ego_ref, euo_ref, edo_ref):
    x = x_ref[...]
    h = _rms(x, ng_ref[...]).astype(BF16)
    gate = jnp.dot(h, wg_ref[...], preferred_element_type=F32)
    ego_ref[...] = eg_ref[...].astype(BF16)
    up = jnp.dot(h, wu_ref[...], preferred_element_type=F32)
    euo_ref[...] = eu_ref[...].astype(BF16)
    act = (jax.nn.silu(gate) * up).astype(BF16)
    xo_ref[...] = x + jnp.dot(act, wd_ref[...], preferred_element_type=F32)
    edo_ref[...] = ed_ref[...].astype(BF16)


def _swiglu(x, norm_g, wg, wu, wd, layer, ffn_index, expert_f32):
    weights = functools.partial(_layer_block, ffn_index, pipeline_mode=pl.Buffered(1))
    slab_steps = NT_ALL - 1
    up_rows = N_EXPERTS * D_MODEL // slab_steps
    down_rows = N_EXPERTS * D_FF_EXPERT // slab_steps
    slab = lambda i: jnp.minimum(i, slab_steps - 1)
    eg, eu, ed = expert_f32
    eg = eg.reshape(-1, N_EXPERTS * D_MODEL, D_FF_EXPERT)
    eu = eu.reshape(-1, N_EXPERTS * D_MODEL, D_FF_EXPERT)
    ed = ed.reshape(-1, N_EXPERTS * D_FF_EXPERT, D_MODEL)
    x_new, eg_bf, eu_bf, ed_bf = pl.pallas_call(
        _swiglu_kernel,
        grid=(NT_ALL,),
        in_specs=[
            pl.BlockSpec((TM, D_MODEL), lambda i: (i, 0)),
            _layer_block(layer, 1, D_MODEL),
            weights(D_MODEL, D_FF), weights(D_MODEL, D_FF), weights(D_FF, D_MODEL),
            pl.BlockSpec((None, up_rows, D_FF_EXPERT), lambda i: (ffn_index, slab(i), 0)),
            pl.BlockSpec((None, up_rows, D_FF_EXPERT), lambda i: (ffn_index, slab(i), 0)),
            pl.BlockSpec((None, down_rows, D_MODEL), lambda i: (ffn_index, slab(i), 0)),
        ],
        out_specs=[
            pl.BlockSpec((TM, D_MODEL), lambda i: (i, 0)),
            pl.BlockSpec((up_rows, D_FF_EXPERT), lambda i: (slab(i), 0)),
            pl.BlockSpec((up_rows, D_FF_EXPERT), lambda i: (slab(i), 0)),
            pl.BlockSpec((down_rows, D_MODEL), lambda i: (slab(i), 0)),
        ],
        out_shape=[
            jax.ShapeDtypeStruct((T_ALL, D_MODEL), F32),
            jax.ShapeDtypeStruct((N_EXPERTS * D_MODEL, D_FF_EXPERT), BF16),
            jax.ShapeDtypeStruct((N_EXPERTS * D_MODEL, D_FF_EXPERT), BF16),
            jax.ShapeDtypeStruct((N_EXPERTS * D_FF_EXPERT, D_MODEL), BF16),
        ],
        compiler_params=pltpu.CompilerParams(
            dimension_semantics=("arbitrary",), vmem_limit_bytes=VMEM_LIMIT),
        name="swiglu",
    )(x, norm_g, wg, wu, wd, eg, eu, ed)
    return (x_new, eg_bf.reshape(1, N_EXPERTS, D_MODEL, D_FF_EXPERT),
            eu_bf.reshape(1, N_EXPERTS, D_MODEL, D_FF_EXPERT),
            ed_bf.reshape(1, N_EXPERTS, D_FF_EXPERT, D_MODEL))


def _moe_sort_kernel(x_ref, ng_ref, wrh_ref, wrl_ref, hs_ref, meta_ref, off_ref, cnt_ref):
    n = SORT_TILES_PER_STEP
    logits, routes = {}, {}
    for step in range(n + 2):
        if step < n:
            logits[step] = _sort_logits(x_ref[step * TM:(step + 1) * TM, :], ng_ref, wrh_ref, wrl_ref)
        j = step - 1
        if 0 <= j < n:
            h_hi, lg = logits.pop(j)
            meta, off, cnt_pad = _sort_route(lg)
            meta_ref[j * TM:(j + 1) * TM, :] = meta
            off_ref[j] = off.astype(jnp.int32)
            cnt_ref[j] = cnt_pad.astype(jnp.int32)
            routes[j] = (h_hi, meta)
        j = step - 2
        if 0 <= j < n:
            hs_ref[j * SORT_ROWS:(j + 1) * SORT_ROWS, :] = _sort_gather(*routes.pop(j))


def _sort_logits(x, ng_ref, wrh_ref, wrl_ref):
    h = _rms(x, ng_ref[...])
    h_hi = h.astype(BF16)
    h_lo = (h - h_hi.astype(F32)).astype(BF16)
    logits = (jnp.dot(h_hi, wrh_ref[...], preferred_element_type=F32)
              + jnp.dot(h_lo, wrh_ref[...], preferred_element_type=F32)
              + jnp.dot(h_hi, wrl_ref[...], preferred_element_type=F32))
    return h_hi, logits


def _sort_route(logits):
    lane = lax.broadcasted_iota(jnp.int32, (1, LANES), 1)
    logits = jnp.where(lane < N_EXPERTS, logits, NEG_INF)
    m1 = jnp.max(logits, axis=-1, keepdims=True)
    i1 = jnp.min(jnp.where(logits == m1, lane, LANES), axis=-1, keepdims=True)
    rest = jnp.where(lane == i1, NEG_INF, logits)
    m2 = jnp.max(rest, axis=-1, keepdims=True)
    i2 = jnp.min(jnp.where(rest == m2, lane, LANES), axis=-1, keepdims=True)
    e2 = jnp.exp(m2 - m1)
    den = 1.0 + e2
    sel0 = lane == i1
    sel1 = lane == i2
    routed = jnp.where(jnp.logical_or(sel0, sel1), 1.0, 0.0)
    r_i = lax.broadcasted_iota(jnp.int32, (TM, TM), 0)
    c_i = lax.broadcasted_iota(jnp.int32, (TM, TM), 1)
    lower = jnp.where(r_i > c_i, 1.0, 0.0).astype(BF16)
    rank = jnp.dot(lower, routed.astype(BF16), preferred_element_type=F32)
    cnt = jnp.broadcast_to(jnp.sum(routed, axis=0, keepdims=True), (8, LANES))
    cnt_pad = jnp.floor((cnt + (PIECE - 1)) * (1.0 / PIECE)) * PIECE
    incl = cnt_pad
    for sh in (1, 2, 4):
        incl = incl + jnp.where(lane >= sh, pltpu.roll(incl, sh, 1), 0.0)
    off = incl - cnt_pad
    pos = off[0:1] + rank
    pos0 = jnp.sum(jnp.where(sel0, pos, 0.0), axis=-1, keepdims=True)
    pos1 = jnp.sum(jnp.where(sel1, pos, 0.0), axis=-1, keepdims=True)
    meta = jnp.where(lane == 0, pos0, jnp.where(lane == 1, pos1,
                     jnp.where(lane == 2, 1.0 / den, jnp.where(lane == 3, e2 / den, 0.0))))
    return meta, off, cnt_pad


def _sort_gather(h_hi, meta):
    meta_t = meta.T
    p_row = lax.broadcasted_iota(jnp.int32, (SORT_ROWS, 1), 0).astype(F32)
    hit = jnp.logical_or(p_row == meta_t[0:1], p_row == meta_t[1:2])
    gather = jnp.where(hit, 1.0, 0.0).astype(BF16)
    return jnp.dot(gather, h_hi, preferred_element_type=F32).astype(BF16)


def _moe_sort(x, norm_g, wr_hi, wr_lo, layer, moe_index):
    return pl.pallas_call(
        _moe_sort_kernel,
        grid=(NT_ALL // SORT_TILES_PER_STEP,),
        in_specs=[
            pl.BlockSpec((SORT_TILES_PER_STEP * TM, D_MODEL), lambda i: (i, 0)),
            _layer_block(layer, 1, D_MODEL),
            _layer_block(moe_index, D_MODEL, LANES),
            _layer_block(moe_index, D_MODEL, LANES),
        ],
        out_specs=[
            pl.BlockSpec((SORT_TILES_PER_STEP * SORT_ROWS, D_MODEL), lambda i: (i, 0)),
            pl.BlockSpec((SORT_TILES_PER_STEP * TM, LANES), lambda i: (i, 0)),
            pl.BlockSpec((SORT_TILES_PER_STEP, 8, LANES), lambda i: (i, 0, 0)),
            pl.BlockSpec((SORT_TILES_PER_STEP, 8, LANES), lambda i: (i, 0, 0)),
        ],
        out_shape=[
            jax.ShapeDtypeStruct((NT_ALL * SORT_ROWS, D_MODEL), BF16),
            jax.ShapeDtypeStruct((T_ALL, LANES), F32),
            jax.ShapeDtypeStruct((NT_ALL, 8, LANES), jnp.int32),
            jax.ShapeDtypeStruct((NT_ALL, 8, LANES), jnp.int32),
        ],
        compiler_params=pltpu.CompilerParams(
            dimension_semantics=("arbitrary",), vmem_limit_bytes=VMEM_LIMIT),
        name="moe_sort",
    )(x, norm_g, wr_hi, wr_lo)


def _moe_experts_kernel(off_ref, cnt_ref, hs_hbm, wg_ref, wu_ref, wrem_ref, wd_ref, ys_hbm,
                        buf, lhs, sem_in, sem_out):
    s = pl.program_id(0)
    e = pl.program_id(1)
    row0 = pl.multiple_of(s * SUPER_ROWS, PIECE)
    is_first = e == 0
    is_last = e == N_EXPERTS - 1

    def copy_in(t):
        r = pl.multiple_of(t * SORT_ROWS, PIECE)
        return pltpu.make_async_copy(hs_hbm.at[pl.ds(row0 + r, SORT_ROWS)], buf.at[pl.ds(r, SORT_ROWS)],
                                     sem_in.at[t])

    def copy_out(t):
        r = pl.multiple_of(t * SORT_ROWS, PIECE)
        return pltpu.make_async_copy(buf.at[pl.ds(r, SORT_ROWS)], ys_hbm.at[pl.ds(row0 + r, SORT_ROWS)],
                                     sem_out.at[t])

    def for_tiles(lo, hi, fn):
        hi = jnp.maximum(lo, hi)

        def body(t, carry):
            fn(t)
            return carry

        lax.fori_loop(lo, hi, body, 0)
        return hi

    @pl.when(is_first)
    def _():
        for t in range(TILES_PER_SUPER):
            copy_in(t).start()

    starts, cums = [], [0]
    for i in range(TILES_PER_SUPER):
        idx = (s * TILES_PER_SUPER + i) * N_EXPERTS + e
        starts.append(i * SORT_ROWS + off_ref[idx])
        cums.append(cums[-1] + cnt_ref[idx] // PIECE)
    n_pieces = cums[-1]

    def locate(p):
        pc = jnp.minimum(p, n_pieces - 1)
        row = starts[0] + PIECE * pc
        tile = jnp.int32(0)
        for i in range(1, TILES_PER_SUPER):
            inside = pc >= cums[i]
            row = jnp.where(inside, starts[i] + PIECE * (pc - cums[i]), row)
            tile = jnp.where(inside, i, tile)
        return row, tile

    def piece_rows(q):
        src, dst, tiles = [], [], []
        for j in range(PIECES_PER_CHUNK):
            p = q * PIECES_PER_CHUNK + j
            row, tile = locate(p)
            src.append(row)
            dst.append(jnp.where(p < n_pieces, row, SUPER_ROWS))
            tiles.append(tile)
        return tuple(src), tuple(dst), tiles[0], tiles[-1]

    def chunk_body(q, carry):
        (src, dst, _, last_tile), arrived, sent = carry
        arrived = for_tiles(arrived, jnp.where(is_first, last_tile + 1, 0), lambda t: copy_in(t).wait())
        for j in range(PIECES_PER_CHUNK):
            lhs[j * PIECE:(j + 1) * PIECE, :] = buf[pl.ds(pl.multiple_of(src[j], PIECE), PIECE), :]
        xs = lhs[...]
        gate = jnp.dot(xs, wg_ref[0, :, :FF_MAIN], preferred_element_type=F32)
        up = jnp.dot(xs, wu_ref[0, :, :FF_MAIN], preferred_element_type=F32)
        rem = jnp.dot(xs, wrem_ref[0], preferred_element_type=F32)
        nxt = piece_rows(q + 1)
        act = jnp.concatenate([jax.nn.silu(gate) * up,
                               jax.nn.silu(rem[:, :FF_REM]) * rem[:, FF_REM:]], axis=1).astype(BF16)
        y = jnp.dot(act, wd_ref[0], preferred_element_type=F32).astype(BF16)
        for j in range(PIECES_PER_CHUNK):
            buf[pl.ds(pl.multiple_of(dst[j], PIECE), PIECE), :] = y[j * PIECE:(j + 1) * PIECE]
        sent = for_tiles(sent, jnp.where(is_last, nxt[2], 0), lambda t: copy_out(t).start())
        return nxt, arrived, sent

    n_chunks = (n_pieces + PIECES_PER_CHUNK - 1) // PIECES_PER_CHUNK
    _, arrived, sent = lax.fori_loop(0, n_chunks, chunk_body, (piece_rows(0), jnp.int32(0), jnp.int32(0)))
    for_tiles(arrived, jnp.where(is_first, TILES_PER_SUPER, 0), lambda t: copy_in(t).wait())
    for_tiles(sent, jnp.where(is_last, TILES_PER_SUPER, 0), lambda t: copy_out(t).start())
    for_tiles(0, jnp.where(is_last, TILES_PER_SUPER, 0), lambda t: copy_out(t).wait())


def _moe_experts(off, cnt, hs, wg, wu, wrem, wd, moe_index):
    expert = lambda *shape: pl.BlockSpec((None, 1) + shape, lambda s, e, *_: (moe_index, e, 0, 0))
    grid_spec = pltpu.PrefetchScalarGridSpec(
        num_scalar_prefetch=2,
        grid=(N_SUPER, N_EXPERTS),
        in_specs=[
            pl.BlockSpec(memory_space=pl.ANY),
            expert(D_MODEL, D_FF_EXPERT), expert(D_MODEL, D_FF_EXPERT), expert(D_MODEL, 2 * FF_REM),
            expert(D_FF_EXPERT, D_MODEL),
        ],
        out_specs=pl.BlockSpec(memory_space=pl.ANY),
        scratch_shapes=[pltpu.VMEM((SUPER_ROWS + PIECE, D_MODEL), BF16),
                        pltpu.VMEM((CHUNK_ROWS, D_MODEL), BF16),
                        pltpu.SemaphoreType.DMA((TILES_PER_SUPER,)),
                        pltpu.SemaphoreType.DMA((TILES_PER_SUPER,))],
    )
    return pl.pallas_call(
        _moe_experts_kernel,
        grid_spec=grid_spec,
        out_shape=jax.ShapeDtypeStruct((NT_ALL * SORT_ROWS, D_MODEL), BF16),
        compiler_params=pltpu.CompilerParams(
            dimension_semantics=("arbitrary", "arbitrary"), vmem_limit_bytes=VMEM_LIMIT),
        name="moe_experts",
    )(off, cnt, hs, wg, wu, wrem, wd)


def _unsorted_residual(x_ref, ys_ref, meta_ref):
    meta = meta_ref[...]
    p_col = lax.broadcasted_iota(jnp.int32, (1, SORT_ROWS), 1).astype(F32)
    scatter = (jnp.where(p_col == meta[:, 0:1], meta[:, 2:3], 0.0)
               + jnp.where(p_col == meta[:, 1:2], meta[:, 3:4], 0.0)).astype(BF16)
    return x_ref[...] + jnp.dot(scatter, ys_ref[...], preferred_element_type=F32)


def _moe_unsort_kernel(x_ref, ys_ref, meta_ref, xo_ref):
    xo_ref[...] = _unsorted_residual(x_ref, ys_ref, meta_ref)


def _moe_unsort_final_kernel(x_ref, ys_ref, meta_ref, g_ref, yp_ref, ysm_ref):
    i = pl.program_id(0)
    y = _rms(_unsorted_residual(x_ref, ys_ref, meta_ref), g_ref[...])

    @pl.when(i < NT_PROMPT)
    def _():
        yp_ref[...] = y

    @pl.when(i == NT_PROMPT)
    def _():
        ysm_ref[...] = y


def _moe_unsort(x, ys, meta, final_g=None):
    in_specs = [
        pl.BlockSpec((TM, D_MODEL), lambda i: (i, 0)),
        pl.BlockSpec((SORT_ROWS, D_MODEL), lambda i: (i, 0)),
        pl.BlockSpec((TM, LANES), lambda i: (i, 0)),
    ]
    params = pltpu.CompilerParams(dimension_semantics=("arbitrary",), vmem_limit_bytes=VMEM_LIMIT)
    if final_g is None:
        return pl.pallas_call(
            _moe_unsort_kernel,
            grid=(NT_ALL,),
            in_specs=in_specs,
            out_specs=pl.BlockSpec((TM, D_MODEL), lambda i: (i, 0)),
            out_shape=jax.ShapeDtypeStruct((T_ALL, D_MODEL), F32),
            compiler_params=params,
            name="moe_unsort",
        )(x, ys, meta)
    return pl.pallas_call(
        _moe_unsort_final_kernel,
        grid=(NT_ALL,),
        in_specs=in_specs + [pl.BlockSpec((1, D_MODEL), lambda i: (0, 0))],
        out_specs=[pl.BlockSpec((TM, D_MODEL), lambda i: (jnp.minimum(i, NT_PROMPT - 1), 0)),
                   pl.BlockSpec((TM, D_MODEL), lambda i: (0, 0))],
        out_shape=[jax.ShapeDtypeStruct((T_PROMPT, D_MODEL), F32),
                   jax.ShapeDtypeStruct((T_SAMPLE, D_MODEL), F32)],
        compiler_params=params,
        name="moe_unsort_final",
    )(x, ys, meta, final_g)


def _moe(x, norm_g, wr_hi, wr_lo, wg, wu, wrem, wd, layer, moe_index, final_g=None):
    hs, meta, off, cnt = _moe_sort(x, norm_g, wr_hi, wr_lo, layer, moe_index)
    off = off[:, 0, :N_EXPERTS].reshape(-1)
    cnt = cnt[:, 0, :N_EXPERTS].reshape(-1)
    ys = _moe_experts(off, cnt, hs, wg, wu, wrem, wd, moe_index)
    return _moe_unsort(x, ys, meta, final_g)


def kernel(x_prompt, x_sample, cache_win_k, cache_win_v, attn_norm_g, w_in, b_in, chunk_ln_g, chunk_ln_b, w_spatial, b_spatial, attn_sinks, mix_norm_a_g, mix_norm_b_g, w_out, ffn_norm_g, ffn_w_gate, ffn_w_up, ffn_w_down, router_w, expert_w_gate, expert_w_up, expert_w_down, final_norm_g):
    w_in_p = jnp.concatenate([w_in[..., :Q0], _g_major(w_in[..., Q0:K0], -1), w_in[..., K0:]], axis=-1).astype(BF16)
    b_in_p = jnp.concatenate([b_in[..., :Q0], _g_major(b_in[..., Q0:K0], -1), b_in[..., K0:]], axis=-1)
    w_out_p = jnp.concatenate([w_out[:, :GMLP_WIDTH], _g_major(w_out[:, GMLP_WIDTH:], 1)], axis=1).astype(BF16)
    g_b_p = _g_major(mix_norm_b_g, 1)
    bs_prompt = jnp.repeat(jnp.transpose(b_spatial, (0, 2, 1)), GMLP_HEAD_DIM, axis=2)
    bs_sample = jnp.tile(bs_prompt[:, :DEC_SEQ], (1, 2, 1))
    t_idx = np.arange(DEC_SEQ)
    ws_small = jnp.stack([w_spatial[:, :, t_idx, np.maximum(t_idx - d, 0)] for d in range(DEC_SEQ)], axis=1)
    ws_sample = jnp.tile(jnp.repeat(jnp.transpose(ws_small, (0, 1, 3, 2)), GMLP_HEAD_DIM, axis=3),
                         (1, 1, 2, 1))
    sink_row = jnp.repeat(attn_sinks, SEQS_PER_TILE * DEC_SEQ, axis=1)[:, None, :]
    bias_prompt = jnp.asarray(_prompt_bias())
    bias_sample = jnp.asarray(_sample_bias())
    wr_pad = jnp.pad(router_w, ((0, 0), (0, 0), (0, LANES - N_EXPERTS)))
    wr_hi = wr_pad.astype(BF16)
    wr_lo = (wr_pad - wr_hi.astype(F32)).astype(BF16)
    ffn_wg, ffn_wu, ffn_wd = ffn_w_gate.astype(BF16), ffn_w_up.astype(BF16), ffn_w_down.astype(BF16)

    ck = cache_win_k.reshape(DEPTH, DEC_BATCH, CACHE_W, KV_WIDTH)
    cv = cache_win_v.reshape(DEPTH, DEC_BATCH, CACHE_W, KV_WIDTH)

    lw = dict(
        norm_g=attn_norm_g[:, None, :], w_in=w_in_p, b_in=b_in_p[:, None, :],
        ln_g=chunk_ln_g.reshape(DEPTH, 1, GMLP_WIDTH), ln_b=chunk_ln_b.reshape(DEPTH, 1, GMLP_WIDTH),
        w_s=w_spatial, b_s_prompt=bs_prompt, w_s_sample=ws_sample, b_s_sample=bs_sample,
        sink_row=sink_row, bias_prompt=bias_prompt, bias_sample=bias_sample,
        g_a=mix_norm_a_g[:, None, :], g_b=g_b_p[:, None, :], w_out=w_out_p)
    ffn_g = ffn_norm_g[:, None, :]

    x = None
    kp, vp, ksm, vsm, cvs = [], [], [], [], []
    for l in range(DEPTH):
        lw["sinks"] = attn_sinks[l]
        if l == 0:
            x_new, k_p, v_p = _mixer_prompt(x_prompt.reshape(T_PROMPT, D_MODEL), lw, l)
            x_new, k_s, v_s, cv_s = _mixer_sample(x_sample.reshape(T_SAMPLE, D_MODEL), 0, x_new, ck, cv, l, lw)
        else:
            x_new, k_p, v_p = _mixer_prompt(x, lw, l)
            x_new, k_s, v_s, cv_s = _mixer_sample(x, NT_PROMPT, x_new, ck, cv, l, lw)
        kp.append(k_p)
        vp.append(v_p)
        ksm.append(k_s)
        vsm.append(v_s)
        cvs.append(cv_s)
        i = l // 2
        if l % 2 == 0:
            x, ex_wg, ex_wu, ex_wd = _swiglu(x_new, ffn_g, ffn_wg, ffn_wu, ffn_wd, l, i,
                                             (expert_w_gate, expert_w_up, expert_w_down))
            ex_wrem = jnp.concatenate([ex_wg[..., FF_MAIN:], ex_wu[..., FF_MAIN:]], axis=-1)
        else:
            final_g = final_norm_g[None] if l == DEPTH - 1 else None
            x = _moe(x_new, ffn_g, wr_hi[i:i + 1], wr_lo[i:i + 1], ex_wg, ex_wu, ex_wrem, ex_wd, l, 0, final_g)
    y_p, y_s = x
    win_p = (DEPTH, BATCH, WINDOW, KV_HEADS, HEAD_DIM)
    win_s = (DEPTH, DEC_BATCH, CACHE_W, KV_HEADS, HEAD_DIM)
    return (y_p.reshape(BATCH, SEQ, D_MODEL), y_s.reshape(DEC_BATCH, DEC_SEQ, D_MODEL),
            jnp.stack(kp).reshape(win_p), jnp.stack(vp).reshape(win_p),
            jnp.stack(ksm).reshape(win_s), jnp.stack(vsm).reshape(win_s),
            jnp.stack(cvs).reshape(DEPTH, DEC_BATCH, DEC_SEQ, GMLP_HEADS, GMLP_HEAD_DIM))
```

```python
import functools

import numpy as np
import jax
import jax.numpy as jnp
from jax import lax
from jax.experimental import pallas as pl
from jax.experimental.pallas import tpu as pltpu

D_MODEL = 1024
BATCH = 4
SEQ = 4096
DEPTH = 4
DEC_BATCH = 128
DEC_SEQ = 4
HEAD_DIM = 64
ATTN_HEADS = 8
KV_HEADS = 2
GQA_GROUP = ATTN_HEADS // KV_HEADS
ATTN_WIDTH = ATTN_HEADS * HEAD_DIM
KV_WIDTH = KV_HEADS * HEAD_DIM
WINDOW = 128
CHUNK = 128
GMLP_HEADS = 4
GMLP_HEAD_DIM = 128
GMLP_WIDTH = GMLP_HEADS * GMLP_HEAD_DIM
IN_WIDTH = 2 * GMLP_WIDTH + ATTN_WIDTH + 2 * KV_WIDTH
D_FF = 2816
N_EXPERTS = 8
D_FF_EXPERT = 1408
CACHE_W = 128
EPS = 1e-6

LANES = 128
TM = 512
T_PROMPT = BATCH * SEQ
T_SAMPLE = DEC_BATCH * DEC_SEQ
T_ALL = T_PROMPT + T_SAMPLE
NT_PROMPT = T_PROMPT // TM
NT_ALL = T_ALL // TM
TILES_PER_SEQ = SEQ // TM
BLOCKS_PER_TILE = TM // WINDOW
SEQ_GROUP = 16
SEQS_PER_TILE = 4
N_SEQ_GROUPS = DEC_BATCH // SEQ_GROUP
KEYS_PAD = CACHE_W + 8
PIECE = 16
SORT_ROWS = 2 * TM + N_EXPERTS * PIECE
TILES_PER_SUPER = 11
N_SUPER = NT_ALL // TILES_PER_SUPER
SUPER_ROWS = TILES_PER_SUPER * SORT_ROWS
SORT_TILES_PER_STEP = 3
CHUNK_ROWS = 256
FF_REM = D_FF_EXPERT % 256
FF_MAIN = D_FF_EXPERT - FF_REM
PIECES_PER_CHUNK = CHUNK_ROWS // PIECE
VMEM_LIMIT = 56 * 1024 * 1024

F32 = jnp.float32
BF16 = jnp.bfloat16
NEG_INF = float("-inf")

Q0 = 2 * GMLP_WIDTH
K0 = Q0 + ATTN_WIDTH
V0 = K0 + KV_WIDTH


def _g_major(a, axis):
    axis = axis % a.ndim
    shape = a.shape
    a = a.reshape(shape[:axis] + (KV_HEADS, GQA_GROUP, HEAD_DIM) + shape[axis + 1:])
    return jnp.swapaxes(a, axis, axis + 1).reshape(shape)


def _slope(kv, g):
    return 2.0 ** (-8.0 * (kv * GQA_GROUP + g + 1) / ATTN_HEADS)


def _prompt_bias():
    c = np.arange(WINDOW)[:, None]
    qi = np.arange(WINDOW)[None, :]
    dist = np.where(c > qi, qi + WINDOW - c, qi - c)
    out = np.zeros((2, KV_HEADS, WINDOW, GQA_GROUP * WINDOW), np.float32)
    for var in range(2):
        ok = ~((var == 1) & (c > qi))
        for kv in range(KV_HEADS):
            for g in range(GQA_GROUP):
                out[var, kv, :, g * WINDOW:(g + 1) * WINDOW] = np.where(ok, -_slope(kv, g) * dist, NEG_INF)
    return out


def _sample_bias():
    out = np.zeros((KEYS_PAD, LANES), np.float32)
    j = np.arange(KEYS_PAD)
    kpos = j - CACHE_W
    for kv in range(KV_HEADS):
        for g in range(GQA_GROUP):
            for s in range(SEQS_PER_TILE):
                for t in range(DEC_SEQ):
                    col = ((kv * GQA_GROUP + g) * SEQS_PER_TILE + s) * DEC_SEQ + t
                    dist = t - kpos
                    ok = (dist >= 0) & (dist < WINDOW) & (j < CACHE_W + DEC_SEQ)
                    out[:, col] = np.where(ok, -_slope(kv, g) * dist, NEG_INF)
    return out


def _rms(x, g):
    return x * lax.rsqrt(jnp.mean(x * x, axis=-1, keepdims=True) + EPS) * g


def _gelu(x):
    return 0.5 * x * (1.0 + lax.erf(x * (2.0 ** -0.5)))


def _gmlp_v(z_v, lng_ref, lnb_ref):
    out = []
    for hh in range(GMLP_HEADS):
        sl = slice(hh * LANES, (hh + 1) * LANES)
        vh = _gelu(z_v[:, sl])
        vc = vh - jnp.mean(vh, axis=-1, keepdims=True)
        y = vc * lax.rsqrt(jnp.mean(vc * vc, axis=-1, keepdims=True) + EPS)
        out.append(y * lng_ref[:, sl] + lnb_ref[:, sl])
    return out


def _dot_nt(a, b):
    return lax.dot_general(a, b, (((1,), (1,)), ((), ())), preferred_element_type=F32)


N_MIXER_INPUTS = 13


def _mixer_prompt_kernel(*refs, n_cast):
    i = pl.program_id(0)
    n_in = N_MIXER_INPUTS + n_cast
    tile_refs = refs[:N_MIXER_INPUTS] + refs[n_in:n_in + 3] + refs[n_in + 3 + n_cast:]

    @pl.when(i < NT_PROMPT)
    def _():
        _mixer_prompt_tile(*tile_refs)

    @pl.when(i == NT_PROMPT)
    def _():
        xo_ref = refs[n_in]
        xo_ref[...] = jnp.zeros_like(xo_ref)

    for src, dst in zip(refs[N_MIXER_INPUTS:n_in], refs[n_in + 3:n_in + 3 + n_cast]):
        dst[...] = src[...].astype(BF16)


def _mixer_prompt_tile(x_ref, ng_ref, win_ref, bin_ref, lng_ref, lnb_ref, ws_ref, bs_ref,
                       sink_ref, bias_ref, ga_ref, gb_ref, wout_ref,
                       xo_ref, ks_ref, vs_ref, kprev_ref, vprev_ref):
    i = pl.program_id(0)

    @pl.when(i == 0)
    def _():
        kprev_ref[...] = jnp.zeros_like(kprev_ref)
        vprev_ref[...] = jnp.zeros_like(vprev_ref)

    x = x_ref[...]
    h = _rms(x, ng_ref[...]).astype(BF16)

    def in_proj(lo, hi):
        return jnp.dot(h, win_ref[:, lo:hi], preferred_element_type=F32) + bin_ref[:, lo:hi]

    z_att = in_proj(Q0, IN_WIDTH)
    q = (z_att[:, :ATTN_WIDTH] * (HEAD_DIM ** -0.5)).astype(BF16)
    k = z_att[:, ATTN_WIDTH:ATTN_WIDTH + KV_WIDTH]
    val = z_att[:, ATTN_WIDTH + KV_WIDTH:]
    k_bf = k.astype(BF16)
    lane = lax.broadcasted_iota(jnp.int32, (1, LANES), 1)
    low = lane < HEAD_DIM
    first = jnp.where(i % TILES_PER_SEQ == 0, 1, 0)
    c_i = lax.broadcasted_iota(jnp.int32, (WINDOW, WINDOW), 0)
    q_i = lax.broadcasted_iota(jnp.int32, (WINDOW, WINDOW), 1)
    from_prev = jnp.concatenate([c_i > q_i] * GQA_GROUP, axis=1)
    g_lane = lax.broadcasted_iota(jnp.int32, (1, GQA_GROUP * WINDOW), 1) // WINDOW
    top = lax.broadcasted_iota(jnp.int32, (KV_WIDTH, 1), 0) < HEAD_DIM
    one = jnp.ones((), BF16)
    vt_bf = val.T.astype(BF16)
    vt_all = jnp.concatenate([vprev_ref[...], vt_bf], axis=1)
    vt_aug = [jnp.where(top, vt_all, one), jnp.where(top, one, vt_all)]
    k_all = jnp.concatenate([kprev_ref[...], k_bf], axis=0)

    zero = jnp.zeros((), BF16)
    sinks = []
    for kv in range(KV_HEADS):
        sink = jnp.full((1, GQA_GROUP * WINDOW), sink_ref[kv * GQA_GROUP], F32)
        for g in range(1, GQA_GROUP):
            sink = jnp.where(g_lane == g, sink_ref[kv * GQA_GROUP + g], sink)
        sinks.append(sink)

    def attention_block(c):
        rows = slice(c * WINDOW, (c + 1) * WINDOW)
        band = slice(c * WINDOW, (c + 2) * WINDOW)
        o_t = []
        for kv in range(KV_HEADS):
            keep = low if kv == 0 else jnp.logical_not(low)
            qm = jnp.concatenate(
                [jnp.where(keep, q[rows, g * LANES:(g + 1) * LANES], zero) for g in range(GQA_GROUP)],
                axis=0)
            s2 = _dot_nt(k_all[band], qm)
            bias = bias_ref[first, kv] if c == 0 else bias_ref[0, kv]
            s = jnp.where(from_prev, s2[:WINDOW], s2[WINDOW:]) + bias
            m = jnp.maximum(jnp.max(s, axis=0, keepdims=True), sinks[kv])
            p = jnp.exp(s - m)
            p2 = jnp.concatenate([jnp.where(from_prev, p, 0.0), jnp.where(from_prev, 0.0, p)],
                                 axis=0).astype(BF16)
            o = jnp.dot(vt_aug[kv][:, band], p2, preferred_element_type=F32)
            own, ones_row = (o[:HEAD_DIM], o[HEAD_DIM:HEAD_DIM + 1]) if kv == 0 else (o[HEAD_DIM:], o[0:1])
            o_t.append(own / (ones_row + jnp.exp(sinks[kv] - m)))
        bt = jnp.concatenate(o_t, axis=0)
        return jnp.concatenate(
            [bt[:, g * WINDOW:(g + 1) * WINDOW].T for g in range(GQA_GROUP)], axis=1)

    z_u = in_proj(0, GMLP_WIDTH)
    b_rows = [attention_block(0)]
    z_v = in_proj(GMLP_WIDTH, Q0)
    b_rows.append(attention_block(1))
    u = _gelu(z_u)
    b_rows.append(attention_block(2))
    v_heads = _gmlp_v(z_v, lng_ref, lnb_ref)
    row = lax.broadcasted_iota(jnp.int32, (CHUNK, CHUNK), 0)
    col = lax.broadcasted_iota(jnp.int32, (CHUNK, CHUNK), 1)
    a_cols = []
    for hh in range(GMLP_HEADS):
        w = jnp.where(row >= col, ws_ref[hh], 0.0).astype(BF16)
        v_cat = jnp.concatenate(
            [v_heads[hh][c * CHUNK:(c + 1) * CHUNK] for c in range(BLOCKS_PER_TILE)], axis=1)
        mixed = jnp.dot(w, v_cat.astype(BF16), preferred_element_type=F32)
        bias = bs_ref[:, hh * LANES:(hh + 1) * LANES]
        a_cols.append(jnp.concatenate(
            [mixed[:, c * LANES:(c + 1) * LANES] + bias for c in range(BLOCKS_PER_TILE)], axis=0))
    a = u * jnp.concatenate(a_cols, axis=1)
    a_n = _rms(a, ga_ref[...]).astype(BF16)
    b_rows.append(attention_block(3))
    out = x + jnp.dot(a_n, wout_ref[:GMLP_WIDTH, :], preferred_element_type=F32)
    b_n = _rms(jnp.concatenate(b_rows, axis=0), gb_ref[...]).astype(BF16)
    xo_ref[...] = out + jnp.dot(b_n, wout_ref[GMLP_WIDTH:, :], preferred_element_type=F32)

    kprev_ref[...] = k_bf[TM - WINDOW:]
    vprev_ref[...] = vt_bf[:, TM - WINDOW:]
    ks_ref[0] = k[TM - WINDOW:]
    vs_ref[0] = val[TM - WINDOW:]


def _layer_block(layer, *shape, **kwargs):
    return pl.BlockSpec((None,) + shape, lambda *_: (layer,) + (0,) * len(shape), **kwargs)


def _mixer_prompt(x, lw, layer, cast=None):
    const = lambda *shape: pl.BlockSpec(shape, lambda i: (0,) * len(shape))
    per_layer = functools.partial(_layer_block, layer)
    cast_in, cast_specs, cast_out_specs, cast_shapes = [], [], [], []
    if cast is not None:
        ffn_index, weights = cast
        for w in weights:
            rows, cols = w.shape[1:]
            n_slabs = NT_PROMPT if (rows // NT_PROMPT) % PIECE == 0 else NT_PROMPT // 2
            slab = lambda i, n=n_slabs: jnp.minimum(i, n - 1)
            cast_in.append(w)
            cast_specs.append(pl.BlockSpec((None, rows // n_slabs, cols),
                                           lambda i, slab=slab: (ffn_index, slab(i), 0)))
            cast_out_specs.append(pl.BlockSpec((rows // n_slabs, cols), lambda i, slab=slab: (slab(i), 0)))
            cast_shapes.append(jax.ShapeDtypeStruct((rows, cols), BF16))
    return pl.pallas_call(
        functools.partial(_mixer_prompt_kernel, n_cast=len(cast_in)),
        grid=(NT_ALL,),
        in_specs=[
            pl.BlockSpec((TM, D_MODEL), lambda i: (jnp.minimum(i, NT_PROMPT - 1), 0)),
            per_layer(1, D_MODEL), per_layer(D_MODEL, IN_WIDTH), per_layer(1, IN_WIDTH),
            per_layer(1, GMLP_WIDTH), per_layer(1, GMLP_WIDTH),
            per_layer(GMLP_HEADS, CHUNK, CHUNK), per_layer(CHUNK, GMLP_WIDTH),
            pl.BlockSpec(memory_space=pltpu.SMEM),
            const(2, KV_HEADS, WINDOW, GQA_GROUP * WINDOW),
            per_layer(1, GMLP_WIDTH), per_layer(1, ATTN_WIDTH), per_layer(D_MODEL, D_MODEL),
        ] + cast_specs,
        out_specs=[
            pl.BlockSpec((TM, D_MODEL), lambda i: (i, 0)),
            pl.BlockSpec((1, WINDOW, KV_WIDTH), lambda i: (jnp.minimum(i // TILES_PER_SEQ, BATCH - 1), 0, 0)),
            pl.BlockSpec((1, WINDOW, KV_WIDTH), lambda i: (jnp.minimum(i // TILES_PER_SEQ, BATCH - 1), 0, 0)),
        ] + cast_out_specs,
        out_shape=[
            jax.ShapeDtypeStruct((T_ALL, D_MODEL), F32),
            jax.ShapeDtypeStruct((BATCH, WINDOW, KV_WIDTH), F32),
            jax.ShapeDtypeStruct((BATCH, WINDOW, KV_WIDTH), F32),
        ] + cast_shapes,
        scratch_shapes=[pltpu.VMEM((WINDOW, KV_WIDTH), BF16), pltpu.VMEM((WINDOW, KV_WIDTH), BF16)],
        compiler_params=pltpu.CompilerParams(
            dimension_semantics=("arbitrary",), vmem_limit_bytes=VMEM_LIMIT),
        name="mixer_prompt",
    )(x, lw["norm_g"], lw["w_in"], lw["b_in"], lw["ln_g"], lw["ln_b"], lw["w_s"], lw["b_s_prompt"],
      lw["sinks"], lw["bias_prompt"], lw["g_a"], lw["g_b"], lw["w_out"], *cast_in)


def _mixer_sample_kernel(x_ref, xalias_ref, ck_ref, cv_ref, ng_ref, win_ref, bin_ref, lng_ref, lnb_ref,
                         wsd_ref, bsd_ref, sinkrow_ref, bias_ref, ga_ref, gb_ref, wout_ref,
                         xo_ref, ko_ref, vo_ref, cvo_ref,
                         q_s, k_s, v_s, a_s, b_s, kfull, vfull):
    del xalias_ref
    j = pl.program_id(0)

    @pl.when(j == 0)
    def _():
        h = _rms(x_ref[...], ng_ref[...]).astype(BF16)
        z = jnp.dot(h, win_ref[...], preferred_element_type=F32) + bin_ref[...]
        u = _gelu(z[:, :GMLP_WIDTH])
        v = jnp.concatenate(_gmlp_v(z[:, GMLP_WIDTH:Q0], lng_ref, lnb_ref), axis=1)
        cvo_ref[...] = v
        t_row = lax.broadcasted_iota(jnp.int32, (8, 1), 0) % DEC_SEQ

        def tile_rows(pat):
            return jnp.broadcast_to(pat[None], (TM // 8,) + pat.shape).reshape(TM, pat.shape[-1])

        mixed = tile_rows(bsd_ref[...])
        for d in range(DEC_SEQ):
            vd = v if d == 0 else pltpu.roll(v, d, 0)
            mixed = mixed + tile_rows(jnp.where(t_row >= d, wsd_ref[d], 0.0)) * vd
        a_s[...] = _rms(u * mixed, ga_ref[...]).astype(BF16)
        q_s[...] = z[:, Q0:K0] * (HEAD_DIM ** -0.5)
        k_s[...] = z[:, K0:V0]
        v_s[...] = z[:, V0:]
        kfull[:, CACHE_W + DEC_SEQ:, :] = jnp.zeros((SEQ_GROUP, KEYS_PAD - CACHE_W - DEC_SEQ, KV_WIDTH), F32)
        vfull[:, CACHE_W + DEC_SEQ:, :] = jnp.zeros((SEQ_GROUP, KEYS_PAD - CACHE_W - DEC_SEQ, KV_WIDTH), F32)

    base = pl.multiple_of(j * (SEQ_GROUP * DEC_SEQ), SEQ_GROUP * DEC_SEQ)
    kfull[:, :CACHE_W, :] = ck_ref[0]
    vfull[:, :CACHE_W, :] = cv_ref[0]
    k_new = k_s[pl.ds(base, SEQ_GROUP * DEC_SEQ), :]
    v_new = v_s[pl.ds(base, SEQ_GROUP * DEC_SEQ), :]
    for s in range(SEQ_GROUP):
        kfull[s, CACHE_W:CACHE_W + DEC_SEQ, :] = k_new[s * DEC_SEQ:(s + 1) * DEC_SEQ]
        vfull[s, CACHE_W:CACHE_W + DEC_SEQ, :] = v_new[s * DEC_SEQ:(s + 1) * DEC_SEQ]
    ko_ref[0] = kfull[:, DEC_SEQ:DEC_SEQ + CACHE_W, :]
    vo_ref[0] = vfull[:, DEC_SEQ:DEC_SEQ + CACHE_W, :]

    lane = lax.broadcasted_iota(jnp.int32, (1, LANES), 1)
    low = lane < HEAD_DIM
    col_seq = (lane // DEC_SEQ) % SEQS_PER_TILE
    row_seq = (lax.broadcasted_iota(jnp.int32, (LANES, 1), 0) // DEC_SEQ) % SEQS_PER_TILE
    sink = sinkrow_ref[...]
    zero = jnp.zeros((), BF16)
    quad_rows = SEQS_PER_TILE * DEC_SEQ
    for m in range(SEQ_GROUP // SEQS_PER_TILE):
        r0 = pl.multiple_of(base + m * quad_rows, quad_rows)
        q16 = q_s[pl.ds(r0, quad_rows), :]
        qt = jnp.concatenate(
            [jnp.where(low if kv == 0 else jnp.logical_not(low), q16[:, g * LANES:(g + 1) * LANES], 0.0)
             for kv in range(KV_HEADS) for g in range(GQA_GROUP)], axis=0).astype(BF16)
        s = bias_ref[...]
        for n in range(SEQS_PER_TILE):
            kn = kfull[SEQS_PER_TILE * m + n].astype(BF16)
            s = s + _dot_nt(kn, jnp.where(row_seq == n, qt, zero))
        mx = jnp.maximum(jnp.max(s, axis=0, keepdims=True), sink)
        p = jnp.exp(s - mx)
        p = (p / (jnp.sum(p, axis=0, keepdims=True) + jnp.exp(sink - mx))).astype(BF16)
        o = jnp.zeros((LANES, KV_WIDTH), F32)
        for n in range(SEQS_PER_TILE):
            vn = vfull[SEQS_PER_TILE * m + n].astype(BF16)
            o = o + lax.dot_general(jnp.where(col_seq == n, p, zero), vn, (((0,), (0,)), ((), ())),
                                    preferred_element_type=F32)
        b_s[pl.ds(r0, quad_rows), :] = jnp.concatenate(
            [jnp.where(low, o[g * quad_rows:(g + 1) * quad_rows],
                       o[(GQA_GROUP + g) * quad_rows:(GQA_GROUP + g + 1) * quad_rows])
             for g in range(GQA_GROUP)], axis=1)

    @pl.when(j == N_SEQ_GROUPS - 1)
    def _():
        b_n = _rms(b_s[...], gb_ref[...]).astype(BF16)
        merged = jnp.concatenate([a_s[...], b_n], axis=1)
        xo_ref[...] = x_ref[...] + jnp.dot(merged, wout_ref[...], preferred_element_type=F32)


def _mixer_sample(x, x_block, x_buf, cache_k, cache_v, layer, lw):
    const = lambda *shape: pl.BlockSpec(shape, lambda j: (0,) * len(shape))
    per_layer = functools.partial(_layer_block, layer)
    cache_spec = pl.BlockSpec((1, SEQ_GROUP, CACHE_W, KV_WIDTH), lambda j: (layer, j, 0, 0))
    state_spec = pl.BlockSpec((1, SEQ_GROUP, CACHE_W, KV_WIDTH), lambda j: (j, 0, 0, 0))
    state_shape = jax.ShapeDtypeStruct((N_SEQ_GROUPS, SEQ_GROUP, CACHE_W, KV_WIDTH), F32)
    return pl.pallas_call(
        _mixer_sample_kernel,
        grid=(N_SEQ_GROUPS,),
        in_specs=[
            pl.BlockSpec((TM, D_MODEL), lambda j: (x_block, 0)),
            pl.BlockSpec(memory_space=pl.ANY),
            cache_spec, cache_spec,
            per_layer(1, D_MODEL), per_layer(D_MODEL, IN_WIDTH), per_layer(1, IN_WIDTH),
            per_layer(1, GMLP_WIDTH), per_layer(1, GMLP_WIDTH),
            per_layer(DEC_SEQ, 8, GMLP_WIDTH), per_layer(8, GMLP_WIDTH),
            per_layer(1, LANES), const(KEYS_PAD, LANES),
            per_layer(1, GMLP_WIDTH), per_layer(1, ATTN_WIDTH), per_layer(D_MODEL, D_MODEL),
        ],
        out_specs=[
            pl.BlockSpec((TM, D_MODEL), lambda j: (NT_PROMPT, 0)),
            state_spec, state_spec,
            const(TM, GMLP_WIDTH),
        ],
        out_shape=[
            jax.ShapeDtypeStruct((T_ALL, D_MODEL), F32),
            state_shape, state_shape,
            jax.ShapeDtypeStruct((TM, GMLP_WIDTH), F32),
        ],
        scratch_shapes=[
            pltpu.VMEM((TM, ATTN_WIDTH), F32), pltpu.VMEM((TM, KV_WIDTH), F32),
            pltpu.VMEM((TM, KV_WIDTH), F32), pltpu.VMEM((TM, GMLP_WIDTH), BF16),
            pltpu.VMEM((TM, ATTN_WIDTH), F32),
            pltpu.VMEM((SEQ_GROUP, KEYS_PAD, KV_WIDTH), F32),
            pltpu.VMEM((SEQ_GROUP, KEYS_PAD, KV_WIDTH), F32),
        ],
        input_output_aliases={1: 0},
        compiler_params=pltpu.CompilerParams(
            dimension_semantics=("arbitrary",), vmem_limit_bytes=VMEM_LIMIT),
        name="mixer_sample",
    )(x, x_buf, cache_k, cache_v, lw["norm_g"], lw["w_in"], lw["b_in"], lw["ln_g"], lw["ln_b"],
      lw["w_s_sample"], lw["b_s_sample"], lw["sink_row"], lw["bias_sample"],
      lw["g_a"], lw["g_b"], lw["w_out"])


def _swiglu_kernel(x_ref, ng_ref, wg_ref, wu_ref, wd_ref, eg_ref, eu_ref, ed_ref,
                   xo_ref, ego_ref, euo_ref, edo_ref):
    x = x_ref[...]
    h = _rms(x, ng_ref[...]).astype(BF16)
    gate = jnp.dot(h, wg_ref[...], preferred_element_type=F32)
    ego_ref[...] = eg_ref[...].astype(BF16)
    up = jnp.dot(h, wu_ref[...], preferred_element_type=F32)
    euo_ref[...] = eu_ref[...].astype(BF16)
    act = (jax.nn.silu(gate) * up).astype(BF16)
    xo_ref[...] = x + jnp.dot(act, wd_ref[...], preferred_element_type=F32)
    edo_ref[...] = ed_ref[...].astype(BF16)


def _swiglu(x, norm_g, wg, wu, wd, layer, ffn_index, expert_f32, expert_index):
    weights = functools.partial(_layer_block, ffn_index, pipeline_mode=pl.Buffered(1))
    slab_steps = NT_ALL - 1
    up_rows = N_EXPERTS * D_MODEL // slab_steps
    down_rows = N_EXPERTS * D_FF_EXPERT // slab_steps
    slab = lambda i: jnp.minimum(i, slab_steps - 1)
    eg, eu, ed = expert_f32
    eg = eg.reshape(-1, N_EXPERTS * D_MODEL, D_FF_EXPERT)
    eu = eu.reshape(-1, N_EXPERTS * D_MODEL, D_FF_EXPERT)
    ed = ed.reshape(-1, N_EXPERTS * D_FF_EXPERT, D_MODEL)
    x_new, eg_bf, eu_bf, ed_bf = pl.pallas_call(
        _swiglu_kernel,
        grid=(NT_ALL,),
        in_specs=[
            pl.BlockSpec((TM, D_MODEL), lambda i: (i, 0)),
            _layer_block(layer, 1, D_MODEL),
            weights(D_MODEL, D_FF), weights(D_MODEL, D_FF), weights(D_FF, D_MODEL),
            pl.BlockSpec((None, up_rows, D_FF_EXPERT), lambda i: (expert_index, slab(i), 0)),
            pl.BlockSpec((None, up_rows, D_FF_EXPERT), lambda i: (expert_index, slab(i), 0)),
            pl.BlockSpec((None, down_rows, D_MODEL), lambda i: (expert_index, slab(i), 0)),
        ],
        out_specs=[
            pl.BlockSpec((TM, D_MODEL), lambda i: (i, 0)),
            pl.BlockSpec((up_rows, D_FF_EXPERT), lambda i: (slab(i), 0)),
            pl.BlockSpec((up_rows, D_FF_EXPERT), lambda i: (slab(i), 0)),
            pl.BlockSpec((down_rows, D_MODEL), lambda i: (slab(i), 0)),
        ],
        out_shape=[
            jax.ShapeDtypeStruct((T_ALL, D_MODEL), F32),
            jax.ShapeDtypeStruct((N_EXPERTS * D_MODEL, D_FF_EXPERT), BF16),
            jax.ShapeDtypeStruct((N_EXPERTS * D_MODEL, D_FF_EXPERT), BF16),
            jax.ShapeDtypeStruct((N_EXPERTS * D_FF_EXPERT, D_MODEL), BF16),
        ],
        compiler_params=pltpu.CompilerParams(
            dimension_semantics=("arbitrary",), vmem_limit_bytes=VMEM_LIMIT),
        name="swiglu",
    )(x, norm_g, wg, wu, wd, eg, eu, ed)
    return (x_new, eg_bf.reshape(1, N_EXPERTS, D_MODEL, D_FF_EXPERT),
            eu_bf.reshape(1, N_EXPERTS, D_MODEL, D_FF_EXPERT),
            ed_bf.reshape(1, N_EXPERTS, D_FF_EXPERT, D_MODEL))


def _moe_sort_kernel(x_ref, ng_ref, wrh_ref, wrl_ref, hs_ref, meta_ref, off_ref, cnt_ref):
    n = SORT_TILES_PER_STEP
    logits, routes = {}, {}
    for step in range(n + 2):
        if step < n:
            logits[step] = _sort_logits(x_ref[step * TM:(step + 1) * TM, :], ng_ref, wrh_ref, wrl_ref)
        j = step - 1
        if 0 <= j < n:
            h_hi, lg = logits.pop(j)
            meta, off, cnt_pad = _sort_route(lg)
            meta_ref[j * TM:(j + 1) * TM, :] = meta
            off_ref[j] = off.astype(jnp.int32)
            cnt_ref[j] = cnt_pad.astype(jnp.int32)
            routes[j] = (h_hi, meta)
        j = step - 2
        if 0 <= j < n:
            hs_ref[j * SORT_ROWS:(j + 1) * SORT_ROWS, :] = _sort_gather(*routes.pop(j))


def _sort_logits(x, ng_ref, wrh_ref, wrl_ref):
    h = _rms(x, ng_ref[...])
    h_hi = h.astype(BF16)
    h_lo = (h - h_hi.astype(F32)).astype(BF16)
    logits = (jnp.dot(h_hi, wrh_ref[...], preferred_element_type=F32)
              + jnp.dot(h_lo, wrh_ref[...], preferred_element_type=F32)
              + jnp.dot(h_hi, wrl_ref[...], preferred_element_type=F32))
    return h_hi, logits


def _sort_route(logits):
    lane = lax.broadcasted_iota(jnp.int32, (1, LANES), 1)
    logits = jnp.where(lane < N_EXPERTS, logits, NEG_INF)
    m1 = jnp.max(logits, axis=-1, keepdims=True)
    i1 = jnp.min(jnp.where(logits == m1, lane, LANES), axis=-1, keepdims=True)
    rest = jnp.where(lane == i1, NEG_INF, logits)
    m2 = jnp.max(rest, axis=-1, keepdims=True)
    i2 = jnp.min(jnp.where(rest == m2, lane, LANES), axis=-1, keepdims=True)
    e2 = jnp.exp(m2 - m1)
    den = 1.0 + e2
    sel0 = lane == i1
    sel1 = lane == i2
    routed = jnp.where(jnp.logical_or(sel0, sel1), 1.0, 0.0)
    r_i = lax.broadcasted_iota(jnp.int32, (TM, TM), 0)
    c_i = lax.broadcasted_iota(jnp.int32, (TM, TM), 1)
    lower = jnp.where(r_i > c_i, 1.0, 0.0).astype(BF16)
    rank = jnp.dot(lower, routed.astype(BF16), preferred_element_type=F32)
    cnt = jnp.broadcast_to(jnp.sum(routed, axis=0, keepdims=True), (8, LANES))
    cnt_pad = jnp.floor((cnt + (PIECE - 1)) * (1.0 / PIECE)) * PIECE
    incl = cnt_pad
    for sh in (1, 2, 4):
        incl = incl + jnp.where(lane >= sh, pltpu.roll(incl, sh, 1), 0.0)
    off = incl - cnt_pad
    pos = off[0:1] + rank
    pos0 = jnp.sum(jnp.where(sel0, pos, 0.0), axis=-1, keepdims=True)
    pos1 = jnp.sum(jnp.where(sel1, pos, 0.0), axis=-1, keepdims=True)
    meta = jnp.where(lane == 0, pos0, jnp.where(lane == 1, pos1,
                     jnp.where(lane == 2, 1.0 / den, jnp.where(lane == 3, e2 / den, 0.0))))
    return meta, off, cnt_pad


def _sort_gather(h_hi, meta):
    meta_t = meta.T
    p_row = lax.broadcasted_iota(jnp.int32, (SORT_ROWS, 1), 0).astype(F32)
    hit = jnp.logical_or(p_row == meta_t[0:1], p_row == meta_t[1:2])
    gather = jnp.where(hit, 1.0, 0.0).astype(BF16)
    return jnp.dot(gather, h_hi, preferred_element_type=F32).astype(BF16)


def _moe_sort(x, norm_g, wr_hi, wr_lo, layer, moe_index):
    return pl.pallas_call(
        _moe_sort_kernel,
        grid=(NT_ALL // SORT_TILES_PER_STEP,),
        in_specs=[
            pl.BlockSpec((SORT_TILES_PER_STEP * TM, D_MODEL), lambda i: (i, 0)),
            _layer_block(layer, 1, D_MODEL),
            _layer_block(moe_index, D_MODEL, LANES),
            _layer_block(moe_index, D_MODEL, LANES),
        ],
        out_specs=[
            pl.BlockSpec((SORT_TILES_PER_STEP * SORT_ROWS, D_MODEL), lambda i: (i, 0)),
            pl.BlockSpec((SORT_TILES_PER_STEP * TM, LANES), lambda i: (i, 0)),
            pl.BlockSpec((SORT_TILES_PER_STEP, 8, LANES), lambda i: (i, 0, 0)),
            pl.BlockSpec((SORT_TILES_PER_STEP, 8, LANES), lambda i: (i, 0, 0)),
        ],
        out_shape=[
            jax.ShapeDtypeStruct((NT_ALL * SORT_ROWS, D_MODEL), BF16),
            jax.ShapeDtypeStruct((T_ALL, LANES), F32),
            jax.ShapeDtypeStruct((NT_ALL, 8, LANES), jnp.int32),
            jax.ShapeDtypeStruct((NT_ALL, 8, LANES), jnp.int32),
        ],
        compiler_params=pltpu.CompilerParams(
            dimension_semantics=("arbitrary",), vmem_limit_bytes=VMEM_LIMIT),
        name="moe_sort",
    )(x, norm_g, wr_hi, wr_lo)


def _moe_experts_kernel(off_ref, cnt_ref, hs_hbm, wg_ref, wu_ref, wrem_ref, wd_ref, ys_hbm,
                        buf, lhs, sem_in, sem_out):
    s = pl.program_id(0)
    e = pl.program_id(1)
    row0 = pl.multiple_of(s * SUPER_ROWS, PIECE)
    is_first = e == 0
    is_last = e == N_EXPERTS - 1

    def copy_in(t):
        r = pl.multiple_of(t * SORT_ROWS, PIECE)
        return pltpu.make_async_copy(hs_hbm.at[pl.ds(row0 + r, SORT_ROWS)], buf.at[pl.ds(r, SORT_ROWS)],
                                     sem_in.at[t])

    def copy_out(t):
        r = pl.multiple_of(t * SORT_ROWS, PIECE)
        return pltpu.make_async_copy(buf.at[pl.ds(r, SORT_ROWS)], ys_hbm.at[pl.ds(row0 + r, SORT_ROWS)],
                                     sem_out.at[t])

    def for_tiles(lo, hi, fn):
        hi = jnp.maximum(lo, hi)

        def body(t, carry):
            fn(t)
            return carry

        lax.fori_loop(lo, hi, body, 0)
        return hi

    @pl.when(is_first)
    def _():
        for t in range(TILES_PER_SUPER):
            copy_in(t).start()

    starts, cums = [], [0]
    for i in range(TILES_PER_SUPER):
        idx = (s * TILES_PER_SUPER + i) * N_EXPERTS + e
        starts.append(i * SORT_ROWS + off_ref[idx])
        cums.append(cums[-1] + cnt_ref[idx] // PIECE)
    n_pieces = cums[-1]

    def locate(p):
        pc = jnp.minimum(p, n_pieces - 1)
        row = starts[0] + PIECE * pc
        tile = jnp.int32(0)
        for i in range(1, TILES_PER_SUPER):
            inside = pc >= cums[i]
            row = jnp.where(inside, starts[i] + PIECE * (pc - cums[i]), row)
            tile = jnp.where(inside, i, tile)
        return row, tile

    def piece_rows(q):
        src, dst, tiles = [], [], []
        for j in range(PIECES_PER_CHUNK):
            p = q * PIECES_PER_CHUNK + j
            row, tile = locate(p)
            src.append(row)
            dst.append(jnp.where(p < n_pieces, row, SUPER_ROWS))
            tiles.append(tile)
        return tuple(src), tuple(dst), tiles[0], tiles[-1]

    def chunk_body(q, carry):
        (src, dst, _, last_tile), arrived, sent = carry
        arrived = for_tiles(arrived, jnp.where(is_first, last_tile + 1, 0), lambda t: copy_in(t).wait())
        for j in range(PIECES_PER_CHUNK):
            lhs[j * PIECE:(j + 1) * PIECE, :] = buf[pl.ds(pl.multiple_of(src[j], PIECE), PIECE), :]
        xs = lhs[...]
        gate = jnp.dot(xs, wg_ref[0, :, :FF_MAIN], preferred_element_type=F32)
        up = jnp.dot(xs, wu_ref[0, :, :FF_MAIN], preferred_element_type=F32)
        rem = jnp.dot(xs, wrem_ref[0], preferred_element_type=F32)
        nxt = piece_rows(q + 1)
        act = jnp.concatenate([jax.nn.silu(gate) * up,
                               jax.nn.silu(rem[:, :FF_REM]) * rem[:, FF_REM:]], axis=1).astype(BF16)
        y = jnp.dot(act, wd_ref[0], preferred_element_type=F32).astype(BF16)
        for j in range(PIECES_PER_CHUNK):
            buf[pl.ds(pl.multiple_of(dst[j], PIECE), PIECE), :] = y[j * PIECE:(j + 1) * PIECE]
        sent = for_tiles(sent, jnp.where(is_last, nxt[2], 0), lambda t: copy_out(t).start())
        return nxt, arrived, sent

    n_chunks = (n_pieces + PIECES_PER_CHUNK - 1) // PIECES_PER_CHUNK
    _, arrived, sent = lax.fori_loop(0, n_chunks, chunk_body, (piece_rows(0), jnp.int32(0), jnp.int32(0)))
    for_tiles(arrived, jnp.where(is_first, TILES_PER_SUPER, 0), lambda t: copy_in(t).wait())
    for_tiles(sent, jnp.where(is_last, TILES_PER_SUPER, 0), lambda t: copy_out(t).start())
    for_tiles(0, jnp.where(is_last, TILES_PER_SUPER, 0), lambda t: copy_out(t).wait())


def _moe_experts(off, cnt, hs, wg, wu, wrem, wd, moe_index):
    expert = lambda *shape: pl.BlockSpec((None, 1) + shape, lambda s, e, *_: (moe_index, e, 0, 0))
    grid_spec = pltpu.PrefetchScalarGridSpec(
        num_scalar_prefetch=2,
        grid=(N_SUPER, N_EXPERTS),
        in_specs=[
            pl.BlockSpec(memory_space=pl.ANY),
            expert(D_MODEL, D_FF_EXPERT), expert(D_MODEL, D_FF_EXPERT), expert(D_MODEL, 2 * FF_REM),
            expert(D_FF_EXPERT, D_MODEL),
        ],
        out_specs=pl.BlockSpec(memory_space=pl.ANY),
        scratch_shapes=[pltpu.VMEM((SUPER_ROWS + PIECE, D_MODEL), BF16),
                        pltpu.VMEM((CHUNK_ROWS, D_MODEL), BF16),
                        pltpu.SemaphoreType.DMA((TILES_PER_SUPER,)),
                        pltpu.SemaphoreType.DMA((TILES_PER_SUPER,))],
    )
    return pl.pallas_call(
        _moe_experts_kernel,
        grid_spec=grid_spec,
        out_shape=jax.ShapeDtypeStruct((NT_ALL * SORT_ROWS, D_MODEL), BF16),
        compiler_params=pltpu.CompilerParams(
            dimension_semantics=("arbitrary", "arbitrary"), vmem_limit_bytes=VMEM_LIMIT),
        name="moe_experts",
    )(off, cnt, hs, wg, wu, wrem, wd)


def _unsorted_residual(x_ref, ys_ref, meta_ref):
    meta = meta_ref[...]
    p_col = lax.broadcasted_iota(jnp.int32, (1, SORT_ROWS), 1).astype(F32)
    scatter = (jnp.where(p_col == meta[:, 0:1], meta[:, 2:3], 0.0)
               + jnp.where(p_col == meta[:, 1:2], meta[:, 3:4], 0.0)).astype(BF16)
    return x_ref[...] + jnp.dot(scatter, ys_ref[...], preferred_element_type=F32)


def _moe_unsort_kernel(x_ref, ys_ref, meta_ref, xo_ref):
    xo_ref[...] = _unsorted_residual(x_ref, ys_ref, meta_ref)


def _moe_unsort_final_kernel(x_ref, ys_ref, meta_ref, g_ref, yp_ref, ysm_ref):
    i = pl.program_id(0)
    y = _rms(_unsorted_residual(x_ref, ys_ref, meta_ref), g_ref[...])

    @pl.when(i < NT_PROMPT)
    def _():
        yp_ref[...] = y

    @pl.when(i == NT_PROMPT)
    def _():
        ysm_ref[...] = y


def _moe_unsort(x, ys, meta, final_g=None):
    in_specs = [
        pl.BlockSpec((TM, D_MODEL), lambda i: (i, 0)),
        pl.BlockSpec((SORT_ROWS, D_MODEL), lambda i: (i, 0)),
        pl.BlockSpec((TM, LANES), lambda i: (i, 0)),
    ]
    params = pltpu.CompilerParams(dimension_semantics=("arbitrary",), vmem_limit_bytes=VMEM_LIMIT)
    if final_g is None:
        return pl.pallas_call(
            _moe_unsort_kernel,
            grid=(NT_ALL,),
            in_specs=in_specs,
            out_specs=pl.BlockSpec((TM, D_MODEL), lambda i: (i, 0)),
            out_shape=jax.ShapeDtypeStruct((T_ALL, D_MODEL), F32),
            compiler_params=params,
            name="moe_unsort",
        )(x, ys, meta)
    return pl.pallas_call(
        _moe_unsort_final_kernel,
        grid=(NT_ALL,),
        in_specs=in_specs + [pl.BlockSpec((1, D_MODEL), lambda i: (0, 0))],
        out_specs=[pl.BlockSpec((TM, D_MODEL), lambda i: (jnp.minimum(i, NT_PROMPT - 1), 0)),
                   pl.BlockSpec((TM, D_MODEL), lambda i: (0, 0))],
        out_shape=[jax.ShapeDtypeStruct((T_PROMPT, D_MODEL), F32),
                   jax.ShapeDtypeStruct((T_SAMPLE, D_MODEL), F32)],
        compiler_params=params,
        name="moe_unsort_final",
    )(x, ys, meta, final_g)


def _moe(x, norm_g, wr_hi, wr_lo, wg, wu, wrem, wd, layer, moe_index, final_g=None):
    hs, meta, off, cnt = _moe_sort(x, norm_g, wr_hi, wr_lo, layer, moe_index)
    off = off[:, 0, :N_EXPERTS].reshape(-1)
    cnt = cnt[:, 0, :N_EXPERTS].reshape(-1)
    ys = _moe_experts(off, cnt, hs, wg, wu, wrem, wd, moe_index)
    return _moe_unsort(x, ys, meta, final_g)


def kernel(x_prompt, x_sample, cache_win_k, cache_win_v, attn_norm_g, w_in, b_in, chunk_ln_g, chunk_ln_b, w_spatial, b_spatial, attn_sinks, mix_norm_a_g, mix_norm_b_g, w_out, ffn_norm_g, ffn_w_gate, ffn_w_up, ffn_w_down, router_w, expert_w_gate, expert_w_up, expert_w_down, final_norm_g):
    w_in_p = jnp.concatenate([w_in[..., :Q0], _g_major(w_in[..., Q0:K0], -1), w_in[..., K0:]], axis=-1).astype(BF16)
    b_in_p = jnp.concatenate([b_in[..., :Q0], _g_major(b_in[..., Q0:K0], -1), b_in[..., K0:]], axis=-1)
    w_out_p = jnp.concatenate([w_out[:, :GMLP_WIDTH], _g_major(w_out[:, GMLP_WIDTH:], 1)], axis=1).astype(BF16)
    g_b_p = _g_major(mix_norm_b_g, 1)
    bs_prompt = jnp.repeat(jnp.transpose(b_spatial, (0, 2, 1)), GMLP_HEAD_DIM, axis=2)
    bs_sample = jnp.tile(bs_prompt[:, :DEC_SEQ], (1, 2, 1))
    t_idx = np.arange(DEC_SEQ)
    ws_small = jnp.stack([w_spatial[:, :, t_idx, np.maximum(t_idx - d, 0)] for d in range(DEC_SEQ)], axis=1)
    ws_sample = jnp.tile(jnp.repeat(jnp.transpose(ws_small, (0, 1, 3, 2)), GMLP_HEAD_DIM, axis=3),
                         (1, 1, 2, 1))
    sink_row = jnp.repeat(attn_sinks, SEQS_PER_TILE * DEC_SEQ, axis=1)[:, None, :]
    bias_prompt = jnp.asarray(_prompt_bias())
    bias_sample = jnp.asarray(_sample_bias())
    wr_pad = jnp.pad(router_w, ((0, 0), (0, 0), (0, LANES - N_EXPERTS)))
    wr_hi = wr_pad.astype(BF16)
    wr_lo = (wr_pad - wr_hi.astype(F32)).astype(BF16)

    ck = cache_win_k.reshape(DEPTH, DEC_BATCH, CACHE_W, KV_WIDTH)
    cv = cache_win_v.reshape(DEPTH, DEC_BATCH, CACHE_W, KV_WIDTH)

    lw = dict(
        norm_g=attn_norm_g[:, None, :], w_in=w_in_p, b_in=b_in_p[:, None, :],
        ln_g=chunk_ln_g.reshape(DEPTH, 1, GMLP_WIDTH), ln_b=chunk_ln_b.reshape(DEPTH, 1, GMLP_WIDTH),
        w_s=w_spatial, b_s_prompt=bs_prompt, w_s_sample=ws_sample, b_s_sample=bs_sample,
        sink_row=sink_row, bias_prompt=bias_prompt, bias_sample=bias_sample,
        g_a=mix_norm_a_g[:, None, :], g_b=g_b_p[:, None, :], w_out=w_out_p)
    ffn_g = ffn_norm_g[:, None, :]

    x = None
    kp, vp, ksm, vsm, cvs = [], [], [], [], []
    for l in range(DEPTH):
        lw["sinks"] = attn_sinks[l]
        i = l // 2
        cast = (i, (ffn_w_gate, ffn_w_up, ffn_w_down)) if l % 2 == 0 else None
        x_in = x_prompt.reshape(T_PROMPT, D_MODEL) if l == 0 else x
        x_new, k_p, v_p, *ffn_bf = _mixer_prompt(x_in, lw, l, cast)
        if l == 0:
            x_new, k_s, v_s, cv_s = _mixer_sample(x_sample.reshape(T_SAMPLE, D_MODEL), 0, x_new, ck, cv, l, lw)
        else:
            x_new, k_s, v_s, cv_s = _mixer_sample(x, NT_PROMPT, x_new, ck, cv, l, lw)
        kp.append(k_p)
        vp.append(v_p)
        ksm.append(k_s)
        vsm.append(v_s)
        cvs.append(cv_s)
        if l % 2 == 0:
            ffn_wg, ffn_wu, ffn_wd = (w[None] for w in ffn_bf)
            x, ex_wg, ex_wu, ex_wd = _swiglu(x_new, ffn_g, ffn_wg, ffn_wu, ffn_wd, l, 0,
                                             (expert_w_gate, expert_w_up, expert_w_down), i)
            ex_wrem = jnp.concatenate([ex_wg[..., FF_MAIN:], ex_wu[..., FF_MAIN:]], axis=-1)
        else:
            final_g = final_norm_g[None] if l == DEPTH - 1 else None
            x = _moe(x_new, ffn_g, wr_hi[i:i + 1], wr_lo[i:i + 1], ex_wg, ex_wu, ex_wrem, ex_wd, l, 0, final_g)
    y_p, y_s = x
    win_p = (DEPTH, BATCH, WINDOW, KV_HEADS, HEAD_DIM)
    win_s = (DEPTH, DEC_BATCH, CACHE_W, KV_HEADS, HEAD_DIM)
    return (y_p.reshape(BATCH, SEQ, D_MODEL), y_s.reshape(DEC_BATCH, DEC_SEQ, D_MODEL),
            jnp.stack(kp).reshape(win_p), jnp.stack(vp).reshape(win_p),
            jnp.stack(ksm).reshape(win_s), jnp.stack(vsm).reshape(win_s),
            jnp.stack(cvs).reshape(DEPTH, DEC_BATCH, DEC_SEQ, GMLP_HEADS, GMLP_HEAD_DIM))
```

```python
import functools

import numpy as np
import jax
import jax.numpy as jnp
from jax import lax
from jax.experimental import pallas as pl
from jax.experimental.pallas import tpu as pltpu

D_MODEL = 1024
BATCH = 4
SEQ = 4096
DEPTH = 4
DEC_BATCH = 128
DEC_SEQ = 4
HEAD_DIM = 64
ATTN_HEADS = 8
KV_HEADS = 2
GQA_GROUP = ATTN_HEADS // KV_HEADS
ATTN_WIDTH = ATTN_HEADS * HEAD_DIM
KV_WIDTH = KV_HEADS * HEAD_DIM
WINDOW = 128
CHUNK = 128
GMLP_HEADS = 4
GMLP_HEAD_DIM = 128
GMLP_WIDTH = GMLP_HEADS * GMLP_HEAD_DIM
IN_WIDTH = 2 * GMLP_WIDTH + ATTN_WIDTH + 2 * KV_WIDTH
D_FF = 2816
N_EXPERTS = 8
D_FF_EXPERT = 1408
CACHE_W = 128
EPS = 1e-6

LANES = 128
TM = 512
T_PROMPT = BATCH * SEQ
T_SAMPLE = DEC_BATCH * DEC_SEQ
T_ALL = T_PROMPT + T_SAMPLE
NT_PROMPT = T_PROMPT // TM
NT_ALL = T_ALL // TM
TILES_PER_SEQ = SEQ // TM
BLOCKS_PER_TILE = TM // WINDOW
SEQ_GROUP = 16
SEQS_PER_TILE = 4
N_SEQ_GROUPS = DEC_BATCH // SEQ_GROUP
KEYS_PAD = CACHE_W + 8
PIECE = 16
SORT_ROWS = 2 * TM + N_EXPERTS * PIECE
TILES_PER_SUPER = 11
N_SUPER = NT_ALL // TILES_PER_SUPER
SUPER_ROWS = TILES_PER_SUPER * SORT_ROWS
SORT_TILES_PER_STEP = 3
CHUNK_ROWS = 256
FF_REM = D_FF_EXPERT % 256
FF_MAIN = D_FF_EXPERT - FF_REM
PIECES_PER_CHUNK = CHUNK_ROWS // PIECE
VMEM_LIMIT = 56 * 1024 * 1024

F32 = jnp.float32
BF16 = jnp.bfloat16
NEG_INF = float("-inf")

Q0 = 2 * GMLP_WIDTH
K0 = Q0 + ATTN_WIDTH
V0 = K0 + KV_WIDTH


def _g_major(a, axis):
    axis = axis % a.ndim
    shape = a.shape
    a = a.reshape(shape[:axis] + (KV_HEADS, GQA_GROUP, HEAD_DIM) + shape[axis + 1:])
    return jnp.swapaxes(a, axis, axis + 1).reshape(shape)


def _slope(kv, g):
    return 2.0 ** (-8.0 * (kv * GQA_GROUP + g + 1) / ATTN_HEADS)


def _prompt_bias():
    c = np.arange(WINDOW)[:, None]
    qi = np.arange(WINDOW)[None, :]
    dist = np.where(c > qi, qi + WINDOW - c, qi - c)
    out = np.zeros((2, KV_HEADS, WINDOW, GQA_GROUP * WINDOW), np.float32)
    for var in range(2):
        ok = ~((var == 1) & (c > qi))
        for kv in range(KV_HEADS):
            for g in range(GQA_GROUP):
                out[var, kv, :, g * WINDOW:(g + 1) * WINDOW] = np.where(ok, -_slope(kv, g) * dist, NEG_INF)
    return out


def _sample_bias():
    out = np.zeros((KEYS_PAD, LANES), np.float32)
    j = np.arange(KEYS_PAD)
    kpos = j - CACHE_W
    for kv in range(KV_HEADS):
        for g in range(GQA_GROUP):
            for s in range(SEQS_PER_TILE):
                for t in range(DEC_SEQ):
                    col = ((kv * GQA_GROUP + g) * SEQS_PER_TILE + s) * DEC_SEQ + t
                    dist = t - kpos
                    ok = (dist >= 0) & (dist < WINDOW) & (j < CACHE_W + DEC_SEQ)
                    out[:, col] = np.where(ok, -_slope(kv, g) * dist, NEG_INF)
    return out


def _rms(x, g):
    return x * lax.rsqrt(jnp.mean(x * x, axis=-1, keepdims=True) + EPS) * g


def _gelu(x):
    return 0.5 * x * (1.0 + lax.erf(x * (2.0 ** -0.5)))


def _gmlp_v(z_v, lng_ref, lnb_ref):
    out = []
    for hh in range(GMLP_HEADS):
        sl = slice(hh * LANES, (hh + 1) * LANES)
        vh = _gelu(z_v[:, sl])
        vc = vh - jnp.mean(vh, axis=-1, keepdims=True)
        y = vc * lax.rsqrt(jnp.mean(vc * vc, axis=-1, keepdims=True) + EPS)
        out.append(y * lng_ref[:, sl] + lnb_ref[:, sl])
    return out


def _dot_nt(a, b):
    return lax.dot_general(a, b, (((1,), (1,)), ((), ())), preferred_element_type=F32)


N_MIXER_INPUTS = 13


def _mixer_prompt_kernel(*refs, n_cast):
    i = pl.program_id(0)
    n_in = N_MIXER_INPUTS + n_cast
    tile_refs = refs[:N_MIXER_INPUTS] + refs[n_in:n_in + 3] + refs[n_in + 3 + n_cast:]

    @pl.when(i < NT_PROMPT)
    def _():
        _mixer_prompt_tile(*tile_refs)

    @pl.when(i == NT_PROMPT)
    def _():
        xo_ref = refs[n_in]
        xo_ref[...] = jnp.zeros_like(xo_ref)

    for src, dst in zip(refs[N_MIXER_INPUTS:n_in], refs[n_in + 3:n_in + 3 + n_cast]):
        dst[...] = src[...].astype(BF16)


def _mixer_prompt_tile(x_ref, ng_ref, win_ref, bin_ref, lng_ref, lnb_ref, ws_ref, bs_ref,
                       sink_ref, bias_ref, ga_ref, gb_ref, wout_ref,
                       xo_ref, ks_ref, vs_ref, kprev_ref, vprev_ref):
    i = pl.program_id(0)

    @pl.when(i == 0)
    def _():
        kprev_ref[...] = jnp.zeros_like(kprev_ref)
        vprev_ref[...] = jnp.zeros_like(vprev_ref)

    x = x_ref[...]
    h = _rms(x, ng_ref[...]).astype(BF16)

    def in_proj(lo, hi):
        return jnp.dot(h, win_ref[:, lo:hi], preferred_element_type=F32) + bin_ref[:, lo:hi]

    z_att = in_proj(Q0, IN_WIDTH)
    q = (z_att[:, :ATTN_WIDTH] * (HEAD_DIM ** -0.5)).astype(BF16)
    k = z_att[:, ATTN_WIDTH:ATTN_WIDTH + KV_WIDTH]
    val = z_att[:, ATTN_WIDTH + KV_WIDTH:]
    k_bf = k.astype(BF16)
    lane = lax.broadcasted_iota(jnp.int32, (1, LANES), 1)
    low = lane < HEAD_DIM
    first = jnp.where(i % TILES_PER_SEQ == 0, 1, 0)
    c_i = lax.broadcasted_iota(jnp.int32, (WINDOW, WINDOW), 0)
    q_i = lax.broadcasted_iota(jnp.int32, (WINDOW, WINDOW), 1)
    from_prev = jnp.concatenate([c_i > q_i] * GQA_GROUP, axis=1)
    g_lane = lax.broadcasted_iota(jnp.int32, (1, GQA_GROUP * WINDOW), 1) // WINDOW
    vt_bf = val.T.astype(BF16)
    vt_all = jnp.concatenate([vprev_ref[...], vt_bf], axis=1)
    k_all = jnp.concatenate([kprev_ref[...], k_bf], axis=0)
    zero = jnp.zeros((), BF16)
    k_heads = [jnp.where(low, k_all, zero), jnp.where(low, zero, k_all)]
    sinks = []
    for kv in range(KV_HEADS):
        sink = jnp.full((1, GQA_GROUP * WINDOW), sink_ref[kv * GQA_GROUP], F32)
        for g in range(1, GQA_GROUP):
            sink = jnp.where(g_lane == g, sink_ref[kv * GQA_GROUP + g], sink)
        sinks.append(sink)

    def attention_block(c):
        rows = slice(c * WINDOW, (c + 1) * WINDOW)
        band = slice(c * WINDOW, (c + 2) * WINDOW)
        o_t = []
        q_rows = jnp.concatenate([q[rows, g * LANES:(g + 1) * LANES] for g in range(GQA_GROUP)], axis=0)
        keys = jnp.concatenate([k_heads[0][band], k_heads[1][band]], axis=0)
        s_both = _dot_nt(keys, q_rows)
        for kv in range(KV_HEADS):
            s2 = s_both[kv * 2 * WINDOW:(kv + 1) * 2 * WINDOW]
            bias = bias_ref[first, kv] if c == 0 else bias_ref[0, kv]
            s = jnp.where(from_prev, s2[:WINDOW], s2[WINDOW:]) + bias
            m = jnp.maximum(jnp.max(s, axis=0, keepdims=True), sinks[kv])
            p = jnp.exp(s - m)
            den = jnp.sum(p, axis=0, keepdims=True) + jnp.exp(sinks[kv] - m)
            p2 = jnp.concatenate([jnp.where(from_prev, p, 0.0), jnp.where(from_prev, 0.0, p)],
                                 axis=0).astype(BF16)
            o = jnp.dot(vt_all[kv * HEAD_DIM:(kv + 1) * HEAD_DIM, band], p2,
                        preferred_element_type=F32)
            o_t.append(o / den)
        bt = jnp.concatenate(o_t, axis=0)
        return jnp.concatenate(
            [bt[:, g * WINDOW:(g + 1) * WINDOW].T for g in range(GQA_GROUP)], axis=1)

    z_u = in_proj(0, GMLP_WIDTH)
    b_rows = [attention_block(0)]
    z_v = in_proj(GMLP_WIDTH, Q0)
    b_rows.append(attention_block(1))
    u = _gelu(z_u)
    b_rows.append(attention_block(2))
    v_heads = _gmlp_v(z_v, lng_ref, lnb_ref)
    row = lax.broadcasted_iota(jnp.int32, (CHUNK, CHUNK), 0)
    col = lax.broadcasted_iota(jnp.int32, (CHUNK, CHUNK), 1)
    a_cols = []
    for hh in range(GMLP_HEADS):
        w = jnp.where(row >= col, ws_ref[hh], 0.0).astype(BF16)
        v_cat = jnp.concatenate(
            [v_heads[hh][c * CHUNK:(c + 1) * CHUNK] for c in range(BLOCKS_PER_TILE)], axis=1)
        mixed = jnp.dot(w, v_cat.astype(BF16), preferred_element_type=F32)
        bias = bs_ref[:, hh * LANES:(hh + 1) * LANES]
        a_cols.append(jnp.concatenate(
            [mixed[:, c * LANES:(c + 1) * LANES] + bias for c in range(BLOCKS_PER_TILE)], axis=0))
    a = u * jnp.concatenate(a_cols, axis=1)
    a_n = _rms(a, ga_ref[...]).astype(BF16)
    b_rows.append(attention_block(3))
    out = x + jnp.dot(a_n, wout_ref[:GMLP_WIDTH, :], preferred_element_type=F32)
    b_n = _rms(jnp.concatenate(b_rows, axis=0), gb_ref[...]).astype(BF16)
    xo_ref[...] = out + jnp.dot(b_n, wout_ref[GMLP_WIDTH:, :], preferred_element_type=F32)

    kprev_ref[...] = k_bf[TM - WINDOW:]
    vprev_ref[...] = vt_bf[:, TM - WINDOW:]
    ks_ref[0] = k[TM - WINDOW:]
    vs_ref[0] = val[TM - WINDOW:]


def _layer_block(layer, *shape, **kwargs):
    return pl.BlockSpec((None,) + shape, lambda *_: (layer,) + (0,) * len(shape), **kwargs)


def _mixer_prompt(x, lw, layer, cast=None):
    const = lambda *shape: pl.BlockSpec(shape, lambda i: (0,) * len(shape))
    per_layer = functools.partial(_layer_block, layer)
    cast_in, cast_specs, cast_out_specs, cast_shapes = [], [], [], []
    if cast is not None:
        ffn_index, weights = cast
        for w in weights:
            rows, cols = w.shape[1:]
            n_slabs = NT_PROMPT if (rows // NT_PROMPT) % PIECE == 0 else NT_PROMPT // 2
            slab = lambda i, n=n_slabs: jnp.minimum(i, n - 1)
            cast_in.append(w)
            cast_specs.append(pl.BlockSpec((None, rows // n_slabs, cols),
                                           lambda i, slab=slab: (ffn_index, slab(i), 0)))
            cast_out_specs.append(pl.BlockSpec((rows // n_slabs, cols), lambda i, slab=slab: (slab(i), 0)))
            cast_shapes.append(jax.ShapeDtypeStruct((rows, cols), BF16))
    return pl.pallas_call(
        functools.partial(_mixer_prompt_kernel, n_cast=len(cast_in)),
        grid=(NT_ALL,),
        in_specs=[
            pl.BlockSpec((TM, D_MODEL), lambda i: (jnp.minimum(i, NT_PROMPT - 1), 0)),
            per_layer(1, D_MODEL), per_layer(D_MODEL, IN_WIDTH), per_layer(1, IN_WIDTH),
            per_layer(1, GMLP_WIDTH), per_layer(1, GMLP_WIDTH),
            per_layer(GMLP_HEADS, CHUNK, CHUNK), per_layer(CHUNK, GMLP_WIDTH),
            pl.BlockSpec(memory_space=pltpu.SMEM),
            const(2, KV_HEADS, WINDOW, GQA_GROUP * WINDOW),
            per_layer(1, GMLP_WIDTH), per_layer(1, ATTN_WIDTH), per_layer(D_MODEL, D_MODEL),
        ] + cast_specs,
        out_specs=[
            pl.BlockSpec((TM, D_MODEL), lambda i: (i, 0)),
            pl.BlockSpec((1, WINDOW, KV_WIDTH), lambda i: (jnp.minimum(i // TILES_PER_SEQ, BATCH - 1), 0, 0)),
            pl.BlockSpec((1, WINDOW, KV_WIDTH), lambda i: (jnp.minimum(i // TILES_PER_SEQ, BATCH - 1), 0, 0)),
        ] + cast_out_specs,
        out_shape=[
            jax.ShapeDtypeStruct((T_ALL, D_MODEL), F32),
            jax.ShapeDtypeStruct((BATCH, WINDOW, KV_WIDTH), F32),
            jax.ShapeDtypeStruct((BATCH, WINDOW, KV_WIDTH), F32),
        ] + cast_shapes,
        scratch_shapes=[pltpu.VMEM((WINDOW, KV_WIDTH), BF16), pltpu.VMEM((WINDOW, KV_WIDTH), BF16)],
        compiler_params=pltpu.CompilerParams(
            dimension_semantics=("arbitrary",), vmem_limit_bytes=VMEM_LIMIT),
        name="mixer_prompt",
    )(x, lw["norm_g"], lw["w_in"], lw["b_in"], lw["ln_g"], lw["ln_b"], lw["w_s"], lw["b_s_prompt"],
      lw["sinks"], lw["bias_prompt"], lw["g_a"], lw["g_b"], lw["w_out"], *cast_in)


def _mixer_sample_kernel(x_ref, xalias_ref, ck_ref, cv_ref, ng_ref, win_ref, bin_ref, lng_ref, lnb_ref,
                         wsd_ref, bsd_ref, sinkrow_ref, bias_ref, ga_ref, gb_ref, wout_ref,
                         xo_ref, ko_ref, vo_ref, cvo_ref,
                         q_s, k_s, v_s, a_s, b_s, kfull, vfull):
    del xalias_ref
    j = pl.program_id(0)

    @pl.when(j == 0)
    def _():
        h = _rms(x_ref[...], ng_ref[...]).astype(BF16)
        z = jnp.dot(h, win_ref[...], preferred_element_type=F32) + bin_ref[...]
        u = _gelu(z[:, :GMLP_WIDTH])
        v = jnp.concatenate(_gmlp_v(z[:, GMLP_WIDTH:Q0], lng_ref, lnb_ref), axis=1)
        cvo_ref[...] = v
        t_row = lax.broadcasted_iota(jnp.int32, (8, 1), 0) % DEC_SEQ

        def tile_rows(pat):
            return jnp.broadcast_to(pat[None], (TM // 8,) + pat.shape).reshape(TM, pat.shape[-1])

        mixed = tile_rows(bsd_ref[...])
        for d in range(DEC_SEQ):
            vd = v if d == 0 else pltpu.roll(v, d, 0)
            mixed = mixed + tile_rows(jnp.where(t_row >= d, wsd_ref[d], 0.0)) * vd
        a_s[...] = _rms(u * mixed, ga_ref[...]).astype(BF16)
        q_s[...] = z[:, Q0:K0] * (HEAD_DIM ** -0.5)
        k_s[...] = z[:, K0:V0]
        v_s[...] = z[:, V0:]
        kfull[:, CACHE_W + DEC_SEQ:, :] = jnp.zeros((SEQ_GROUP, KEYS_PAD - CACHE_W - DEC_SEQ, KV_WIDTH), F32)
        vfull[:, CACHE_W + DEC_SEQ:, :] = jnp.zeros((SEQ_GROUP, KEYS_PAD - CACHE_W - DEC_SEQ, KV_WIDTH), F32)

    base = pl.multiple_of(j * (SEQ_GROUP * DEC_SEQ), SEQ_GROUP * DEC_SEQ)
    kfull[:, :CACHE_W, :] = ck_ref[0]
    vfull[:, :CACHE_W, :] = cv_ref[0]
    k_new = k_s[pl.ds(base, SEQ_GROUP * DEC_SEQ), :]
    v_new = v_s[pl.ds(base, SEQ_GROUP * DEC_SEQ), :]
    for s in range(SEQ_GROUP):
        kfull[s, CACHE_W:CACHE_W + DEC_SEQ, :] = k_new[s * DEC_SEQ:(s + 1) * DEC_SEQ]
        vfull[s, CACHE_W:CACHE_W + DEC_SEQ, :] = v_new[s * DEC_SEQ:(s + 1) * DEC_SEQ]
    ko_ref[0] = kfull[:, DEC_SEQ:DEC_SEQ + CACHE_W, :]
    vo_ref[0] = vfull[:, DEC_SEQ:DEC_SEQ + CACHE_W, :]

    lane = lax.broadcasted_iota(jnp.int32, (1, LANES), 1)
    low = lane < HEAD_DIM
    col_seq = (lane // DEC_SEQ) % SEQS_PER_TILE
    row_seq = (lax.broadcasted_iota(jnp.int32, (LANES, 1), 0) // DEC_SEQ) % SEQS_PER_TILE
    sink = sinkrow_ref[...]
    zero = jnp.zeros((), BF16)
    quad_rows = SEQS_PER_TILE * DEC_SEQ
    for m in range(SEQ_GROUP // SEQS_PER_TILE):
        r0 = pl.multiple_of(base + m * quad_rows, quad_rows)
        q16 = q_s[pl.ds(r0, quad_rows), :]
        qt = jnp.concatenate(
            [jnp.where(low if kv == 0 else jnp.logical_not(low), q16[:, g * LANES:(g + 1) * LANES], 0.0)
             for kv in range(KV_HEADS) for g in range(GQA_GROUP)], axis=0).astype(BF16)
        s = bias_ref[...]
        for n in range(SEQS_PER_TILE):
            kn = kfull[SEQS_PER_TILE * m + n].astype(BF16)
            s = s + _dot_nt(kn, jnp.where(row_seq == n, qt, zero))
        mx = jnp.maximum(jnp.max(s, axis=0, keepdims=True), sink)
        p = jnp.exp(s - mx)
        p = (p / (jnp.sum(p, axis=0, keepdims=True) + jnp.exp(sink - mx))).astype(BF16)
        o = jnp.zeros((LANES, KV_WIDTH), F32)
        for n in range(SEQS_PER_TILE):
            vn = vfull[SEQS_PER_TILE * m + n].astype(BF16)
            o = o + lax.dot_general(jnp.where(col_seq == n, p, zero), vn, (((0,), (0,)), ((), ())),
                                    preferred_element_type=F32)
        b_s[pl.ds(r0, quad_rows), :] = jnp.concatenate(
            [jnp.where(low, o[g * quad_rows:(g + 1) * quad_rows],
                       o[(GQA_GROUP + g) * quad_rows:(GQA_GROUP + g + 1) * quad_rows])
             for g in range(GQA_GROUP)], axis=1)

    @pl.when(j == N_SEQ_GROUPS - 1)
    def _():
        b_n = _rms(b_s[...], gb_ref[...]).astype(BF16)
        merged = jnp.concatenate([a_s[...], b_n], axis=1)
        xo_ref[...] = x_ref[...] + jnp.dot(merged, wout_ref[...], preferred_element_type=F32)


def _mixer_sample(x, x_block, x_buf, cache_k, cache_v, layer, lw):
    const = lambda *shape: pl.BlockSpec(shape, lambda j: (0,) * len(shape))
    per_layer = functools.partial(_layer_block, layer)
    cache_spec = pl.BlockSpec((1, SEQ_GROUP, CACHE_W, KV_WIDTH), lambda j: (layer, j, 0, 0))
    state_spec = pl.BlockSpec((1, SEQ_GROUP, CACHE_W, KV_WIDTH), lambda j: (j, 0, 0, 0))
    state_shape = jax.ShapeDtypeStruct((N_SEQ_GROUPS, SEQ_GROUP, CACHE_W, KV_WIDTH), F32)
    return pl.pallas_call(
        _mixer_sample_kernel,
        grid=(N_SEQ_GROUPS,),
        in_specs=[
            pl.BlockSpec((TM, D_MODEL), lambda j: (x_block, 0)),
            pl.BlockSpec(memory_space=pl.ANY),
            cache_spec, cache_spec,
            per_layer(1, D_MODEL), per_layer(D_MODEL, IN_WIDTH), per_layer(1, IN_WIDTH),
            per_layer(1, GMLP_WIDTH), per_layer(1, GMLP_WIDTH),
            per_layer(DEC_SEQ, 8, GMLP_WIDTH), per_layer(8, GMLP_WIDTH),
            per_layer(1, LANES), const(KEYS_PAD, LANES),
            per_layer(1, GMLP_WIDTH), per_layer(1, ATTN_WIDTH), per_layer(D_MODEL, D_MODEL),
        ],
        out_specs=[
            pl.BlockSpec((TM, D_MODEL), lambda j: (NT_PROMPT, 0)),
            state_spec, state_spec,
            const(TM, GMLP_WIDTH),
        ],
        out_shape=[
            jax.ShapeDtypeStruct((T_ALL, D_MODEL), F32),
            state_shape, state_shape,
            jax.ShapeDtypeStruct((TM, GMLP_WIDTH), F32),
        ],
        scratch_shapes=[
            pltpu.VMEM((TM, ATTN_WIDTH), F32), pltpu.VMEM((TM, KV_WIDTH), F32),
            pltpu.VMEM((TM, KV_WIDTH), F32), pltpu.VMEM((TM, GMLP_WIDTH), BF16),
            pltpu.VMEM((TM, ATTN_WIDTH), F32),
            pltpu.VMEM((SEQ_GROUP, KEYS_PAD, KV_WIDTH), F32),
            pltpu.VMEM((SEQ_GROUP, KEYS_PAD, KV_WIDTH), F32),
        ],
        input_output_aliases={1: 0},
        compiler_params=pltpu.CompilerParams(
            dimension_semantics=("arbitrary",), vmem_limit_bytes=VMEM_LIMIT),
        name="mixer_sample",
    )(x, x_buf, cache_k, cache_v, lw["norm_g"], lw["w_in"], lw["b_in"], lw["ln_g"], lw["ln_b"],
      lw["w_s_sample"], lw["b_s_sample"], lw["sink_row"], lw["bias_sample"],
      lw["g_a"], lw["g_b"], lw["w_out"])


def _swiglu_kernel(x_ref, ng_ref, wg_ref, wu_ref, wd_ref, eg_ref, eu_ref, ed_ref,
                   xo_ref, ego_ref, euo_ref, edo_ref):
    x = x_ref[...]
    h = _rms(x, ng_ref[...]).astype(BF16)
    gate = jnp.dot(h, wg_ref[...], preferred_element_type=F32)
    ego_ref[...] = eg_ref[...].astype(BF16)
    up = jnp.dot(h, wu_ref[...], preferred_element_type=F32)
    euo_ref[...] = eu_ref[...].astype(BF16)
    act = (jax.nn.silu(gate) * up).astype(BF16)
    xo_ref[...] = x + jnp.dot(act, wd_ref[...], preferred_element_type=F32)
    edo_ref[...] = ed_ref[...].astype(BF16)


def _swiglu(x, norm_g, wg, wu, wd, layer, ffn_index, expert_f32, expert_index):
    weights = functools.partial(_layer_block, ffn_index, pipeline_mode=pl.Buffered(1))
    slab_steps = NT_ALL - 1
    up_rows = N_EXPERTS * D_MODEL // slab_steps
    down_rows = N_EXPERTS * D_FF_EXPERT // slab_steps
    slab = lambda i: jnp.minimum(i, slab_steps - 1)
    eg, eu, ed = expert_f32
    eg = eg.reshape(-1, N_EXPERTS * D_MODEL, D_FF_EXPERT)
    eu = eu.reshape(-1, N_EXPERTS * D_MODEL, D_FF_EXPERT)
    ed = ed.reshape(-1, N_EXPERTS * D_FF_EXPERT, D_MODEL)
    x_new, eg_bf, eu_bf, ed_bf = pl.pallas_call(
        _swiglu_kernel,
        grid=(NT_ALL,),
        in_specs=[
            pl.BlockSpec((TM, D_MODEL), lambda i: (i, 0)),
            _layer_block(layer, 1, D_MODEL),
            weights(D_MODEL, D_FF), weights(D_MODEL, D_FF), weights(D_FF, D_MODEL),
            pl.BlockSpec((None, up_rows, D_FF_EXPERT), lambda i: (expert_index, slab(i), 0)),
            pl.BlockSpec((None, up_rows, D_FF_EXPERT), lambda i: (expert_index, slab(i), 0)),
            pl.BlockSpec((None, down_rows, D_MODEL), lambda i: (expert_index, slab(i), 0)),
        ],
        out_specs=[
            pl.BlockSpec((TM, D_MODEL), lambda i: (i, 0)),
            pl.BlockSpec((up_rows, D_FF_EXPERT), lambda i: (slab(i), 0)),
            pl.BlockSpec((up_rows, D_FF_EXPERT), lambda i: (slab(i), 0)),
            pl.BlockSpec((down_rows, D_MODEL), lambda i: (slab(i), 0)),
        ],
        out_shape=[
            jax.ShapeDtypeStruct((T_ALL, D_MODEL), F32),
            jax.ShapeDtypeStruct((N_EXPERTS * D_MODEL, D_FF_EXPERT), BF16),
            jax.ShapeDtypeStruct((N_EXPERTS * D_MODEL, D_FF_EXPERT), BF16),
            jax.ShapeDtypeStruct((N_EXPERTS * D_FF_EXPERT, D_MODEL), BF16),
        ],
        compiler_params=pltpu.CompilerParams(
            dimension_semantics=("arbitrary",), vmem_limit_bytes=VMEM_LIMIT),
        name="swiglu",
    )(x, norm_g, wg, wu, wd, eg, eu, ed)
    return (x_new, eg_bf.reshape(1, N_EXPERTS, D_MODEL, D_FF_EXPERT),
            eu_bf.reshape(1, N_EXPERTS, D_MODEL, D_FF_EXPERT),
            ed_bf.reshape(1, N_EXPERTS, D_FF_EXPERT, D_MODEL))


def _moe_sort_kernel(x_ref, ng_ref, wrc_ref, hs_ref, meta_ref, off_ref, cnt_ref):
    n = SORT_TILES_PER_STEP
    logits, routes = {}, {}
    for step in range(n + 2):
        if step < n:
            logits[step] = _sort_logits(x_ref[step * TM:(step + 1) * TM, :], ng_ref, wrc_ref)
        j = step - 1
        if 0 <= j < n:
            h_hi, lg = logits.pop(j)
            meta, off, cnt_pad = _sort_route(lg)
            meta_ref[j * TM:(j + 1) * TM, :] = meta
            off_ref[j] = off.astype(jnp.int32)
            cnt_ref[j] = cnt_pad.astype(jnp.int32)
            routes[j] = (h_hi, meta)
        j = step - 2
        if 0 <= j < n:
            hs_ref[j * SORT_ROWS:(j + 1) * SORT_ROWS, :] = _sort_gather(*routes.pop(j))


def _sort_logits(x, ng_ref, wrc_ref):
    h = _rms(x, ng_ref[...])
    h_hi = h.astype(BF16)
    h_lo = (h - h_hi.astype(F32)).astype(BF16)
    both = jnp.dot(h_hi, wrc_ref[...], preferred_element_type=F32)
    logits = both[:, :LANES] + both[:, LANES:] + jnp.dot(h_lo, wrc_ref[:, :LANES], preferred_element_type=F32)
    return h_hi, logits


def _sort_route(logits):
    lane = lax.broadcasted_iota(jnp.int32, (1, LANES), 1)
    logits = jnp.where(lane < N_EXPERTS, logits, NEG_INF)
    m1 = jnp.max(logits, axis=-1, keepdims=True)
    i1 = jnp.min(jnp.where(logits == m1, lane, LANES), axis=-1, keepdims=True)
    rest = jnp.where(lane == i1, NEG_INF, logits)
    m2 = jnp.max(rest, axis=-1, keepdims=True)
    i2 = jnp.min(jnp.where(rest == m2, lane, LANES), axis=-1, keepdims=True)
    e2 = jnp.exp(m2 - m1)
    den = 1.0 + e2
    sel0 = lane == i1
    sel1 = lane == i2
    routed = jnp.where(jnp.logical_or(sel0, sel1), 1.0, 0.0)
    r_i = lax.broadcasted_iota(jnp.int32, (TM, TM), 0)
    c_i = lax.broadcasted_iota(jnp.int32, (TM, TM), 1)
    lower = jnp.where(r_i > c_i, 1.0, 0.0).astype(BF16)
    rank = jnp.dot(lower, routed.astype(BF16), preferred_element_type=F32)
    cnt = jnp.broadcast_to(jnp.sum(routed, axis=0, keepdims=True), (8, LANES))
    cnt_pad = jnp.floor((cnt + (PIECE - 1)) * (1.0 / PIECE)) * PIECE
    incl = cnt_pad
    for sh in (1, 2, 4):
        incl = incl + jnp.where(lane >= sh, pltpu.roll(incl, sh, 1), 0.0)
    off = incl - cnt_pad
    pos = off[0:1] + rank
    pos0 = jnp.sum(jnp.where(sel0, pos, 0.0), axis=-1, keepdims=True)
    pos1 = jnp.sum(jnp.where(sel1, pos, 0.0), axis=-1, keepdims=True)
    meta = jnp.where(lane == 0, pos0, jnp.where(lane == 1, pos1,
                     jnp.where(lane == 2, 1.0 / den, jnp.where(lane == 3, e2 / den, 0.0))))
    return meta, off, cnt_pad


def _sort_gather(h_hi, meta):
    meta_t = meta.T
    p_row = lax.broadcasted_iota(jnp.int32, (SORT_ROWS, 1), 0).astype(F32)
    hit = jnp.logical_or(p_row == meta_t[0:1], p_row == meta_t[1:2])
    gather = jnp.where(hit, 1.0, 0.0).astype(BF16)
    return jnp.dot(gather, h_hi, preferred_element_type=F32).astype(BF16)


def _moe_sort(x, norm_g, wr_cat, layer, moe_index):
    return pl.pallas_call(
        _moe_sort_kernel,
        grid=(NT_ALL // SORT_TILES_PER_STEP,),
        in_specs=[
            pl.BlockSpec((SORT_TILES_PER_STEP * TM, D_MODEL), lambda i: (i, 0)),
            _layer_block(layer, 1, D_MODEL),
            _layer_block(moe_index, D_MODEL, 2 * LANES),
        ],
        out_specs=[
            pl.BlockSpec((SORT_TILES_PER_STEP * SORT_ROWS, D_MODEL), lambda i: (i, 0)),
            pl.BlockSpec((SORT_TILES_PER_STEP * TM, LANES), lambda i: (i, 0)),
            pl.BlockSpec((SORT_TILES_PER_STEP, 8, LANES), lambda i: (i, 0, 0)),
            pl.BlockSpec((SORT_TILES_PER_STEP, 8, LANES), lambda i: (i, 0, 0)),
        ],
        out_shape=[
            jax.ShapeDtypeStruct((NT_ALL * SORT_ROWS, D_MODEL), BF16),
            jax.ShapeDtypeStruct((T_ALL, LANES), F32),
            jax.ShapeDtypeStruct((NT_ALL, 8, LANES), jnp.int32),
            jax.ShapeDtypeStruct((NT_ALL, 8, LANES), jnp.int32),
        ],
        compiler_params=pltpu.CompilerParams(
            dimension_semantics=("arbitrary",), vmem_limit_bytes=VMEM_LIMIT),
        name="moe_sort",
    )(x, norm_g, wr_cat)


def _moe_experts_kernel(off_ref, cnt_ref, hs_hbm, wg_ref, wu_ref, wrem_ref, wd_ref, ys_hbm,
                        buf, lhs, sem_in, sem_out):
    s = pl.program_id(0)
    e = pl.program_id(1)
    row0 = pl.multiple_of(s * SUPER_ROWS, PIECE)
    is_first = e == 0
    is_last = e == N_EXPERTS - 1

    def copy_in(t):
        r = pl.multiple_of(t * SORT_ROWS, PIECE)
        return pltpu.make_async_copy(hs_hbm.at[pl.ds(row0 + r, SORT_ROWS)], buf.at[pl.ds(r, SORT_ROWS)],
                                     sem_in.at[t])

    def copy_out(t):
        r = pl.multiple_of(t * SORT_ROWS, PIECE)
        return pltpu.make_async_copy(buf.at[pl.ds(r, SORT_ROWS)], ys_hbm.at[pl.ds(row0 + r, SORT_ROWS)],
                                     sem_out.at[t])

    def for_tiles(lo, hi, fn):
        hi = jnp.maximum(lo, hi)

        def body(t, carry):
            fn(t)
            return carry

        lax.fori_loop(lo, hi, body, 0)
        return hi

    @pl.when(is_first)
    def _():
        for t in range(TILES_PER_SUPER):
            copy_in(t).start()

    starts, cums = [], [0]
    for i in range(TILES_PER_SUPER):
        idx = (s * TILES_PER_SUPER + i) * N_EXPERTS + e
        starts.append(i * SORT_ROWS + off_ref[idx])
        cums.append(cums[-1] + cnt_ref[idx] // PIECE)
    n_pieces = cums[-1]

    def locate(p):
        pc = jnp.minimum(p, n_pieces - 1)
        row = starts[0] + PIECE * pc
        tile = jnp.int32(0)
        for i in range(1, TILES_PER_SUPER):
            inside = pc >= cums[i]
            row = jnp.where(inside, starts[i] + PIECE * (pc - cums[i]), row)
            tile = jnp.where(inside, i, tile)
        return row, tile

    def piece_rows(q):
        src, dst, tiles = [], [], []
        for j in range(PIECES_PER_CHUNK):
            p = q * PIECES_PER_CHUNK + j
            row, tile = locate(p)
            src.append(row)
            dst.append(jnp.where(p < n_pieces, row, SUPER_ROWS))
            tiles.append(tile)
        return tuple(src), tuple(dst), tiles[0], tiles[-1]

    def chunk_body(q, carry):
        (src, dst, _, last_tile), arrived, sent = carry
        arrived = for_tiles(arrived, jnp.where(is_first, last_tile + 1, 0), lambda t: copy_in(t).wait())
        for j in range(PIECES_PER_CHUNK):
            lhs[j * PIECE:(j + 1) * PIECE, :] = buf[pl.ds(pl.multiple_of(src[j], PIECE), PIECE), :]
        xs = lhs[...]
        gate = jnp.dot(xs, wg_ref[0, :, :FF_MAIN], preferred_element_type=F32)
        up = jnp.dot(xs, wu_ref[0, :, :FF_MAIN], preferred_element_type=F32)
        rem = jnp.dot(xs, wrem_ref[0], preferred_element_type=F32)
        nxt = piece_rows(q + 1)
        act = jnp.concatenate([jax.nn.silu(gate) * up,
                               jax.nn.silu(rem[:, :FF_REM]) * rem[:, FF_REM:]], axis=1).astype(BF16)
        y = jnp.dot(act, wd_ref[0], preferred_element_type=F32).astype(BF16)
        for j in range(PIECES_PER_CHUNK):
            buf[pl.ds(pl.multiple_of(dst[j], PIECE), PIECE), :] = y[j * PIECE:(j + 1) * PIECE]
        sent = for_tiles(sent, jnp.where(is_last, nxt[2], 0), lambda t: copy_out(t).start())
        return nxt, arrived, sent

    n_chunks = (n_pieces + PIECES_PER_CHUNK - 1) // PIECES_PER_CHUNK
    _, arrived, sent = lax.fori_loop(0, n_chunks, chunk_body, (piece_rows(0), jnp.int32(0), jnp.int32(0)))
    for_tiles(arrived, jnp.where(is_first, TILES_PER_SUPER, 0), lambda t: copy_in(t).wait())
    for_tiles(sent, jnp.where(is_last, TILES_PER_SUPER, 0), lambda t: copy_out(t).start())
    for_tiles(0, jnp.where(is_last, TILES_PER_SUPER, 0), lambda t: copy_out(t).wait())


def _moe_experts(off, cnt, hs, wg, wu, wrem, wd, moe_index):
    expert = lambda *shape: pl.BlockSpec((None, 1) + shape, lambda s, e, *_: (moe_index, e, 0, 0))
    grid_spec = pltpu.PrefetchScalarGridSpec(
        num_scalar_prefetch=2,
        grid=(N_SUPER, N_EXPERTS),
        in_specs=[
            pl.BlockSpec(memory_space=pl.ANY),
            expert(D_MODEL, D_FF_EXPERT), expert(D_MODEL, D_FF_EXPERT), expert(D_MODEL, 2 * FF_REM),
            expert(D_FF_EXPERT, D_MODEL),
        ],
        out_specs=pl.BlockSpec(memory_space=pl.ANY),
        scratch_shapes=[pltpu.VMEM((SUPER_ROWS + PIECE, D_MODEL), BF16),
                        pltpu.VMEM((CHUNK_ROWS, D_MODEL), BF16),
                        pltpu.SemaphoreType.DMA((TILES_PER_SUPER,)),
                        pltpu.SemaphoreType.DMA((TILES_PER_SUPER,))],
    )
    return pl.pallas_call(
        _moe_experts_kernel,
        grid_spec=grid_spec,
        out_shape=jax.ShapeDtypeStruct((NT_ALL * SORT_ROWS, D_MODEL), BF16),
        compiler_params=pltpu.CompilerParams(
            dimension_semantics=("arbitrary", "arbitrary"), vmem_limit_bytes=VMEM_LIMIT),
        name="moe_experts",
    )(off, cnt, hs, wg, wu, wrem, wd)


def _unsorted_residual(x_ref, ys_ref, meta_ref):
    meta = meta_ref[...]
    p_col = lax.broadcasted_iota(jnp.int32, (1, SORT_ROWS), 1).astype(F32)
    scatter = (jnp.where(p_col == meta[:, 0:1], meta[:, 2:3], 0.0)
               + jnp.where(p_col == meta[:, 1:2], meta[:, 3:4], 0.0)).astype(BF16)
    return x_ref[...] + jnp.dot(scatter, ys_ref[...], preferred_element_type=F32)


def _moe_unsort_kernel(x_ref, ys_ref, meta_ref, xo_ref):
    xo_ref[...] = _unsorted_residual(x_ref, ys_ref, meta_ref)


def _moe_unsort_final_kernel(x_ref, ys_ref, meta_ref, g_ref, yp_ref, ysm_ref):
    i = pl.program_id(0)
    y = _rms(_unsorted_residual(x_ref, ys_ref, meta_ref), g_ref[...])

    @pl.when(i < NT_PROMPT)
    def _():
        yp_ref[...] = y

    @pl.when(i == NT_PROMPT)
    def _():
        ysm_ref[...] = y


def _moe_unsort(x, ys, meta, final_g=None):
    in_specs = [
        pl.BlockSpec((TM, D_MODEL), lambda i: (i, 0)),
        pl.BlockSpec((SORT_ROWS, D_MODEL), lambda i: (i, 0)),
        pl.BlockSpec((TM, LANES), lambda i: (i, 0)),
    ]
    params = pltpu.CompilerParams(dimension_semantics=("arbitrary",), vmem_limit_bytes=VMEM_LIMIT)
    if final_g is None:
        return pl.pallas_call(
            _moe_unsort_kernel,
            grid=(NT_ALL,),
            in_specs=in_specs,
            out_specs=pl.BlockSpec((TM, D_MODEL), lambda i: (i, 0)),
            out_shape=jax.ShapeDtypeStruct((T_ALL, D_MODEL), F32),
            compiler_params=params,
            name="moe_unsort",
        )(x, ys, meta)
    return pl.pallas_call(
        _moe_unsort_final_kernel,
        grid=(NT_ALL,),
        in_specs=in_specs + [pl.BlockSpec((1, D_MODEL), lambda i: (0, 0))],
        out_specs=[pl.BlockSpec((TM, D_MODEL), lambda i: (jnp.minimum(i, NT_PROMPT - 1), 0)),
                   pl.BlockSpec((TM, D_MODEL), lambda i: (0, 0))],
        out_shape=[jax.ShapeDtypeStruct((T_PROMPT, D_MODEL), F32),
                   jax.ShapeDtypeStruct((T_SAMPLE, D_MODEL), F32)],
        compiler_params=params,
        name="moe_unsort_final",
    )(x, ys, meta, final_g)


def _moe(x, norm_g, wr_cat, wg, wu, wrem, wd, layer, moe_index, final_g=None):
    hs, meta, off, cnt = _moe_sort(x, norm_g, wr_cat, layer, moe_index)
    off = off[:, 0, :N_EXPERTS].reshape(-1)
    cnt = cnt[:, 0, :N_EXPERTS].reshape(-1)
    ys = _moe_experts(off, cnt, hs, wg, wu, wrem, wd, moe_index)
    return _moe_unsort(x, ys, meta, final_g)


def kernel(x_prompt, x_sample, cache_win_k, cache_win_v, attn_norm_g, w_in, b_in, chunk_ln_g, chunk_ln_b, w_spatial, b_spatial, attn_sinks, mix_norm_a_g, mix_norm_b_g, w_out, ffn_norm_g, ffn_w_gate, ffn_w_up, ffn_w_down, router_w, expert_w_gate, expert_w_up, expert_w_down, final_norm_g):
    w_in_p = jnp.concatenate([w_in[..., :Q0], _g_major(w_in[..., Q0:K0], -1), w_in[..., K0:]], axis=-1).astype(BF16)
    b_in_p = jnp.concatenate([b_in[..., :Q0], _g_major(b_in[..., Q0:K0], -1), b_in[..., K0:]], axis=-1)
    w_out_p = jnp.concatenate([w_out[:, :GMLP_WIDTH], _g_major(w_out[:, GMLP_WIDTH:], 1)], axis=1).astype(BF16)
    g_b_p = _g_major(mix_norm_b_g, 1)
    bs_prompt = jnp.repeat(jnp.transpose(b_spatial, (0, 2, 1)), GMLP_HEAD_DIM, axis=2)
    bs_sample = jnp.tile(bs_prompt[:, :DEC_SEQ], (1, 2, 1))
    t_idx = np.arange(DEC_SEQ)
    ws_small = jnp.stack([w_spatial[:, :, t_idx, np.maximum(t_idx - d, 0)] for d in range(DEC_SEQ)], axis=1)
    ws_sample = jnp.tile(jnp.repeat(jnp.transpose(ws_small, (0, 1, 3, 2)), GMLP_HEAD_DIM, axis=3),
                         (1, 1, 2, 1))
    sink_row = jnp.repeat(attn_sinks, SEQS_PER_TILE * DEC_SEQ, axis=1)[:, None, :]
    bias_prompt = jnp.asarray(_prompt_bias())
    bias_sample = jnp.asarray(_sample_bias())
    wr_pad = jnp.pad(router_w, ((0, 0), (0, 0), (0, LANES - N_EXPERTS)))
    wr_hi = wr_pad.astype(BF16)
    wr_cat = jnp.concatenate([wr_hi, (wr_pad - wr_hi.astype(F32)).astype(BF16)], axis=-1)

    ck = cache_win_k.reshape(DEPTH, DEC_BATCH, CACHE_W, KV_WIDTH)
    cv = cache_win_v.reshape(DEPTH, DEC_BATCH, CACHE_W, KV_WIDTH)

    lw = dict(
        norm_g=attn_norm_g[:, None, :], w_in=w_in_p, b_in=b_in_p[:, None, :],
        ln_g=chunk_ln_g.reshape(DEPTH, 1, GMLP_WIDTH), ln_b=chunk_ln_b.reshape(DEPTH, 1, GMLP_WIDTH),
        w_s=w_spatial, b_s_prompt=bs_prompt, w_s_sample=ws_sample, b_s_sample=bs_sample,
        sink_row=sink_row, bias_prompt=bias_prompt, bias_sample=bias_sample,
        g_a=mix_norm_a_g[:, None, :], g_b=g_b_p[:, None, :], w_out=w_out_p)
    ffn_g = ffn_norm_g[:, None, :]

    x = None
    kp, vp, ksm, vsm, cvs = [], [], [], [], []
    for l in range(DEPTH):
        lw["sinks"] = attn_sinks[l]
        i = l // 2
        cast = (i, (ffn_w_gate, ffn_w_up, ffn_w_down)) if l % 2 == 0 else None
        x_in = x_prompt.reshape(T_PROMPT, D_MODEL) if l == 0 else x
        x_new, k_p, v_p, *ffn_bf = _mixer_prompt(x_in, lw, l, cast)
        if l == 0:
            x_new, k_s, v_s, cv_s = _mixer_sample(x_sample.reshape(T_SAMPLE, D_MODEL), 0, x_new, ck, cv, l, lw)
        else:
            x_new, k_s, v_s, cv_s = _mixer_sample(x, NT_PROMPT, x_new, ck, cv, l, lw)
        kp.append(k_p)
        vp.append(v_p)
        ksm.append(k_s)
        vsm.append(v_s)
        cvs.append(cv_s)
        if l % 2 == 0:
            ffn_wg, ffn_wu, ffn_wd = (w[None] for w in ffn_bf)
            x, ex_wg, ex_wu, ex_wd = _swiglu(x_new, ffn_g, ffn_wg, ffn_wu, ffn_wd, l, 0,
                                             (expert_w_gate, expert_w_up, expert_w_down), i)
            ex_wrem = jnp.concatenate([ex_wg[..., FF_MAIN:], ex_wu[..., FF_MAIN:]], axis=-1)
        else:
            final_g = final_norm_g[None] if l == DEPTH - 1 else None
            x = _moe(x_new, ffn_g, wr_cat[i:i + 1], ex_wg, ex_wu, ex_wrem, ex_wd, l, 0, final_g)
    y_p, y_s = x
    win_p = (DEPTH, BATCH, WINDOW, KV_HEADS, HEAD_DIM)
    win_s = (DEPTH, DEC_BATCH, CACHE_W, KV_HEADS, HEAD_DIM)
    return (y_p.reshape(BATCH, SEQ, D_MODEL), y_s.reshape(DEC_BATCH, DEC_SEQ, D_MODEL),
            jnp.stack(kp).reshape(win_p), jnp.stack(vp).reshape(win_p),
            jnp.stack(ksm).reshape(win_s), jnp.stack(vsm).reshape(win_s),
            jnp.stack(cvs).reshape(DEPTH, DEC_BATCH, DEC_SEQ, GMLP_HEADS, GMLP_HEAD_DIM))
```

```python
import functools

import numpy as np
import jax
import jax.numpy as jnp
from jax import lax
from jax.experimental import pallas as pl
from jax.experimental.pallas import tpu as pltpu

D_MODEL = 1024
BATCH = 4
SEQ = 4096
DEPTH = 4
DEC_BATCH = 128
DEC_SEQ = 4
HEAD_DIM = 64
ATTN_HEADS = 8
KV_HEADS = 2
GQA_GROUP = ATTN_HEADS // KV_HEADS
ATTN_WIDTH = ATTN_HEADS * HEAD_DIM
KV_WIDTH = KV_HEADS * HEAD_DIM
WINDOW = 128
CHUNK = 128
GMLP_HEADS = 4
GMLP_HEAD_DIM = 128
GMLP_WIDTH = GMLP_HEADS * GMLP_HEAD_DIM
IN_WIDTH = 2 * GMLP_WIDTH + ATTN_WIDTH + 2 * KV_WIDTH
D_FF = 2816
N_EXPERTS = 8
D_FF_EXPERT = 1408
CACHE_W = 128
EPS = 1e-6

LANES = 128
TM = 512
T_PROMPT = BATCH * SEQ
T_SAMPLE = DEC_BATCH * DEC_SEQ
T_ALL = T_PROMPT + T_SAMPLE
NT_PROMPT = T_PROMPT // TM
NT_ALL = T_ALL // TM
TILES_PER_SEQ = SEQ // TM
BLOCKS_PER_TILE = TM // WINDOW
SEQ_GROUP = 16
SEQS_PER_TILE = 4
N_SEQ_GROUPS = DEC_BATCH // SEQ_GROUP
KEYS_PAD = CACHE_W + 8
PIECE = 16
SORT_ROWS = 2 * TM + N_EXPERTS * PIECE
TILES_PER_SUPER = 11
N_SUPER = NT_ALL // TILES_PER_SUPER
SUPER_ROWS = TILES_PER_SUPER * SORT_ROWS
SORT_TILES_PER_STEP = 3
CHUNK_ROWS = 256
FF_REM = D_FF_EXPERT % 256
FF_MAIN = D_FF_EXPERT - FF_REM
PIECES_PER_CHUNK = CHUNK_ROWS // PIECE
VMEM_LIMIT = 56 * 1024 * 1024

F32 = jnp.float32
BF16 = jnp.bfloat16
NEG_INF = float("-inf")

Q0 = 2 * GMLP_WIDTH
K0 = Q0 + ATTN_WIDTH
V0 = K0 + KV_WIDTH


def _g_major(a, axis):
    axis = axis % a.ndim
    shape = a.shape
    a = a.reshape(shape[:axis] + (KV_HEADS, GQA_GROUP, HEAD_DIM) + shape[axis + 1:])
    return jnp.swapaxes(a, axis, axis + 1).reshape(shape)


def _slope(kv, g):
    return 2.0 ** (-8.0 * (kv * GQA_GROUP + g + 1) / ATTN_HEADS)


def _prompt_bias():
    c = np.arange(WINDOW)[:, None]
    qi = np.arange(WINDOW)[None, :]
    dist = np.where(c > qi, qi + WINDOW - c, qi - c)
    out = np.zeros((2, KV_HEADS, WINDOW, GQA_GROUP * WINDOW), np.float32)
    for var in range(2):
        ok = ~((var == 1) & (c > qi))
        for kv in range(KV_HEADS):
            for g in range(GQA_GROUP):
                out[var, kv, :, g * WINDOW:(g + 1) * WINDOW] = np.where(ok, -_slope(kv, g) * dist, NEG_INF)
    return out


def _sample_bias():
    out = np.zeros((KEYS_PAD, LANES), np.float32)
    j = np.arange(KEYS_PAD)
    kpos = j - CACHE_W
    for kv in range(KV_HEADS):
        for g in range(GQA_GROUP):
            for s in range(SEQS_PER_TILE):
                for t in range(DEC_SEQ):
                    col = ((kv * GQA_GROUP + g) * SEQS_PER_TILE + s) * DEC_SEQ + t
                    dist = t - kpos
                    ok = (dist >= 0) & (dist < WINDOW) & (j < CACHE_W + DEC_SEQ)
                    out[:, col] = np.where(ok, -_slope(kv, g) * dist, NEG_INF)
    return out


def _rms(x, g):
    return x * lax.rsqrt(jnp.mean(x * x, axis=-1, keepdims=True) + EPS) * g


def _gelu(x):
    return 0.5 * x * (1.0 + lax.erf(x * (2.0 ** -0.5)))


def _gmlp_v(z_v, lng_ref, lnb_ref):
    out = []
    for hh in range(GMLP_HEADS):
        sl = slice(hh * LANES, (hh + 1) * LANES)
        vh = _gelu(z_v[:, sl])
        vc = vh - jnp.mean(vh, axis=-1, keepdims=True)
        y = vc * lax.rsqrt(jnp.mean(vc * vc, axis=-1, keepdims=True) + EPS)
        out.append(y * lng_ref[:, sl] + lnb_ref[:, sl])
    return out


def _dot_nt(a, b):
    return lax.dot_general(a, b, (((1,), (1,)), ((), ())), preferred_element_type=F32)


N_MIXER_INPUTS = 13


def _mixer_prompt_kernel(*refs, n_cast):
    i = pl.program_id(0)
    n_in = N_MIXER_INPUTS + n_cast
    tile_refs = refs[:N_MIXER_INPUTS] + refs[n_in:n_in + 3] + refs[n_in + 3 + n_cast:]

    @pl.when(i < NT_PROMPT)
    def _():
        _mixer_prompt_tile(*tile_refs)

    @pl.when(i == NT_PROMPT)
    def _():
        xo_ref = refs[n_in]
        xo_ref[...] = jnp.zeros_like(xo_ref)

    for src, dst in zip(refs[N_MIXER_INPUTS:n_in], refs[n_in + 3:n_in + 3 + n_cast]):
        dst[...] = src[...].astype(BF16)


def _mixer_prompt_tile(x_ref, ng_ref, win_ref, bin_ref, lng_ref, lnb_ref, ws_ref, bs_ref,
                       sink_ref, bias_ref, ga_ref, gb_ref, wout_ref,
                       xo_ref, ks_ref, vs_ref, kprev_ref, vprev_ref):
    i = pl.program_id(0)

    @pl.when(i == 0)
    def _():
        kprev_ref[...] = jnp.zeros_like(kprev_ref)
        vprev_ref[...] = jnp.zeros_like(vprev_ref)

    x = x_ref[...]
    h = _rms(x, ng_ref[...]).astype(BF16)

    def in_proj(lo, hi):
        return jnp.dot(h, win_ref[:, lo:hi], preferred_element_type=F32) + bin_ref[:, lo:hi]

    z_att = in_proj(Q0, IN_WIDTH)
    q = (z_att[:, :ATTN_WIDTH] * (HEAD_DIM ** -0.5)).astype(BF16)
    k = z_att[:, ATTN_WIDTH:ATTN_WIDTH + KV_WIDTH]
    val = z_att[:, ATTN_WIDTH + KV_WIDTH:]
    k_bf = k.astype(BF16)
    lane = lax.broadcasted_iota(jnp.int32, (1, LANES), 1)
    low = lane < HEAD_DIM
    first = jnp.where(i % TILES_PER_SEQ == 0, 1, 0)
    c_i = lax.broadcasted_iota(jnp.int32, (WINDOW, WINDOW), 0)
    q_i = lax.broadcasted_iota(jnp.int32, (WINDOW, WINDOW), 1)
    from_prev = jnp.concatenate([c_i > q_i] * GQA_GROUP, axis=1)
    g_lane = lax.broadcasted_iota(jnp.int32, (1, GQA_GROUP * WINDOW), 1) // WINDOW
    vt_bf = val.T.astype(BF16)
    vt_all = jnp.concatenate([vprev_ref[...], vt_bf], axis=1)
    k_all = jnp.concatenate([kprev_ref[...], k_bf], axis=0)
    zero = jnp.zeros((), BF16)
    k_heads = [jnp.where(low, k_all, zero), jnp.where(low, zero, k_all)]
    sinks = []
    for kv in range(KV_HEADS):
        sink = jnp.full((1, GQA_GROUP * WINDOW), sink_ref[kv * GQA_GROUP], F32)
        for g in range(1, GQA_GROUP):
            sink = jnp.where(g_lane == g, sink_ref[kv * GQA_GROUP + g], sink)
        sinks.append(sink)

    def attention_block(c):
        rows = slice(c * WINDOW, (c + 1) * WINDOW)
        band = slice(c * WINDOW, (c + 2) * WINDOW)
        o_t = []
        q_rows = jnp.concatenate([q[rows, g * LANES:(g + 1) * LANES] for g in range(GQA_GROUP)], axis=0)
        keys = jnp.concatenate([k_heads[0][band], k_heads[1][band]], axis=0)
        s_both = _dot_nt(keys, q_rows)
        for kv in range(KV_HEADS):
            s2 = s_both[kv * 2 * WINDOW:(kv + 1) * 2 * WINDOW]
            bias = bias_ref[first, kv] if c == 0 else bias_ref[0, kv]
            s = jnp.where(from_prev, s2[:WINDOW], s2[WINDOW:]) + bias
            m = jnp.maximum(jnp.max(s, axis=0, keepdims=True), sinks[kv])
            p = jnp.exp(s - m)
            den = jnp.sum(p, axis=0, keepdims=True) + jnp.exp(sinks[kv] - m)
            p2 = jnp.concatenate([jnp.where(from_prev, p, 0.0), jnp.where(from_prev, 0.0, p)],
                                 axis=0).astype(BF16)
            o = jnp.dot(vt_all[kv * HEAD_DIM:(kv + 1) * HEAD_DIM, band], p2,
                        preferred_element_type=F32)
            o_t.append(o / den)
        bt = jnp.concatenate(o_t, axis=0)
        return jnp.concatenate(
            [bt[:, g * WINDOW:(g + 1) * WINDOW].T for g in range(GQA_GROUP)], axis=1)

    z_u = in_proj(0, GMLP_WIDTH)
    b_rows = [attention_block(0)]
    z_v = in_proj(GMLP_WIDTH, Q0)
    b_rows.append(attention_block(1))
    u = _gelu(z_u)
    b_rows.append(attention_block(2))
    v_heads = _gmlp_v(z_v, lng_ref, lnb_ref)
    row = lax.broadcasted_iota(jnp.int32, (CHUNK, CHUNK), 0)
    col = lax.broadcasted_iota(jnp.int32, (CHUNK, CHUNK), 1)
    a_cols = []
    for hh in range(GMLP_HEADS):
        w = jnp.where(row >= col, ws_ref[hh], 0.0).astype(BF16)
        v_cat = jnp.concatenate(
            [v_heads[hh][c * CHUNK:(c + 1) * CHUNK] for c in range(BLOCKS_PER_TILE)], axis=1)
        mixed = jnp.dot(w, v_cat.astype(BF16), preferred_element_type=F32)
        bias = bs_ref[:, hh * LANES:(hh + 1) * LANES]
        a_cols.append(jnp.concatenate(
            [mixed[:, c * LANES:(c + 1) * LANES] + bias for c in range(BLOCKS_PER_TILE)], axis=0))
    a = u * jnp.concatenate(a_cols, axis=1)
    a_n = _rms(a, ga_ref[...]).astype(BF16)
    b_rows.append(attention_block(3))
    out = x + jnp.dot(a_n, wout_ref[:GMLP_WIDTH, :], preferred_element_type=F32)
    b_n = _rms(jnp.concatenate(b_rows, axis=0), gb_ref[...]).astype(BF16)
    xo_ref[...] = out + jnp.dot(b_n, wout_ref[GMLP_WIDTH:, :], preferred_element_type=F32)

    kprev_ref[...] = k_bf[TM - WINDOW:]
    vprev_ref[...] = vt_bf[:, TM - WINDOW:]
    ks_ref[0] = k[TM - WINDOW:]
    vs_ref[0] = val[TM - WINDOW:]


def _layer_block(layer, *shape, **kwargs):
    return pl.BlockSpec((None,) + shape, lambda *_: (layer,) + (0,) * len(shape), **kwargs)


def _mixer_prompt(x, lw, layer, cast=None):
    const = lambda *shape: pl.BlockSpec(shape, lambda i: (0,) * len(shape))
    per_layer = functools.partial(_layer_block, layer)
    cast_in, cast_specs, cast_out_specs, cast_shapes = [], [], [], []
    if cast is not None:
        ffn_index, weights = cast
        for w in weights:
            rows, cols = w.shape[1:]
            n_slabs = NT_PROMPT if (rows // NT_PROMPT) % PIECE == 0 else NT_PROMPT // 2
            slab = lambda i, n=n_slabs: jnp.minimum(i, n - 1)
            cast_in.append(w)
            cast_specs.append(pl.BlockSpec((None, rows // n_slabs, cols),
                                           lambda i, slab=slab: (ffn_index, slab(i), 0)))
            cast_out_specs.append(pl.BlockSpec((rows // n_slabs, cols), lambda i, slab=slab: (slab(i), 0)))
            cast_shapes.append(jax.ShapeDtypeStruct((rows, cols), BF16))
    return pl.pallas_call(
        functools.partial(_mixer_prompt_kernel, n_cast=len(cast_in)),
        grid=(NT_ALL,),
        in_specs=[
            pl.BlockSpec((TM, D_MODEL), lambda i: (jnp.minimum(i, NT_PROMPT - 1), 0)),
            per_layer(1, D_MODEL), per_layer(D_MODEL, IN_WIDTH), per_layer(1, IN_WIDTH),
            per_layer(1, GMLP_WIDTH), per_layer(1, GMLP_WIDTH),
            per_layer(GMLP_HEADS, CHUNK, CHUNK), per_layer(CHUNK, GMLP_WIDTH),
            pl.BlockSpec(memory_space=pltpu.SMEM),
            const(2, KV_HEADS, WINDOW, GQA_GROUP * WINDOW),
            per_layer(1, GMLP_WIDTH), per_layer(1, ATTN_WIDTH), per_layer(D_MODEL, D_MODEL),
        ] + cast_specs,
        out_specs=[
            pl.BlockSpec((TM, D_MODEL), lambda i: (i, 0)),
            pl.BlockSpec((1, WINDOW, KV_WIDTH), lambda i: (jnp.minimum(i // TILES_PER_SEQ, BATCH - 1), 0, 0)),
            pl.BlockSpec((1, WINDOW, KV_WIDTH), lambda i: (jnp.minimum(i // TILES_PER_SEQ, BATCH - 1), 0, 0)),
        ] + cast_out_specs,
        out_shape=[
            jax.ShapeDtypeStruct((T_ALL, D_MODEL), F32),
            jax.ShapeDtypeStruct((BATCH, WINDOW, KV_WIDTH), F32),
            jax.ShapeDtypeStruct((BATCH, WINDOW, KV_WIDTH), F32),
        ] + cast_shapes,
        scratch_shapes=[pltpu.VMEM((WINDOW, KV_WIDTH), BF16), pltpu.VMEM((WINDOW, KV_WIDTH), BF16)],
        compiler_params=pltpu.CompilerParams(
            dimension_semantics=("arbitrary",), vmem_limit_bytes=VMEM_LIMIT),
        name="mixer_prompt",
    )(x, lw["norm_g"], lw["w_in"], lw["b_in"], lw["ln_g"], lw["ln_b"], lw["w_s"], lw["b_s_prompt"],
      lw["sinks"], lw["bias_prompt"], lw["g_a"], lw["g_b"], lw["w_out"], *cast_in)


def _mixer_sample_kernel(x_ref, xalias_ref, ck_ref, cv_ref, ng_ref, win_ref, bin_ref, lng_ref, lnb_ref,
                         wsd_ref, bsd_ref, sinkrow_ref, bias_ref, ga_ref, gb_ref, wout_ref,
                         xo_ref, ko_ref, vo_ref, cvo_ref,
                         q_s, k_s, v_s, a_s, b_s, kfull, vfull):
    del xalias_ref
    j = pl.program_id(0)

    @pl.when(j == 0)
    def _():
        h = _rms(x_ref[...], ng_ref[...]).astype(BF16)
        z = jnp.dot(h, win_ref[...], preferred_element_type=F32) + bin_ref[...]
        u = _gelu(z[:, :GMLP_WIDTH])
        v = jnp.concatenate(_gmlp_v(z[:, GMLP_WIDTH:Q0], lng_ref, lnb_ref), axis=1)
        cvo_ref[...] = v
        t_row = lax.broadcasted_iota(jnp.int32, (8, 1), 0) % DEC_SEQ

        def tile_rows(pat):
            return jnp.broadcast_to(pat[None], (TM // 8,) + pat.shape).reshape(TM, pat.shape[-1])

        mixed = tile_rows(bsd_ref[...])
        for d in range(DEC_SEQ):
            vd = v if d == 0 else pltpu.roll(v, d, 0)
            mixed = mixed + tile_rows(jnp.where(t_row >= d, wsd_ref[d], 0.0)) * vd
        a_s[...] = _rms(u * mixed, ga_ref[...]).astype(BF16)
        q_s[...] = z[:, Q0:K0] * (HEAD_DIM ** -0.5)
        k_s[...] = z[:, K0:V0]
        v_s[...] = z[:, V0:]
        kfull[:, CACHE_W + DEC_SEQ:, :] = jnp.zeros((SEQ_GROUP, KEYS_PAD - CACHE_W - DEC_SEQ, KV_WIDTH), F32)
        vfull[:, CACHE_W + DEC_SEQ:, :] = jnp.zeros((SEQ_GROUP, KEYS_PAD - CACHE_W - DEC_SEQ, KV_WIDTH), F32)

    base = pl.multiple_of(j * (SEQ_GROUP * DEC_SEQ), SEQ_GROUP * DEC_SEQ)
    kfull[:, :CACHE_W, :] = ck_ref[0]
    vfull[:, :CACHE_W, :] = cv_ref[0]
    k_new = k_s[pl.ds(base, SEQ_GROUP * DEC_SEQ), :]
    v_new = v_s[pl.ds(base, SEQ_GROUP * DEC_SEQ), :]
    for s in range(SEQ_GROUP):
        kfull[s, CACHE_W:CACHE_W + DEC_SEQ, :] = k_new[s * DEC_SEQ:(s + 1) * DEC_SEQ]
        vfull[s, CACHE_W:CACHE_W + DEC_SEQ, :] = v_new[s * DEC_SEQ:(s + 1) * DEC_SEQ]
    ko_ref[0] = kfull[:, DEC_SEQ:DEC_SEQ + CACHE_W, :]
    vo_ref[0] = vfull[:, DEC_SEQ:DEC_SEQ + CACHE_W, :]

    lane = lax.broadcasted_iota(jnp.int32, (1, LANES), 1)
    low = lane < HEAD_DIM
    col_seq = (lane // DEC_SEQ) % SEQS_PER_TILE
    row_seq = (lax.broadcasted_iota(jnp.int32, (LANES, 1), 0) // DEC_SEQ) % SEQS_PER_TILE
    sink = sinkrow_ref[...]
    zero = jnp.zeros((), BF16)
    quad_rows = SEQS_PER_TILE * DEC_SEQ
    for m in range(SEQ_GROUP // SEQS_PER_TILE):
        r0 = pl.multiple_of(base + m * quad_rows, quad_rows)
        q16 = q_s[pl.ds(r0, quad_rows), :]
        qt = jnp.concatenate(
            [jnp.where(low if kv == 0 else jnp.logical_not(low), q16[:, g * LANES:(g + 1) * LANES], 0.0)
             for kv in range(KV_HEADS) for g in range(GQA_GROUP)], axis=0).astype(BF16)
        s = bias_ref[...]
        for n in range(SEQS_PER_TILE):
            kn = kfull[SEQS_PER_TILE * m + n].astype(BF16)
            s = s + _dot_nt(kn, jnp.where(row_seq == n, qt, zero))
        mx = jnp.maximum(jnp.max(s, axis=0, keepdims=True), sink)
        p = jnp.exp(s - mx)
        p = (p / (jnp.sum(p, axis=0, keepdims=True) + jnp.exp(sink - mx))).astype(BF16)
        o = jnp.zeros((LANES, KV_WIDTH), F32)
        for n in range(SEQS_PER_TILE):
            vn = vfull[SEQS_PER_TILE * m + n].astype(BF16)
            o = o + lax.dot_general(jnp.where(col_seq == n, p, zero), vn, (((0,), (0,)), ((), ())),
                                    preferred_element_type=F32)
        b_s[pl.ds(r0, quad_rows), :] = jnp.concatenate(
            [jnp.where(low, o[g * quad_rows:(g + 1) * quad_rows],
                       o[(GQA_GROUP + g) * quad_rows:(GQA_GROUP + g + 1) * quad_rows])
             for g in range(GQA_GROUP)], axis=1)

    @pl.when(j == N_SEQ_GROUPS - 1)
    def _():
        b_n = _rms(b_s[...], gb_ref[...]).astype(BF16)
        merged = jnp.concatenate([a_s[...], b_n], axis=1)
        xo_ref[...] = x_ref[...] + jnp.dot(merged, wout_ref[...], preferred_element_type=F32)


def _mixer_sample(x, x_block, x_buf, cache_k, cache_v, layer, lw):
    const = lambda *shape: pl.BlockSpec(shape, lambda j: (0,) * len(shape))
    per_layer = functools.partial(_layer_block, layer)
    cache_spec = pl.BlockSpec((1, SEQ_GROUP, CACHE_W, KV_WIDTH), lambda j: (layer, j, 0, 0))
    state_spec = pl.BlockSpec((1, SEQ_GROUP, CACHE_W, KV_WIDTH), lambda j: (j, 0, 0, 0))
    state_shape = jax.ShapeDtypeStruct((N_SEQ_GROUPS, SEQ_GROUP, CACHE_W, KV_WIDTH), F32)
    return pl.pallas_call(
        _mixer_sample_kernel,
        grid=(N_SEQ_GROUPS,),
        in_specs=[
            pl.BlockSpec((TM, D_MODEL), lambda j: (x_block, 0)),
            pl.BlockSpec(memory_space=pl.ANY),
            cache_spec, cache_spec,
            per_layer(1, D_MODEL), per_layer(D_MODEL, IN_WIDTH), per_layer(1, IN_WIDTH),
            per_layer(1, GMLP_WIDTH), per_layer(1, GMLP_WIDTH),
            per_layer(DEC_SEQ, 8, GMLP_WIDTH), per_layer(8, GMLP_WIDTH),
            per_layer(1, LANES), const(KEYS_PAD, LANES),
            per_layer(1, GMLP_WIDTH), per_layer(1, ATTN_WIDTH), per_layer(D_MODEL, D_MODEL),
        ],
        out_specs=[
            pl.BlockSpec((TM, D_MODEL), lambda j: (NT_PROMPT, 0)),
            state_spec, state_spec,
            const(TM, GMLP_WIDTH),
        ],
        out_shape=[
            jax.ShapeDtypeStruct((T_ALL, D_MODEL), F32),
            state_shape, state_shape,
            jax.ShapeDtypeStruct((TM, GMLP_WIDTH), F32),
        ],
        scratch_shapes=[
            pltpu.VMEM((TM, ATTN_WIDTH), F32), pltpu.VMEM((TM, KV_WIDTH), F32),
            pltpu.VMEM((TM, KV_WIDTH), F32), pltpu.VMEM((TM, GMLP_WIDTH), BF16),
            pltpu.VMEM((TM, ATTN_WIDTH), F32),
            pltpu.VMEM((SEQ_GROUP, KEYS_PAD, KV_WIDTH), F32),
            pltpu.VMEM((SEQ_GROUP, KEYS_PAD, KV_WIDTH), F32),
        ],
        input_output_aliases={1: 0},
        compiler_params=pltpu.CompilerParams(
            dimension_semantics=("arbitrary",), vmem_limit_bytes=VMEM_LIMIT),
        name="mixer_sample",
    )(x, x_buf, cache_k, cache_v, lw["norm_g"], lw["w_in"], lw["b_in"], lw["ln_g"], lw["ln_b"],
      lw["w_s_sample"], lw["b_s_sample"], lw["sink_row"], lw["bias_sample"],
      lw["g_a"], lw["g_b"], lw["w_out"])


def _swiglu_kernel(x_ref, ng_ref, wg_ref, wu_ref, wd_ref, eg_ref, eu_ref, ed_ref,
                   xo_ref, ego_ref, euo_ref, edo_ref):
    x = x_ref[...]
    h = _rms(x, ng_ref[...]).astype(BF16)
    gate = jnp.dot(h, wg_ref[...], preferred_element_type=F32)
    ego_ref[...] = eg_ref[...].astype(BF16)
    up = jnp.dot(h, wu_ref[...], preferred_element_type=F32)
    euo_ref[...] = eu_ref[...].astype(BF16)
    act = (jax.nn.silu(gate) * up).astype(BF16)
    xo_ref[...] = x + jnp.dot(act, wd_ref[...], preferred_element_type=F32)
    edo_ref[...] = ed_ref[...].astype(BF16)


def _swiglu(x, norm_g, wg, wu, wd, layer, ffn_index, expert_f32, expert_index):
    weights = functools.partial(_layer_block, ffn_index, pipeline_mode=pl.Buffered(1))
    slab_steps = NT_ALL - 1
    up_rows = N_EXPERTS * D_MODEL // slab_steps
    down_rows = N_EXPERTS * D_FF_EXPERT // slab_steps
    slab = lambda i: jnp.minimum(i, slab_steps - 1)
    eg, eu, ed = expert_f32
    eg = eg.reshape(-1, N_EXPERTS * D_MODEL, D_FF_EXPERT)
    eu = eu.reshape(-1, N_EXPERTS * D_MODEL, D_FF_EXPERT)
    ed = ed.reshape(-1, N_EXPERTS * D_FF_EXPERT, D_MODEL)
    x_new, eg_bf, eu_bf, ed_bf = pl.pallas_call(
        _swiglu_kernel,
        grid=(NT_ALL,),
        in_specs=[
            pl.BlockSpec((TM, D_MODEL), lambda i: (i, 0)),
            _layer_block(layer, 1, D_MODEL),
            weights(D_MODEL, D_FF), weights(D_MODEL, D_FF), weights(D_FF, D_MODEL),
            pl.BlockSpec((None, up_rows, D_FF_EXPERT), lambda i: (expert_index, slab(i), 0)),
            pl.BlockSpec((None, up_rows, D_FF_EXPERT), lambda i: (expert_index, slab(i), 0)),
            pl.BlockSpec((None, down_rows, D_MODEL), lambda i: (expert_index, slab(i), 0)),
        ],
        out_specs=[
            pl.BlockSpec((TM, D_MODEL), lambda i: (i, 0)),
            pl.BlockSpec((up_rows, D_FF_EXPERT), lambda i: (slab(i), 0)),
            pl.BlockSpec((up_rows, D_FF_EXPERT), lambda i: (slab(i), 0)),
            pl.BlockSpec((down_rows, D_MODEL), lambda i: (slab(i), 0)),
        ],
        out_shape=[
            jax.ShapeDtypeStruct((T_ALL, D_MODEL), F32),
            jax.ShapeDtypeStruct((N_EXPERTS * D_MODEL, D_FF_EXPERT), BF16),
            jax.ShapeDtypeStruct((N_EXPERTS * D_MODEL, D_FF_EXPERT), BF16),
            jax.ShapeDtypeStruct((N_EXPERTS * D_FF_EXPERT, D_MODEL), BF16),
        ],
        compiler_params=pltpu.CompilerParams(
            dimension_semantics=("arbitrary",), vmem_limit_bytes=VMEM_LIMIT),
        name="swiglu",
    )(x, norm_g, wg, wu, wd, eg, eu, ed)
    return (x_new, eg_bf.reshape(1, N_EXPERTS, D_MODEL, D_FF_EXPERT),
            eu_bf.reshape(1, N_EXPERTS, D_MODEL, D_FF_EXPERT),
            ed_bf.reshape(1, N_EXPERTS, D_FF_EXPERT, D_MODEL))


def _moe_sort_kernel(x_ref, ng_ref, wrc_ref, hs_ref, meta_ref, off_ref, cnt_ref):
    n = SORT_TILES_PER_STEP
    logits, routes = {}, {}
    for step in range(n + 2):
        if step < n:
            logits[step] = _sort_logits(x_ref[step * TM:(step + 1) * TM, :], ng_ref, wrc_ref)
        j = step - 1
        if 0 <= j < n:
            h_hi, lg = logits.pop(j)
            meta, off, cnt_pad = _sort_route(lg)
            meta_ref[j * TM:(j + 1) * TM, :] = meta
            off_ref[j] = off.astype(jnp.int32)
            cnt_ref[j] = cnt_pad.astype(jnp.int32)
            routes[j] = (h_hi, meta)
        j = step - 2
        if 0 <= j < n:
            hs_ref[j * SORT_ROWS:(j + 1) * SORT_ROWS, :] = _sort_gather(*routes.pop(j))


def _sort_logits(x, ng_ref, wrc_ref):
    h = _rms(x, ng_ref[...])
    h_hi = h.astype(BF16)
    h_lo = (h - h_hi.astype(F32)).astype(BF16)
    both = jnp.dot(h_hi, wrc_ref[...], preferred_element_type=F32)
    logits = both[:, :LANES] + both[:, LANES:] + jnp.dot(h_lo, wrc_ref[:, :LANES], preferred_element_type=F32)
    return h_hi, logits


def _sort_route(logits):
    lane = lax.broadcasted_iota(jnp.int32, (1, LANES), 1)
    logits = jnp.where(lane < N_EXPERTS, logits, NEG_INF)
    m1 = jnp.max(logits, axis=-1, keepdims=True)
    i1 = jnp.min(jnp.where(logits == m1, lane, LANES), axis=-1, keepdims=True)
    rest = jnp.where(lane == i1, NEG_INF, logits)
    m2 = jnp.max(rest, axis=-1, keepdims=True)
    i2 = jnp.min(jnp.where(rest == m2, lane, LANES), axis=-1, keepdims=True)
    e2 = jnp.exp(m2 - m1)
    den = 1.0 + e2
    sel0 = lane == i1
    sel1 = lane == i2
    routed = jnp.where(jnp.logical_or(sel0, sel1), 1.0, 0.0)
    r_i = lax.broadcasted_iota(jnp.int32, (TM, TM), 0)
    c_i = lax.broadcasted_iota(jnp.int32, (TM, TM), 1)
    lower = jnp.where(r_i > c_i, 1.0, 0.0).astype(BF16)
    rank = jnp.dot(lower, routed.astype(BF16), preferred_element_type=F32)
    cnt = jnp.broadcast_to(jnp.sum(routed, axis=0, keepdims=True), (8, LANES))
    cnt_pad = jnp.floor((cnt + (PIECE - 1)) * (1.0 / PIECE)) * PIECE
    incl = cnt_pad
    for sh in (1, 2, 4):
        incl = incl + jnp.where(lane >= sh, pltpu.roll(incl, sh, 1), 0.0)
    off = incl - cnt_pad
    pos = off[0:1] + rank
    pos0 = jnp.sum(jnp.where(sel0, pos, 0.0), axis=-1, keepdims=True)
    pos1 = jnp.sum(jnp.where(sel1, pos, 0.0), axis=-1, keepdims=True)
    meta = jnp.where(lane == 0, pos0, jnp.where(lane == 1, pos1,
                     jnp.where(lane == 2, 1.0 / den, jnp.where(lane == 3, e2 / den, 0.0))))
    return meta, off, cnt_pad


def _sort_gather(h_hi, meta):
    meta_t = meta.T
    p_row = lax.broadcasted_iota(jnp.int32, (SORT_ROWS, 1), 0).astype(F32)
    hit = jnp.logical_or(p_row == meta_t[0:1], p_row == meta_t[1:2])
    gather = jnp.where(hit, 1.0, 0.0).astype(BF16)
    return jnp.dot(gather, h_hi, preferred_element_type=F32).astype(BF16)


def _moe_sort(x, norm_g, wr_cat, layer, moe_index):
    return pl.pallas_call(
        _moe_sort_kernel,
        grid=(NT_ALL // SORT_TILES_PER_STEP,),
        in_specs=[
            pl.BlockSpec((SORT_TILES_PER_STEP * TM, D_MODEL), lambda i: (i, 0)),
            _layer_block(layer, 1, D_MODEL),
            _layer_block(moe_index, D_MODEL, 2 * LANES),
        ],
        out_specs=[
            pl.BlockSpec((SORT_TILES_PER_STEP * SORT_ROWS, D_MODEL), lambda i: (i, 0)),
            pl.BlockSpec((SORT_TILES_PER_STEP * TM, LANES), lambda i: (i, 0)),
            pl.BlockSpec((SORT_TILES_PER_STEP, 8, LANES), lambda i: (i, 0, 0)),
            pl.BlockSpec((SORT_TILES_PER_STEP, 8, LANES), lambda i: (i, 0, 0)),
        ],
        out_shape=[
            jax.ShapeDtypeStruct((NT_ALL * SORT_ROWS, D_MODEL), BF16),
            jax.ShapeDtypeStruct((T_ALL, LANES), F32),
            jax.ShapeDtypeStruct((NT_ALL, 8, LANES), jnp.int32),
            jax.ShapeDtypeStruct((NT_ALL, 8, LANES), jnp.int32),
        ],
        compiler_params=pltpu.CompilerParams(
            dimension_semantics=("arbitrary",), vmem_limit_bytes=VMEM_LIMIT),
        name="moe_sort",
    )(x, norm_g, wr_cat)


def _moe_experts_kernel(off_ref, cnt_ref, hs_hbm, wg_ref, wu_ref, wrem_ref, wd_ref, ys_hbm,
                        buf, lhs, sem_in, sem_out):
    s = pl.program_id(0)
    e = pl.program_id(1)
    row0 = pl.multiple_of(s * SUPER_ROWS, PIECE)
    is_first = e == 0
    is_last = e == N_EXPERTS - 1

    def copy_in(t):
        r = pl.multiple_of(t * SORT_ROWS, PIECE)
        return pltpu.make_async_copy(hs_hbm.at[pl.ds(row0 + r, SORT_ROWS)], buf.at[pl.ds(r, SORT_ROWS)],
                                     sem_in.at[t])

    def copy_out(t):
        r = pl.multiple_of(t * SORT_ROWS, PIECE)
        return pltpu.make_async_copy(buf.at[pl.ds(r, SORT_ROWS)], ys_hbm.at[pl.ds(row0 + r, SORT_ROWS)],
                                     sem_out.at[t])

    def for_tiles(lo, hi, fn):
        hi = jnp.maximum(lo, hi)

        def body(t, carry):
            fn(t)
            return carry

        lax.fori_loop(lo, hi, body, 0)
        return hi

    @pl.when(is_first)
    def _():
        for t in range(TILES_PER_SUPER):
            copy_in(t).start()

    starts, cums = [], [0]
    for i in range(TILES_PER_SUPER):
        idx = (s * TILES_PER_SUPER + i) * N_EXPERTS + e
        starts.append(i * SORT_ROWS + off_ref[idx])
        cums.append(cums[-1] + cnt_ref[idx] // PIECE)
    n_pieces = cums[-1]

    def locate(p):
        pc = jnp.minimum(p, n_pieces - 1)
        row = starts[0] + PIECE * pc
        tile = jnp.int32(0)
        for i in range(1, TILES_PER_SUPER):
            inside = pc >= cums[i]
            row = jnp.where(inside, starts[i] + PIECE * (pc - cums[i]), row)
            tile = jnp.where(inside, i, tile)
        return row, tile

    def piece_rows(q):
        src, dst, tiles = [], [], []
        for j in range(PIECES_PER_CHUNK):
            p = q * PIECES_PER_CHUNK + j
            row, tile = locate(p)
            src.append(row)
            dst.append(jnp.where(p < n_pieces, row, SUPER_ROWS))
            tiles.append(tile)
        return tuple(src), tuple(dst), tiles[0], tiles[-1]

    def expert_ffn(xs, between=lambda: None):
        gate = jnp.dot(xs, wg_ref[0, :, :FF_MAIN], preferred_element_type=F32)
        up = jnp.dot(xs, wu_ref[0, :, :FF_MAIN], preferred_element_type=F32)
        rem = jnp.dot(xs, wrem_ref[0], preferred_element_type=F32)
        extra = between()
        act = jnp.concatenate([jax.nn.silu(gate) * up,
                               jax.nn.silu(rem[:, :FF_REM]) * rem[:, FF_REM:]], axis=1).astype(BF16)
        return jnp.dot(act, wd_ref[0], preferred_element_type=F32).astype(BF16), extra

    def chunk_body(q, carry):
        (src, dst, _, last_tile), arrived, sent = carry
        arrived = for_tiles(arrived, jnp.where(is_first, last_tile + 1, 0), lambda t: copy_in(t).wait())
        for j in range(PIECES_PER_CHUNK):
            lhs[j * PIECE:(j + 1) * PIECE, :] = buf[pl.ds(pl.multiple_of(src[j], PIECE), PIECE), :]
        y, nxt = expert_ffn(lhs[...], lambda: piece_rows(q + 1))
        for j in range(PIECES_PER_CHUNK):
            buf[pl.ds(pl.multiple_of(dst[j], PIECE), PIECE), :] = y[j * PIECE:(j + 1) * PIECE]
        sent = for_tiles(sent, jnp.where(is_last, nxt[2], 0), lambda t: copy_out(t).start())
        return nxt, arrived, sent

    half = PIECES_PER_CHUNK // 2
    n_chunks = n_pieces // PIECES_PER_CHUNK + jnp.where(n_pieces % PIECES_PER_CHUNK > half, 1, 0)
    _, arrived, sent = lax.fori_loop(0, n_chunks, chunk_body, (piece_rows(0), jnp.int32(0), jnp.int32(0)))
    for_tiles(arrived, jnp.where(is_first, TILES_PER_SUPER, 0), lambda t: copy_in(t).wait())

    first_left = n_chunks * PIECES_PER_CHUNK

    @pl.when(first_left < n_pieces)
    def _():
        dst = []
        for j in range(half):
            row, _ = locate(first_left + j)
            dst.append(jnp.where(first_left + j < n_pieces, row, SUPER_ROWS))
            lhs[j * PIECE:(j + 1) * PIECE, :] = buf[pl.ds(pl.multiple_of(row, PIECE), PIECE), :]
        y, _ = expert_ffn(lhs[:half * PIECE, :])
        for j in range(half):
            buf[pl.ds(pl.multiple_of(dst[j], PIECE), PIECE), :] = y[j * PIECE:(j + 1) * PIECE]

    for_tiles(sent, jnp.where(is_last, TILES_PER_SUPER, 0), lambda t: copy_out(t).start())
    for_tiles(0, jnp.where(is_last, TILES_PER_SUPER, 0), lambda t: copy_out(t).wait())


def _moe_experts(off, cnt, hs, wg, wu, wrem, wd, moe_index):
    expert = lambda *shape: pl.BlockSpec((None, 1) + shape, lambda s, e, *_: (moe_index, e, 0, 0))
    grid_spec = pltpu.PrefetchScalarGridSpec(
        num_scalar_prefetch=2,
        grid=(N_SUPER, N_EXPERTS),
        in_specs=[
            pl.BlockSpec(memory_space=pl.ANY),
            expert(D_MODEL, D_FF_EXPERT), expert(D_MODEL, D_FF_EXPERT), expert(D_MODEL, 2 * FF_REM),
            expert(D_FF_EXPERT, D_MODEL),
        ],
        out_specs=pl.BlockSpec(memory_space=pl.ANY),
        scratch_shapes=[pltpu.VMEM((SUPER_ROWS + PIECE, D_MODEL), BF16),
                        pltpu.VMEM((CHUNK_ROWS, D_MODEL), BF16),
                        pltpu.SemaphoreType.DMA((TILES_PER_SUPER,)),
                        pltpu.SemaphoreType.DMA((TILES_PER_SUPER,))],
    )
    return pl.pallas_call(
        _moe_experts_kernel,
        grid_spec=grid_spec,
        out_shape=jax.ShapeDtypeStruct((NT_ALL * SORT_ROWS, D_MODEL), BF16),
        compiler_params=pltpu.CompilerParams(
            dimension_semantics=("arbitrary", "arbitrary"), vmem_limit_bytes=VMEM_LIMIT),
        name="moe_experts",
    )(off, cnt, hs, wg, wu, wrem, wd)


def _unsorted_residual(x_ref, ys_ref, meta_ref):
    meta = meta_ref[...]
    p_col = lax.broadcasted_iota(jnp.int32, (1, SORT_ROWS), 1).astype(F32)
    scatter = (jnp.where(p_col == meta[:, 0:1], meta[:, 2:3], 0.0)
               + jnp.where(p_col == meta[:, 1:2], meta[:, 3:4], 0.0)).astype(BF16)
    return x_ref[...] + jnp.dot(scatter, ys_ref[...], preferred_element_type=F32)


def _moe_unsort_kernel(x_ref, ys_ref, meta_ref, xo_ref):
    xo_ref[...] = _unsorted_residual(x_ref, ys_ref, meta_ref)


def _moe_unsort_final_kernel(x_ref, ys_ref, meta_ref, g_ref, yp_ref, ysm_ref):
    i = pl.program_id(0)
    y = _rms(_unsorted_residual(x_ref, ys_ref, meta_ref), g_ref[...])

    @pl.when(i < NT_PROMPT)
    def _():
        yp_ref[...] = y

    @pl.when(i == NT_PROMPT)
    def _():
        ysm_ref[...] = y


def _moe_unsort(x, ys, meta, final_g=None):
    in_specs = [
        pl.BlockSpec((TM, D_MODEL), lambda i: (i, 0)),
        pl.BlockSpec((SORT_ROWS, D_MODEL), lambda i: (i, 0)),
        pl.BlockSpec((TM, LANES), lambda i: (i, 0)),
    ]
    params = pltpu.CompilerParams(dimension_semantics=("arbitrary",), vmem_limit_bytes=VMEM_LIMIT)
    if final_g is None:
        return pl.pallas_call(
            _moe_unsort_kernel,
            grid=(NT_ALL,),
            in_specs=in_specs,
            out_specs=pl.BlockSpec((TM, D_MODEL), lambda i: (i, 0)),
            out_shape=jax.ShapeDtypeStruct((T_ALL, D_MODEL), F32),
            compiler_params=params,
            name="moe_unsort",
        )(x, ys, meta)
    return pl.pallas_call(
        _moe_unsort_final_kernel,
        grid=(NT_ALL,),
        in_specs=in_specs + [pl.BlockSpec((1, D_MODEL), lambda i: (0, 0))],
        out_specs=[pl.BlockSpec((TM, D_MODEL), lambda i: (jnp.minimum(i, NT_PROMPT - 1), 0)),
                   pl.BlockSpec((TM, D_MODEL), lambda i: (0, 0))],
        out_shape=[jax.ShapeDtypeStruct((T_PROMPT, D_MODEL), F32),
                   jax.ShapeDtypeStruct((T_SAMPLE, D_MODEL), F32)],
        compiler_params=params,
        name="moe_unsort_final",
    )(x, ys, meta, final_g)


def _moe(x, norm_g, wr_cat, wg, wu, wrem, wd, layer, moe_index, final_g=None):
    hs, meta, off, cnt = _moe_sort(x, norm_g, wr_cat, layer, moe_index)
    off = off[:, 0, :N_EXPERTS].reshape(-1)
    cnt = cnt[:, 0, :N_EXPERTS].reshape(-1)
    ys = _moe_experts(off, cnt, hs, wg, wu, wrem, wd, moe_index)
    return _moe_unsort(x, ys, meta, final_g)


def kernel(x_prompt, x_sample, cache_win_k, cache_win_v, attn_norm_g, w_in, b_in, chunk_ln_g, chunk_ln_b, w_spatial, b_spatial, attn_sinks, mix_norm_a_g, mix_norm_b_g, w_out, ffn_norm_g, ffn_w_gate, ffn_w_up, ffn_w_down, router_w, expert_w_gate, expert_w_up, expert_w_down, final_norm_g):
    w_in_p = jnp.concatenate([w_in[..., :Q0], _g_major(w_in[..., Q0:K0], -1), w_in[..., K0:]], axis=-1).astype(BF16)
    b_in_p = jnp.concatenate([b_in[..., :Q0], _g_major(b_in[..., Q0:K0], -1), b_in[..., K0:]], axis=-1)
    w_out_p = jnp.concatenate([w_out[:, :GMLP_WIDTH], _g_major(w_out[:, GMLP_WIDTH:], 1)], axis=1).astype(BF16)
    g_b_p = _g_major(mix_norm_b_g, 1)
    bs_prompt = jnp.repeat(jnp.transpose(b_spatial, (0, 2, 1)), GMLP_HEAD_DIM, axis=2)
    bs_sample = jnp.tile(bs_prompt[:, :DEC_SEQ], (1, 2, 1))
    t_idx = np.arange(DEC_SEQ)
    ws_small = jnp.stack([w_spatial[:, :, t_idx, np.maximum(t_idx - d, 0)] for d in range(DEC_SEQ)], axis=1)
    ws_sample = jnp.tile(jnp.repeat(jnp.transpose(ws_small, (0, 1, 3, 2)), GMLP_HEAD_DIM, axis=3),
                         (1, 1, 2, 1))
    sink_row = jnp.repeat(attn_sinks, SEQS_PER_TILE * DEC_SEQ, axis=1)[:, None, :]
    bias_prompt = jnp.asarray(_prompt_bias())
    bias_sample = jnp.asarray(_sample_bias())
    wr_pad = jnp.pad(router_w, ((0, 0), (0, 0), (0, LANES - N_EXPERTS)))
    wr_hi = wr_pad.astype(BF16)
    wr_cat = jnp.concatenate([wr_hi, (wr_pad - wr_hi.astype(F32)).astype(BF16)], axis=-1)

    ck = cache_win_k.reshape(DEPTH, DEC_BATCH, CACHE_W, KV_WIDTH)
    cv = cache_win_v.reshape(DEPTH, DEC_BATCH, CACHE_W, KV_WIDTH)

    lw = dict(
        norm_g=attn_norm_g[:, None, :], w_in=w_in_p, b_in=b_in_p[:, None, :],
        ln_g=chunk_ln_g.reshape(DEPTH, 1, GMLP_WIDTH), ln_b=chunk_ln_b.reshape(DEPTH, 1, GMLP_WIDTH),
        w_s=w_spatial, b_s_prompt=bs_prompt, w_s_sample=ws_sample, b_s_sample=bs_sample,
        sink_row=sink_row, bias_prompt=bias_prompt, bias_sample=bias_sample,
        g_a=mix_norm_a_g[:, None, :], g_b=g_b_p[:, None, :], w_out=w_out_p)
    ffn_g = ffn_norm_g[:, None, :]

    x = None
    kp, vp, ksm, vsm, cvs = [], [], [], [], []
    for l in range(DEPTH):
        lw["sinks"] = attn_sinks[l]
        i = l // 2
        cast = (i, (ffn_w_gate, ffn_w_up, ffn_w_down)) if l % 2 == 0 else None
        x_in = x_prompt.reshape(T_PROMPT, D_MODEL) if l == 0 else x
        x_new, k_p, v_p, *ffn_bf = _mixer_prompt(x_in, lw, l, cast)
        if l == 0:
            x_new, k_s, v_s, cv_s = _mixer_sample(x_sample.reshape(T_SAMPLE, D_MODEL), 0, x_new, ck, cv, l, lw)
        else:
            x_new, k_s, v_s, cv_s = _mixer_sample(x, NT_PROMPT, x_new, ck, cv, l, lw)
        kp.append(k_p)
        vp.append(v_p)
        ksm.append(k_s)
        vsm.append(v_s)
        cvs.append(cv_s)
        if l % 2 == 0:
            ffn_wg, ffn_wu, ffn_wd = (w[None] for w in ffn_bf)
            x, ex_wg, ex_wu, ex_wd = _swiglu(x_new, ffn_g, ffn_wg, ffn_wu, ffn_wd, l, 0,
                                             (expert_w_gate, expert_w_up, expert_w_down), i)
            ex_wrem = jnp.concatenate([ex_wg[..., FF_MAIN:], ex_wu[..., FF_MAIN:]], axis=-1)
        else:
            final_g = final_norm_g[None] if l == DEPTH - 1 else None
            x = _moe(x_new, ffn_g, wr_cat[i:i + 1], ex_wg, ex_wu, ex_wrem, ex_wd, l, 0, final_g)
    y_p, y_s = x
    win_p = (DEPTH, BATCH, WINDOW, KV_HEADS, HEAD_DIM)
    win_s = (DEPTH, DEC_BATCH, CACHE_W, KV_HEADS, HEAD_DIM)
    return (y_p.reshape(BATCH, SEQ, D_MODEL), y_s.reshape(DEC_BATCH, DEC_SEQ, D_MODEL),
            jnp.stack(kp).reshape(win_p), jnp.stack(vp).reshape(win_p),
            jnp.stack(ksm).reshape(win_s), jnp.stack(vsm).reshape(win_s),
            jnp.stack(cvs).reshape(DEPTH, DEC_BATCH, DEC_SEQ, GMLP_HEADS, GMLP_HEAD_DIM))
```

```python
import functools

import numpy as np
import jax
import jax.numpy as jnp
from jax import lax
from jax.experimental import pallas as pl
from jax.experimental.pallas import tpu as pltpu

D_MODEL = 1024
BATCH = 4
SEQ = 4096
DEPTH = 4
DEC_BATCH = 128
DEC_SEQ = 4
HEAD_DIM = 64
ATTN_HEADS = 8
KV_HEADS = 2
GQA_GROUP = ATTN_HEADS // KV_HEADS
ATTN_WIDTH = ATTN_HEADS * HEAD_DIM
KV_WIDTH = KV_HEADS * HEAD_DIM
WINDOW = 128
CHUNK = 128
GMLP_HEADS = 4
GMLP_HEAD_DIM = 128
GMLP_WIDTH = GMLP_HEADS * GMLP_HEAD_DIM
IN_WIDTH = 2 * GMLP_WIDTH + ATTN_WIDTH + 2 * KV_WIDTH
D_FF = 2816
N_EXPERTS = 8
D_FF_EXPERT = 1408
CACHE_W = 128
EPS = 1e-6

LANES = 128
TM = 512
T_PROMPT = BATCH * SEQ
T_SAMPLE = DEC_BATCH * DEC_SEQ
T_ALL = T_PROMPT + T_SAMPLE
NT_PROMPT = T_PROMPT // TM
NT_ALL = T_ALL // TM
TILES_PER_SEQ = SEQ // TM
BLOCKS_PER_TILE = TM // WINDOW
SEQ_GROUP = 16
SEQS_PER_TILE = 4
N_SEQ_GROUPS = DEC_BATCH // SEQ_GROUP
KEYS_PAD = CACHE_W + 8
PIECE = 16
SORT_ROWS = 2 * TM + N_EXPERTS * PIECE
TILES_PER_SUPER = 11
N_SUPER = NT_ALL // TILES_PER_SUPER
SUPER_ROWS = TILES_PER_SUPER * SORT_ROWS
SORT_TILES_PER_STEP = 3
CHUNK_ROWS = 256
FF_REM = D_FF_EXPERT % 256
FF_MAIN = D_FF_EXPERT - FF_REM
PIECES_PER_CHUNK = CHUNK_ROWS // PIECE
VMEM_LIMIT = 56 * 1024 * 1024

F32 = jnp.float32
BF16 = jnp.bfloat16
NEG_INF = float("-inf")

Q0 = 2 * GMLP_WIDTH
K0 = Q0 + ATTN_WIDTH
V0 = K0 + KV_WIDTH


def _g_major(a, axis):
    axis = axis % a.ndim
    shape = a.shape
    a = a.reshape(shape[:axis] + (KV_HEADS, GQA_GROUP, HEAD_DIM) + shape[axis + 1:])
    return jnp.swapaxes(a, axis, axis + 1).reshape(shape)


def _slope(kv, g):
    return 2.0 ** (-8.0 * (kv * GQA_GROUP + g + 1) / ATTN_HEADS)


def _prompt_bias():
    c = np.arange(WINDOW)[:, None]
    qi = np.arange(WINDOW)[None, :]
    dist = np.where(c > qi, qi + WINDOW - c, qi - c)
    out = np.zeros((2, KV_HEADS, WINDOW, GQA_GROUP * WINDOW), np.float32)
    for var in range(2):
        ok = ~((var == 1) & (c > qi))
        for kv in range(KV_HEADS):
            for g in range(GQA_GROUP):
                out[var, kv, :, g * WINDOW:(g + 1) * WINDOW] = np.where(ok, -_slope(kv, g) * dist, NEG_INF)
    return out


def _sample_bias():
    out = np.zeros((KEYS_PAD, LANES), np.float32)
    j = np.arange(KEYS_PAD)
    kpos = j - CACHE_W
    for kv in range(KV_HEADS):
        for g in range(GQA_GROUP):
            for s in range(SEQS_PER_TILE):
                for t in range(DEC_SEQ):
                    col = ((kv * GQA_GROUP + g) * SEQS_PER_TILE + s) * DEC_SEQ + t
                    dist = t - kpos
                    ok = (dist >= 0) & (dist < WINDOW) & (j < CACHE_W + DEC_SEQ)
                    out[:, col] = np.where(ok, -_slope(kv, g) * dist, NEG_INF)
    return out


def _rms(x, g):
    return x * lax.rsqrt(jnp.mean(x * x, axis=-1, keepdims=True) + EPS) * g


def _gelu(x):
    return 0.5 * x * (1.0 + lax.erf(x * (2.0 ** -0.5)))


def _gmlp_v(z_v, lng_ref, lnb_ref):
    out = []
    for hh in range(GMLP_HEADS):
        sl = slice(hh * LANES, (hh + 1) * LANES)
        vh = _gelu(z_v[:, sl])
        vc = vh - jnp.mean(vh, axis=-1, keepdims=True)
        y = vc * lax.rsqrt(jnp.mean(vc * vc, axis=-1, keepdims=True) + EPS)
        out.append(y * lng_ref[:, sl] + lnb_ref[:, sl])
    return out


def _dot_nt(a, b):
    return lax.dot_general(a, b, (((1,), (1,)), ((), ())), preferred_element_type=F32)


N_MIXER_INPUTS = 13


def _mixer_prompt_kernel(*refs, n_cast):
    i = pl.program_id(0)
    n_in = N_MIXER_INPUTS + n_cast
    tile_refs = refs[:N_MIXER_INPUTS] + refs[n_in:n_in + 3] + refs[n_in + 3 + n_cast:]

    @pl.when(i < NT_PROMPT)
    def _():
        _mixer_prompt_tile(*tile_refs)

    @pl.when(i == NT_PROMPT)
    def _():
        xo_ref = refs[n_in]
        xo_ref[...] = jnp.zeros_like(xo_ref)

    for src, dst in zip(refs[N_MIXER_INPUTS:n_in], refs[n_in + 3:n_in + 3 + n_cast]):
        dst[...] = src[...].astype(BF16)


def _mixer_prompt_tile(x_ref, ng_ref, win_ref, bin_ref, lng_ref, lnb_ref, ws_ref, bs_ref,
                       sink_ref, bias_ref, ga_ref, gb_ref, wout_ref,
                       xo_ref, ks_ref, vs_ref, kprev_ref, vprev_ref):
    i = pl.program_id(0)

    @pl.when(i == 0)
    def _():
        kprev_ref[...] = jnp.zeros_like(kprev_ref)
        vprev_ref[...] = jnp.zeros_like(vprev_ref)

    x = x_ref[...]
    h = _rms(x, ng_ref[...]).astype(BF16)

    def in_proj(lo, hi):
        return jnp.dot(h, win_ref[:, lo:hi], preferred_element_type=F32) + bin_ref[:, lo:hi]

    z_att = in_proj(Q0, IN_WIDTH)
    q = (z_att[:, :ATTN_WIDTH] * (HEAD_DIM ** -0.5)).astype(BF16)
    k = z_att[:, ATTN_WIDTH:ATTN_WIDTH + KV_WIDTH]
    val = z_att[:, ATTN_WIDTH + KV_WIDTH:]
    k_bf = k.astype(BF16)
    lane = lax.broadcasted_iota(jnp.int32, (1, LANES), 1)
    low = lane < HEAD_DIM
    first = jnp.where(i % TILES_PER_SEQ == 0, 1, 0)
    c_i = lax.broadcasted_iota(jnp.int32, (WINDOW, WINDOW), 0)
    q_i = lax.broadcasted_iota(jnp.int32, (WINDOW, WINDOW), 1)
    from_prev = jnp.concatenate([c_i > q_i] * GQA_GROUP, axis=1)
    g_lane = lax.broadcasted_iota(jnp.int32, (1, GQA_GROUP * WINDOW), 1) // WINDOW
    vt_bf = val.T.astype(BF16)
    vt_all = jnp.concatenate([vprev_ref[...], vt_bf], axis=1)
    k_all = jnp.concatenate([kprev_ref[...], k_bf], axis=0)
    zero = jnp.zeros((), BF16)
    k_heads = [jnp.where(low, k_all, zero), jnp.where(low, zero, k_all)]
    sinks = []
    for kv in range(KV_HEADS):
        sink = jnp.full((1, GQA_GROUP * WINDOW), sink_ref[kv * GQA_GROUP], F32)
        for g in range(1, GQA_GROUP):
            sink = jnp.where(g_lane == g, sink_ref[kv * GQA_GROUP + g], sink)
        sinks.append(sink)

    def attention_block(c):
        rows = slice(c * WINDOW, (c + 1) * WINDOW)
        band = slice(c * WINDOW, (c + 2) * WINDOW)
        o_t = []
        q_rows = jnp.concatenate([q[rows, g * LANES:(g + 1) * LANES] for g in range(GQA_GROUP)], axis=0)
        keys = jnp.concatenate([k_heads[0][band], k_heads[1][band]], axis=0)
        s_both = _dot_nt(keys, q_rows)
        for kv in range(KV_HEADS):
            s2 = s_both[kv * 2 * WINDOW:(kv + 1) * 2 * WINDOW]
            bias = bias_ref[first, kv] if c == 0 else bias_ref[0, kv]
            s = jnp.where(from_prev, s2[:WINDOW], s2[WINDOW:]) + bias
            m = jnp.maximum(jnp.max(s, axis=0, keepdims=True), sinks[kv])
            p = jnp.exp(s - m)
            den = jnp.sum(p, axis=0, keepdims=True) + jnp.exp(sinks[kv] - m)
            p2 = jnp.concatenate([jnp.where(from_prev, p, 0.0), jnp.where(from_prev, 0.0, p)],
                                 axis=0).astype(BF16)
            o = jnp.dot(vt_all[kv * HEAD_DIM:(kv + 1) * HEAD_DIM, band], p2,
                        preferred_element_type=F32)
            o_t.append(o / den)
        bt = jnp.concatenate(o_t, axis=0)
        return jnp.concatenate(
            [bt[:, g * WINDOW:(g + 1) * WINDOW].T for g in range(GQA_GROUP)], axis=1)

    z_u = in_proj(0, GMLP_WIDTH)
    b_rows = [attention_block(0)]
    z_v = in_proj(GMLP_WIDTH, Q0)
    b_rows.append(attention_block(1))
    u = _gelu(z_u)
    b_rows.append(attention_block(2))
    v_heads = _gmlp_v(z_v, lng_ref, lnb_ref)
    row = lax.broadcasted_iota(jnp.int32, (CHUNK, CHUNK), 0)
    col = lax.broadcasted_iota(jnp.int32, (CHUNK, CHUNK), 1)
    a_cols = []
    for hh in range(GMLP_HEADS):
        w = jnp.where(row >= col, ws_ref[hh], 0.0).astype(BF16)
        v_cat = jnp.concatenate(
            [v_heads[hh][c * CHUNK:(c + 1) * CHUNK] for c in range(BLOCKS_PER_TILE)], axis=1)
        mixed = jnp.dot(w, v_cat.astype(BF16), preferred_element_type=F32)
        bias = bs_ref[:, hh * LANES:(hh + 1) * LANES]
        a_cols.append(jnp.concatenate(
            [mixed[:, c * LANES:(c + 1) * LANES] + bias for c in range(BLOCKS_PER_TILE)], axis=0))
    a = u * jnp.concatenate(a_cols, axis=1)
    a_n = _rms(a, ga_ref[...]).astype(BF16)
    b_rows.append(attention_block(3))
    out = x + jnp.dot(a_n, wout_ref[:GMLP_WIDTH, :], preferred_element_type=F32)
    b_n = _rms(jnp.concatenate(b_rows, axis=0), gb_ref[...]).astype(BF16)
    xo_ref[...] = out + jnp.dot(b_n, wout_ref[GMLP_WIDTH:, :], preferred_element_type=F32)

    kprev_ref[...] = k_bf[TM - WINDOW:]
    vprev_ref[...] = vt_bf[:, TM - WINDOW:]
    ks_ref[0] = k[TM - WINDOW:]
    vs_ref[0] = val[TM - WINDOW:]


def _layer_block(layer, *shape, **kwargs):
    return pl.BlockSpec((None,) + shape, lambda *_: (layer,) + (0,) * len(shape), **kwargs)


def _mixer_prompt(x, lw, layer, cast=None):
    const = lambda *shape: pl.BlockSpec(shape, lambda i: (0,) * len(shape))
    per_layer = functools.partial(_layer_block, layer)
    cast_in, cast_specs, cast_out_specs, cast_shapes = [], [], [], []
    if cast is not None:
        ffn_index, weights = cast
        for w in weights:
            rows, cols = w.shape[1:]
            n_slabs = NT_PROMPT if (rows // NT_PROMPT) % PIECE == 0 else NT_PROMPT // 2
            slab = lambda i, n=n_slabs: jnp.minimum(i, n - 1)
            cast_in.append(w)
            cast_specs.append(pl.BlockSpec((None, rows // n_slabs, cols),
                                           lambda i, slab=slab: (ffn_index, slab(i), 0)))
            cast_out_specs.append(pl.BlockSpec((rows // n_slabs, cols), lambda i, slab=slab: (slab(i), 0)))
            cast_shapes.append(jax.ShapeDtypeStruct((rows, cols), BF16))
    return pl.pallas_call(
        functools.partial(_mixer_prompt_kernel, n_cast=len(cast_in)),
        grid=(NT_ALL,),
        in_specs=[
            pl.BlockSpec((TM, D_MODEL), lambda i: (jnp.minimum(i, NT_PROMPT - 1), 0)),
            per_layer(1, D_MODEL), per_layer(D_MODEL, IN_WIDTH), per_layer(1, IN_WIDTH),
            per_layer(1, GMLP_WIDTH), per_layer(1, GMLP_WIDTH),
            per_layer(GMLP_HEADS, CHUNK, CHUNK), per_layer(CHUNK, GMLP_WIDTH),
            pl.BlockSpec(memory_space=pltpu.SMEM),
            const(2, KV_HEADS, WINDOW, GQA_GROUP * WINDOW),
            per_layer(1, GMLP_WIDTH), per_layer(1, ATTN_WIDTH), per_layer(D_MODEL, D_MODEL),
        ] + cast_specs,
        out_specs=[
            pl.BlockSpec((TM, D_MODEL), lambda i: (i, 0)),
            pl.BlockSpec((1, WINDOW, KV_WIDTH), lambda i: (jnp.minimum(i // TILES_PER_SEQ, BATCH - 1), 0, 0)),
            pl.BlockSpec((1, WINDOW, KV_WIDTH), lambda i: (jnp.minimum(i // TILES_PER_SEQ, BATCH - 1), 0, 0)),
        ] + cast_out_specs,
        out_shape=[
            jax.ShapeDtypeStruct((T_ALL, D_MODEL), F32),
            jax.ShapeDtypeStruct((BATCH, WINDOW, KV_WIDTH), F32),
            jax.ShapeDtypeStruct((BATCH, WINDOW, KV_WIDTH), F32),
        ] + cast_shapes,
        scratch_shapes=[pltpu.VMEM((WINDOW, KV_WIDTH), BF16), pltpu.VMEM((WINDOW, KV_WIDTH), BF16)],
        compiler_params=pltpu.CompilerParams(
            dimension_semantics=("arbitrary",), vmem_limit_bytes=VMEM_LIMIT),
        name="mixer_prompt",
    )(x, lw["norm_g"], lw["w_in"], lw["b_in"], lw["ln_g"], lw["ln_b"], lw["w_s"], lw["b_s_prompt"],
      lw["sinks"], lw["bias_prompt"], lw["g_a"], lw["g_b"], lw["w_out"], *cast_in)


def _mixer_sample_kernel(*refs, first_layer):
    n_alias = 1 if first_layer else 3
    _mixer_sample_body(refs[0], *refs[1 + n_alias:], first_layer=first_layer)


def _mixer_sample_body(x_ref, ck_ref, cv_ref, ng_ref, win_ref, bin_ref, lng_ref, lnb_ref,
                       wsd_ref, bsd_ref, sinkrow_ref, bias_ref, ga_ref, gb_ref, wout_ref,
                       xo_ref, ko_ref, vo_ref, cvo_ref,
                       q_s, k_s, v_s, a_s, b_s, kfull, vfull, *, first_layer):
    j = pl.program_id(0)

    @pl.when(j == 0)
    def _():
        h = _rms(x_ref[...], ng_ref[...]).astype(BF16)
        z = jnp.dot(h, win_ref[...], preferred_element_type=F32) + bin_ref[...]
        u = _gelu(z[:, :GMLP_WIDTH])
        v = jnp.concatenate(_gmlp_v(z[:, GMLP_WIDTH:Q0], lng_ref, lnb_ref), axis=1)
        cvo_ref[...] = v
        t_row = lax.broadcasted_iota(jnp.int32, (8, 1), 0) % DEC_SEQ

        def tile_rows(pat):
            return jnp.broadcast_to(pat[None], (TM // 8,) + pat.shape).reshape(TM, pat.shape[-1])

        mixed = tile_rows(bsd_ref[...])
        for d in range(DEC_SEQ):
            vd = v if d == 0 else pltpu.roll(v, d, 0)
            mixed = mixed + tile_rows(jnp.where(t_row >= d, wsd_ref[d], 0.0)) * vd
        a_s[...] = _rms(u * mixed, ga_ref[...]).astype(BF16)
        q_s[...] = z[:, Q0:K0] * (HEAD_DIM ** -0.5)
        k_s[...] = z[:, K0:V0]
        v_s[...] = z[:, V0:]
        kfull[:, CACHE_W + DEC_SEQ:, :] = jnp.zeros((SEQ_GROUP, KEYS_PAD - CACHE_W - DEC_SEQ, KV_WIDTH), F32)
        vfull[:, CACHE_W + DEC_SEQ:, :] = jnp.zeros((SEQ_GROUP, KEYS_PAD - CACHE_W - DEC_SEQ, KV_WIDTH), F32)

    base = pl.multiple_of(j * (SEQ_GROUP * DEC_SEQ), SEQ_GROUP * DEC_SEQ)
    kfull[:, :CACHE_W, :] = ck_ref[0]
    vfull[:, :CACHE_W, :] = cv_ref[0]
    k_new = k_s[pl.ds(base, SEQ_GROUP * DEC_SEQ), :]
    v_new = v_s[pl.ds(base, SEQ_GROUP * DEC_SEQ), :]
    for s in range(SEQ_GROUP):
        kfull[s, CACHE_W:CACHE_W + DEC_SEQ, :] = k_new[s * DEC_SEQ:(s + 1) * DEC_SEQ]
        vfull[s, CACHE_W:CACHE_W + DEC_SEQ, :] = v_new[s * DEC_SEQ:(s + 1) * DEC_SEQ]
    ko_ref[0] = kfull[:, DEC_SEQ:DEC_SEQ + CACHE_W, :]
    vo_ref[0] = vfull[:, DEC_SEQ:DEC_SEQ + CACHE_W, :]
    if first_layer:
        ko_ref[1:] = jnp.zeros((DEPTH - 1,) + ko_ref.shape[1:], F32)
        vo_ref[1:] = jnp.zeros((DEPTH - 1,) + vo_ref.shape[1:], F32)

    lane = lax.broadcasted_iota(jnp.int32, (1, LANES), 1)
    low = lane < HEAD_DIM
    col_seq = (lane // DEC_SEQ) % SEQS_PER_TILE
    row_seq = (lax.broadcasted_iota(jnp.int32, (LANES, 1), 0) // DEC_SEQ) % SEQS_PER_TILE
    sink = sinkrow_ref[...]
    zero = jnp.zeros((), BF16)
    quad_rows = SEQS_PER_TILE * DEC_SEQ
    for m in range(SEQ_GROUP // SEQS_PER_TILE):
        r0 = pl.multiple_of(base + m * quad_rows, quad_rows)
        q16 = q_s[pl.ds(r0, quad_rows), :]
        qt = jnp.concatenate(
            [jnp.where(low if kv == 0 else jnp.logical_not(low), q16[:, g * LANES:(g + 1) * LANES], 0.0)
             for kv in range(KV_HEADS) for g in range(GQA_GROUP)], axis=0).astype(BF16)
        s = bias_ref[...]
        for n in range(SEQS_PER_TILE):
            kn = kfull[SEQS_PER_TILE * m + n].astype(BF16)
            s = s + _dot_nt(kn, jnp.where(row_seq == n, qt, zero))
        mx = jnp.maximum(jnp.max(s, axis=0, keepdims=True), sink)
        p = jnp.exp(s - mx)
        p = (p / (jnp.sum(p, axis=0, keepdims=True) + jnp.exp(sink - mx))).astype(BF16)
        o = jnp.zeros((LANES, KV_WIDTH), F32)
        for n in range(SEQS_PER_TILE):
            vn = vfull[SEQS_PER_TILE * m + n].astype(BF16)
            o = o + lax.dot_general(jnp.where(col_seq == n, p, zero), vn, (((0,), (0,)), ((), ())),
                                    preferred_element_type=F32)
        b_s[pl.ds(r0, quad_rows), :] = jnp.concatenate(
            [jnp.where(low, o[g * quad_rows:(g + 1) * quad_rows],
                       o[(GQA_GROUP + g) * quad_rows:(GQA_GROUP + g + 1) * quad_rows])
             for g in range(GQA_GROUP)], axis=1)

    @pl.when(j == N_SEQ_GROUPS - 1)
    def _():
        b_n = _rms(b_s[...], gb_ref[...]).astype(BF16)
        merged = jnp.concatenate([a_s[...], b_n], axis=1)
        xo_ref[...] = x_ref[...] + jnp.dot(merged, wout_ref[...], preferred_element_type=F32)


def _mixer_sample(x, x_block, x_buf, cache_k, cache_v, layer, lw, states=()):
    const = lambda *shape: pl.BlockSpec(shape, lambda j: (0,) * len(shape))
    per_layer = functools.partial(_layer_block, layer)
    first_layer = layer == 0
    cache_spec = pl.BlockSpec((1, SEQ_GROUP, CACHE_W, KV_WIDTH), lambda j: (layer, j, 0, 0))
    state_layers = DEPTH if first_layer else 1
    state_spec = pl.BlockSpec((state_layers, SEQ_GROUP, CACHE_W, KV_WIDTH), lambda j: (layer, j, 0, 0))
    state_shape = jax.ShapeDtypeStruct((DEPTH, DEC_BATCH, CACHE_W, KV_WIDTH), F32)
    aliased = (x_buf,) + tuple(states)
    return pl.pallas_call(
        functools.partial(_mixer_sample_kernel, first_layer=first_layer),
        grid=(N_SEQ_GROUPS,),
        in_specs=[
            pl.BlockSpec((TM, D_MODEL), lambda j: (x_block, 0)),
        ] + [pl.BlockSpec(memory_space=pl.ANY)] * len(aliased) + [
            cache_spec, cache_spec,
            per_layer(1, D_MODEL), per_layer(D_MODEL, IN_WIDTH), per_layer(1, IN_WIDTH),
            per_layer(1, GMLP_WIDTH), per_layer(1, GMLP_WIDTH),
            per_layer(DEC_SEQ, 8, GMLP_WIDTH), per_layer(8, GMLP_WIDTH),
            per_layer(1, LANES), const(KEYS_PAD, LANES),
            per_layer(1, GMLP_WIDTH), per_layer(1, ATTN_WIDTH), per_layer(D_MODEL, D_MODEL),
        ],
        out_specs=[
            pl.BlockSpec((TM, D_MODEL), lambda j: (NT_PROMPT, 0)),
            state_spec, state_spec,
            const(TM, GMLP_WIDTH),
        ],
        out_shape=[
            jax.ShapeDtypeStruct((T_ALL, D_MODEL), F32),
            state_shape, state_shape,
            jax.ShapeDtypeStruct((TM, GMLP_WIDTH), F32),
        ],
        scratch_shapes=[
            pltpu.VMEM((TM, ATTN_WIDTH), F32), pltpu.VMEM((TM, KV_WIDTH), F32),
            pltpu.VMEM((TM, KV_WIDTH), F32), pltpu.VMEM((TM, GMLP_WIDTH), BF16),
            pltpu.VMEM((TM, ATTN_WIDTH), F32),
            pltpu.VMEM((SEQ_GROUP, KEYS_PAD, KV_WIDTH), F32),
            pltpu.VMEM((SEQ_GROUP, KEYS_PAD, KV_WIDTH), F32),
        ],
        input_output_aliases={1 + n: n for n in range(len(aliased))},
        compiler_params=pltpu.CompilerParams(
            dimension_semantics=("arbitrary",), vmem_limit_bytes=VMEM_LIMIT),
        name="mixer_sample",
    )(x, *aliased, cache_k, cache_v, lw["norm_g"], lw["w_in"], lw["b_in"], lw["ln_g"], lw["ln_b"],
      lw["w_s_sample"], lw["b_s_sample"], lw["sink_row"], lw["bias_sample"],
      lw["g_a"], lw["g_b"], lw["w_out"])


def _swiglu_kernel(x_ref, ng_ref, wg_ref, wu_ref, wd_ref, eg_ref, eu_ref, ed_ref,
                   xo_ref, ego_ref, euo_ref, edo_ref):
    x = x_ref[...]
    h = _rms(x, ng_ref[...]).astype(BF16)
    gate = jnp.dot(h, wg_ref[...], preferred_element_type=F32)
    ego_ref[...] = eg_ref[...].astype(BF16)
    up = jnp.dot(h, wu_ref[...], preferred_element_type=F32)
    euo_ref[...] = eu_ref[...].astype(BF16)
    act = (jax.nn.silu(gate) * up).astype(BF16)
    xo_ref[...] = x + jnp.dot(act, wd_ref[...], preferred_element_type=F32)
    edo_ref[...] = ed_ref[...].astype(BF16)


def _swiglu(x, norm_g, wg, wu, wd, layer, ffn_index, expert_f32, expert_index):
    weights = functools.partial(_layer_block, ffn_index, pipeline_mode=pl.Buffered(1))
    slab_steps = NT_ALL - 1
    up_rows = N_EXPERTS * D_MODEL // slab_steps
    down_rows = N_EXPERTS * D_FF_EXPERT // slab_steps
    slab = lambda i: jnp.minimum(i, slab_steps - 1)
    eg, eu, ed = expert_f32
    eg = eg.reshape(-1, N_EXPERTS * D_MODEL, D_FF_EXPERT)
    eu = eu.reshape(-1, N_EXPERTS * D_MODEL, D_FF_EXPERT)
    ed = ed.reshape(-1, N_EXPERTS * D_FF_EXPERT, D_MODEL)
    x_new, eg_bf, eu_bf, ed_bf = pl.pallas_call(
        _swiglu_kernel,
        grid=(NT_ALL,),
        in_specs=[
            pl.BlockSpec((TM, D_MODEL), lambda i: (i, 0)),
            _layer_block(layer, 1, D_MODEL),
            weights(D_MODEL, D_FF), weights(D_MODEL, D_FF), weights(D_FF, D_MODEL),
            pl.BlockSpec((None, up_rows, D_FF_EXPERT), lambda i: (expert_index, slab(i), 0)),
            pl.BlockSpec((None, up_rows, D_FF_EXPERT), lambda i: (expert_index, slab(i), 0)),
            pl.BlockSpec((None, down_rows, D_MODEL), lambda i: (expert_index, slab(i), 0)),
        ],
        out_specs=[
            pl.BlockSpec((TM, D_MODEL), lambda i: (i, 0)),
            pl.BlockSpec((up_rows, D_FF_EXPERT), lambda i: (slab(i), 0)),
            pl.BlockSpec((up_rows, D_FF_EXPERT), lambda i: (slab(i), 0)),
            pl.BlockSpec((down_rows, D_MODEL), lambda i: (slab(i), 0)),
        ],
        out_shape=[
            jax.ShapeDtypeStruct((T_ALL, D_MODEL), F32),
            jax.ShapeDtypeStruct((N_EXPERTS * D_MODEL, D_FF_EXPERT), BF16),
            jax.ShapeDtypeStruct((N_EXPERTS * D_MODEL, D_FF_EXPERT), BF16),
            jax.ShapeDtypeStruct((N_EXPERTS * D_FF_EXPERT, D_MODEL), BF16),
        ],
        compiler_params=pltpu.CompilerParams(
            dimension_semantics=("arbitrary",), vmem_limit_bytes=VMEM_LIMIT),
        name="swiglu",
    )(x, norm_g, wg, wu, wd, eg, eu, ed)
    return (x_new, eg_bf.reshape(1, N_EXPERTS, D_MODEL, D_FF_EXPERT),
            eu_bf.reshape(1, N_EXPERTS, D_MODEL, D_FF_EXPERT),
            ed_bf.reshape(1, N_EXPERTS, D_FF_EXPERT, D_MODEL))


def _moe_sort_kernel(x_ref, ng_ref, wrc_ref, hs_ref, meta_ref, off_ref, cnt_ref):
    n = SORT_TILES_PER_STEP
    logits, routes = {}, {}
    for step in range(n + 2):
        if step < n:
            logits[step] = _sort_logits(x_ref[step * TM:(step + 1) * TM, :], ng_ref, wrc_ref)
        j = step - 1
        if 0 <= j < n:
            h_hi, lg = logits.pop(j)
            meta, off, cnt_pad = _sort_route(lg)
            meta_ref[j * TM:(j + 1) * TM, :] = meta
            off_ref[j] = off.astype(jnp.int32)
            cnt_ref[j] = cnt_pad.astype(jnp.int32)
            routes[j] = (h_hi, meta)
        j = step - 2
        if 0 <= j < n:
            hs_ref[j * SORT_ROWS:(j + 1) * SORT_ROWS, :] = _sort_gather(*routes.pop(j))


def _sort_logits(x, ng_ref, wrc_ref):
    h = _rms(x, ng_ref[...])
    h_hi = h.astype(BF16)
    h_lo = (h - h_hi.astype(F32)).astype(BF16)
    both = jnp.dot(h_hi, wrc_ref[...], preferred_element_type=F32)
    logits = both[:, :LANES] + both[:, LANES:] + jnp.dot(h_lo, wrc_ref[:, :LANES], preferred_element_type=F32)
    return h_hi, logits


def _sort_route(logits):
    lane = lax.broadcasted_iota(jnp.int32, (1, LANES), 1)
    logits = jnp.where(lane < N_EXPERTS, logits, NEG_INF)
    m1 = jnp.max(logits, axis=-1, keepdims=True)
    i1 = jnp.min(jnp.where(logits == m1, lane, LANES), axis=-1, keepdims=True)
    rest = jnp.where(lane == i1, NEG_INF, logits)
    m2 = jnp.max(rest, axis=-1, keepdims=True)
    i2 = jnp.min(jnp.where(rest == m2, lane, LANES), axis=-1, keepdims=True)
    e2 = jnp.exp(m2 - m1)
    den = 1.0 + e2
    sel0 = lane == i1
    sel1 = lane == i2
    routed = jnp.where(jnp.logical_or(sel0, sel1), 1.0, 0.0)
    r_i = lax.broadcasted_iota(jnp.int32, (TM, TM), 0)
    c_i = lax.broadcasted_iota(jnp.int32, (TM, TM), 1)
    lower = jnp.where(r_i > c_i, 1.0, 0.0).astype(BF16)
    rank = jnp.dot(lower, routed.astype(BF16), preferred_element_type=F32)
    cnt = jnp.broadcast_to(jnp.sum(routed, axis=0, keepdims=True), (8, LANES))
    cnt_pad = jnp.floor((cnt + (PIECE - 1)) * (1.0 / PIECE)) * PIECE
    incl = cnt_pad
    for sh in (1, 2, 4):
        incl = incl + jnp.where(lane >= sh, pltpu.roll(incl, sh, 1), 0.0)
    off = incl - cnt_pad
    pos = off[0:1] + rank
    pos0 = jnp.sum(jnp.where(sel0, pos, 0.0), axis=-1, keepdims=True)
    pos1 = jnp.sum(jnp.where(sel1, pos, 0.0), axis=-1, keepdims=True)
    meta = jnp.where(lane == 0, pos0, jnp.where(lane == 1, pos1,
                     jnp.where(lane == 2, 1.0 / den, jnp.where(lane == 3, e2 / den, 0.0))))
    return meta, off, cnt_pad


def _sort_gather(h_hi, meta):
    meta_t = meta.T
    p_row = lax.broadcasted_iota(jnp.int32, (SORT_ROWS, 1), 0).astype(F32)
    hit = jnp.logical_or(p_row == meta_t[0:1], p_row == meta_t[1:2])
    gather = jnp.where(hit, 1.0, 0.0).astype(BF16)
    return jnp.dot(gather, h_hi, preferred_element_type=F32).astype(BF16)


def _moe_sort(x, norm_g, wr_cat, layer, moe_index):
    return pl.pallas_call(
        _moe_sort_kernel,
        grid=(NT_ALL // SORT_TILES_PER_STEP,),
        in_specs=[
            pl.BlockSpec((SORT_TILES_PER_STEP * TM, D_MODEL), lambda i: (i, 0)),
            _layer_block(layer, 1, D_MODEL),
            _layer_block(moe_index, D_MODEL, 2 * LANES),
        ],
        out_specs=[
            pl.BlockSpec((SORT_TILES_PER_STEP * SORT_ROWS, D_MODEL), lambda i: (i, 0)),
            pl.BlockSpec((SORT_TILES_PER_STEP * TM, LANES), lambda i: (i, 0)),
            pl.BlockSpec((SORT_TILES_PER_STEP, 8, LANES), lambda i: (i, 0, 0)),
            pl.BlockSpec((SORT_TILES_PER_STEP, 8, LANES), lambda i: (i, 0, 0)),
        ],
        out_shape=[
            jax.ShapeDtypeStruct((NT_ALL * SORT_ROWS, D_MODEL), BF16),
            jax.ShapeDtypeStruct((T_ALL, LANES), F32),
            jax.ShapeDtypeStruct((NT_ALL, 8, LANES), jnp.int32),
            jax.ShapeDtypeStruct((NT_ALL, 8, LANES), jnp.int32),
        ],
        compiler_params=pltpu.CompilerParams(
            dimension_semantics=("arbitrary",), vmem_limit_bytes=VMEM_LIMIT),
        name="moe_sort",
    )(x, norm_g, wr_cat)


def _moe_experts_kernel(off_ref, cnt_ref, hs_hbm, wg_ref, wu_ref, wrem_ref, wd_ref, ys_hbm,
                        buf, lhs, sem_in, sem_out):
    s = pl.program_id(0)
    e = pl.program_id(1)
    row0 = pl.multiple_of(s * SUPER_ROWS, PIECE)
    is_first = e == 0
    is_last = e == N_EXPERTS - 1

    def copy_in(t):
        r = pl.multiple_of(t * SORT_ROWS, PIECE)
        return pltpu.make_async_copy(hs_hbm.at[pl.ds(row0 + r, SORT_ROWS)], buf.at[pl.ds(r, SORT_ROWS)],
                                     sem_in.at[t])

    def copy_out(t):
        r = pl.multiple_of(t * SORT_ROWS, PIECE)
        return pltpu.make_async_copy(buf.at[pl.ds(r, SORT_ROWS)], ys_hbm.at[pl.ds(row0 + r, SORT_ROWS)],
                                     sem_out.at[t])

    def for_tiles(lo, hi, fn):
        hi = jnp.maximum(lo, hi)

        def body(t, carry):
            fn(t)
            return carry

        lax.fori_loop(lo, hi, body, 0)
        return hi

    @pl.when(is_first)
    def _():
        for t in range(TILES_PER_SUPER):
            copy_in(t).start()

    starts, cums = [], [0]
    for i in range(TILES_PER_SUPER):
        idx = (s * TILES_PER_SUPER + i) * N_EXPERTS + e
        starts.append(i * SORT_ROWS + off_ref[idx])
        cums.append(cums[-1] + cnt_ref[idx] // PIECE)
    n_pieces = cums[-1]

    def locate(p):
        pc = jnp.minimum(p, n_pieces - 1)
        row = starts[0] + PIECE * pc
        tile = jnp.int32(0)
        for i in range(1, TILES_PER_SUPER):
            inside = pc >= cums[i]
            row = jnp.where(inside, starts[i] + PIECE * (pc - cums[i]), row)
            tile = jnp.where(inside, i, tile)
        return row, tile

    def piece_rows(q):
        src, dst, tiles = [], [], []
        for j in range(PIECES_PER_CHUNK):
            p = q * PIECES_PER_CHUNK + j
            row, tile = locate(p)
            src.append(row)
            dst.append(jnp.where(p < n_pieces, row, SUPER_ROWS))
            tiles.append(tile)
        return tuple(src), tuple(dst), tiles[0], tiles[-1]

    def expert_ffn(xs, between=lambda: None):
        gate = jnp.dot(xs, wg_ref[0, :, :FF_MAIN], preferred_element_type=F32)
        up = jnp.dot(xs, wu_ref[0, :, :FF_MAIN], preferred_element_type=F32)
        rem = jnp.dot(xs, wrem_ref[0], preferred_element_type=F32)
        extra = between()
        act = jnp.concatenate([jax.nn.silu(gate) * up,
                               jax.nn.silu(rem[:, :FF_REM]) * rem[:, FF_REM:]], axis=1).astype(BF16)
        return jnp.dot(act, wd_ref[0], preferred_element_type=F32).astype(BF16), extra

    def chunk_body(q, carry):
        (src, dst, _, last_tile), arrived, sent = carry
        arrived = for_tiles(arrived, jnp.where(is_first, last_tile + 1, 0), lambda t: copy_in(t).wait())
        for j in range(PIECES_PER_CHUNK):
            lhs[j * PIECE:(j + 1) * PIECE, :] = buf[pl.ds(pl.multiple_of(src[j], PIECE), PIECE), :]
        y, nxt = expert_ffn(lhs[...], lambda: piece_rows(q + 1))
        for j in range(PIECES_PER_CHUNK):
            buf[pl.ds(pl.multiple_of(dst[j], PIECE), PIECE), :] = y[j * PIECE:(j + 1) * PIECE]
        sent = for_tiles(sent, jnp.where(is_last, nxt[2], 0), lambda t: copy_out(t).start())
        return nxt, arrived, sent

    half = PIECES_PER_CHUNK // 2
    n_chunks = n_pieces // PIECES_PER_CHUNK + jnp.where(n_pieces % PIECES_PER_CHUNK > half, 1, 0)
    _, arrived, sent = lax.fori_loop(0, n_chunks, chunk_body, (piece_rows(0), jnp.int32(0), jnp.int32(0)))
    for_tiles(arrived, jnp.where(is_first, TILES_PER_SUPER, 0), lambda t: copy_in(t).wait())

    first_left = n_chunks * PIECES_PER_CHUNK

    @pl.when(first_left < n_pieces)
    def _():
        dst = []
        for j in range(half):
            row, _ = locate(first_left + j)
            dst.append(jnp.where(first_left + j < n_pieces, row, SUPER_ROWS))
            lhs[j * PIECE:(j + 1) * PIECE, :] = buf[pl.ds(pl.multiple_of(row, PIECE), PIECE), :]
        y, _ = expert_ffn(lhs[:half * PIECE, :])
        for j in range(half):
            buf[pl.ds(pl.multiple_of(dst[j], PIECE), PIECE), :] = y[j * PIECE:(j + 1) * PIECE]

    for_tiles(sent, jnp.where(is_last, TILES_PER_SUPER, 0), lambda t: copy_out(t).start())
    for_tiles(0, jnp.where(is_last, TILES_PER_SUPER, 0), lambda t: copy_out(t).wait())


def _moe_experts(off, cnt, hs, wg, wu, wrem, wd, moe_index):
    expert = lambda *shape: pl.BlockSpec((None, 1) + shape, lambda s, e, *_: (moe_index, e, 0, 0))
    grid_spec = pltpu.PrefetchScalarGridSpec(
        num_scalar_prefetch=2,
        grid=(N_SUPER, N_EXPERTS),
        in_specs=[
            pl.BlockSpec(memory_space=pl.ANY),
            expert(D_MODEL, D_FF_EXPERT), expert(D_MODEL, D_FF_EXPERT), expert(D_MODEL, 2 * FF_REM),
            expert(D_FF_EXPERT, D_MODEL),
        ],
        out_specs=pl.BlockSpec(memory_space=pl.ANY),
        scratch_shapes=[pltpu.VMEM((SUPER_ROWS + PIECE, D_MODEL), BF16),
                        pltpu.VMEM((CHUNK_ROWS, D_MODEL), BF16),
                        pltpu.SemaphoreType.DMA((TILES_PER_SUPER,)),
                        pltpu.SemaphoreType.DMA((TILES_PER_SUPER,))],
    )
    return pl.pallas_call(
        _moe_experts_kernel,
        grid_spec=grid_spec,
        out_shape=jax.ShapeDtypeStruct((NT_ALL * SORT_ROWS, D_MODEL), BF16),
        compiler_params=pltpu.CompilerParams(
            dimension_semantics=("arbitrary", "arbitrary"), vmem_limit_bytes=VMEM_LIMIT),
        name="moe_experts",
    )(off, cnt, hs, wg, wu, wrem, wd)


def _unsorted_residual(x_ref, ys_ref, meta_ref):
    meta = meta_ref[...]
    p_col = lax.broadcasted_iota(jnp.int32, (1, SORT_ROWS), 1).astype(F32)
    scatter = (jnp.where(p_col == meta[:, 0:1], meta[:, 2:3], 0.0)
               + jnp.where(p_col == meta[:, 1:2], meta[:, 3:4], 0.0)).astype(BF16)
    return x_ref[...] + jnp.dot(scatter, ys_ref[...], preferred_element_type=F32)


def _moe_unsort_kernel(x_ref, ys_ref, meta_ref, xo_ref):
    xo_ref[...] = _unsorted_residual(x_ref, ys_ref, meta_ref)


def _moe_unsort_final_kernel(x_ref, ys_ref, meta_ref, g_ref, yp_ref, ysm_ref):
    i = pl.program_id(0)
    y = _rms(_unsorted_residual(x_ref, ys_ref, meta_ref), g_ref[...])

    @pl.when(i < NT_PROMPT)
    def _():
        yp_ref[...] = y

    @pl.when(i == NT_PROMPT)
    def _():
        ysm_ref[...] = y


def _moe_unsort(x, ys, meta, final_g=None):
    in_specs = [
        pl.BlockSpec((TM, D_MODEL), lambda i: (i, 0)),
        pl.BlockSpec((SORT_ROWS, D_MODEL), lambda i: (i, 0)),
        pl.BlockSpec((TM, LANES), lambda i: (i, 0)),
    ]
    params = pltpu.CompilerParams(dimension_semantics=("arbitrary",), vmem_limit_bytes=VMEM_LIMIT)
    if final_g is None:
        return pl.pallas_call(
            _moe_unsort_kernel,
            grid=(NT_ALL,),
            in_specs=in_specs,
            out_specs=pl.BlockSpec((TM, D_MODEL), lambda i: (i, 0)),
            out_shape=jax.ShapeDtypeStruct((T_ALL, D_MODEL), F32),
            compiler_params=params,
            name="moe_unsort",
        )(x, ys, meta)
    return pl.pallas_call(
        _moe_unsort_final_kernel,
        grid=(NT_ALL,),
        in_specs=in_specs + [pl.BlockSpec((1, D_MODEL), lambda i: (0, 0))],
        out_specs=[pl.BlockSpec((TM, D_MODEL), lambda i: (jnp.minimum(i, NT_PROMPT - 1), 0)),
                   pl.BlockSpec((TM, D_MODEL), lambda i: (0, 0))],
        out_shape=[jax.ShapeDtypeStruct((T_PROMPT, D_MODEL), F32),
                   jax.ShapeDtypeStruct((T_SAMPLE, D_MODEL), F32)],
        compiler_params=params,
        name="moe_unsort_final",
    )(x, ys, meta, final_g)


def _moe(x, norm_g, wr_cat, wg, wu, wrem, wd, layer, moe_index, final_g=None):
    hs, meta, off, cnt = _moe_sort(x, norm_g, wr_cat, layer, moe_index)
    off = off[:, 0, :N_EXPERTS].reshape(-1)
    cnt = cnt[:, 0, :N_EXPERTS].reshape(-1)
    ys = _moe_experts(off, cnt, hs, wg, wu, wrem, wd, moe_index)
    return _moe_unsort(x, ys, meta, final_g)


def kernel(x_prompt, x_sample, cache_win_k, cache_win_v, attn_norm_g, w_in, b_in, chunk_ln_g, chunk_ln_b, w_spatial, b_spatial, attn_sinks, mix_norm_a_g, mix_norm_b_g, w_out, ffn_norm_g, ffn_w_gate, ffn_w_up, ffn_w_down, router_w, expert_w_gate, expert_w_up, expert_w_down, final_norm_g):
    w_in_p = jnp.concatenate([w_in[..., :Q0], _g_major(w_in[..., Q0:K0], -1), w_in[..., K0:]], axis=-1).astype(BF16)
    b_in_p = jnp.concatenate([b_in[..., :Q0], _g_major(b_in[..., Q0:K0], -1), b_in[..., K0:]], axis=-1)
    w_out_p = jnp.concatenate([w_out[:, :GMLP_WIDTH], _g_major(w_out[:, GMLP_WIDTH:], 1)], axis=1).astype(BF16)
    g_b_p = _g_major(mix_norm_b_g, 1)
    bs_prompt = jnp.repeat(jnp.transpose(b_spatial, (0, 2, 1)), GMLP_HEAD_DIM, axis=2)
    bs_sample = jnp.tile(bs_prompt[:, :DEC_SEQ], (1, 2, 1))
    t_idx = np.arange(DEC_SEQ)
    ws_small = jnp.stack([w_spatial[:, :, t_idx, np.maximum(t_idx - d, 0)] for d in range(DEC_SEQ)], axis=1)
    ws_sample = jnp.tile(jnp.repeat(jnp.transpose(ws_small, (0, 1, 3, 2)), GMLP_HEAD_DIM, axis=3),
                         (1, 1, 2, 1))
    sink_row = jnp.repeat(attn_sinks, SEQS_PER_TILE * DEC_SEQ, axis=1)[:, None, :]
    bias_prompt = jnp.asarray(_prompt_bias())
    bias_sample = jnp.asarray(_sample_bias())
    wr_pad = jnp.pad(router_w, ((0, 0), (0, 0), (0, LANES - N_EXPERTS)))
    wr_hi = wr_pad.astype(BF16)
    wr_cat = jnp.concatenate([wr_hi, (wr_pad - wr_hi.astype(F32)).astype(BF16)], axis=-1)

    ck = cache_win_k.reshape(DEPTH, DEC_BATCH, CACHE_W, KV_WIDTH)
    cv = cache_win_v.reshape(DEPTH, DEC_BATCH, CACHE_W, KV_WIDTH)

    lw = dict(
        norm_g=attn_norm_g[:, None, :], w_in=w_in_p, b_in=b_in_p[:, None, :],
        ln_g=chunk_ln_g.reshape(DEPTH, 1, GMLP_WIDTH), ln_b=chunk_ln_b.reshape(DEPTH, 1, GMLP_WIDTH),
        w_s=w_spatial, b_s_prompt=bs_prompt, w_s_sample=ws_sample, b_s_sample=bs_sample,
        sink_row=sink_row, bias_prompt=bias_prompt, bias_sample=bias_sample,
        g_a=mix_norm_a_g[:, None, :], g_b=g_b_p[:, None, :], w_out=w_out_p)
    ffn_g = ffn_norm_g[:, None, :]

    x = None
    kp, vp, cvs = [], [], []
    states = ()
    for l in range(DEPTH):
        lw["sinks"] = attn_sinks[l]
        i = l // 2
        cast = (i, (ffn_w_gate, ffn_w_up, ffn_w_down)) if l % 2 == 0 else None
        x_in = x_prompt.reshape(T_PROMPT, D_MODEL) if l == 0 else x
        x_new, k_p, v_p, *ffn_bf = _mixer_prompt(x_in, lw, l, cast)
        if l == 0:
            x_new, *states, cv_s = _mixer_sample(x_sample.reshape(T_SAMPLE, D_MODEL), 0, x_new, ck, cv, l, lw)
        else:
            x_new, *states, cv_s = _mixer_sample(x, NT_PROMPT, x_new, ck, cv, l, lw, states)
        kp.append(k_p)
        vp.append(v_p)
        cvs.append(cv_s)
        if l % 2 == 0:
            ffn_wg, ffn_wu, ffn_wd = (w[None] for w in ffn_bf)
            x, ex_wg, ex_wu, ex_wd = _swiglu(x_new, ffn_g, ffn_wg, ffn_wu, ffn_wd, l, 0,
                                             (expert_w_gate, expert_w_up, expert_w_down), i)
            ex_wrem = jnp.concatenate([ex_wg[..., FF_MAIN:], ex_wu[..., FF_MAIN:]], axis=-1)
        else:
            final_g = final_norm_g[None] if l == DEPTH - 1 else None
            x = _moe(x_new, ffn_g, wr_cat[i:i + 1], ex_wg, ex_wu, ex_wrem, ex_wd, l, 0, final_g)
    y_p, y_s = x
    win_p = (DEPTH, BATCH, WINDOW, KV_HEADS, HEAD_DIM)
    win_s = (DEPTH, DEC_BATCH, CACHE_W, KV_HEADS, HEAD_DIM)
    return (y_p.reshape(BATCH, SEQ, D_MODEL), y_s.reshape(DEC_BATCH, DEC_SEQ, D_MODEL),
            jnp.stack(kp).reshape(win_p), jnp.stack(vp).reshape(win_p),
            states[0].reshape(win_s), states[1].reshape(win_s),
            jnp.stack(cvs).reshape(DEPTH, DEC_BATCH, DEC_SEQ, GMLP_HEADS, GMLP_HEAD_DIM))
```

```python
import functools

import numpy as np
import jax
import jax.numpy as jnp
from jax import lax
from jax.experimental import pallas as pl
from jax.experimental.pallas import tpu as pltpu

D_MODEL = 1024
BATCH = 4
SEQ = 4096
DEPTH = 4
DEC_BATCH = 128
DEC_SEQ = 4
HEAD_DIM = 64
ATTN_HEADS = 8
KV_HEADS = 2
GQA_GROUP = ATTN_HEADS // KV_HEADS
ATTN_WIDTH = ATTN_HEADS * HEAD_DIM
KV_WIDTH = KV_HEADS * HEAD_DIM
WINDOW = 128
CHUNK = 128
GMLP_HEADS = 4
GMLP_HEAD_DIM = 128
GMLP_WIDTH = GMLP_HEADS * GMLP_HEAD_DIM
IN_WIDTH = 2 * GMLP_WIDTH + ATTN_WIDTH + 2 * KV_WIDTH
D_FF = 2816
N_EXPERTS = 8
D_FF_EXPERT = 1408
CACHE_W = 128
EPS = 1e-6

LANES = 128
SUBLANES = 8
MXU_WIDTH = 256
TM = 512
T_PROMPT = BATCH * SEQ
T_SAMPLE = DEC_BATCH * DEC_SEQ
T_ALL = T_PROMPT + T_SAMPLE
NT_PROMPT = T_PROMPT // TM
NT_ALL = T_ALL // TM
TILES_PER_SEQ = SEQ // TM
BLOCKS_PER_TILE = TM // WINDOW
SEQ_GROUP = 16
SEQS_PER_TILE = 4
N_SEQ_GROUPS = DEC_BATCH // SEQ_GROUP
KEYS_PAD = CACHE_W + SUBLANES
PIECE = 2 * SUBLANES
SORT_ROWS = 2 * TM + N_EXPERTS * PIECE
TILES_PER_SUPER = 11
N_SUPER = NT_ALL // TILES_PER_SUPER
SUPER_ROWS = TILES_PER_SUPER * SORT_ROWS
SORT_TILES_PER_STEP = 3
CHUNK_ROWS = MXU_WIDTH
FF_REM = D_FF_EXPERT % MXU_WIDTH
FF_MAIN = D_FF_EXPERT - FF_REM
PIECES_PER_CHUNK = CHUNK_ROWS // PIECE
VMEM_LIMIT = 56 * 1024 * 1024

F32 = jnp.float32
BF16 = jnp.bfloat16
NEG_INF = float("-inf")

Q0 = 2 * GMLP_WIDTH
K0 = Q0 + ATTN_WIDTH
V0 = K0 + KV_WIDTH


def _g_major(a, axis):
    axis = axis % a.ndim
    shape = a.shape
    a = a.reshape(shape[:axis] + (KV_HEADS, GQA_GROUP, HEAD_DIM) + shape[axis + 1:])
    return jnp.swapaxes(a, axis, axis + 1).reshape(shape)


def _slope(kv, g):
    return 2.0 ** (-8.0 * (kv * GQA_GROUP + g + 1) / ATTN_HEADS)


def _prompt_bias():
    c = np.arange(WINDOW)[:, None]
    qi = np.arange(WINDOW)[None, :]
    dist = np.where(c > qi, qi + WINDOW - c, qi - c)
    out = np.zeros((2, KV_HEADS, WINDOW, GQA_GROUP * WINDOW), np.float32)
    for var in range(2):
        ok = ~((var == 1) & (c > qi))
        for kv in range(KV_HEADS):
            for g in range(GQA_GROUP):
                out[var, kv, :, g * WINDOW:(g + 1) * WINDOW] = np.where(ok, -_slope(kv, g) * dist, NEG_INF)
    return out


def _sample_bias():
    out = np.zeros((KEYS_PAD, LANES), np.float32)
    j = np.arange(KEYS_PAD)
    kpos = j - CACHE_W
    for kv in range(KV_HEADS):
        for g in range(GQA_GROUP):
            for s in range(SEQS_PER_TILE):
                for t in range(DEC_SEQ):
                    col = ((kv * GQA_GROUP + g) * SEQS_PER_TILE + s) * DEC_SEQ + t
                    dist = t - kpos
                    ok = (dist >= 0) & (dist < WINDOW) & (j < CACHE_W + DEC_SEQ)
                    out[:, col] = np.where(ok, -_slope(kv, g) * dist, NEG_INF)
    return out


def _rms(x, g):
    return x * lax.rsqrt(jnp.mean(x * x, axis=-1, keepdims=True) + EPS) * g


def _gelu(x):
    return 0.5 * x * (1.0 + lax.erf(x * (2.0 ** -0.5)))


def _gmlp_v(z_v, lng_ref, lnb_ref):
    out = []
    for hh in range(GMLP_HEADS):
        sl = slice(hh * LANES, (hh + 1) * LANES)
        vh = _gelu(z_v[:, sl])
        vc = vh - jnp.mean(vh, axis=-1, keepdims=True)
        y = vc * lax.rsqrt(jnp.mean(vc * vc, axis=-1, keepdims=True) + EPS)
        out.append(y * lng_ref[:, sl] + lnb_ref[:, sl])
    return out


def _dot_nt(a, b):
    return lax.dot_general(a, b, (((1,), (1,)), ((), ())), preferred_element_type=F32)


N_MIXER_INPUTS = 13


def _mixer_prompt_kernel(*refs, n_cast):
    i = pl.program_id(0)
    n_in = N_MIXER_INPUTS + n_cast
    tile_refs = refs[:N_MIXER_INPUTS] + refs[n_in:n_in + 3] + refs[n_in + 3 + n_cast:]

    @pl.when(i < NT_PROMPT)
    def _():
        _mixer_prompt_tile(*tile_refs)

    @pl.when(i == NT_PROMPT)
    def _():
        xo_ref = refs[n_in]
        xo_ref[...] = jnp.zeros_like(xo_ref)

    for src, dst in zip(refs[N_MIXER_INPUTS:n_in], refs[n_in + 3:n_in + 3 + n_cast]):
        dst[...] = src[...].astype(BF16)


def _mixer_prompt_tile(x_ref, ng_ref, win_ref, bin_ref, lng_ref, lnb_ref, ws_ref, bs_ref,
                       sink_ref, bias_ref, ga_ref, gb_ref, wout_ref,
                       xo_ref, ks_ref, vs_ref, kprev_ref, vprev_ref):
    i = pl.program_id(0)

    @pl.when(i == 0)
    def _():
        kprev_ref[...] = jnp.zeros_like(kprev_ref)
        vprev_ref[...] = jnp.zeros_like(vprev_ref)

    x = x_ref[...]
    h = _rms(x, ng_ref[...]).astype(BF16)

    def in_proj(lo, hi):
        return jnp.dot(h, win_ref[:, lo:hi], preferred_element_type=F32) + bin_ref[:, lo:hi]

    z_att = in_proj(Q0, IN_WIDTH)
    q = (z_att[:, :ATTN_WIDTH] * (HEAD_DIM ** -0.5)).astype(BF16)
    k = z_att[:, ATTN_WIDTH:ATTN_WIDTH + KV_WIDTH]
    val = z_att[:, ATTN_WIDTH + KV_WIDTH:]
    k_bf = k.astype(BF16)
    lane = lax.broadcasted_iota(jnp.int32, (1, LANES), 1)
    low = lane < HEAD_DIM
    first = jnp.where(i % TILES_PER_SEQ == 0, 1, 0)
    c_i = lax.broadcasted_iota(jnp.int32, (WINDOW, WINDOW), 0)
    q_i = lax.broadcasted_iota(jnp.int32, (WINDOW, WINDOW), 1)
    from_prev = jnp.concatenate([c_i > q_i] * GQA_GROUP, axis=1)
    g_lane = lax.broadcasted_iota(jnp.int32, (1, GQA_GROUP * WINDOW), 1) // WINDOW
    vt_bf = val.T.astype(BF16)
    vt_all = jnp.concatenate([vprev_ref[...], vt_bf], axis=1)
    k_all = jnp.concatenate([kprev_ref[...], k_bf], axis=0)
    zero = jnp.zeros((), BF16)
    k_heads = [jnp.where(low, k_all, zero), jnp.where(low, zero, k_all)]
    sinks = []
    for kv in range(KV_HEADS):
        sink = jnp.full((1, GQA_GROUP * WINDOW), sink_ref[kv * GQA_GROUP], F32)
        for g in range(1, GQA_GROUP):
            sink = jnp.where(g_lane == g, sink_ref[kv * GQA_GROUP + g], sink)
        sinks.append(sink)

    def attention_block(c):
        rows = slice(c * WINDOW, (c + 1) * WINDOW)
        band = slice(c * WINDOW, (c + 2) * WINDOW)
        o_t = []
        q_rows = jnp.concatenate([q[rows, g * LANES:(g + 1) * LANES] for g in range(GQA_GROUP)], axis=0)
        keys = jnp.concatenate([k_heads[0][band], k_heads[1][band]], axis=0)
        s_both = _dot_nt(keys, q_rows)
        for kv in range(KV_HEADS):
            s2 = s_both[kv * 2 * WINDOW:(kv + 1) * 2 * WINDOW]
            bias = bias_ref[first, kv] if c == 0 else bias_ref[0, kv]
            s = jnp.where(from_prev, s2[:WINDOW], s2[WINDOW:]) + bias
            m = jnp.maximum(jnp.max(s, axis=0, keepdims=True), sinks[kv])
            p = jnp.exp(s - m)
            den = jnp.sum(p, axis=0, keepdims=True) + jnp.exp(sinks[kv] - m)
            p2 = jnp.concatenate([jnp.where(from_prev, p, 0.0), jnp.where(from_prev, 0.0, p)],
                                 axis=0).astype(BF16)
            o = jnp.dot(vt_all[kv * HEAD_DIM:(kv + 1) * HEAD_DIM, band], p2,
                        preferred_element_type=F32)
            o_t.append(o / den)
        bt = jnp.concatenate(o_t, axis=0)
        return jnp.concatenate(
            [bt[:, g * WINDOW:(g + 1) * WINDOW].T for g in range(GQA_GROUP)], axis=1)

    z_u = in_proj(0, GMLP_WIDTH)
    b_rows = [attention_block(0)]
    z_v = in_proj(GMLP_WIDTH, Q0)
    b_rows.append(attention_block(1))
    u = _gelu(z_u)
    b_rows.append(attention_block(2))
    v_heads = _gmlp_v(z_v, lng_ref, lnb_ref)
    row = lax.broadcasted_iota(jnp.int32, (CHUNK, CHUNK), 0)
    col = lax.broadcasted_iota(jnp.int32, (CHUNK, CHUNK), 1)
    a_cols = []
    for hh in range(GMLP_HEADS):
        w = jnp.where(row >= col, ws_ref[hh], 0.0).astype(BF16)
        v_cat = jnp.concatenate(
            [v_heads[hh][c * CHUNK:(c + 1) * CHUNK] for c in range(BLOCKS_PER_TILE)], axis=1)
        mixed = jnp.dot(w, v_cat.astype(BF16), preferred_element_type=F32)
        bias = bs_ref[:, hh * LANES:(hh + 1) * LANES]
        a_cols.append(jnp.concatenate(
            [mixed[:, c * LANES:(c + 1) * LANES] + bias for c in range(BLOCKS_PER_TILE)], axis=0))
    a = u * jnp.concatenate(a_cols, axis=1)
    a_n = _rms(a, ga_ref[...]).astype(BF16)
    b_rows.append(attention_block(3))
    out = x + jnp.dot(a_n, wout_ref[:GMLP_WIDTH, :], preferred_element_type=F32)
    b_n = _rms(jnp.concatenate(b_rows, axis=0), gb_ref[...]).astype(BF16)
    xo_ref[...] = out + jnp.dot(b_n, wout_ref[GMLP_WIDTH:, :], preferred_element_type=F32)

    kprev_ref[...] = k_bf[TM - WINDOW:]
    vprev_ref[...] = vt_bf[:, TM - WINDOW:]
    ks_ref[0] = k[TM - WINDOW:]
    vs_ref[0] = val[TM - WINDOW:]


def _layer_block(layer, *shape, **kwargs):
    return pl.BlockSpec((None,) + shape, lambda *_: (layer,) + (0,) * len(shape), **kwargs)


def _mixer_prompt(x, lw, layer, cast=None):
    const = lambda *shape: pl.BlockSpec(shape, lambda i: (0,) * len(shape))
    per_layer = functools.partial(_layer_block, layer)
    cast_in, cast_specs, cast_out_specs, cast_shapes = [], [], [], []
    if cast is not None:
        ffn_index, weights = cast
        for w in weights:
            rows, cols = w.shape[1:]
            n_slabs = NT_PROMPT if (rows // NT_PROMPT) % PIECE == 0 else NT_PROMPT // 2
            slab = lambda i, n=n_slabs: jnp.minimum(i, n - 1)
            cast_in.append(w)
            cast_specs.append(pl.BlockSpec((None, rows // n_slabs, cols),
                                           lambda i, slab=slab: (ffn_index, slab(i), 0)))
            cast_out_specs.append(pl.BlockSpec((rows // n_slabs, cols), lambda i, slab=slab: (slab(i), 0)))
            cast_shapes.append(jax.ShapeDtypeStruct((rows, cols), BF16))
    return pl.pallas_call(
        functools.partial(_mixer_prompt_kernel, n_cast=len(cast_in)),
        grid=(NT_ALL,),
        in_specs=[
            pl.BlockSpec((TM, D_MODEL), lambda i: (jnp.minimum(i, NT_PROMPT - 1), 0)),
            per_layer(1, D_MODEL), per_layer(D_MODEL, IN_WIDTH), per_layer(1, IN_WIDTH),
            per_layer(1, GMLP_WIDTH), per_layer(1, GMLP_WIDTH),
            per_layer(GMLP_HEADS, CHUNK, CHUNK), per_layer(CHUNK, GMLP_WIDTH),
            pl.BlockSpec(memory_space=pltpu.SMEM),
            const(2, KV_HEADS, WINDOW, GQA_GROUP * WINDOW),
            per_layer(1, GMLP_WIDTH), per_layer(1, ATTN_WIDTH), per_layer(D_MODEL, D_MODEL),
        ] + cast_specs,
        out_specs=[
            pl.BlockSpec((TM, D_MODEL), lambda i: (i, 0)),
            pl.BlockSpec((1, WINDOW, KV_WIDTH), lambda i: (jnp.minimum(i // TILES_PER_SEQ, BATCH - 1), 0, 0)),
            pl.BlockSpec((1, WINDOW, KV_WIDTH), lambda i: (jnp.minimum(i // TILES_PER_SEQ, BATCH - 1), 0, 0)),
        ] + cast_out_specs,
        out_shape=[
            jax.ShapeDtypeStruct((T_ALL, D_MODEL), F32),
            jax.ShapeDtypeStruct((BATCH, WINDOW, KV_WIDTH), F32),
            jax.ShapeDtypeStruct((BATCH, WINDOW, KV_WIDTH), F32),
        ] + cast_shapes,
        scratch_shapes=[pltpu.VMEM((WINDOW, KV_WIDTH), BF16), pltpu.VMEM((WINDOW, KV_WIDTH), BF16)],
        compiler_params=pltpu.CompilerParams(
            dimension_semantics=("arbitrary",), vmem_limit_bytes=VMEM_LIMIT),
        name="mixer_prompt",
    )(x, lw["norm_g"], lw["w_in"], lw["b_in"], lw["ln_g"], lw["ln_b"], lw["w_s"], lw["b_s_prompt"],
      lw["sinks"], lw["bias_prompt"], lw["g_a"], lw["g_b"], lw["w_out"], *cast_in)


def _mixer_sample_kernel(*refs, first_layer):
    n_alias = 1 if first_layer else 3
    _mixer_sample_body(refs[0], *refs[1 + n_alias:], first_layer=first_layer)


def _mixer_sample_body(x_ref, ck_ref, cv_ref, ng_ref, win_ref, bin_ref, lng_ref, lnb_ref,
                       wsd_ref, bsd_ref, sinkrow_ref, bias_ref, ga_ref, gb_ref, wout_ref,
                       xo_ref, ko_ref, vo_ref, cvo_ref,
                       q_s, k_s, v_s, a_s, b_s, kfull, vfull, *, first_layer):
    j = pl.program_id(0)

    @pl.when(j == 0)
    def _():
        h = _rms(x_ref[...], ng_ref[...]).astype(BF16)
        z = jnp.dot(h, win_ref[...], preferred_element_type=F32) + bin_ref[...]
        u = _gelu(z[:, :GMLP_WIDTH])
        v = jnp.concatenate(_gmlp_v(z[:, GMLP_WIDTH:Q0], lng_ref, lnb_ref), axis=1)
        cvo_ref[...] = v
        t_row = lax.broadcasted_iota(jnp.int32, (SUBLANES, 1), 0) % DEC_SEQ

        def tile_rows(pat):
            return jnp.broadcast_to(pat[None], (TM // SUBLANES,) + pat.shape).reshape(TM, pat.shape[-1])

        mixed = tile_rows(bsd_ref[...])
        for d in range(DEC_SEQ):
            vd = v if d == 0 else pltpu.roll(v, d, 0)
            mixed = mixed + tile_rows(jnp.where(t_row >= d, wsd_ref[d], 0.0)) * vd
        a_s[...] = _rms(u * mixed, ga_ref[...]).astype(BF16)
        q_s[...] = z[:, Q0:K0] * (HEAD_DIM ** -0.5)
        k_s[...] = z[:, K0:V0]
        v_s[...] = z[:, V0:]
        kfull[:, CACHE_W + DEC_SEQ:, :] = jnp.zeros((SEQ_GROUP, KEYS_PAD - CACHE_W - DEC_SEQ, KV_WIDTH), F32)
        vfull[:, CACHE_W + DEC_SEQ:, :] = jnp.zeros((SEQ_GROUP, KEYS_PAD - CACHE_W - DEC_SEQ, KV_WIDTH), F32)

    base = pl.multiple_of(j * (SEQ_GROUP * DEC_SEQ), SEQ_GROUP * DEC_SEQ)
    kfull[:, :CACHE_W, :] = ck_ref[0]
    vfull[:, :CACHE_W, :] = cv_ref[0]
    k_new = k_s[pl.ds(base, SEQ_GROUP * DEC_SEQ), :]
    v_new = v_s[pl.ds(base, SEQ_GROUP * DEC_SEQ), :]
    for s in range(SEQ_GROUP):
        kfull[s, CACHE_W:CACHE_W + DEC_SEQ, :] = k_new[s * DEC_SEQ:(s + 1) * DEC_SEQ]
        vfull[s, CACHE_W:CACHE_W + DEC_SEQ, :] = v_new[s * DEC_SEQ:(s + 1) * DEC_SEQ]
    ko_ref[0] = kfull[:, DEC_SEQ:DEC_SEQ + CACHE_W, :]
    vo_ref[0] = vfull[:, DEC_SEQ:DEC_SEQ + CACHE_W, :]
    if first_layer:
        ko_ref[1:] = jnp.zeros((DEPTH - 1,) + ko_ref.shape[1:], F32)
        vo_ref[1:] = jnp.zeros((DEPTH - 1,) + vo_ref.shape[1:], F32)

    lane = lax.broadcasted_iota(jnp.int32, (1, LANES), 1)
    low = lane < HEAD_DIM
    col_seq = (lane // DEC_SEQ) % SEQS_PER_TILE
    row_seq = (lax.broadcasted_iota(jnp.int32, (LANES, 1), 0) // DEC_SEQ) % SEQS_PER_TILE
    sink = sinkrow_ref[...]
    zero = jnp.zeros((), BF16)
    quad_rows = SEQS_PER_TILE * DEC_SEQ
    for m in range(SEQ_GROUP // SEQS_PER_TILE):
        r0 = pl.multiple_of(base + m * quad_rows, quad_rows)
        q16 = q_s[pl.ds(r0, quad_rows), :]
        qt = jnp.concatenate(
            [jnp.where(low if kv == 0 else jnp.logical_not(low), q16[:, g * LANES:(g + 1) * LANES], 0.0)
             for kv in range(KV_HEADS) for g in range(GQA_GROUP)], axis=0).astype(BF16)
        s = bias_ref[...]
        for n in range(SEQS_PER_TILE):
            kn = kfull[SEQS_PER_TILE * m + n].astype(BF16)
            s = s + _dot_nt(kn, jnp.where(row_seq == n, qt, zero))
        mx = jnp.maximum(jnp.max(s, axis=0, keepdims=True), sink)
        p = jnp.exp(s - mx)
        p = (p / (jnp.sum(p, axis=0, keepdims=True) + jnp.exp(sink - mx))).astype(BF16)
        o = jnp.zeros((LANES, KV_WIDTH), F32)
        for n in range(SEQS_PER_TILE):
            vn = vfull[SEQS_PER_TILE * m + n].astype(BF16)
            o = o + lax.dot_general(jnp.where(col_seq == n, p, zero), vn, (((0,), (0,)), ((), ())),
                                    preferred_element_type=F32)
        b_s[pl.ds(r0, quad_rows), :] = jnp.concatenate(
            [jnp.where(low, o[g * quad_rows:(g + 1) * quad_rows],
                       o[(GQA_GROUP + g) * quad_rows:(GQA_GROUP + g + 1) * quad_rows])
             for g in range(GQA_GROUP)], axis=1)

    @pl.when(j == N_SEQ_GROUPS - 1)
    def _():
        b_n = _rms(b_s[...], gb_ref[...]).astype(BF16)
        merged = jnp.concatenate([a_s[...], b_n], axis=1)
        xo_ref[...] = x_ref[...] + jnp.dot(merged, wout_ref[...], preferred_element_type=F32)


def _mixer_sample(x, x_block, x_buf, cache_k, cache_v, layer, lw, states=()):
    const = lambda *shape: pl.BlockSpec(shape, lambda j: (0,) * len(shape))
    per_layer = functools.partial(_layer_block, layer)
    first_layer = layer == 0
    cache_spec = pl.BlockSpec((1, SEQ_GROUP, CACHE_W, KV_WIDTH), lambda j: (layer, j, 0, 0))
    state_layers = DEPTH if first_layer else 1
    state_spec = pl.BlockSpec((state_layers, SEQ_GROUP, CACHE_W, KV_WIDTH), lambda j: (layer, j, 0, 0))
    state_shape = jax.ShapeDtypeStruct((DEPTH, DEC_BATCH, CACHE_W, KV_WIDTH), F32)
    aliased = (x_buf,) + tuple(states)
    return pl.pallas_call(
        functools.partial(_mixer_sample_kernel, first_layer=first_layer),
        grid=(N_SEQ_GROUPS,),
        in_specs=[
            pl.BlockSpec((TM, D_MODEL), lambda j: (x_block, 0)),
        ] + [pl.BlockSpec(memory_space=pl.ANY)] * len(aliased) + [
            cache_spec, cache_spec,
            per_layer(1, D_MODEL), per_layer(D_MODEL, IN_WIDTH), per_layer(1, IN_WIDTH),
            per_layer(1, GMLP_WIDTH), per_layer(1, GMLP_WIDTH),
            per_layer(DEC_SEQ, SUBLANES, GMLP_WIDTH), per_layer(SUBLANES, GMLP_WIDTH),
            per_layer(1, LANES), const(KEYS_PAD, LANES),
            per_layer(1, GMLP_WIDTH), per_layer(1, ATTN_WIDTH), per_layer(D_MODEL, D_MODEL),
        ],
        out_specs=[
            pl.BlockSpec((TM, D_MODEL), lambda j: (NT_PROMPT, 0)),
            state_spec, state_spec,
            const(TM, GMLP_WIDTH),
        ],
        out_shape=[
            jax.ShapeDtypeStruct((T_ALL, D_MODEL), F32),
            state_shape, state_shape,
            jax.ShapeDtypeStruct((TM, GMLP_WIDTH), F32),
        ],
        scratch_shapes=[
            pltpu.VMEM((TM, ATTN_WIDTH), F32), pltpu.VMEM((TM, KV_WIDTH), F32),
            pltpu.VMEM((TM, KV_WIDTH), F32), pltpu.VMEM((TM, GMLP_WIDTH), BF16),
            pltpu.VMEM((TM, ATTN_WIDTH), F32),
            pltpu.VMEM((SEQ_GROUP, KEYS_PAD, KV_WIDTH), F32),
            pltpu.VMEM((SEQ_GROUP, KEYS_PAD, KV_WIDTH), F32),
        ],
        input_output_aliases={1 + n: n for n in range(len(aliased))},
        compiler_params=pltpu.CompilerParams(
            dimension_semantics=("arbitrary",), vmem_limit_bytes=VMEM_LIMIT),
        name="mixer_sample",
    )(x, *aliased, cache_k, cache_v, lw["norm_g"], lw["w_in"], lw["b_in"], lw["ln_g"], lw["ln_b"],
      lw["w_s_sample"], lw["b_s_sample"], lw["sink_row"], lw["bias_sample"],
      lw["g_a"], lw["g_b"], lw["w_out"])


def _swiglu_kernel(x_ref, ng_ref, wg_ref, wu_ref, wd_ref, eg_ref, eu_ref, ed_ref,
                   xo_ref, ego_ref, euo_ref, edo_ref):
    x = x_ref[...]
    h = _rms(x, ng_ref[...]).astype(BF16)
    gate = jnp.dot(h, wg_ref[...], preferred_element_type=F32)
    ego_ref[...] = eg_ref[...].astype(BF16)
    up = jnp.dot(h, wu_ref[...], preferred_element_type=F32)
    euo_ref[...] = eu_ref[...].astype(BF16)
    act = (jax.nn.silu(gate) * up).astype(BF16)
    xo_ref[...] = x + jnp.dot(act, wd_ref[...], preferred_element_type=F32)
    edo_ref[...] = ed_ref[...].astype(BF16)


def _swiglu(x, norm_g, wg, wu, wd, layer, ffn_index, expert_f32, expert_index):
    weights = functools.partial(_layer_block, ffn_index, pipeline_mode=pl.Buffered(1))
    slab_steps = NT_ALL - 1
    up_rows = N_EXPERTS * D_MODEL // slab_steps
    down_rows = N_EXPERTS * D_FF_EXPERT // slab_steps
    slab = lambda i: jnp.minimum(i, slab_steps - 1)
    eg, eu, ed = expert_f32
    eg = eg.reshape(-1, N_EXPERTS * D_MODEL, D_FF_EXPERT)
    eu = eu.reshape(-1, N_EXPERTS * D_MODEL, D_FF_EXPERT)
    ed = ed.reshape(-1, N_EXPERTS * D_FF_EXPERT, D_MODEL)
    x_new, eg_bf, eu_bf, ed_bf = pl.pallas_call(
        _swiglu_kernel,
        grid=(NT_ALL,),
        in_specs=[
            pl.BlockSpec((TM, D_MODEL), lambda i: (i, 0)),
            _layer_block(layer, 1, D_MODEL),
            weights(D_MODEL, D_FF), weights(D_MODEL, D_FF), weights(D_FF, D_MODEL),
            pl.BlockSpec((None, up_rows, D_FF_EXPERT), lambda i: (expert_index, slab(i), 0)),
            pl.BlockSpec((None, up_rows, D_FF_EXPERT), lambda i: (expert_index, slab(i), 0)),
            pl.BlockSpec((None, down_rows, D_MODEL), lambda i: (expert_index, slab(i), 0)),
        ],
        out_specs=[
            pl.BlockSpec((TM, D_MODEL), lambda i: (i, 0)),
            pl.BlockSpec((up_rows, D_FF_EXPERT), lambda i: (slab(i), 0)),
            pl.BlockSpec((up_rows, D_FF_EXPERT), lambda i: (slab(i), 0)),
            pl.BlockSpec((down_rows, D_MODEL), lambda i: (slab(i), 0)),
        ],
        out_shape=[
            jax.ShapeDtypeStruct((T_ALL, D_MODEL), F32),
            jax.ShapeDtypeStruct((N_EXPERTS * D_MODEL, D_FF_EXPERT), BF16),
            jax.ShapeDtypeStruct((N_EXPERTS * D_MODEL, D_FF_EXPERT), BF16),
            jax.ShapeDtypeStruct((N_EXPERTS * D_FF_EXPERT, D_MODEL), BF16),
        ],
        compiler_params=pltpu.CompilerParams(
            dimension_semantics=("arbitrary",), vmem_limit_bytes=VMEM_LIMIT),
        name="swiglu",
    )(x, norm_g, wg, wu, wd, eg, eu, ed)
    return (x_new, eg_bf.reshape(1, N_EXPERTS, D_MODEL, D_FF_EXPERT),
            eu_bf.reshape(1, N_EXPERTS, D_MODEL, D_FF_EXPERT),
            ed_bf.reshape(1, N_EXPERTS, D_FF_EXPERT, D_MODEL))


def _moe_sort_kernel(x_ref, ng_ref, wrc_ref, hs_ref, meta_ref, off_ref, cnt_ref):
    n = SORT_TILES_PER_STEP
    logits, routes = {}, {}
    for step in range(n + 2):
        if step < n:
            logits[step] = _sort_logits(x_ref[step * TM:(step + 1) * TM, :], ng_ref, wrc_ref)
        j = step - 1
        if 0 <= j < n:
            h_hi, lg = logits.pop(j)
            meta, off, cnt_pad = _sort_route(lg)
            meta_ref[j * TM:(j + 1) * TM, :] = meta
            off_ref[j] = off.astype(jnp.int32)
            cnt_ref[j] = cnt_pad.astype(jnp.int32)
            routes[j] = (h_hi, meta)
        j = step - 2
        if 0 <= j < n:
            hs_ref[j * SORT_ROWS:(j + 1) * SORT_ROWS, :] = _sort_gather(*routes.pop(j))


def _sort_logits(x, ng_ref, wrc_ref):
    h = _rms(x, ng_ref[...])
    h_hi = h.astype(BF16)
    h_lo = (h - h_hi.astype(F32)).astype(BF16)
    both = jnp.dot(h_hi, wrc_ref[...], preferred_element_type=F32)
    logits = both[:, :LANES] + both[:, LANES:] + jnp.dot(h_lo, wrc_ref[:, :LANES], preferred_element_type=F32)
    return h_hi, logits


def _sort_route(logits):
    lane = lax.broadcasted_iota(jnp.int32, (1, LANES), 1)
    logits = jnp.where(lane < N_EXPERTS, logits, NEG_INF)
    m1 = jnp.max(logits, axis=-1, keepdims=True)
    i1 = jnp.min(jnp.where(logits == m1, lane, LANES), axis=-1, keepdims=True)
    rest = jnp.where(lane == i1, NEG_INF, logits)
    m2 = jnp.max(rest, axis=-1, keepdims=True)
    i2 = jnp.min(jnp.where(rest == m2, lane, LANES), axis=-1, keepdims=True)
    e2 = jnp.exp(m2 - m1)
    den = 1.0 + e2
    sel0 = lane == i1
    sel1 = lane == i2
    routed = jnp.where(jnp.logical_or(sel0, sel1), 1.0, 0.0)
    r_i = lax.broadcasted_iota(jnp.int32, (TM, TM), 0)
    c_i = lax.broadcasted_iota(jnp.int32, (TM, TM), 1)
    lower = jnp.where(r_i > c_i, 1.0, 0.0).astype(BF16)
    rank = jnp.dot(lower, routed.astype(BF16), preferred_element_type=F32)
    cnt = jnp.broadcast_to(jnp.sum(routed, axis=0, keepdims=True), (SUBLANES, LANES))
    cnt_pad = jnp.floor((cnt + (PIECE - 1)) * (1.0 / PIECE)) * PIECE
    incl = cnt_pad
    for sh in (1, 2, 4):
        incl = incl + jnp.where(lane >= sh, pltpu.roll(incl, sh, 1), 0.0)
    off = incl - cnt_pad
    pos = off[0:1] + rank
    pos0 = jnp.sum(jnp.where(sel0, pos, 0.0), axis=-1, keepdims=True)
    pos1 = jnp.sum(jnp.where(sel1, pos, 0.0), axis=-1, keepdims=True)
    meta = jnp.where(lane == 0, pos0, jnp.where(lane == 1, pos1,
                     jnp.where(lane == 2, 1.0 / den, jnp.where(lane == 3, e2 / den, 0.0))))
    return meta, off, cnt_pad


def _sort_gather(h_hi, meta):
    meta_t = meta.T
    p_row = lax.broadcasted_iota(jnp.int32, (SORT_ROWS, 1), 0).astype(F32)
    hit = jnp.logical_or(p_row == meta_t[0:1], p_row == meta_t[1:2])
    gather = jnp.where(hit, 1.0, 0.0).astype(BF16)
    return jnp.dot(gather, h_hi, preferred_element_type=F32).astype(BF16)


def _moe_sort(x, norm_g, wr_cat, layer, moe_index):
    return pl.pallas_call(
        _moe_sort_kernel,
        grid=(NT_ALL // SORT_TILES_PER_STEP,),
        in_specs=[
            pl.BlockSpec((SORT_TILES_PER_STEP * TM, D_MODEL), lambda i: (i, 0)),
            _layer_block(layer, 1, D_MODEL),
            _layer_block(moe_index, D_MODEL, 2 * LANES),
        ],
        out_specs=[
            pl.BlockSpec((SORT_TILES_PER_STEP * SORT_ROWS, D_MODEL), lambda i: (i, 0)),
            pl.BlockSpec((SORT_TILES_PER_STEP * TM, LANES), lambda i: (i, 0)),
            pl.BlockSpec((SORT_TILES_PER_STEP, SUBLANES, LANES), lambda i: (i, 0, 0)),
            pl.BlockSpec((SORT_TILES_PER_STEP, SUBLANES, LANES), lambda i: (i, 0, 0)),
        ],
        out_shape=[
            jax.ShapeDtypeStruct((NT_ALL * SORT_ROWS, D_MODEL), BF16),
            jax.ShapeDtypeStruct((T_ALL, LANES), F32),
            jax.ShapeDtypeStruct((NT_ALL, SUBLANES, LANES), jnp.int32),
            jax.ShapeDtypeStruct((NT_ALL, SUBLANES, LANES), jnp.int32),
        ],
        compiler_params=pltpu.CompilerParams(
            dimension_semantics=("arbitrary",), vmem_limit_bytes=VMEM_LIMIT),
        name="moe_sort",
    )(x, norm_g, wr_cat)


def _moe_experts_kernel(off_ref, cnt_ref, hs_hbm, wg_ref, wu_ref, wrem_ref, wd_ref, ys_hbm,
                        buf, lhs, sem_in, sem_out):
    s = pl.program_id(0)
    e = pl.program_id(1)
    row0 = pl.multiple_of(s * SUPER_ROWS, PIECE)
    is_first = e == 0
    is_last = e == N_EXPERTS - 1

    def copy_in(t):
        r = pl.multiple_of(t * SORT_ROWS, PIECE)
        return pltpu.make_async_copy(hs_hbm.at[pl.ds(row0 + r, SORT_ROWS)], buf.at[pl.ds(r, SORT_ROWS)],
                                     sem_in.at[t])

    def copy_out(t):
        r = pl.multiple_of(t * SORT_ROWS, PIECE)
        return pltpu.make_async_copy(buf.at[pl.ds(r, SORT_ROWS)], ys_hbm.at[pl.ds(row0 + r, SORT_ROWS)],
                                     sem_out.at[t])

    def for_tiles(lo, hi, fn):
        hi = jnp.maximum(lo, hi)

        def body(t, carry):
            fn(t)
            return carry

        lax.fori_loop(lo, hi, body, 0)
        return hi

    @pl.when(is_first)
    def _():
        for t in range(TILES_PER_SUPER):
            copy_in(t).start()

    starts, cums = [], [0]
    for i in range(TILES_PER_SUPER):
        idx = (s * TILES_PER_SUPER + i) * N_EXPERTS + e
        starts.append(i * SORT_ROWS + off_ref[idx])
        cums.append(cums[-1] + cnt_ref[idx] // PIECE)
    n_pieces = cums[-1]

    def locate(p):
        pc = jnp.minimum(p, n_pieces - 1)
        row = starts[0] + PIECE * pc
        tile = jnp.int32(0)
        for i in range(1, TILES_PER_SUPER):
            inside = pc >= cums[i]
            row = jnp.where(inside, starts[i] + PIECE * (pc - cums[i]), row)
            tile = jnp.where(inside, i, tile)
        return row, tile

    def piece_rows(q):
        src, dst, tiles = [], [], []
        for j in range(PIECES_PER_CHUNK):
            p = q * PIECES_PER_CHUNK + j
            row, tile = locate(p)
            src.append(row)
            dst.append(jnp.where(p < n_pieces, row, SUPER_ROWS))
            tiles.append(tile)
        return tuple(src), tuple(dst), tiles[0], tiles[-1]

    def expert_ffn(xs, between=lambda: None):
        gate = jnp.dot(xs, wg_ref[0, :, :FF_MAIN], preferred_element_type=F32)
        up = jnp.dot(xs, wu_ref[0, :, :FF_MAIN], preferred_element_type=F32)
        rem = jnp.dot(xs, wrem_ref[0], preferred_element_type=F32)
        extra = between()
        act = jnp.concatenate([jax.nn.silu(gate) * up,
                               jax.nn.silu(rem[:, :FF_REM]) * rem[:, FF_REM:]], axis=1).astype(BF16)
        return jnp.dot(act, wd_ref[0], preferred_element_type=F32).astype(BF16), extra

    def chunk_body(q, carry):
        (src, dst, _, last_tile), arrived, sent = carry
        arrived = for_tiles(arrived, jnp.where(is_first, last_tile + 1, 0), lambda t: copy_in(t).wait())
        for j in range(PIECES_PER_CHUNK):
            lhs[j * PIECE:(j + 1) * PIECE, :] = buf[pl.ds(pl.multiple_of(src[j], PIECE), PIECE), :]
        y, nxt = expert_ffn(lhs[...], lambda: piece_rows(q + 1))
        for j in range(PIECES_PER_CHUNK):
            buf[pl.ds(pl.multiple_of(dst[j], PIECE), PIECE), :] = y[j * PIECE:(j + 1) * PIECE]
        sent = for_tiles(sent, jnp.where(is_last, nxt[2], 0), lambda t: copy_out(t).start())
        return nxt, arrived, sent

    half = PIECES_PER_CHUNK // 2
    n_chunks = n_pieces // PIECES_PER_CHUNK + jnp.where(n_pieces % PIECES_PER_CHUNK > half, 1, 0)
    _, arrived, sent = lax.fori_loop(0, n_chunks, chunk_body, (piece_rows(0), jnp.int32(0), jnp.int32(0)))
    for_tiles(arrived, jnp.where(is_first, TILES_PER_SUPER, 0), lambda t: copy_in(t).wait())

    first_left = n_chunks * PIECES_PER_CHUNK

    @pl.when(first_left < n_pieces)
    def _():
        dst = []
        for j in range(half):
            row, _ = locate(first_left + j)
            dst.append(jnp.where(first_left + j < n_pieces, row, SUPER_ROWS))
            lhs[j * PIECE:(j + 1) * PIECE, :] = buf[pl.ds(pl.multiple_of(row, PIECE), PIECE), :]
        y, _ = expert_ffn(lhs[:half * PIECE, :])
        for j in range(half):
            buf[pl.ds(pl.multiple_of(dst[j], PIECE), PIECE), :] = y[j * PIECE:(j + 1) * PIECE]

    for_tiles(sent, jnp.where(is_last, TILES_PER_SUPER, 0), lambda t: copy_out(t).start())
    for_tiles(0, jnp.where(is_last, TILES_PER_SUPER, 0), lambda t: copy_out(t).wait())


def _moe_experts(off, cnt, hs, wg, wu, wrem, wd, moe_index):
    expert = lambda *shape: pl.BlockSpec((None, 1) + shape, lambda s, e, *_: (moe_index, e, 0, 0))
    grid_spec = pltpu.PrefetchScalarGridSpec(
        num_scalar_prefetch=2,
        grid=(N_SUPER, N_EXPERTS),
        in_specs=[
            pl.BlockSpec(memory_space=pl.ANY),
            expert(D_MODEL, D_FF_EXPERT), expert(D_MODEL, D_FF_EXPERT), expert(D_MODEL, 2 * FF_REM),
            expert(D_FF_EXPERT, D_MODEL),
        ],
        out_specs=pl.BlockSpec(memory_space=pl.ANY),
        scratch_shapes=[pltpu.VMEM((SUPER_ROWS + PIECE, D_MODEL), BF16),
                        pltpu.VMEM((CHUNK_ROWS, D_MODEL), BF16),
                        pltpu.SemaphoreType.DMA((TILES_PER_SUPER,)),
                        pltpu.SemaphoreType.DMA((TILES_PER_SUPER,))],
    )
    return pl.pallas_call(
        _moe_experts_kernel,
        grid_spec=grid_spec,
        out_shape=jax.ShapeDtypeStruct((NT_ALL * SORT_ROWS, D_MODEL), BF16),
        compiler_params=pltpu.CompilerParams(
            dimension_semantics=("arbitrary", "arbitrary"), vmem_limit_bytes=VMEM_LIMIT),
        name="moe_experts",
    )(off, cnt, hs, wg, wu, wrem, wd)


def _unsorted_residual(x_ref, ys_ref, meta_ref):
    meta = meta_ref[...]
    p_col = lax.broadcasted_iota(jnp.int32, (1, SORT_ROWS), 1).astype(F32)
    scatter = (jnp.where(p_col == meta[:, 0:1], meta[:, 2:3], 0.0)
               + jnp.where(p_col == meta[:, 1:2], meta[:, 3:4], 0.0)).astype(BF16)
    return x_ref[...] + jnp.dot(scatter, ys_ref[...], preferred_element_type=F32)


def _moe_unsort_kernel(x_ref, ys_ref, meta_ref, xo_ref):
    xo_ref[...] = _unsorted_residual(x_ref, ys_ref, meta_ref)


def _moe_unsort_final_kernel(x_ref, ys_ref, meta_ref, g_ref, yp_ref, ysm_ref):
    i = pl.program_id(0)
    y = _rms(_unsorted_residual(x_ref, ys_ref, meta_ref), g_ref[...])

    @pl.when(i < NT_PROMPT)
    def _():
        yp_ref[...] = y

    @pl.when(i == NT_PROMPT)
    def _():
        ysm_ref[...] = y


def _moe_unsort(x, ys, meta, final_g=None):
    in_specs = [
        pl.BlockSpec((TM, D_MODEL), lambda i: (i, 0)),
        pl.BlockSpec((SORT_ROWS, D_MODEL), lambda i: (i, 0)),
        pl.BlockSpec((TM, LANES), lambda i: (i, 0)),
    ]
    params = pltpu.CompilerParams(dimension_semantics=("arbitrary",), vmem_limit_bytes=VMEM_LIMIT)
    if final_g is None:
        return pl.pallas_call(
            _moe_unsort_kernel,
            grid=(NT_ALL,),
            in_specs=in_specs,
            out_specs=pl.BlockSpec((TM, D_MODEL), lambda i: (i, 0)),
            out_shape=jax.ShapeDtypeStruct((T_ALL, D_MODEL), F32),
            compiler_params=params,
            name="moe_unsort",
        )(x, ys, meta)
    return pl.pallas_call(
        _moe_unsort_final_kernel,
        grid=(NT_ALL,),
        in_specs=in_specs + [pl.BlockSpec((1, D_MODEL), lambda i: (0, 0))],
        out_specs=[pl.BlockSpec((TM, D_MODEL), lambda i: (jnp.minimum(i, NT_PROMPT - 1), 0)),
                   pl.BlockSpec((TM, D_MODEL), lambda i: (0, 0))],
        out_shape=[jax.ShapeDtypeStruct((T_PROMPT, D_MODEL), F32),
                   jax.ShapeDtypeStruct((T_SAMPLE, D_MODEL), F32)],
        compiler_params=params,
        name="moe_unsort_final",
    )(x, ys, meta, final_g)


def _moe(x, norm_g, wr_cat, wg, wu, wrem, wd, layer, moe_index, final_g=None):
    hs, meta, off, cnt = _moe_sort(x, norm_g, wr_cat, layer, moe_index)
    off = off[:, 0, :N_EXPERTS].reshape(-1)
    cnt = cnt[:, 0, :N_EXPERTS].reshape(-1)
    ys = _moe_experts(off, cnt, hs, wg, wu, wrem, wd, moe_index)
    return _moe_unsort(x, ys, meta, final_g)


def kernel(x_prompt, x_sample, cache_win_k, cache_win_v, attn_norm_g, w_in, b_in, chunk_ln_g, chunk_ln_b, w_spatial, b_spatial, attn_sinks, mix_norm_a_g, mix_norm_b_g, w_out, ffn_norm_g, ffn_w_gate, ffn_w_up, ffn_w_down, router_w, expert_w_gate, expert_w_up, expert_w_down, final_norm_g):
    w_in_p = jnp.concatenate([w_in[..., :Q0], _g_major(w_in[..., Q0:K0], -1), w_in[..., K0:]], axis=-1).astype(BF16)
    b_in_p = jnp.concatenate([b_in[..., :Q0], _g_major(b_in[..., Q0:K0], -1), b_in[..., K0:]], axis=-1)
    w_out_p = jnp.concatenate([w_out[:, :GMLP_WIDTH], _g_major(w_out[:, GMLP_WIDTH:], 1)], axis=1).astype(BF16)
    g_b_p = _g_major(mix_norm_b_g, 1)
    bs_prompt = jnp.repeat(jnp.transpose(b_spatial, (0, 2, 1)), GMLP_HEAD_DIM, axis=2)
    bs_sample = jnp.tile(bs_prompt[:, :DEC_SEQ], (1, SUBLANES // DEC_SEQ, 1))
    t_idx = np.arange(DEC_SEQ)
    ws_small = jnp.stack([w_spatial[:, :, t_idx, np.maximum(t_idx - d, 0)] for d in range(DEC_SEQ)], axis=1)
    ws_sample = jnp.tile(jnp.repeat(jnp.transpose(ws_small, (0, 1, 3, 2)), GMLP_HEAD_DIM, axis=3),
                         (1, 1, SUBLANES // DEC_SEQ, 1))
    sink_row = jnp.repeat(attn_sinks, SEQS_PER_TILE * DEC_SEQ, axis=1)[:, None, :]
    bias_prompt = jnp.asarray(_prompt_bias())
    bias_sample = jnp.asarray(_sample_bias())
    wr_pad = jnp.pad(router_w, ((0, 0), (0, 0), (0, LANES - N_EXPERTS)))
    wr_hi = wr_pad.astype(BF16)
    wr_cat = jnp.concatenate([wr_hi, (wr_pad - wr_hi.astype(F32)).astype(BF16)], axis=-1)

    ck = cache_win_k.reshape(DEPTH, DEC_BATCH, CACHE_W, KV_WIDTH)
    cv = cache_win_v.reshape(DEPTH, DEC_BATCH, CACHE_W, KV_WIDTH)

    lw = dict(
        norm_g=attn_norm_g[:, None, :], w_in=w_in_p, b_in=b_in_p[:, None, :],
        ln_g=chunk_ln_g.reshape(DEPTH, 1, GMLP_WIDTH), ln_b=chunk_ln_b.reshape(DEPTH, 1, GMLP_WIDTH),
        w_s=w_spatial, b_s_prompt=bs_prompt, w_s_sample=ws_sample, b_s_sample=bs_sample,
        sink_row=sink_row, bias_prompt=bias_prompt, bias_sample=bias_sample,
        g_a=mix_norm_a_g[:, None, :], g_b=g_b_p[:, None, :], w_out=w_out_p)
    ffn_g = ffn_norm_g[:, None, :]

    x = None
    kp, vp, cvs = [], [], []
    states = ()
    for l in range(DEPTH):
        lw["sinks"] = attn_sinks[l]
        i = l // 2
        cast = (i, (ffn_w_gate, ffn_w_up, ffn_w_down)) if l % 2 == 0 else None
        x_in = x_prompt.reshape(T_PROMPT, D_MODEL) if l == 0 else x
        x_new, k_p, v_p, *ffn_bf = _mixer_prompt(x_in, lw, l, cast)
        if l == 0:
            x_new, *states, cv_s = _mixer_sample(x_sample.reshape(T_SAMPLE, D_MODEL), 0, x_new, ck, cv, l, lw)
        else:
            x_new, *states, cv_s = _mixer_sample(x, NT_PROMPT, x_new, ck, cv, l, lw, states)
        kp.append(k_p)
        vp.append(v_p)
        cvs.append(cv_s)
        if l % 2 == 0:
            ffn_wg, ffn_wu, ffn_wd = (w[None] for w in ffn_bf)
            x, ex_wg, ex_wu, ex_wd = _swiglu(x_new, ffn_g, ffn_wg, ffn_wu, ffn_wd, l, 0,
                                             (expert_w_gate, expert_w_up, expert_w_down), i)
            ex_wrem = jnp.concatenate([ex_wg[..., FF_MAIN:], ex_wu[..., FF_MAIN:]], axis=-1)
        else:
            final_g = final_norm_g[None] if l == DEPTH - 1 else None
            x = _moe(x_new, ffn_g, wr_cat[i:i + 1], ex_wg, ex_wu, ex_wrem, ex_wd, l, 0, final_g)
    y_p, y_s = x
    win_p = (DEPTH, BATCH, WINDOW, KV_HEADS, HEAD_DIM)
    win_s = (DEPTH, DEC_BATCH, CACHE_W, KV_HEADS, HEAD_DIM)
    return (y_p.reshape(BATCH, SEQ, D_MODEL), y_s.reshape(DEC_BATCH, DEC_SEQ, D_MODEL),
            jnp.stack(kp).reshape(win_p), jnp.stack(vp).reshape(win_p),
            states[0].reshape(win_s), states[1].reshape(win_s),
            jnp.stack(cvs).reshape(DEPTH, DEC_BATCH, DEC_SEQ, GMLP_HEADS, GMLP_HEAD_DIM))
```

```python
import functools

import numpy as np
import jax
import jax.numpy as jnp
from jax import lax
from jax.experimental import pallas as pl
from jax.experimental.pallas import tpu as pltpu

D_MODEL = 1024
BATCH = 4
SEQ = 4096
DEPTH = 4
DEC_BATCH = 128
DEC_SEQ = 4
HEAD_DIM = 64
ATTN_HEADS = 8
KV_HEADS = 2
GQA_GROUP = ATTN_HEADS // KV_HEADS
ATTN_WIDTH = ATTN_HEADS * HEAD_DIM
KV_WIDTH = KV_HEADS * HEAD_DIM
WINDOW = 128
CHUNK = 128
GMLP_HEADS = 4
GMLP_HEAD_DIM = 128
GMLP_WIDTH = GMLP_HEADS * GMLP_HEAD_DIM
IN_WIDTH = 2 * GMLP_WIDTH + ATTN_WIDTH + 2 * KV_WIDTH
D_FF = 2816
N_EXPERTS = 8
D_FF_EXPERT = 1408
CACHE_W = 128
EPS = 1e-6

LANES = 128
SUBLANES = 8
MXU_WIDTH = 256
TM = 512
T_PROMPT = BATCH * SEQ
T_SAMPLE = DEC_BATCH * DEC_SEQ
T_ALL = T_PROMPT + T_SAMPLE
NT_PROMPT = T_PROMPT // TM
NT_ALL = T_ALL // TM
TILES_PER_SEQ = SEQ // TM
BLOCKS_PER_TILE = TM // WINDOW
SEQ_GROUP = 16
SEQS_PER_TILE = 4
N_SEQ_GROUPS = DEC_BATCH // SEQ_GROUP
KEYS_PAD = CACHE_W + SUBLANES
PIECE = 2 * SUBLANES
SORT_ROWS = 2 * TM + N_EXPERTS * PIECE
TILES_PER_SUPER = 11
N_SUPER = NT_ALL // TILES_PER_SUPER
SUPER_ROWS = TILES_PER_SUPER * SORT_ROWS
SORT_TILES_PER_STEP = 3
CHUNK_ROWS = MXU_WIDTH
FF_REM = D_FF_EXPERT % MXU_WIDTH
FF_MAIN = D_FF_EXPERT - FF_REM
PIECES_PER_CHUNK = CHUNK_ROWS // PIECE
VMEM_LIMIT = 56 * 1024 * 1024

F32 = jnp.float32
BF16 = jnp.bfloat16
NEG_INF = float("-inf")

Q0 = 2 * GMLP_WIDTH
K0 = Q0 + ATTN_WIDTH
V0 = K0 + KV_WIDTH


def _g_major(a, axis):
    axis = axis % a.ndim
    shape = a.shape
    a = a.reshape(shape[:axis] + (KV_HEADS, GQA_GROUP, HEAD_DIM) + shape[axis + 1:])
    return jnp.swapaxes(a, axis, axis + 1).reshape(shape)


def _slope(kv, g):
    return 2.0 ** (-8.0 * (kv * GQA_GROUP + g + 1) / ATTN_HEADS)


def _prompt_bias():
    c = np.arange(WINDOW)[:, None]
    qi = np.arange(WINDOW)[None, :]
    dist = np.where(c > qi, qi + WINDOW - c, qi - c)
    out = np.zeros((2, KV_HEADS, WINDOW, GQA_GROUP * WINDOW), np.float32)
    for var in range(2):
        ok = ~((var == 1) & (c > qi))
        for kv in range(KV_HEADS):
            for g in range(GQA_GROUP):
                out[var, kv, :, g * WINDOW:(g + 1) * WINDOW] = np.where(ok, -_slope(kv, g) * dist, NEG_INF)
    return out


def _sample_bias():
    out = np.zeros((KEYS_PAD, LANES), np.float32)
    j = np.arange(KEYS_PAD)
    kpos = j - CACHE_W
    for kv in range(KV_HEADS):
        for g in range(GQA_GROUP):
            for s in range(SEQS_PER_TILE):
                for t in range(DEC_SEQ):
                    col = ((kv * GQA_GROUP + g) * SEQS_PER_TILE + s) * DEC_SEQ + t
                    dist = t - kpos
                    ok = (dist >= 0) & (dist < WINDOW) & (j < CACHE_W + DEC_SEQ)
                    out[:, col] = np.where(ok, -_slope(kv, g) * dist, NEG_INF)
    return out


def _rms(x, g):
    return x * lax.rsqrt(jnp.mean(x * x, axis=-1, keepdims=True) + EPS) * g


def _gelu(x):
    return 0.5 * x * (1.0 + lax.erf(x * (2.0 ** -0.5)))


def _gmlp_v(z_v, lng_ref, lnb_ref):
    out = []
    for hh in range(GMLP_HEADS):
        sl = slice(hh * LANES, (hh + 1) * LANES)
        vh = _gelu(z_v[:, sl])
        vc = vh - jnp.mean(vh, axis=-1, keepdims=True)
        y = vc * lax.rsqrt(jnp.mean(vc * vc, axis=-1, keepdims=True) + EPS)
        out.append(y * lng_ref[:, sl] + lnb_ref[:, sl])
    return out


def _dot_nt(a, b):
    return lax.dot_general(a, b, (((1,), (1,)), ((), ())), preferred_element_type=F32)


N_MIXER_INPUTS = 13


def _mixer_prompt_kernel(*refs, n_cast):
    i = pl.program_id(0)
    n_in = N_MIXER_INPUTS + n_cast
    tile_refs = refs[:N_MIXER_INPUTS] + refs[n_in:n_in + 3] + refs[n_in + 3 + n_cast:]

    @pl.when(i < NT_PROMPT)
    def _():
        _mixer_prompt_tile(*tile_refs)

    @pl.when(i == NT_PROMPT)
    def _():
        xo_ref = refs[n_in]
        xo_ref[...] = jnp.zeros_like(xo_ref)

    for src, dst in zip(refs[N_MIXER_INPUTS:n_in], refs[n_in + 3:n_in + 3 + n_cast]):
        dst[...] = src[...].astype(BF16)


def _mixer_prompt_tile(x_ref, ng_ref, win_ref, bin_ref, lng_ref, lnb_ref, ws_ref, bs_ref,
                       sink_ref, bias_ref, ga_ref, gb_ref, wout_ref,
                       xo_ref, ks_ref, vs_ref, kprev_ref, vprev_ref):
    i = pl.program_id(0)

    @pl.when(i == 0)
    def _():
        kprev_ref[...] = jnp.zeros_like(kprev_ref)
        vprev_ref[...] = jnp.zeros_like(vprev_ref)

    x = x_ref[...]
    h = _rms(x, ng_ref[...]).astype(BF16)

    def in_proj(lo, hi):
        return jnp.dot(h, win_ref[:, lo:hi], preferred_element_type=F32) + bin_ref[:, lo:hi]

    z_att = in_proj(Q0, IN_WIDTH)
    q = (z_att[:, :ATTN_WIDTH] * (HEAD_DIM ** -0.5)).astype(BF16)
    k = z_att[:, ATTN_WIDTH:ATTN_WIDTH + KV_WIDTH]
    val = z_att[:, ATTN_WIDTH + KV_WIDTH:]
    k_bf = k.astype(BF16)
    lane = lax.broadcasted_iota(jnp.int32, (1, LANES), 1)
    low = lane < HEAD_DIM
    first = jnp.where(i % TILES_PER_SEQ == 0, 1, 0)
    c_i = lax.broadcasted_iota(jnp.int32, (WINDOW, WINDOW), 0)
    q_i = lax.broadcasted_iota(jnp.int32, (WINDOW, WINDOW), 1)
    from_prev = jnp.concatenate([c_i > q_i] * GQA_GROUP, axis=1)
    g_lane = lax.broadcasted_iota(jnp.int32, (1, GQA_GROUP * WINDOW), 1) // WINDOW
    vt_bf = val.T.astype(BF16)
    vt_all = jnp.concatenate([vprev_ref[...], vt_bf], axis=1)
    k_all = jnp.concatenate([kprev_ref[...], k_bf], axis=0)
    zero = jnp.zeros((), BF16)
    k_heads = [jnp.where(low, k_all, zero), jnp.where(low, zero, k_all)]
    sinks = []
    for kv in range(KV_HEADS):
        sink = jnp.full((1, GQA_GROUP * WINDOW), sink_ref[kv * GQA_GROUP], F32)
        for g in range(1, GQA_GROUP):
            sink = jnp.where(g_lane == g, sink_ref[kv * GQA_GROUP + g], sink)
        sinks.append(sink)

    def attention_block(c):
        rows = slice(c * WINDOW, (c + 1) * WINDOW)
        band = slice(c * WINDOW, (c + 2) * WINDOW)
        o_t = []
        q_rows = jnp.concatenate([q[rows, g * LANES:(g + 1) * LANES] for g in range(GQA_GROUP)], axis=0)
        keys = jnp.concatenate([k_heads[0][band], k_heads[1][band]], axis=0)
        s_both = _dot_nt(keys, q_rows)
        for kv in range(KV_HEADS):
            s2 = s_both[kv * 2 * WINDOW:(kv + 1) * 2 * WINDOW]
            bias = bias_ref[first, kv] if c == 0 else bias_ref[0, kv]
            s = jnp.where(from_prev, s2[:WINDOW], s2[WINDOW:]) + bias
            m = jnp.maximum(jnp.max(s, axis=0, keepdims=True), sinks[kv])
            p = jnp.exp(s - m)
            den = jnp.sum(p, axis=0, keepdims=True) + jnp.exp(sinks[kv] - m)
            p2 = jnp.concatenate([jnp.where(from_prev, p, 0.0), jnp.where(from_prev, 0.0, p)],
                                 axis=0).astype(BF16)
            o = jnp.dot(vt_all[kv * HEAD_DIM:(kv + 1) * HEAD_DIM, band], p2,
                        preferred_element_type=F32)
            o_t.append(o / den)
        bt = jnp.concatenate(o_t, axis=0)
        return jnp.concatenate(
            [bt[:, g * WINDOW:(g + 1) * WINDOW].T for g in range(GQA_GROUP)], axis=1)

    z_u = in_proj(0, GMLP_WIDTH)
    b_rows = [attention_block(0)]
    z_v = in_proj(GMLP_WIDTH, Q0)
    b_rows.append(attention_block(1))
    u = _gelu(z_u)
    b_rows.append(attention_block(2))
    v_heads = _gmlp_v(z_v, lng_ref, lnb_ref)
    row = lax.broadcasted_iota(jnp.int32, (CHUNK, CHUNK), 0)
    col = lax.broadcasted_iota(jnp.int32, (CHUNK, CHUNK), 1)
    a_cols = []
    for hh in range(GMLP_HEADS):
        w = jnp.where(row >= col, ws_ref[hh], 0.0).astype(BF16)
        v_cat = jnp.concatenate(
            [v_heads[hh][c * CHUNK:(c + 1) * CHUNK] for c in range(BLOCKS_PER_TILE)], axis=1)
        mixed = jnp.dot(w, v_cat.astype(BF16), preferred_element_type=F32)
        bias = bs_ref[:, hh * LANES:(hh + 1) * LANES]
        a_cols.append(jnp.concatenate(
            [mixed[:, c * LANES:(c + 1) * LANES] + bias for c in range(BLOCKS_PER_TILE)], axis=0))
    a = u * jnp.concatenate(a_cols, axis=1)
    a_n = _rms(a, ga_ref[...]).astype(BF16)
    b_rows.append(attention_block(3))
    out = x + jnp.dot(a_n, wout_ref[:GMLP_WIDTH, :], preferred_element_type=F32)
    b_n = _rms(jnp.concatenate(b_rows, axis=0), gb_ref[...]).astype(BF16)
    xo_ref[...] = out + jnp.dot(b_n, wout_ref[GMLP_WIDTH:, :], preferred_element_type=F32)

    kprev_ref[...] = k_bf[TM - WINDOW:]
    vprev_ref[...] = vt_bf[:, TM - WINDOW:]
    ks_ref[0] = k[TM - WINDOW:]
    vs_ref[0] = val[TM - WINDOW:]


def _layer_block(layer, *shape, **kwargs):
    return pl.BlockSpec((None,) + shape, lambda *_: (layer,) + (0,) * len(shape), **kwargs)


def _mixer_prompt(x, lw, layer, cast=None):
    const = lambda *shape: pl.BlockSpec(shape, lambda i: (0,) * len(shape))
    per_layer = functools.partial(_layer_block, layer)
    cast_in, cast_specs, cast_out_specs, cast_shapes = [], [], [], []
    if cast is not None:
        ffn_index, weights = cast
        for w in weights:
            rows, cols = w.shape[1:]
            n_slabs = NT_PROMPT if (rows // NT_PROMPT) % PIECE == 0 else NT_PROMPT // 2
            slab = lambda i, n=n_slabs: jnp.minimum(i, n - 1)
            cast_in.append(w)
            cast_specs.append(pl.BlockSpec((None, rows // n_slabs, cols),
                                           lambda i, slab=slab: (ffn_index, slab(i), 0)))
            cast_out_specs.append(pl.BlockSpec((rows // n_slabs, cols), lambda i, slab=slab: (slab(i), 0)))
            cast_shapes.append(jax.ShapeDtypeStruct((rows, cols), BF16))
    return pl.pallas_call(
        functools.partial(_mixer_prompt_kernel, n_cast=len(cast_in)),
        grid=(NT_ALL,),
        in_specs=[
            pl.BlockSpec((TM, D_MODEL), lambda i: (jnp.minimum(i, NT_PROMPT - 1), 0)),
            per_layer(1, D_MODEL), per_layer(D_MODEL, IN_WIDTH), per_layer(1, IN_WIDTH),
            per_layer(1, GMLP_WIDTH), per_layer(1, GMLP_WIDTH),
            per_layer(GMLP_HEADS, CHUNK, CHUNK), per_layer(CHUNK, GMLP_WIDTH),
            pl.BlockSpec(memory_space=pltpu.SMEM),
            const(2, KV_HEADS, WINDOW, GQA_GROUP * WINDOW),
            per_layer(1, GMLP_WIDTH), per_layer(1, ATTN_WIDTH), per_layer(D_MODEL, D_MODEL),
        ] + cast_specs,
        out_specs=[
            pl.BlockSpec((TM, D_MODEL), lambda i: (i, 0)),
            pl.BlockSpec((1, WINDOW, KV_WIDTH), lambda i: (jnp.minimum(i // TILES_PER_SEQ, BATCH - 1), 0, 0)),
            pl.BlockSpec((1, WINDOW, KV_WIDTH), lambda i: (jnp.minimum(i // TILES_PER_SEQ, BATCH - 1), 0, 0)),
        ] + cast_out_specs,
        out_shape=[
            jax.ShapeDtypeStruct((T_ALL, D_MODEL), F32),
            jax.ShapeDtypeStruct((BATCH, WINDOW, KV_WIDTH), F32),
            jax.ShapeDtypeStruct((BATCH, WINDOW, KV_WIDTH), F32),
        ] + cast_shapes,
        scratch_shapes=[pltpu.VMEM((WINDOW, KV_WIDTH), BF16), pltpu.VMEM((WINDOW, KV_WIDTH), BF16)],
        compiler_params=pltpu.CompilerParams(
            dimension_semantics=("arbitrary",), vmem_limit_bytes=VMEM_LIMIT),
        name="mixer_prompt",
    )(x, lw["norm_g"], lw["w_in"], lw["b_in"], lw["ln_g"], lw["ln_b"], lw["w_s"], lw["b_s_prompt"],
      lw["sinks"], lw["bias_prompt"], lw["g_a"], lw["g_b"], lw["w_out"], *cast_in)


def _mixer_sample_kernel(*refs, first_layer):
    n_alias = 1 if first_layer else 3
    _mixer_sample_body(refs[0], *refs[1 + n_alias:], first_layer=first_layer)


def _mixer_sample_body(x_ref, ck_ref, cv_ref, ng_ref, win_ref, bin_ref, lng_ref, lnb_ref,
                       wsd_ref, bsd_ref, sinkrow_ref, bias_ref, ga_ref, gb_ref, wout_ref,
                       xo_ref, ko_ref, vo_ref, cvo_ref,
                       q_s, k_s, v_s, a_s, b_s, kfull, vfull, *, first_layer):
    j = pl.program_id(0)

    @pl.when(j == 0)
    def _():
        h = _rms(x_ref[...], ng_ref[...]).astype(BF16)
        z = jnp.dot(h, win_ref[...], preferred_element_type=F32) + bin_ref[...]
        u = _gelu(z[:, :GMLP_WIDTH])
        v = jnp.concatenate(_gmlp_v(z[:, GMLP_WIDTH:Q0], lng_ref, lnb_ref), axis=1)
        cvo_ref[...] = v
        t_row = lax.broadcasted_iota(jnp.int32, (SUBLANES, 1), 0) % DEC_SEQ

        def tile_rows(pat):
            return jnp.broadcast_to(pat[None], (TM // SUBLANES,) + pat.shape).reshape(TM, pat.shape[-1])

        mixed = tile_rows(bsd_ref[...])
        for d in range(DEC_SEQ):
            vd = v if d == 0 else pltpu.roll(v, d, 0)
            mixed = mixed + tile_rows(jnp.where(t_row >= d, wsd_ref[d], 0.0)) * vd
        a_s[...] = _rms(u * mixed, ga_ref[...]).astype(BF16)
        q_s[...] = z[:, Q0:K0] * (HEAD_DIM ** -0.5)
        k_s[...] = z[:, K0:V0]
        v_s[...] = z[:, V0:]
        kfull[:, CACHE_W + DEC_SEQ:, :] = jnp.zeros((SEQ_GROUP, KEYS_PAD - CACHE_W - DEC_SEQ, KV_WIDTH), F32)
        vfull[:, CACHE_W + DEC_SEQ:, :] = jnp.zeros((SEQ_GROUP, KEYS_PAD - CACHE_W - DEC_SEQ, KV_WIDTH), F32)

    base = pl.multiple_of(j * (SEQ_GROUP * DEC_SEQ), SEQ_GROUP * DEC_SEQ)
    kfull[:, :CACHE_W, :] = ck_ref[0]
    vfull[:, :CACHE_W, :] = cv_ref[0]
    k_new = k_s[pl.ds(base, SEQ_GROUP * DEC_SEQ), :]
    v_new = v_s[pl.ds(base, SEQ_GROUP * DEC_SEQ), :]
    for s in range(SEQ_GROUP):
        kfull[s, CACHE_W:CACHE_W + DEC_SEQ, :] = k_new[s * DEC_SEQ:(s + 1) * DEC_SEQ]
        vfull[s, CACHE_W:CACHE_W + DEC_SEQ, :] = v_new[s * DEC_SEQ:(s + 1) * DEC_SEQ]
    ko_ref[0] = kfull[:, DEC_SEQ:DEC_SEQ + CACHE_W, :]
    vo_ref[0] = vfull[:, DEC_SEQ:DEC_SEQ + CACHE_W, :]
    if first_layer:
        ko_ref[1:] = jnp.zeros((DEPTH - 1,) + ko_ref.shape[1:], F32)
        vo_ref[1:] = jnp.zeros((DEPTH - 1,) + vo_ref.shape[1:], F32)

    lane = lax.broadcasted_iota(jnp.int32, (1, LANES), 1)
    low = lane < HEAD_DIM
    col_seq = (lane // DEC_SEQ) % SEQS_PER_TILE
    row_seq = (lax.broadcasted_iota(jnp.int32, (LANES, 1), 0) // DEC_SEQ) % SEQS_PER_TILE
    sink = sinkrow_ref[...]
    zero = jnp.zeros((), BF16)
    quad_rows = SEQS_PER_TILE * DEC_SEQ
    for m in range(SEQ_GROUP // SEQS_PER_TILE):
        r0 = pl.multiple_of(base + m * quad_rows, quad_rows)
        q16 = q_s[pl.ds(r0, quad_rows), :]
        qt = jnp.concatenate(
            [jnp.where(low if kv == 0 else jnp.logical_not(low), q16[:, g * LANES:(g + 1) * LANES], 0.0)
             for kv in range(KV_HEADS) for g in range(GQA_GROUP)], axis=0).astype(BF16)
        s = bias_ref[...]
        for n in range(SEQS_PER_TILE):
            kn = kfull[SEQS_PER_TILE * m + n].astype(BF16)
            s = s + _dot_nt(kn, jnp.where(row_seq == n, qt, zero))
        mx = jnp.maximum(jnp.max(s, axis=0, keepdims=True), sink)
        p = jnp.exp(s - mx)
        p = (p / (jnp.sum(p, axis=0, keepdims=True) + jnp.exp(sink - mx))).astype(BF16)
        o = jnp.zeros((LANES, KV_WIDTH), F32)
        for n in range(SEQS_PER_TILE):
            vn = vfull[SEQS_PER_TILE * m + n].astype(BF16)
            o = o + lax.dot_general(jnp.where(col_seq == n, p, zero), vn, (((0,), (0,)), ((), ())),
                                    preferred_element_type=F32)
        b_s[pl.ds(r0, quad_rows), :] = jnp.concatenate(
            [jnp.where(low, o[g * quad_rows:(g + 1) * quad_rows],
                       o[(GQA_GROUP + g) * quad_rows:(GQA_GROUP + g + 1) * quad_rows])
             for g in range(GQA_GROUP)], axis=1)

    @pl.when(j == N_SEQ_GROUPS - 1)
    def _():
        b_n = _rms(b_s[...], gb_ref[...]).astype(BF16)
        merged = jnp.concatenate([a_s[...], b_n], axis=1)
        xo_ref[...] = x_ref[...] + jnp.dot(merged, wout_ref[...], preferred_element_type=F32)


def _mixer_sample(x, x_block, x_buf, cache_k, cache_v, layer, lw, states=()):
    const = lambda *shape: pl.BlockSpec(shape, lambda j: (0,) * len(shape))
    per_layer = functools.partial(_layer_block, layer)
    first_layer = layer == 0
    cache_spec = pl.BlockSpec((1, SEQ_GROUP, CACHE_W, KV_WIDTH), lambda j: (layer, j, 0, 0))
    state_layers = DEPTH if first_layer else 1
    state_spec = pl.BlockSpec((state_layers, SEQ_GROUP, CACHE_W, KV_WIDTH), lambda j: (layer, j, 0, 0))
    state_shape = jax.ShapeDtypeStruct((DEPTH, DEC_BATCH, CACHE_W, KV_WIDTH), F32)
    aliased = (x_buf,) + tuple(states)
    return pl.pallas_call(
        functools.partial(_mixer_sample_kernel, first_layer=first_layer),
        grid=(N_SEQ_GROUPS,),
        in_specs=[
            pl.BlockSpec((TM, D_MODEL), lambda j: (x_block, 0)),
        ] + [pl.BlockSpec(memory_space=pl.ANY)] * len(aliased) + [
            cache_spec, cache_spec,
            per_layer(1, D_MODEL), per_layer(D_MODEL, IN_WIDTH), per_layer(1, IN_WIDTH),
            per_layer(1, GMLP_WIDTH), per_layer(1, GMLP_WIDTH),
            per_layer(DEC_SEQ, SUBLANES, GMLP_WIDTH), per_layer(SUBLANES, GMLP_WIDTH),
            per_layer(1, LANES), const(KEYS_PAD, LANES),
            per_layer(1, GMLP_WIDTH), per_layer(1, ATTN_WIDTH), per_layer(D_MODEL, D_MODEL),
        ],
        out_specs=[
            pl.BlockSpec((TM, D_MODEL), lambda j: (NT_PROMPT, 0)),
            state_spec, state_spec,
            const(TM, GMLP_WIDTH),
        ],
        out_shape=[
            jax.ShapeDtypeStruct((T_ALL, D_MODEL), F32),
            state_shape, state_shape,
            jax.ShapeDtypeStruct((TM, GMLP_WIDTH), F32),
        ],
        scratch_shapes=[
            pltpu.VMEM((TM, ATTN_WIDTH), F32), pltpu.VMEM((TM, KV_WIDTH), F32),
            pltpu.VMEM((TM, KV_WIDTH), F32), pltpu.VMEM((TM, GMLP_WIDTH), BF16),
            pltpu.VMEM((TM, ATTN_WIDTH), F32),
            pltpu.VMEM((SEQ_GROUP, KEYS_PAD, KV_WIDTH), F32),
            pltpu.VMEM((SEQ_GROUP, KEYS_PAD, KV_WIDTH), F32),
        ],
        input_output_aliases={1 + n: n for n in range(len(aliased))},
        compiler_params=pltpu.CompilerParams(
            dimension_semantics=("arbitrary",), vmem_limit_bytes=VMEM_LIMIT),
        name="mixer_sample",
    )(x, *aliased, cache_k, cache_v, lw["norm_g"], lw["w_in"], lw["b_in"], lw["ln_g"], lw["ln_b"],
      lw["w_s_sample"], lw["b_s_sample"], lw["sink_row"], lw["bias_sample"],
      lw["g_a"], lw["g_b"], lw["w_out"])


def _swiglu_kernel(x_ref, ng_ref, wg_ref, wu_ref, wd_ref, eg_ref, eu_ref, ed_ref,
                   xo_ref, ego_ref, euo_ref, edo_ref):
    x = x_ref[...]
    h = _rms(x, ng_ref[...]).astype(BF16)
    gate = jnp.dot(h, wg_ref[...], preferred_element_type=F32)
    ego_ref[...] = eg_ref[...].astype(BF16)
    up = jnp.dot(h, wu_ref[...], preferred_element_type=F32)
    euo_ref[...] = eu_ref[...].astype(BF16)
    act = (jax.nn.silu(gate) * up).astype(BF16)
    xo_ref[...] = x + jnp.dot(act, wd_ref[...], preferred_element_type=F32)
    edo_ref[...] = ed_ref[...].astype(BF16)


def _swiglu(x, norm_g, wg, wu, wd, layer, ffn_index, expert_f32, expert_index):
    weights = functools.partial(_layer_block, ffn_index, pipeline_mode=pl.Buffered(1))
    slab_steps = NT_ALL - 1
    up_rows = N_EXPERTS * D_MODEL // slab_steps
    down_rows = N_EXPERTS * D_FF_EXPERT // slab_steps
    slab = lambda i: jnp.minimum(i, slab_steps - 1)
    eg, eu, ed = expert_f32
    eg = eg.reshape(-1, N_EXPERTS * D_MODEL, D_FF_EXPERT)
    eu = eu.reshape(-1, N_EXPERTS * D_MODEL, D_FF_EXPERT)
    ed = ed.reshape(-1, N_EXPERTS * D_FF_EXPERT, D_MODEL)
    x_new, eg_bf, eu_bf, ed_bf = pl.pallas_call(
        _swiglu_kernel,
        grid=(NT_ALL,),
        in_specs=[
            pl.BlockSpec((TM, D_MODEL), lambda i: (i, 0)),
            _layer_block(layer, 1, D_MODEL),
            weights(D_MODEL, D_FF), weights(D_MODEL, D_FF), weights(D_FF, D_MODEL),
            pl.BlockSpec((None, up_rows, D_FF_EXPERT), lambda i: (expert_index, slab(i), 0)),
            pl.BlockSpec((None, up_rows, D_FF_EXPERT), lambda i: (expert_index, slab(i), 0)),
            pl.BlockSpec((None, down_rows, D_MODEL), lambda i: (expert_index, slab(i), 0)),
        ],
        out_specs=[
            pl.BlockSpec((TM, D_MODEL), lambda i: (i, 0)),
            pl.BlockSpec((up_rows, D_FF_EXPERT), lambda i: (slab(i), 0)),
            pl.BlockSpec((up_rows, D_FF_EXPERT), lambda i: (slab(i), 0)),
            pl.BlockSpec((down_rows, D_MODEL), lambda i: (slab(i), 0)),
        ],
        out_shape=[
            jax.ShapeDtypeStruct((T_ALL, D_MODEL), F32),
            jax.ShapeDtypeStruct((N_EXPERTS * D_MODEL, D_FF_EXPERT), BF16),
            jax.ShapeDtypeStruct((N_EXPERTS * D_MODEL, D_FF_EXPERT), BF16),
            jax.ShapeDtypeStruct((N_EXPERTS * D_FF_EXPERT, D_MODEL), BF16),
        ],
        compiler_params=pltpu.CompilerParams(
            dimension_semantics=("arbitrary",), vmem_limit_bytes=VMEM_LIMIT),
        name="swiglu",
    )(x, norm_g, wg, wu, wd, eg, eu, ed)
    return (x_new, eg_bf.reshape(1, N_EXPERTS, D_MODEL, D_FF_EXPERT),
            eu_bf.reshape(1, N_EXPERTS, D_MODEL, D_FF_EXPERT),
            ed_bf.reshape(1, N_EXPERTS, D_FF_EXPERT, D_MODEL))


def _moe_sort_kernel(x_ref, ng_ref, wrc_ref, hs_ref, meta_ref, off_ref, cnt_ref):
    n = SORT_TILES_PER_STEP
    logits, routes = {}, {}
    for step in range(n + 2):
        if step < n:
            logits[step] = _sort_logits(x_ref[step * TM:(step + 1) * TM, :], ng_ref, wrc_ref)
        j = step - 1
        if 0 <= j < n:
            h_hi, lg = logits.pop(j)
            meta, off, cnt_pad = _sort_route(lg)
            meta_ref[j * TM:(j + 1) * TM, :] = meta
            off_ref[j] = off.astype(jnp.int32)
            cnt_ref[j] = cnt_pad.astype(jnp.int32)
            routes[j] = (h_hi, meta)
        j = step - 2
        if 0 <= j < n:
            hs_ref[j * SORT_ROWS:(j + 1) * SORT_ROWS, :] = _sort_gather(*routes.pop(j))


def _sort_logits(x, ng_ref, wrc_ref):
    h = _rms(x, ng_ref[...])
    h_hi = h.astype(BF16)
    h_lo = (h - h_hi.astype(F32)).astype(BF16)
    both = jnp.dot(h_hi, wrc_ref[...], preferred_element_type=F32)
    logits = both[:, :LANES] + both[:, LANES:] + jnp.dot(h_lo, wrc_ref[:, :LANES], preferred_element_type=F32)
    return h_hi, logits


def _sort_route(logits):
    lane = lax.broadcasted_iota(jnp.int32, (1, LANES), 1)
    logits = jnp.where(lane < N_EXPERTS, logits, NEG_INF)
    m1 = jnp.max(logits, axis=-1, keepdims=True)
    i1 = jnp.min(jnp.where(logits == m1, lane, LANES), axis=-1, keepdims=True)
    rest = jnp.where(lane == i1, NEG_INF, logits)
    m2 = jnp.max(rest, axis=-1, keepdims=True)
    i2 = jnp.min(jnp.where(rest == m2, lane, LANES), axis=-1, keepdims=True)
    e2 = jnp.exp(m2 - m1)
    den = 1.0 + e2
    sel0 = lane == i1
    sel1 = lane == i2
    routed = jnp.where(jnp.logical_or(sel0, sel1), 1.0, 0.0)
    r_i = lax.broadcasted_iota(jnp.int32, (TM, TM), 0)
    c_i = lax.broadcasted_iota(jnp.int32, (TM, TM), 1)
    lower = jnp.where(r_i > c_i, 1.0, 0.0).astype(BF16)
    rank = jnp.dot(lower, routed.astype(BF16), preferred_element_type=F32)
    cnt = jnp.broadcast_to(jnp.sum(routed, axis=0, keepdims=True), (SUBLANES, LANES))
    cnt_pad = jnp.floor((cnt + (PIECE - 1)) * (1.0 / PIECE)) * PIECE
    incl = cnt_pad
    for sh in (1, 2, 4):
        incl = incl + jnp.where(lane >= sh, pltpu.roll(incl, sh, 1), 0.0)
    off = incl - cnt_pad
    pos = off[0:1] + rank
    pos0 = jnp.sum(jnp.where(sel0, pos, 0.0), axis=-1, keepdims=True)
    pos1 = jnp.sum(jnp.where(sel1, pos, 0.0), axis=-1, keepdims=True)
    meta = jnp.where(lane == 0, pos0, jnp.where(lane == 1, pos1,
                     jnp.where(lane == 2, 1.0 / den, jnp.where(lane == 3, e2 / den, 0.0))))
    return meta, off, cnt_pad


def _sort_gather(h_hi, meta):
    meta_t = meta.T
    p_row = lax.broadcasted_iota(jnp.int32, (SORT_ROWS, 1), 0).astype(F32)
    hit = jnp.logical_or(p_row == meta_t[0:1], p_row == meta_t[1:2])
    gather = jnp.where(hit, 1.0, 0.0).astype(BF16)
    return jnp.dot(gather, h_hi, preferred_element_type=F32).astype(BF16)


def _moe_sort(x, norm_g, wr_cat, layer, moe_index):
    return pl.pallas_call(
        _moe_sort_kernel,
        grid=(NT_ALL // SORT_TILES_PER_STEP,),
        in_specs=[
            pl.BlockSpec((SORT_TILES_PER_STEP * TM, D_MODEL), lambda i: (i, 0)),
            _layer_block(layer, 1, D_MODEL),
            _layer_block(moe_index, D_MODEL, 2 * LANES),
        ],
        out_specs=[
            pl.BlockSpec((SORT_TILES_PER_STEP * SORT_ROWS, D_MODEL), lambda i: (i, 0)),
            pl.BlockSpec((SORT_TILES_PER_STEP * TM, LANES), lambda i: (i, 0)),
            pl.BlockSpec((SORT_TILES_PER_STEP, SUBLANES, LANES), lambda i: (i, 0, 0)),
            pl.BlockSpec((SORT_TILES_PER_STEP, SUBLANES, LANES), lambda i: (i, 0, 0)),
        ],
        out_shape=[
            jax.ShapeDtypeStruct((NT_ALL * SORT_ROWS, D_MODEL), BF16),
            jax.ShapeDtypeStruct((T_ALL, LANES), F32),
            jax.ShapeDtypeStruct((NT_ALL, SUBLANES, LANES), jnp.int32),
            jax.ShapeDtypeStruct((NT_ALL, SUBLANES, LANES), jnp.int32),
        ],
        compiler_params=pltpu.CompilerParams(
            dimension_semantics=("arbitrary",), vmem_limit_bytes=VMEM_LIMIT),
        name="moe_sort",
    )(x, norm_g, wr_cat)


def _moe_experts_kernel(off_ref, cnt_ref, hs_hbm, wg_ref, wu_ref, wrem_ref, wd_ref, ys_hbm,
                        buf, lhs, sem_in, sem_out):
    s = pl.program_id(0)
    e = pl.program_id(1)
    row0 = pl.multiple_of(s * SUPER_ROWS, PIECE)
    is_first = e == 0
    is_last = e == N_EXPERTS - 1

    def copy_in(t):
        r = pl.multiple_of(t * SORT_ROWS, PIECE)
        return pltpu.make_async_copy(hs_hbm.at[pl.ds(row0 + r, SORT_ROWS)], buf.at[pl.ds(r, SORT_ROWS)],
                                     sem_in.at[t])

    def copy_out(t):
        r = pl.multiple_of(t * SORT_ROWS, PIECE)
        return pltpu.make_async_copy(buf.at[pl.ds(r, SORT_ROWS)], ys_hbm.at[pl.ds(row0 + r, SORT_ROWS)],
                                     sem_out.at[t])

    def for_tiles(lo, hi, fn):
        hi = jnp.maximum(lo, hi)

        def body(t, carry):
            fn(t)
            return carry

        lax.fori_loop(lo, hi, body, 0)
        return hi

    @pl.when(is_first)
    def _():
        for t in range(TILES_PER_SUPER):
            copy_in(t).start()

    starts, cums = [], [0]
    for i in range(TILES_PER_SUPER):
        idx = (s * TILES_PER_SUPER + i) * N_EXPERTS + e
        starts.append(i * SORT_ROWS + off_ref[idx])
        cums.append(cums[-1] + cnt_ref[idx] // PIECE)
    n_pieces = cums[-1]

    def locate(p):
        pc = jnp.minimum(p, n_pieces - 1)
        row = starts[0] + PIECE * pc
        tile = jnp.int32(0)
        for i in range(1, TILES_PER_SUPER):
            inside = pc >= cums[i]
            row = jnp.where(inside, starts[i] + PIECE * (pc - cums[i]), row)
            tile = jnp.where(inside, i, tile)
        return row, tile

    def piece_rows(q):
        src, dst, tiles = [], [], []
        for j in range(PIECES_PER_CHUNK):
            p = q * PIECES_PER_CHUNK + j
            row, tile = locate(p)
            src.append(row)
            dst.append(jnp.where(p < n_pieces, row, SUPER_ROWS))
            tiles.append(tile)
        return tuple(src), tuple(dst), tiles[0], tiles[-1]

    def expert_ffn(xs, between=lambda: None):
        gate = jnp.dot(xs, wg_ref[0, :, :FF_MAIN], preferred_element_type=F32)
        up = jnp.dot(xs, wu_ref[0, :, :FF_MAIN], preferred_element_type=F32)
        rem = jnp.dot(xs, wrem_ref[0], preferred_element_type=F32)
        extra = between()
        act = jnp.concatenate([jax.nn.silu(gate) * up,
                               jax.nn.silu(rem[:, :FF_REM]) * rem[:, FF_REM:]], axis=1).astype(BF16)
        return jnp.dot(act, wd_ref[0], preferred_element_type=F32).astype(BF16), extra

    def chunk_body(q, carry):
        (src, dst, _, last_tile), arrived, sent = carry
        arrived = for_tiles(arrived, jnp.where(is_first, last_tile + 1, 0), lambda t: copy_in(t).wait())
        for j in range(PIECES_PER_CHUNK):
            lhs[j * PIECE:(j + 1) * PIECE, :] = buf[pl.ds(pl.multiple_of(src[j], PIECE), PIECE), :]
        y, nxt = expert_ffn(lhs[...], lambda: piece_rows(q + 1))
        for j in range(PIECES_PER_CHUNK):
            buf[pl.ds(pl.multiple_of(dst[j], PIECE), PIECE), :] = y[j * PIECE:(j + 1) * PIECE]
        sent = for_tiles(sent, jnp.where(is_last, nxt[2], 0), lambda t: copy_out(t).start())
        return nxt, arrived, sent

    half = PIECES_PER_CHUNK // 2
    n_chunks = n_pieces // PIECES_PER_CHUNK + jnp.where(n_pieces % PIECES_PER_CHUNK > half, 1, 0)
    _, arrived, sent = lax.fori_loop(0, n_chunks, chunk_body, (piece_rows(0), jnp.int32(0), jnp.int32(0)))
    for_tiles(arrived, jnp.where(is_first, TILES_PER_SUPER, 0), lambda t: copy_in(t).wait())

    first_left = n_chunks * PIECES_PER_CHUNK

    @pl.when(first_left < n_pieces)
    def _():
        dst = []
        for j in range(half):
            row, _ = locate(first_left + j)
            dst.append(jnp.where(first_left + j < n_pieces, row, SUPER_ROWS))
            lhs[j * PIECE:(j + 1) * PIECE, :] = buf[pl.ds(pl.multiple_of(row, PIECE), PIECE), :]
        y, _ = expert_ffn(lhs[:half * PIECE, :])
        for j in range(half):
            buf[pl.ds(pl.multiple_of(dst[j], PIECE), PIECE), :] = y[j * PIECE:(j + 1) * PIECE]

    for_tiles(sent, jnp.where(is_last, TILES_PER_SUPER, 0), lambda t: copy_out(t).start())
    for_tiles(0, jnp.where(is_last, TILES_PER_SUPER, 0), lambda t: copy_out(t).wait())


def _moe_experts(off, cnt, hs, wg, wu, wrem, wd, moe_index):
    expert = lambda *shape: pl.BlockSpec((None, 1) + shape, lambda s, e, *_: (moe_index, e, 0, 0))
    grid_spec = pltpu.PrefetchScalarGridSpec(
        num_scalar_prefetch=2,
        grid=(N_SUPER, N_EXPERTS),
        in_specs=[
            pl.BlockSpec(memory_space=pl.ANY),
            expert(D_MODEL, D_FF_EXPERT), expert(D_MODEL, D_FF_EXPERT), expert(D_MODEL, 2 * FF_REM),
            expert(D_FF_EXPERT, D_MODEL),
        ],
        out_specs=pl.BlockSpec(memory_space=pl.ANY),
        scratch_shapes=[pltpu.VMEM((SUPER_ROWS + PIECE, D_MODEL), BF16),
                        pltpu.VMEM((CHUNK_ROWS, D_MODEL), BF16),
                        pltpu.SemaphoreType.DMA((TILES_PER_SUPER,)),
                        pltpu.SemaphoreType.DMA((TILES_PER_SUPER,))],
    )
    return pl.pallas_call(
        _moe_experts_kernel,
        grid_spec=grid_spec,
        out_shape=jax.ShapeDtypeStruct((NT_ALL * SORT_ROWS, D_MODEL), BF16),
        compiler_params=pltpu.CompilerParams(
            dimension_semantics=("arbitrary", "arbitrary"), vmem_limit_bytes=VMEM_LIMIT),
        name="moe_experts",
    )(off, cnt, hs, wg, wu, wrem, wd)


def _unsorted_residual(x_ref, ys_ref, meta_ref):
    meta = meta_ref[...]
    p_col = lax.broadcasted_iota(jnp.int32, (1, SORT_ROWS), 1).astype(F32)
    scatter = (jnp.where(p_col == meta[:, 0:1], meta[:, 2:3], 0.0)
               + jnp.where(p_col == meta[:, 1:2], meta[:, 3:4], 0.0)).astype(BF16)
    return x_ref[...] + jnp.dot(scatter, ys_ref[...], preferred_element_type=F32)


def _moe_unsort_kernel(x_ref, ys_ref, meta_ref, xo_ref):
    xo_ref[...] = _unsorted_residual(x_ref, ys_ref, meta_ref)


def _moe_unsort_final_kernel(x_ref, ys_ref, meta_ref, g_ref, yp_ref, ysm_ref):
    i = pl.program_id(0)
    y = _rms(_unsorted_residual(x_ref, ys_ref, meta_ref), g_ref[...])

    @pl.when(i < NT_PROMPT)
    def _():
        yp_ref[...] = y

    @pl.when(i == NT_PROMPT)
    def _():
        ysm_ref[...] = y


def _moe_unsort(x, ys, meta, final_g=None):
    in_specs = [
        pl.BlockSpec((TM, D_MODEL), lambda i: (i, 0)),
        pl.BlockSpec((SORT_ROWS, D_MODEL), lambda i: (i, 0)),
        pl.BlockSpec((TM, LANES), lambda i: (i, 0)),
    ]
    params = pltpu.CompilerParams(dimension_semantics=("arbitrary",), vmem_limit_bytes=VMEM_LIMIT)
    if final_g is None:
        return pl.pallas_call(
            _moe_unsort_kernel,
            grid=(NT_ALL,),
            in_specs=in_specs,
            out_specs=pl.BlockSpec((TM, D_MODEL), lambda i: (i, 0)),
            out_shape=jax.ShapeDtypeStruct((T_ALL, D_MODEL), F32),
            compiler_params=params,
            name="moe_unsort",
        )(x, ys, meta)
    return pl.pallas_call(
        _moe_unsort_final_kernel,
        grid=(NT_ALL,),
        in_specs=in_specs + [pl.BlockSpec((1, D_MODEL), lambda i: (0, 0))],
        out_specs=[pl.BlockSpec((TM, D_MODEL), lambda i: (jnp.minimum(i, NT_PROMPT - 1), 0)),
                   pl.BlockSpec((TM, D_MODEL), lambda i: (0, 0))],
        out_shape=[jax.ShapeDtypeStruct((T_PROMPT, D_MODEL), F32),
                   jax.ShapeDtypeStruct((T_SAMPLE, D_MODEL), F32)],
        compiler_params=params,
        name="moe_unsort_final",
    )(x, ys, meta, final_g)


def _moe(x, norm_g, wr_cat, wg, wu, wrem, wd, layer, moe_index, final_g=None):
    hs, meta, off, cnt = _moe_sort(x, norm_g, wr_cat, layer, moe_index)
    off = off[:, 0, :N_EXPERTS].reshape(-1)
    cnt = cnt[:, 0, :N_EXPERTS].reshape(-1)
    ys = _moe_experts(off, cnt, hs, wg, wu, wrem, wd, moe_index)
    return _moe_unsort(x, ys, meta, final_g)


def kernel(x_prompt, x_sample, cache_win_k, cache_win_v, attn_norm_g, w_in, b_in, chunk_ln_g, chunk_ln_b, w_spatial, b_spatial, attn_sinks, mix_norm_a_g, mix_norm_b_g, w_out, ffn_norm_g, ffn_w_gate, ffn_w_up, ffn_w_down, router_w, expert_w_gate, expert_w_up, expert_w_down, final_norm_g):
    w_in_p = jnp.concatenate([w_in[..., :Q0], _g_major(w_in[..., Q0:K0], -1), w_in[..., K0:]], axis=-1).astype(BF16)
    b_in_p = jnp.concatenate([b_in[..., :Q0], _g_major(b_in[..., Q0:K0], -1), b_in[..., K0:]], axis=-1)
    w_out_p = jnp.concatenate([w_out[:, :GMLP_WIDTH], _g_major(w_out[:, GMLP_WIDTH:], 1)], axis=1).astype(BF16)
    g_b_p = _g_major(mix_norm_b_g, 1)
    bs_prompt = jnp.repeat(jnp.transpose(b_spatial, (0, 2, 1)), GMLP_HEAD_DIM, axis=2)
    bs_sample = jnp.tile(bs_prompt[:, :DEC_SEQ], (1, SUBLANES // DEC_SEQ, 1))
    ws_corner = w_spatial[:, :, :DEC_SEQ, :DEC_SEQ]
    ws_small = jnp.stack([jnp.stack([ws_corner[:, :, t, max(t - d, 0)] for t in range(DEC_SEQ)], axis=-1)
                          for d in range(DEC_SEQ)], axis=1)
    ws_sample = jnp.tile(jnp.repeat(jnp.transpose(ws_small, (0, 1, 3, 2)), GMLP_HEAD_DIM, axis=3),
                         (1, 1, SUBLANES // DEC_SEQ, 1))
    sink_row = jnp.repeat(attn_sinks, SEQS_PER_TILE * DEC_SEQ, axis=1)[:, None, :]
    bias_prompt = jnp.asarray(_prompt_bias())
    bias_sample = jnp.asarray(_sample_bias())
    wr_pad = jnp.pad(router_w, ((0, 0), (0, 0), (0, LANES - N_EXPERTS)))
    wr_hi = wr_pad.astype(BF16)
    wr_cat = jnp.concatenate([wr_hi, (wr_pad - wr_hi.astype(F32)).astype(BF16)], axis=-1)

    ck = cache_win_k.reshape(DEPTH, DEC_BATCH, CACHE_W, KV_WIDTH)
    cv = cache_win_v.reshape(DEPTH, DEC_BATCH, CACHE_W, KV_WIDTH)

    lw = dict(
        norm_g=attn_norm_g[:, None, :], w_in=w_in_p, b_in=b_in_p[:, None, :],
        ln_g=chunk_ln_g.reshape(DEPTH, 1, GMLP_WIDTH), ln_b=chunk_ln_b.reshape(DEPTH, 1, GMLP_WIDTH),
        w_s=w_spatial, b_s_prompt=bs_prompt, w_s_sample=ws_sample, b_s_sample=bs_sample,
        sink_row=sink_row, bias_prompt=bias_prompt, bias_sample=bias_sample,
        g_a=mix_norm_a_g[:, None, :], g_b=g_b_p[:, None, :], w_out=w_out_p)
    ffn_g = ffn_norm_g[:, None, :]

    x = None
    kp, vp, cvs = [], [], []
    states = ()
    for l in range(DEPTH):
        lw["sinks"] = attn_sinks[l]
        i = l // 2
        cast = (i, (ffn_w_gate, ffn_w_up, ffn_w_down)) if l % 2 == 0 else None
        x_in = x_prompt.reshape(T_PROMPT, D_MODEL) if l == 0 else x
        x_new, k_p, v_p, *ffn_bf = _mixer_prompt(x_in, lw, l, cast)
        if l == 0:
            x_new, *states, cv_s = _mixer_sample(x_sample.reshape(T_SAMPLE, D_MODEL), 0, x_new, ck, cv, l, lw)
        else:
            x_new, *states, cv_s = _mixer_sample(x, NT_PROMPT, x_new, ck, cv, l, lw, states)
        kp.append(k_p)
        vp.append(v_p)
        cvs.append(cv_s)
        if l % 2 == 0:
            ffn_wg, ffn_wu, ffn_wd = (w[None] for w in ffn_bf)
            x, ex_wg, ex_wu, ex_wd = _swiglu(x_new, ffn_g, ffn_wg, ffn_wu, ffn_wd, l, 0,
                                             (expert_w_gate, expert_w_up, expert_w_down), i)
            ex_wrem = jnp.concatenate([ex_wg[..., FF_MAIN:], ex_wu[..., FF_MAIN:]], axis=-1)
        else:
            final_g = final_norm_g[None] if l == DEPTH - 1 else None
            x = _moe(x_new, ffn_g, wr_cat[i:i + 1], ex_wg, ex_wu, ex_wrem, ex_wd, l, 0, final_g)
    y_p, y_s = x
    win_p = (DEPTH, BATCH, WINDOW, KV_HEADS, HEAD_DIM)
    win_s = (DEPTH, DEC_BATCH, CACHE_W, KV_HEADS, HEAD_DIM)
    return (y_p.reshape(BATCH, SEQ, D_MODEL), y_s.reshape(DEC_BATCH, DEC_SEQ, D_MODEL),
            jnp.stack(kp).reshape(win_p), jnp.stack(vp).reshape(win_p),
            states[0].reshape(win_s), states[1].reshape(win_s),
            jnp.stack(cvs).reshape(DEPTH, DEC_BATCH, DEC_SEQ, GMLP_HEADS, GMLP_HEAD_DIM))
```

```python
import functools

import numpy as np
import jax
import jax.numpy as jnp
from jax import lax
from jax.experimental import pallas as pl
from jax.experimental.pallas import tpu as pltpu

D_MODEL = 1024
BATCH = 4
SEQ = 4096
DEPTH = 4
DEC_BATCH = 128
DEC_SEQ = 4
HEAD_DIM = 64
ATTN_HEADS = 8
KV_HEADS = 2
GQA_GROUP = ATTN_HEADS // KV_HEADS
ATTN_WIDTH = ATTN_HEADS * HEAD_DIM
KV_WIDTH = KV_HEADS * HEAD_DIM
WINDOW = 128
CHUNK = 128
GMLP_HEADS = 4
GMLP_HEAD_DIM = 128
GMLP_WIDTH = GMLP_HEADS * GMLP_HEAD_DIM
IN_WIDTH = 2 * GMLP_WIDTH + ATTN_WIDTH + 2 * KV_WIDTH
D_FF = 2816
N_EXPERTS = 8
D_FF_EXPERT = 1408
CACHE_W = 128
EPS = 1e-6

LANES = 128
SUBLANES = 8
MXU_WIDTH = 256
TM = 512
T_PROMPT = BATCH * SEQ
T_SAMPLE = DEC_BATCH * DEC_SEQ
T_ALL = T_PROMPT + T_SAMPLE
NT_PROMPT = T_PROMPT // TM
NT_ALL = T_ALL // TM
TILES_PER_SEQ = SEQ // TM
BLOCKS_PER_TILE = TM // WINDOW
SEQ_GROUP = 16
SEQS_PER_TILE = 4
N_SEQ_GROUPS = DEC_BATCH // SEQ_GROUP
KEYS_PAD = CACHE_W + SUBLANES
PIECE = 2 * SUBLANES
SORT_ROWS = 2 * TM + N_EXPERTS * PIECE
TILES_PER_SUPER = 11
N_SUPER = NT_ALL // TILES_PER_SUPER
SUPER_ROWS = TILES_PER_SUPER * SORT_ROWS
SORT_TILES_PER_STEP = 3
CHUNK_ROWS = MXU_WIDTH
FF_REM = D_FF_EXPERT % MXU_WIDTH
FF_MAIN = D_FF_EXPERT - FF_REM
PIECES_PER_CHUNK = CHUNK_ROWS // PIECE
VMEM_LIMIT = 56 * 1024 * 1024

F32 = jnp.float32
BF16 = jnp.bfloat16
NEG_INF = float("-inf")

Q0 = 2 * GMLP_WIDTH
K0 = Q0 + ATTN_WIDTH
V0 = K0 + KV_WIDTH


def _g_major(a, axis):
    axis = axis % a.ndim
    shape = a.shape
    a = a.reshape(shape[:axis] + (KV_HEADS, GQA_GROUP, HEAD_DIM) + shape[axis + 1:])
    return jnp.swapaxes(a, axis, axis + 1).reshape(shape)


def _slope(kv, g):
    return 2.0 ** (-8.0 * (kv * GQA_GROUP + g + 1) / ATTN_HEADS)


def _prompt_bias():
    c = np.arange(WINDOW)[:, None]
    qi = np.arange(WINDOW)[None, :]
    dist = np.where(c > qi, qi + WINDOW - c, qi - c)
    out = np.zeros((2, KV_HEADS, WINDOW, GQA_GROUP * WINDOW), np.float32)
    for var in range(2):
        ok = ~((var == 1) & (c > qi))
        for kv in range(KV_HEADS):
            for g in range(GQA_GROUP):
                out[var, kv, :, g * WINDOW:(g + 1) * WINDOW] = np.where(ok, -_slope(kv, g) * dist, NEG_INF)
    return out


def _sample_bias():
    out = np.zeros((KEYS_PAD, LANES), np.float32)
    j = np.arange(KEYS_PAD)
    kpos = j - CACHE_W
    for kv in range(KV_HEADS):
        for g in range(GQA_GROUP):
            for s in range(SEQS_PER_TILE):
                for t in range(DEC_SEQ):
                    col = ((kv * GQA_GROUP + g) * SEQS_PER_TILE + s) * DEC_SEQ + t
                    dist = t - kpos
                    ok = (dist >= 0) & (dist < WINDOW) & (j < CACHE_W + DEC_SEQ)
                    out[:, col] = np.where(ok, -_slope(kv, g) * dist, NEG_INF)
    return out


def _rms(x, g):
    return x * lax.rsqrt(jnp.mean(x * x, axis=-1, keepdims=True) + EPS) * g


def _gelu(x):
    return 0.5 * x * (1.0 + lax.erf(x * (2.0 ** -0.5)))


def _gmlp_v(z_v, lng_ref, lnb_ref):
    out = []
    for hh in range(GMLP_HEADS):
        sl = slice(hh * LANES, (hh + 1) * LANES)
        vh = _gelu(z_v[:, sl])
        vc = vh - jnp.mean(vh, axis=-1, keepdims=True)
        y = vc * lax.rsqrt(jnp.mean(vc * vc, axis=-1, keepdims=True) + EPS)
        out.append(y * lng_ref[:, sl] + lnb_ref[:, sl])
    return out


def _dot_nt(a, b):
    return lax.dot_general(a, b, (((1,), (1,)), ((), ())), preferred_element_type=F32)


N_MIXER_INPUTS = 13


def _mixer_prompt_kernel(*refs, n_cast):
    i = pl.program_id(0)
    n_in = N_MIXER_INPUTS + n_cast
    tile_refs = refs[:N_MIXER_INPUTS] + refs[n_in:n_in + 3] + refs[n_in + 3 + n_cast:]

    @pl.when(i < NT_PROMPT)
    def _():
        _mixer_prompt_tile(*tile_refs)

    @pl.when(i == NT_PROMPT)
    def _():
        xo_ref = refs[n_in]
        xo_ref[...] = jnp.zeros_like(xo_ref)

    for src, dst in zip(refs[N_MIXER_INPUTS:n_in], refs[n_in + 3:n_in + 3 + n_cast]):
        dst[...] = src[...].astype(BF16)


def _mixer_prompt_tile(x_ref, ng_ref, win_ref, bin_ref, lng_ref, lnb_ref, ws_ref, bs_ref,
                       sink_ref, bias_ref, ga_ref, gb_ref, wout_ref,
                       xo_ref, ks_ref, vs_ref, kprev_ref, vprev_ref):
    i = pl.program_id(0)

    @pl.when(i == 0)
    def _():
        kprev_ref[...] = jnp.zeros_like(kprev_ref)
        vprev_ref[...] = jnp.zeros_like(vprev_ref)

    x = x_ref[...]
    h = _rms(x, ng_ref[...]).astype(BF16)

    def in_proj(lo, hi):
        return jnp.dot(h, win_ref[:, lo:hi], preferred_element_type=F32) + bin_ref[:, lo:hi]

    z_att = in_proj(Q0, IN_WIDTH)
    q = (z_att[:, :ATTN_WIDTH] * (HEAD_DIM ** -0.5)).astype(BF16)
    k = z_att[:, ATTN_WIDTH:ATTN_WIDTH + KV_WIDTH]
    val = z_att[:, ATTN_WIDTH + KV_WIDTH:]
    k_bf = k.astype(BF16)
    lane = lax.broadcasted_iota(jnp.int32, (1, LANES), 1)
    low = lane < HEAD_DIM
    first = jnp.where(i % TILES_PER_SEQ == 0, 1, 0)
    c_i = lax.broadcasted_iota(jnp.int32, (WINDOW, WINDOW), 0)
    q_i = lax.broadcasted_iota(jnp.int32, (WINDOW, WINDOW), 1)
    from_prev = jnp.concatenate([c_i > q_i] * GQA_GROUP, axis=1)
    g_lane = lax.broadcasted_iota(jnp.int32, (1, GQA_GROUP * WINDOW), 1) // WINDOW
    vt_bf = val.T.astype(BF16)
    vt_all = jnp.concatenate([vprev_ref[...], vt_bf], axis=1)
    k_all = jnp.concatenate([kprev_ref[...], k_bf], axis=0)
    zero = jnp.zeros((), BF16)
    k_heads = [jnp.where(low, k_all, zero), jnp.where(low, zero, k_all)]
    sinks = []
    for kv in range(KV_HEADS):
        sink = jnp.full((1, GQA_GROUP * WINDOW), sink_ref[kv * GQA_GROUP], F32)
        for g in range(1, GQA_GROUP):
            sink = jnp.where(g_lane == g, sink_ref[kv * GQA_GROUP + g], sink)
        sinks.append(sink)

    def attention_block(c):
        rows = slice(c * WINDOW, (c + 1) * WINDOW)
        band = slice(c * WINDOW, (c + 2) * WINDOW)
        o_t = []
        q_rows = jnp.concatenate([q[rows, g * LANES:(g + 1) * LANES] for g in range(GQA_GROUP)], axis=0)
        keys = jnp.concatenate([k_heads[0][band], k_heads[1][band]], axis=0)
        s_both = _dot_nt(keys, q_rows)
        for kv in range(KV_HEADS):
            s2 = s_both[kv * 2 * WINDOW:(kv + 1) * 2 * WINDOW]
            bias = bias_ref[first, kv] if c == 0 else bias_ref[0, kv]
            s = jnp.where(from_prev, s2[:WINDOW], s2[WINDOW:]) + bias
            m = jnp.maximum(jnp.max(s, axis=0, keepdims=True), sinks[kv])
            p = jnp.exp(s - m)
            den = jnp.sum(p, axis=0, keepdims=True) + jnp.exp(sinks[kv] - m)
            p2 = jnp.concatenate([jnp.where(from_prev, p, 0.0), jnp.where(from_prev, 0.0, p)],
                                 axis=0).astype(BF16)
            o = jnp.dot(vt_all[kv * HEAD_DIM:(kv + 1) * HEAD_DIM, band], p2,
                        preferred_element_type=F32)
            o_t.append(o / den)
        bt = jnp.concatenate(o_t, axis=0)
        return jnp.concatenate(
            [bt[:, g * WINDOW:(g + 1) * WINDOW].T for g in range(GQA_GROUP)], axis=1)

    z_u = in_proj(0, GMLP_WIDTH)
    b_rows = [attention_block(0)]
    z_v = in_proj(GMLP_WIDTH, Q0)
    b_rows.append(attention_block(1))
    u = _gelu(z_u)
    b_rows.append(attention_block(2))
    v_heads = _gmlp_v(z_v, lng_ref, lnb_ref)
    row = lax.broadcasted_iota(jnp.int32, (CHUNK, CHUNK), 0)
    col = lax.broadcasted_iota(jnp.int32, (CHUNK, CHUNK), 1)
    a_cols = []
    for hh in range(GMLP_HEADS):
        w = jnp.where(row >= col, ws_ref[hh], 0.0).astype(BF16)
        v_cat = jnp.concatenate(
            [v_heads[hh][c * CHUNK:(c + 1) * CHUNK] for c in range(BLOCKS_PER_TILE)], axis=1)
        mixed = jnp.dot(w, v_cat.astype(BF16), preferred_element_type=F32)
        bias = bs_ref[:, hh * LANES:(hh + 1) * LANES]
        a_cols.append(jnp.concatenate(
            [mixed[:, c * LANES:(c + 1) * LANES] + bias for c in range(BLOCKS_PER_TILE)], axis=0))
    a = u * jnp.concatenate(a_cols, axis=1)
    a_n = _rms(a, ga_ref[...]).astype(BF16)
    b_rows.append(attention_block(3))
    out = x + jnp.dot(a_n, wout_ref[:GMLP_WIDTH, :], preferred_element_type=F32)
    b_n = _rms(jnp.concatenate(b_rows, axis=0), gb_ref[...]).astype(BF16)
    xo_ref[...] = out + jnp.dot(b_n, wout_ref[GMLP_WIDTH:, :], preferred_element_type=F32)

    kprev_ref[...] = k_bf[TM - WINDOW:]
    vprev_ref[...] = vt_bf[:, TM - WINDOW:]
    ks_ref[0] = k[TM - WINDOW:]
    vs_ref[0] = val[TM - WINDOW:]


def _layer_block(layer, *shape, **kwargs):
    return pl.BlockSpec((None,) + shape, lambda *_: (layer,) + (0,) * len(shape), **kwargs)


def _mixer_prompt(x, lw, layer, cast=None):
    const = lambda *shape: pl.BlockSpec(shape, lambda i: (0,) * len(shape))
    per_layer = functools.partial(_layer_block, layer)
    cast_in, cast_specs, cast_out_specs, cast_shapes = [], [], [], []
    if cast is not None:
        ffn_index, weights = cast
        for w in weights:
            rows, cols = w.shape[1:]
            n_slabs = NT_PROMPT if (rows // NT_PROMPT) % PIECE == 0 else NT_PROMPT // 2
            slab = lambda i, n=n_slabs: jnp.minimum(i, n - 1)
            cast_in.append(w)
            cast_specs.append(pl.BlockSpec((None, rows // n_slabs, cols),
                                           lambda i, slab=slab: (ffn_index, slab(i), 0)))
            cast_out_specs.append(pl.BlockSpec((rows // n_slabs, cols), lambda i, slab=slab: (slab(i), 0)))
            cast_shapes.append(jax.ShapeDtypeStruct((rows, cols), BF16))
    return pl.pallas_call(
        functools.partial(_mixer_prompt_kernel, n_cast=len(cast_in)),
        grid=(NT_ALL,),
        in_specs=[
            pl.BlockSpec((TM, D_MODEL), lambda i: (jnp.minimum(i, NT_PROMPT - 1), 0)),
            per_layer(1, D_MODEL), per_layer(D_MODEL, IN_WIDTH), per_layer(1, IN_WIDTH),
            per_layer(1, GMLP_WIDTH), per_layer(1, GMLP_WIDTH),
            per_layer(GMLP_HEADS, CHUNK, CHUNK), per_layer(CHUNK, GMLP_WIDTH),
            pl.BlockSpec(memory_space=pltpu.SMEM),
            const(2, KV_HEADS, WINDOW, GQA_GROUP * WINDOW),
            per_layer(1, GMLP_WIDTH), per_layer(1, ATTN_WIDTH), per_layer(D_MODEL, D_MODEL),
        ] + cast_specs,
        out_specs=[
            pl.BlockSpec((TM, D_MODEL), lambda i: (i, 0)),
            pl.BlockSpec((1, WINDOW, KV_WIDTH), lambda i: (jnp.minimum(i // TILES_PER_SEQ, BATCH - 1), 0, 0)),
            pl.BlockSpec((1, WINDOW, KV_WIDTH), lambda i: (jnp.minimum(i // TILES_PER_SEQ, BATCH - 1), 0, 0)),
        ] + cast_out_specs,
        out_shape=[
            jax.ShapeDtypeStruct((T_ALL, D_MODEL), F32),
            jax.ShapeDtypeStruct((BATCH, WINDOW, KV_WIDTH), F32),
            jax.ShapeDtypeStruct((BATCH, WINDOW, KV_WIDTH), F32),
        ] + cast_shapes,
        scratch_shapes=[pltpu.VMEM((WINDOW, KV_WIDTH), BF16), pltpu.VMEM((WINDOW, KV_WIDTH), BF16)],
        compiler_params=pltpu.CompilerParams(
            dimension_semantics=("arbitrary",), vmem_limit_bytes=VMEM_LIMIT),
        name="mixer_prompt",
    )(x, lw["norm_g"], lw["w_in"], lw["b_in"], lw["ln_g"], lw["ln_b"], lw["w_s"], lw["b_s_prompt"],
      lw["sinks"], lw["bias_prompt"], lw["g_a"], lw["g_b"], lw["w_out"], *cast_in)


def _mixer_sample_kernel(*refs, first_layer):
    n_alias = 1 if first_layer else 3
    _mixer_sample_body(refs[0], *refs[1 + n_alias:], first_layer=first_layer)


def _mixer_sample_body(x_ref, ck_ref, cv_ref, ng_ref, win_ref, bin_ref, lng_ref, lnb_ref,
                       wsd_ref, bsd_ref, sinkrow_ref, bias_ref, ga_ref, gb_ref, wout_ref,
                       xo_ref, ko_ref, vo_ref, cvo_ref,
                       q_s, k_s, v_s, a_s, b_s, kfull, vfull, *, first_layer):
    j = pl.program_id(0)

    @pl.when(j == 0)
    def _():
        h = _rms(x_ref[...], ng_ref[...]).astype(BF16)
        z = jnp.dot(h, win_ref[...], preferred_element_type=F32) + bin_ref[...]
        u = _gelu(z[:, :GMLP_WIDTH])
        v = jnp.concatenate(_gmlp_v(z[:, GMLP_WIDTH:Q0], lng_ref, lnb_ref), axis=1)
        cvo_ref[...] = v
        t_row = lax.broadcasted_iota(jnp.int32, (SUBLANES, 1), 0) % DEC_SEQ

        def tile_rows(pat):
            return jnp.broadcast_to(pat[None], (TM // SUBLANES,) + pat.shape).reshape(TM, pat.shape[-1])

        mixed = tile_rows(bsd_ref[...])
        for d in range(DEC_SEQ):
            vd = v if d == 0 else pltpu.roll(v, d, 0)
            mixed = mixed + tile_rows(jnp.where(t_row >= d, wsd_ref[d], 0.0)) * vd
        a_s[...] = _rms(u * mixed, ga_ref[...]).astype(BF16)
        q_s[...] = z[:, Q0:K0] * (HEAD_DIM ** -0.5)
        k_s[...] = z[:, K0:V0]
        v_s[...] = z[:, V0:]
        kfull[:, CACHE_W + DEC_SEQ:, :] = jnp.zeros((SEQ_GROUP, KEYS_PAD - CACHE_W - DEC_SEQ, KV_WIDTH), F32)
        vfull[:, CACHE_W + DEC_SEQ:, :] = jnp.zeros((SEQ_GROUP, KEYS_PAD - CACHE_W - DEC_SEQ, KV_WIDTH), F32)

    base = pl.multiple_of(j * (SEQ_GROUP * DEC_SEQ), SEQ_GROUP * DEC_SEQ)
    kfull[:, :CACHE_W, :] = jnp.swapaxes(ck_ref[0], 1, 2)
    vfull[:, :CACHE_W, :] = jnp.swapaxes(cv_ref[0], 1, 2)
    k_new = k_s[pl.ds(base, SEQ_GROUP * DEC_SEQ), :]
    v_new = v_s[pl.ds(base, SEQ_GROUP * DEC_SEQ), :]
    for s in range(SEQ_GROUP):
        kfull[s, CACHE_W:CACHE_W + DEC_SEQ, :] = k_new[s * DEC_SEQ:(s + 1) * DEC_SEQ]
        vfull[s, CACHE_W:CACHE_W + DEC_SEQ, :] = v_new[s * DEC_SEQ:(s + 1) * DEC_SEQ]
    ko_ref[0] = kfull[:, DEC_SEQ:DEC_SEQ + CACHE_W, :]
    vo_ref[0] = vfull[:, DEC_SEQ:DEC_SEQ + CACHE_W, :]
    if first_layer:
        ko_ref[1:] = jnp.zeros((DEPTH - 1,) + ko_ref.shape[1:], F32)
        vo_ref[1:] = jnp.zeros((DEPTH - 1,) + vo_ref.shape[1:], F32)

    lane = lax.broadcasted_iota(jnp.int32, (1, LANES), 1)
    low = lane < HEAD_DIM
    col_seq = (lane // DEC_SEQ) % SEQS_PER_TILE
    row_seq = (lax.broadcasted_iota(jnp.int32, (LANES, 1), 0) // DEC_SEQ) % SEQS_PER_TILE
    sink = sinkrow_ref[...]
    zero = jnp.zeros((), BF16)
    quad_rows = SEQS_PER_TILE * DEC_SEQ
    for m in range(SEQ_GROUP // SEQS_PER_TILE):
        r0 = pl.multiple_of(base + m * quad_rows, quad_rows)
        q16 = q_s[pl.ds(r0, quad_rows), :]
        qt = jnp.concatenate(
            [jnp.where(low if kv == 0 else jnp.logical_not(low), q16[:, g * LANES:(g + 1) * LANES], 0.0)
             for kv in range(KV_HEADS) for g in range(GQA_GROUP)], axis=0).astype(BF16)
        s = bias_ref[...]
        for n in range(SEQS_PER_TILE):
            kn = kfull[SEQS_PER_TILE * m + n].astype(BF16)
            s = s + _dot_nt(kn, jnp.where(row_seq == n, qt, zero))
        mx = jnp.maximum(jnp.max(s, axis=0, keepdims=True), sink)
        p = jnp.exp(s - mx)
        p = (p / (jnp.sum(p, axis=0, keepdims=True) + jnp.exp(sink - mx))).astype(BF16)
        o = jnp.zeros((LANES, KV_WIDTH), F32)
        for n in range(SEQS_PER_TILE):
            vn = vfull[SEQS_PER_TILE * m + n].astype(BF16)
            o = o + lax.dot_general(jnp.where(col_seq == n, p, zero), vn, (((0,), (0,)), ((), ())),
                                    preferred_element_type=F32)
        b_s[pl.ds(r0, quad_rows), :] = jnp.concatenate(
            [jnp.where(low, o[g * quad_rows:(g + 1) * quad_rows],
                       o[(GQA_GROUP + g) * quad_rows:(GQA_GROUP + g + 1) * quad_rows])
             for g in range(GQA_GROUP)], axis=1)

    @pl.when(j == N_SEQ_GROUPS - 1)
    def _():
        b_n = _rms(b_s[...], gb_ref[...]).astype(BF16)
        merged = jnp.concatenate([a_s[...], b_n], axis=1)
        xo_ref[...] = x_ref[...] + jnp.dot(merged, wout_ref[...], preferred_element_type=F32)


def _mixer_sample(x, x_block, x_buf, cache_k, cache_v, layer, lw, states=()):
    const = lambda *shape: pl.BlockSpec(shape, lambda j: (0,) * len(shape))
    per_layer = functools.partial(_layer_block, layer)
    first_layer = layer == 0
    cache_spec = pl.BlockSpec((1, SEQ_GROUP, KV_WIDTH, CACHE_W), lambda j: (layer, j, 0, 0))
    state_layers = DEPTH if first_layer else 1
    state_spec = pl.BlockSpec((state_layers, SEQ_GROUP, CACHE_W, KV_WIDTH), lambda j: (layer, j, 0, 0))
    state_shape = jax.ShapeDtypeStruct((DEPTH, DEC_BATCH, CACHE_W, KV_WIDTH), F32)
    aliased = (x_buf,) + tuple(states)
    return pl.pallas_call(
        functools.partial(_mixer_sample_kernel, first_layer=first_layer),
        grid=(N_SEQ_GROUPS,),
        in_specs=[
            pl.BlockSpec((TM, D_MODEL), lambda j: (x_block, 0)),
        ] + [pl.BlockSpec(memory_space=pl.ANY)] * len(aliased) + [
            cache_spec, cache_spec,
            per_layer(1, D_MODEL), per_layer(D_MODEL, IN_WIDTH), per_layer(1, IN_WIDTH),
            per_layer(1, GMLP_WIDTH), per_layer(1, GMLP_WIDTH),
            per_layer(DEC_SEQ, SUBLANES, GMLP_WIDTH), per_layer(SUBLANES, GMLP_WIDTH),
            per_layer(1, LANES), const(KEYS_PAD, LANES),
            per_layer(1, GMLP_WIDTH), per_layer(1, ATTN_WIDTH), per_layer(D_MODEL, D_MODEL),
        ],
        out_specs=[
            pl.BlockSpec((TM, D_MODEL), lambda j: (NT_PROMPT, 0)),
            state_spec, state_spec,
            const(TM, GMLP_WIDTH),
        ],
        out_shape=[
            jax.ShapeDtypeStruct((T_ALL, D_MODEL), F32),
            state_shape, state_shape,
            jax.ShapeDtypeStruct((TM, GMLP_WIDTH), F32),
        ],
        scratch_shapes=[
            pltpu.VMEM((TM, ATTN_WIDTH), F32), pltpu.VMEM((TM, KV_WIDTH), F32),
            pltpu.VMEM((TM, KV_WIDTH), F32), pltpu.VMEM((TM, GMLP_WIDTH), BF16),
            pltpu.VMEM((TM, ATTN_WIDTH), F32),
            pltpu.VMEM((SEQ_GROUP, KEYS_PAD, KV_WIDTH), F32),
            pltpu.VMEM((SEQ_GROUP, KEYS_PAD, KV_WIDTH), F32),
        ],
        input_output_aliases={1 + n: n for n in range(len(aliased))},
        compiler_params=pltpu.CompilerParams(
            dimension_semantics=("arbitrary",), vmem_limit_bytes=VMEM_LIMIT),
        name="mixer_sample",
    )(x, *aliased, cache_k, cache_v, lw["norm_g"], lw["w_in"], lw["b_in"], lw["ln_g"], lw["ln_b"],
      lw["w_s_sample"], lw["b_s_sample"], lw["sink_row"], lw["bias_sample"],
      lw["g_a"], lw["g_b"], lw["w_out"])


def _swiglu_kernel(x_ref, ng_ref, wg_ref, wu_ref, wd_ref, eg_ref, eu_ref, ed_ref,
                   xo_ref, ego_ref, euo_ref, edo_ref):
    x = x_ref[...]
    h = _rms(x, ng_ref[...]).astype(BF16)
    gate = jnp.dot(h, wg_ref[...], preferred_element_type=F32)
    ego_ref[...] = eg_ref[...].astype(BF16)
    up = jnp.dot(h, wu_ref[...], preferred_element_type=F32)
    euo_ref[...] = eu_ref[...].astype(BF16)
    act = (jax.nn.silu(gate) * up).astype(BF16)
    xo_ref[...] = x + jnp.dot(act, wd_ref[...], preferred_element_type=F32)
    edo_ref[...] = ed_ref[...].astype(BF16)


def _swiglu(x, norm_g, wg, wu, wd, layer, ffn_index, expert_f32, expert_index):
    weights = functools.partial(_layer_block, ffn_index, pipeline_mode=pl.Buffered(1))
    slab_steps = NT_ALL - 1
    up_rows = N_EXPERTS * D_MODEL // slab_steps
    down_rows = N_EXPERTS * D_FF_EXPERT // slab_steps
    slab = lambda i: jnp.minimum(i, slab_steps - 1)
    eg, eu, ed = expert_f32
    eg = eg.reshape(-1, N_EXPERTS * D_MODEL, D_FF_EXPERT)
    eu = eu.reshape(-1, N_EXPERTS * D_MODEL, D_FF_EXPERT)
    ed = ed.reshape(-1, N_EXPERTS * D_FF_EXPERT, D_MODEL)
    x_new, eg_bf, eu_bf, ed_bf = pl.pallas_call(
        _swiglu_kernel,
        grid=(NT_ALL,),
        in_specs=[
            pl.BlockSpec((TM, D_MODEL), lambda i: (i, 0)),
            _layer_block(layer, 1, D_MODEL),
            weights(D_MODEL, D_FF), weights(D_MODEL, D_FF), weights(D_FF, D_MODEL),
            pl.BlockSpec((None, up_rows, D_FF_EXPERT), lambda i: (expert_index, slab(i), 0)),
            pl.BlockSpec((None, up_rows, D_FF_EXPERT), lambda i: (expert_index, slab(i), 0)),
            pl.BlockSpec((None, down_rows, D_MODEL), lambda i: (expert_index, slab(i), 0)),
        ],
        out_specs=[
            pl.BlockSpec((TM, D_MODEL), lambda i: (i, 0)),
            pl.BlockSpec((up_rows, D_FF_EXPERT), lambda i: (slab(i), 0)),
            pl.BlockSpec((up_rows, D_FF_EXPERT), lambda i: (slab(i), 0)),
            pl.BlockSpec((down_rows, D_MODEL), lambda i: (slab(i), 0)),
        ],
        out_shape=[
            jax.ShapeDtypeStruct((T_ALL, D_MODEL), F32),
            jax.ShapeDtypeStruct((N_EXPERTS * D_MODEL, D_FF_EXPERT), BF16),
            jax.ShapeDtypeStruct((N_EXPERTS * D_MODEL, D_FF_EXPERT), BF16),
            jax.ShapeDtypeStruct((N_EXPERTS * D_FF_EXPERT, D_MODEL), BF16),
        ],
        compiler_params=pltpu.CompilerParams(
            dimension_semantics=("arbitrary",), vmem_limit_bytes=VMEM_LIMIT),
        name="swiglu",
    )(x, norm_g, wg, wu, wd, eg, eu, ed)
    return (x_new, eg_bf.reshape(1, N_EXPERTS, D_MODEL, D_FF_EXPERT),
            eu_bf.reshape(1, N_EXPERTS, D_MODEL, D_FF_EXPERT),
            ed_bf.reshape(1, N_EXPERTS, D_FF_EXPERT, D_MODEL))


def _moe_sort_kernel(x_ref, ng_ref, wrc_ref, hs_ref, meta_ref, off_ref, cnt_ref):
    n = SORT_TILES_PER_STEP
    logits, routes = {}, {}
    for step in range(n + 2):
        if step < n:
            logits[step] = _sort_logits(x_ref[step * TM:(step + 1) * TM, :], ng_ref, wrc_ref)
        j = step - 1
        if 0 <= j < n:
            h_hi, lg = logits.pop(j)
            meta, off, cnt_pad = _sort_route(lg)
            meta_ref[j * TM:(j + 1) * TM, :] = meta
            off_ref[j] = off.astype(jnp.int32)
            cnt_ref[j] = cnt_pad.astype(jnp.int32)
            routes[j] = (h_hi, meta)
        j = step - 2
        if 0 <= j < n:
            hs_ref[j * SORT_ROWS:(j + 1) * SORT_ROWS, :] = _sort_gather(*routes.pop(j))


def _sort_logits(x, ng_ref, wrc_ref):
    h = _rms(x, ng_ref[...])
    h_hi = h.astype(BF16)
    h_lo = (h - h_hi.astype(F32)).astype(BF16)
    both = jnp.dot(h_hi, wrc_ref[...], preferred_element_type=F32)
    logits = both[:, :LANES] + both[:, LANES:] + jnp.dot(h_lo, wrc_ref[:, :LANES], preferred_element_type=F32)
    return h_hi, logits


def _sort_route(logits):
    lane = lax.broadcasted_iota(jnp.int32, (1, LANES), 1)
    logits = jnp.where(lane < N_EXPERTS, logits, NEG_INF)
    m1 = jnp.max(logits, axis=-1, keepdims=True)
    i1 = jnp.min(jnp.where(logits == m1, lane, LANES), axis=-1, keepdims=True)
    rest = jnp.where(lane == i1, NEG_INF, logits)
    m2 = jnp.max(rest, axis=-1, keepdims=True)
    i2 = jnp.min(jnp.where(rest == m2, lane, LANES), axis=-1, keepdims=True)
    e2 = jnp.exp(m2 - m1)
    den = 1.0 + e2
    sel0 = lane == i1
    sel1 = lane == i2
    routed = jnp.where(jnp.logical_or(sel0, sel1), 1.0, 0.0)
    r_i = lax.broadcasted_iota(jnp.int32, (TM, TM), 0)
    c_i = lax.broadcasted_iota(jnp.int32, (TM, TM), 1)
    lower = jnp.where(r_i > c_i, 1.0, 0.0).astype(BF16)
    rank = jnp.dot(lower, routed.astype(BF16), preferred_element_type=F32)
    cnt = jnp.broadcast_to(jnp.sum(routed, axis=0, keepdims=True), (SUBLANES, LANES))
    cnt_pad = jnp.floor((cnt + (PIECE - 1)) * (1.0 / PIECE)) * PIECE
    incl = cnt_pad
    for sh in (1, 2, 4):
        incl = incl + jnp.where(lane >= sh, pltpu.roll(incl, sh, 1), 0.0)
    off = incl - cnt_pad
    pos = off[0:1] + rank
    pos0 = jnp.sum(jnp.where(sel0, pos, 0.0), axis=-1, keepdims=True)
    pos1 = jnp.sum(jnp.where(sel1, pos, 0.0), axis=-1, keepdims=True)
    meta = jnp.where(lane == 0, pos0, jnp.where(lane == 1, pos1,
                     jnp.where(lane == 2, 1.0 / den, jnp.where(lane == 3, e2 / den, 0.0))))
    return meta, off, cnt_pad


def _sort_gather(h_hi, meta):
    meta_t = meta.T
    p_row = lax.broadcasted_iota(jnp.int32, (SORT_ROWS, 1), 0).astype(F32)
    hit = jnp.logical_or(p_row == meta_t[0:1], p_row == meta_t[1:2])
    gather = jnp.where(hit, 1.0, 0.0).astype(BF16)
    return jnp.dot(gather, h_hi, preferred_element_type=F32).astype(BF16)


def _moe_sort(x, norm_g, wr_cat, layer, moe_index):
    return pl.pallas_call(
        _moe_sort_kernel,
        grid=(NT_ALL // SORT_TILES_PER_STEP,),
        in_specs=[
            pl.BlockSpec((SORT_TILES_PER_STEP * TM, D_MODEL), lambda i: (i, 0)),
            _layer_block(layer, 1, D_MODEL),
            _layer_block(moe_index, D_MODEL, 2 * LANES),
        ],
        out_specs=[
            pl.BlockSpec((SORT_TILES_PER_STEP * SORT_ROWS, D_MODEL), lambda i: (i, 0)),
            pl.BlockSpec((SORT_TILES_PER_STEP * TM, LANES), lambda i: (i, 0)),
            pl.BlockSpec((SORT_TILES_PER_STEP, SUBLANES, LANES), lambda i: (i, 0, 0)),
            pl.BlockSpec((SORT_TILES_PER_STEP, SUBLANES, LANES), lambda i: (i, 0, 0)),
        ],
        out_shape=[
            jax.ShapeDtypeStruct((NT_ALL * SORT_ROWS, D_MODEL), BF16),
            jax.ShapeDtypeStruct((T_ALL, LANES), F32),
            jax.ShapeDtypeStruct((NT_ALL, SUBLANES, LANES), jnp.int32),
            jax.ShapeDtypeStruct((NT_ALL, SUBLANES, LANES), jnp.int32),
        ],
        compiler_params=pltpu.CompilerParams(
            dimension_semantics=("arbitrary",), vmem_limit_bytes=VMEM_LIMIT),
        name="moe_sort",
    )(x, norm_g, wr_cat)


def _moe_experts_kernel(off_ref, cnt_ref, hs_hbm, wg_ref, wu_ref, wrem_ref, wd_ref, ys_hbm,
                        buf, lhs, sem_in, sem_out):
    s = pl.program_id(0)
    e = pl.program_id(1)
    row0 = pl.multiple_of(s * SUPER_ROWS, PIECE)
    is_first = e == 0
    is_last = e == N_EXPERTS - 1

    def copy_in(t):
        r = pl.multiple_of(t * SORT_ROWS, PIECE)
        return pltpu.make_async_copy(hs_hbm.at[pl.ds(row0 + r, SORT_ROWS)], buf.at[pl.ds(r, SORT_ROWS)],
                                     sem_in.at[t])

    def copy_out(t):
        r = pl.multiple_of(t * SORT_ROWS, PIECE)
        return pltpu.make_async_copy(buf.at[pl.ds(r, SORT_ROWS)], ys_hbm.at[pl.ds(row0 + r, SORT_ROWS)],
                                     sem_out.at[t])

    def for_tiles(lo, hi, fn):
        hi = jnp.maximum(lo, hi)

        def body(t, carry):
            fn(t)
            return carry

        lax.fori_loop(lo, hi, body, 0)
        return hi

    @pl.when(is_first)
    def _():
        for t in range(TILES_PER_SUPER):
            copy_in(t).start()

    starts, cums = [], [0]
    for i in range(TILES_PER_SUPER):
        idx = (s * TILES_PER_SUPER + i) * N_EXPERTS + e
        starts.append(i * SORT_ROWS + off_ref[idx])
        cums.append(cums[-1] + cnt_ref[idx] // PIECE)
    n_pieces = cums[-1]

    def locate(p):
        pc = jnp.minimum(p, n_pieces - 1)
        row = starts[0] + PIECE * pc
        tile = jnp.int32(0)
        for i in range(1, TILES_PER_SUPER):
            inside = pc >= cums[i]
            row = jnp.where(inside, starts[i] + PIECE * (pc - cums[i]), row)
            tile = jnp.where(inside, i, tile)
        return row, tile

    def piece_rows(q):
        src, dst, tiles = [], [], []
        for j in range(PIECES_PER_CHUNK):
            p = q * PIECES_PER_CHUNK + j
            row, tile = locate(p)
            src.append(row)
            dst.append(jnp.where(p < n_pieces, row, SUPER_ROWS))
            tiles.append(tile)
        return tuple(src), tuple(dst), tiles[0], tiles[-1]

    def expert_ffn(xs, between=lambda: None):
        gate = jnp.dot(xs, wg_ref[0, :, :FF_MAIN], preferred_element_type=F32)
        up = jnp.dot(xs, wu_ref[0, :, :FF_MAIN], preferred_element_type=F32)
        rem = jnp.dot(xs, wrem_ref[0], preferred_element_type=F32)
        extra = between()
        act = jnp.concatenate([jax.nn.silu(gate) * up,
                               jax.nn.silu(rem[:, :FF_REM]) * rem[:, FF_REM:]], axis=1).astype(BF16)
        return jnp.dot(act, wd_ref[0], preferred_element_type=F32).astype(BF16), extra

    def chunk_body(q, carry):
        (src, dst, _, last_tile), arrived, sent = carry
        arrived = for_tiles(arrived, jnp.where(is_first, last_tile + 1, 0), lambda t: copy_in(t).wait())
        for j in range(PIECES_PER_CHUNK):
            lhs[j * PIECE:(j + 1) * PIECE, :] = buf[pl.ds(pl.multiple_of(src[j], PIECE), PIECE), :]
        y, nxt = expert_ffn(lhs[...], lambda: piece_rows(q + 1))
        for j in range(PIECES_PER_CHUNK):
            buf[pl.ds(pl.multiple_of(dst[j], PIECE), PIECE), :] = y[j * PIECE:(j + 1) * PIECE]
        sent = for_tiles(sent, jnp.where(is_last, nxt[2], 0), lambda t: copy_out(t).start())
        return nxt, arrived, sent

    half = PIECES_PER_CHUNK // 2
    n_chunks = n_pieces // PIECES_PER_CHUNK + jnp.where(n_pieces % PIECES_PER_CHUNK > half, 1, 0)
    _, arrived, sent = lax.fori_loop(0, n_chunks, chunk_body, (piece_rows(0), jnp.int32(0), jnp.int32(0)))
    for_tiles(arrived, jnp.where(is_first, TILES_PER_SUPER, 0), lambda t: copy_in(t).wait())

    first_left = n_chunks * PIECES_PER_CHUNK

    @pl.when(first_left < n_pieces)
    def _():
        dst = []
        for j in range(half):
            row, _ = locate(first_left + j)
            dst.append(jnp.where(first_left + j < n_pieces, row, SUPER_ROWS))
            lhs[j * PIECE:(j + 1) * PIECE, :] = buf[pl.ds(pl.multiple_of(row, PIECE), PIECE), :]
        y, _ = expert_ffn(lhs[:half * PIECE, :])
        for j in range(half):
            buf[pl.ds(pl.multiple_of(dst[j], PIECE), PIECE), :] = y[j * PIECE:(j + 1) * PIECE]

    for_tiles(sent, jnp.where(is_last, TILES_PER_SUPER, 0), lambda t: copy_out(t).start())
    for_tiles(0, jnp.where(is_last, TILES_PER_SUPER, 0), lambda t: copy_out(t).wait())


def _moe_experts(off, cnt, hs, wg, wu, wrem, wd, moe_index):
    expert = lambda *shape: pl.BlockSpec((None, 1) + shape, lambda s, e, *_: (moe_index, e, 0, 0))
    grid_spec = pltpu.PrefetchScalarGridSpec(
        num_scalar_prefetch=2,
        grid=(N_SUPER, N_EXPERTS),
        in_specs=[
            pl.BlockSpec(memory_space=pl.ANY),
            expert(D_MODEL, D_FF_EXPERT), expert(D_MODEL, D_FF_EXPERT), expert(D_MODEL, 2 * FF_REM),
            expert(D_FF_EXPERT, D_MODEL),
        ],
        out_specs=pl.BlockSpec(memory_space=pl.ANY),
        scratch_shapes=[pltpu.VMEM((SUPER_ROWS + PIECE, D_MODEL), BF16),
                        pltpu.VMEM((CHUNK_ROWS, D_MODEL), BF16),
                        pltpu.SemaphoreType.DMA((TILES_PER_SUPER,)),
                        pltpu.SemaphoreType.DMA((TILES_PER_SUPER,))],
    )
    return pl.pallas_call(
        _moe_experts_kernel,
        grid_spec=grid_spec,
        out_shape=jax.ShapeDtypeStruct((NT_ALL * SORT_ROWS, D_MODEL), BF16),
        compiler_params=pltpu.CompilerParams(
            dimension_semantics=("arbitrary", "arbitrary"), vmem_limit_bytes=VMEM_LIMIT),
        name="moe_experts",
    )(off, cnt, hs, wg, wu, wrem, wd)


def _unsorted_residual(x_ref, ys_ref, meta_ref):
    meta = meta_ref[...]
    p_col = lax.broadcasted_iota(jnp.int32, (1, SORT_ROWS), 1).astype(F32)
    scatter = (jnp.where(p_col == meta[:, 0:1], meta[:, 2:3], 0.0)
               + jnp.where(p_col == meta[:, 1:2], meta[:, 3:4], 0.0)).astype(BF16)
    return x_ref[...] + jnp.dot(scatter, ys_ref[...], preferred_element_type=F32)


def _moe_unsort_kernel(x_ref, ys_ref, meta_ref, xo_ref):
    xo_ref[...] = _unsorted_residual(x_ref, ys_ref, meta_ref)


def _moe_unsort_final_kernel(x_ref, ys_ref, meta_ref, g_ref, yp_ref, ysm_ref):
    i = pl.program_id(0)
    y = _rms(_unsorted_residual(x_ref, ys_ref, meta_ref), g_ref[...])

    @pl.when(i < NT_PROMPT)
    def _():
        yp_ref[...] = y

    @pl.when(i == NT_PROMPT)
    def _():
        ysm_ref[...] = y


def _moe_unsort(x, ys, meta, final_g=None):
    in_specs = [
        pl.BlockSpec((TM, D_MODEL), lambda i: (i, 0)),
        pl.BlockSpec((SORT_ROWS, D_MODEL), lambda i: (i, 0)),
        pl.BlockSpec((TM, LANES), lambda i: (i, 0)),
    ]
    params = pltpu.CompilerParams(dimension_semantics=("arbitrary",), vmem_limit_bytes=VMEM_LIMIT)
    if final_g is None:
        return pl.pallas_call(
            _moe_unsort_kernel,
            grid=(NT_ALL,),
            in_specs=in_specs,
            out_specs=pl.BlockSpec((TM, D_MODEL), lambda i: (i, 0)),
            out_shape=jax.ShapeDtypeStruct((T_ALL, D_MODEL), F32),
            compiler_params=params,
            name="moe_unsort",
        )(x, ys, meta)
    return pl.pallas_call(
        _moe_unsort_final_kernel,
        grid=(NT_ALL,),
        in_specs=in_specs + [pl.BlockSpec((1, D_MODEL), lambda i: (0, 0))],
        out_specs=[pl.BlockSpec((TM, D_MODEL), lambda i: (jnp.minimum(i, NT_PROMPT - 1), 0)),
                   pl.BlockSpec((TM, D_MODEL), lambda i: (0, 0))],
        out_shape=[jax.ShapeDtypeStruct((T_PROMPT, D_MODEL), F32),
                   jax.ShapeDtypeStruct((T_SAMPLE, D_MODEL), F32)],
        compiler_params=params,
        name="moe_unsort_final",
    )(x, ys, meta, final_g)


def _moe(x, norm_g, wr_cat, wg, wu, wrem, wd, layer, moe_index, final_g=None):
    hs, meta, off, cnt = _moe_sort(x, norm_g, wr_cat, layer, moe_index)
    off = off[:, 0, :N_EXPERTS].reshape(-1)
    cnt = cnt[:, 0, :N_EXPERTS].reshape(-1)
    ys = _moe_experts(off, cnt, hs, wg, wu, wrem, wd, moe_index)
    return _moe_unsort(x, ys, meta, final_g)


def kernel(x_prompt, x_sample, cache_win_k, cache_win_v, attn_norm_g, w_in, b_in, chunk_ln_g, chunk_ln_b, w_spatial, b_spatial, attn_sinks, mix_norm_a_g, mix_norm_b_g, w_out, ffn_norm_g, ffn_w_gate, ffn_w_up, ffn_w_down, router_w, expert_w_gate, expert_w_up, expert_w_down, final_norm_g):
    w_in_p = jnp.concatenate([w_in[..., :Q0], _g_major(w_in[..., Q0:K0], -1), w_in[..., K0:]], axis=-1).astype(BF16)
    b_in_p = jnp.concatenate([b_in[..., :Q0], _g_major(b_in[..., Q0:K0], -1), b_in[..., K0:]], axis=-1)
    w_out_p = jnp.concatenate([w_out[:, :GMLP_WIDTH], _g_major(w_out[:, GMLP_WIDTH:], 1)], axis=1).astype(BF16)
    g_b_p = _g_major(mix_norm_b_g, 1)
    bs_prompt = jnp.repeat(jnp.transpose(b_spatial, (0, 2, 1)), GMLP_HEAD_DIM, axis=2)
    bs_sample = jnp.tile(bs_prompt[:, :DEC_SEQ], (1, SUBLANES // DEC_SEQ, 1))
    ws_corner = w_spatial[:, :, :DEC_SEQ, :DEC_SEQ]
    ws_small = jnp.stack([jnp.stack([ws_corner[:, :, t, max(t - d, 0)] for t in range(DEC_SEQ)], axis=-1)
                          for d in range(DEC_SEQ)], axis=1)
    ws_sample = jnp.tile(jnp.repeat(jnp.transpose(ws_small, (0, 1, 3, 2)), GMLP_HEAD_DIM, axis=3),
                         (1, 1, SUBLANES // DEC_SEQ, 1))
    sink_row = jnp.repeat(attn_sinks, SEQS_PER_TILE * DEC_SEQ, axis=1)[:, None, :]
    bias_prompt = jnp.asarray(_prompt_bias())
    bias_sample = jnp.asarray(_sample_bias())
    wr_pad = jnp.pad(router_w, ((0, 0), (0, 0), (0, LANES - N_EXPERTS)))
    wr_hi = wr_pad.astype(BF16)
    wr_cat = jnp.concatenate([wr_hi, (wr_pad - wr_hi.astype(F32)).astype(BF16)], axis=-1)

    ck = jnp.swapaxes(cache_win_k.reshape(DEPTH, DEC_BATCH, CACHE_W, KV_WIDTH), 2, 3)
    cv = jnp.swapaxes(cache_win_v.reshape(DEPTH, DEC_BATCH, CACHE_W, KV_WIDTH), 2, 3)

    lw = dict(
        norm_g=attn_norm_g[:, None, :], w_in=w_in_p, b_in=b_in_p[:, None, :],
        ln_g=chunk_ln_g.reshape(DEPTH, 1, GMLP_WIDTH), ln_b=chunk_ln_b.reshape(DEPTH, 1, GMLP_WIDTH),
        w_s=w_spatial, b_s_prompt=bs_prompt, w_s_sample=ws_sample, b_s_sample=bs_sample,
        sink_row=sink_row, bias_prompt=bias_prompt, bias_sample=bias_sample,
        g_a=mix_norm_a_g[:, None, :], g_b=g_b_p[:, None, :], w_out=w_out_p)
    ffn_g = ffn_norm_g[:, None, :]

    x = None
    kp, vp, cvs = [], [], []
    states = ()
    for l in range(DEPTH):
        lw["sinks"] = attn_sinks[l]
        i = l // 2
        cast = (i, (ffn_w_gate, ffn_w_up, ffn_w_down)) if l % 2 == 0 else None
        x_in = x_prompt.reshape(T_PROMPT, D_MODEL) if l == 0 else x
        x_new, k_p, v_p, *ffn_bf = _mixer_prompt(x_in, lw, l, cast)
        if l == 0:
            x_new, *states, cv_s = _mixer_sample(x_sample.reshape(T_SAMPLE, D_MODEL), 0, x_new, ck, cv, l, lw)
        else:
            x_new, *states, cv_s = _mixer_sample(x, NT_PROMPT, x_new, ck, cv, l, lw, states)
        kp.append(k_p)
        vp.append(v_p)
        cvs.append(cv_s)
        if l % 2 == 0:
            ffn_wg, ffn_wu, ffn_wd = (w[None] for w in ffn_bf)
            x, ex_wg, ex_wu, ex_wd = _swiglu(x_new, ffn_g, ffn_wg, ffn_wu, ffn_wd, l, 0,
                                             (expert_w_gate, expert_w_up, expert_w_down), i)
            ex_wrem = jnp.concatenate([ex_wg[..., FF_MAIN:], ex_wu[..., FF_MAIN:]], axis=-1)
        else:
            final_g = final_norm_g[None] if l == DEPTH - 1 else None
            x = _moe(x_new, ffn_g, wr_cat[i:i + 1], ex_wg, ex_wu, ex_wrem, ex_wd, l, 0, final_g)
    y_p, y_s = x
    win_p = (DEPTH, BATCH, WINDOW, KV_HEADS, HEAD_DIM)
    win_s = (DEPTH, DEC_BATCH, CACHE_W, KV_HEADS, HEAD_DIM)
    return (y_p.reshape(BATCH, SEQ, D_MODEL), y_s.reshape(DEC_BATCH, DEC_SEQ, D_MODEL),
            jnp.stack(kp).reshape(win_p), jnp.stack(vp).reshape(win_p),
            states[0].reshape(win_s), states[1].reshape(win_s),
            jnp.stack(cvs).reshape(DEPTH, DEC_BATCH, DEC_SEQ, GMLP_HEADS, GMLP_HEAD_DIM))
```

```python
import functools

import numpy as np
import jax
import jax.numpy as jnp
from jax import lax
from jax.experimental import pallas as pl
from jax.experimental.pallas import tpu as pltpu

D_MODEL = 1024
BATCH = 4
SEQ = 4096
DEPTH = 4
DEC_BATCH = 128
DEC_SEQ = 4
HEAD_DIM = 64
ATTN_HEADS = 8
KV_HEADS = 2
GQA_GROUP = ATTN_HEADS // KV_HEADS
ATTN_WIDTH = ATTN_HEADS * HEAD_DIM
KV_WIDTH = KV_HEADS * HEAD_DIM
WINDOW = 128
CHUNK = 128
GMLP_HEADS = 4
GMLP_HEAD_DIM = 128
GMLP_WIDTH = GMLP_HEADS * GMLP_HEAD_DIM
IN_WIDTH = 2 * GMLP_WIDTH + ATTN_WIDTH + 2 * KV_WIDTH
D_FF = 2816
N_EXPERTS = 8
D_FF_EXPERT = 1408
CACHE_W = 128
EPS = 1e-6

LANES = 128
SUBLANES = 8
MXU_WIDTH = 256
TM = 512
T_PROMPT = BATCH * SEQ
T_SAMPLE = DEC_BATCH * DEC_SEQ
T_ALL = T_PROMPT + T_SAMPLE
NT_PROMPT = T_PROMPT // TM
NT_ALL = T_ALL // TM
TILES_PER_SEQ = SEQ // TM
BLOCKS_PER_TILE = TM // WINDOW
SEQ_GROUP = 16
SEQS_PER_TILE = 4
N_SEQ_GROUPS = DEC_BATCH // SEQ_GROUP
KEYS_PAD = CACHE_W + SUBLANES
PIECE = 2 * SUBLANES
SORT_ROWS = 2 * TM + N_EXPERTS * PIECE
TILES_PER_SUPER = 11
N_SUPER = NT_ALL // TILES_PER_SUPER
SUPER_ROWS = TILES_PER_SUPER * SORT_ROWS
SORT_TILES_PER_STEP = 3
CHUNK_ROWS = MXU_WIDTH
FF_REM = D_FF_EXPERT % MXU_WIDTH
FF_MAIN = D_FF_EXPERT - FF_REM
PIECES_PER_CHUNK = CHUNK_ROWS // PIECE
VMEM_LIMIT = 56 * 1024 * 1024

F32 = jnp.float32
BF16 = jnp.bfloat16
NEG_INF = float("-inf")

Q0 = 2 * GMLP_WIDTH
K0 = Q0 + ATTN_WIDTH
V0 = K0 + KV_WIDTH


def _g_major(a, axis):
    axis = axis % a.ndim
    shape = a.shape
    a = a.reshape(shape[:axis] + (KV_HEADS, GQA_GROUP, HEAD_DIM) + shape[axis + 1:])
    return jnp.swapaxes(a, axis, axis + 1).reshape(shape)


def _slope(kv, g):
    return 2.0 ** (-8.0 * (kv * GQA_GROUP + g + 1) / ATTN_HEADS)


def _prompt_bias():
    c = np.arange(WINDOW)[:, None]
    qi = np.arange(WINDOW)[None, :]
    dist = np.where(c > qi, qi + WINDOW - c, qi - c)
    out = np.zeros((2, KV_HEADS, WINDOW, GQA_GROUP * WINDOW), np.float32)
    for var in range(2):
        ok = ~((var == 1) & (c > qi))
        for kv in range(KV_HEADS):
            for g in range(GQA_GROUP):
                out[var, kv, :, g * WINDOW:(g + 1) * WINDOW] = np.where(ok, -_slope(kv, g) * dist, NEG_INF)
    return out


def _sample_bias():
    out = np.zeros((KEYS_PAD, LANES), np.float32)
    j = np.arange(KEYS_PAD)
    kpos = j - CACHE_W
    for kv in range(KV_HEADS):
        for g in range(GQA_GROUP):
            for s in range(SEQS_PER_TILE):
                for t in range(DEC_SEQ):
                    col = ((kv * GQA_GROUP + g) * SEQS_PER_TILE + s) * DEC_SEQ + t
                    dist = t - kpos
                    ok = (dist >= 0) & (dist < WINDOW) & (j < CACHE_W + DEC_SEQ)
                    out[:, col] = np.where(ok, -_slope(kv, g) * dist, NEG_INF)
    return out


def _rms(x, g):
    return x * lax.rsqrt(jnp.mean(x * x, axis=-1, keepdims=True) + EPS) * g


def _gelu(x):
    return 0.5 * x * (1.0 + lax.erf(x * (2.0 ** -0.5)))


def _gmlp_v(z_v, lng_ref, lnb_ref):
    out = []
    for hh in range(GMLP_HEADS):
        sl = slice(hh * LANES, (hh + 1) * LANES)
        vh = _gelu(z_v[:, sl])
        vc = vh - jnp.mean(vh, axis=-1, keepdims=True)
        y = vc * lax.rsqrt(jnp.mean(vc * vc, axis=-1, keepdims=True) + EPS)
        out.append(y * lng_ref[:, sl] + lnb_ref[:, sl])
    return out


def _dot_nt(a, b):
    return lax.dot_general(a, b, (((1,), (1,)), ((), ())), preferred_element_type=F32)


N_MIXER_INPUTS = 13


def _mixer_prompt_kernel(*refs, n_cast):
    i = pl.program_id(0)
    n_in = N_MIXER_INPUTS + n_cast
    tile_refs = refs[:N_MIXER_INPUTS] + refs[n_in:n_in + 3] + refs[n_in + 3 + n_cast:]

    @pl.when(i < NT_PROMPT)
    def _():
        _mixer_prompt_tile(*tile_refs)

    @pl.when(i == NT_PROMPT)
    def _():
        xo_ref = refs[n_in]
        xo_ref[...] = jnp.zeros_like(xo_ref)

    for src, dst in zip(refs[N_MIXER_INPUTS:n_in], refs[n_in + 3:n_in + 3 + n_cast]):
        dst[...] = src[...].astype(BF16)


def _mixer_prompt_tile(x_ref, ng_ref, win_ref, bin_ref, lng_ref, lnb_ref, ws_ref, bs_ref,
                       sink_ref, bias_ref, ga_ref, gb_ref, wout_ref,
                       xo_ref, ks_ref, vs_ref, kprev_ref, vprev_ref):
    i = pl.program_id(0)

    @pl.when(i == 0)
    def _():
        kprev_ref[...] = jnp.zeros_like(kprev_ref)
        vprev_ref[...] = jnp.zeros_like(vprev_ref)

    x = x_ref[...]
    h = _rms(x, ng_ref[...]).astype(BF16)

    def in_proj(lo, hi):
        return jnp.dot(h, win_ref[:, lo:hi], preferred_element_type=F32) + bin_ref[:, lo:hi]

    z_att = in_proj(Q0, IN_WIDTH)
    q = (z_att[:, :ATTN_WIDTH] * (HEAD_DIM ** -0.5)).astype(BF16)
    k = z_att[:, ATTN_WIDTH:ATTN_WIDTH + KV_WIDTH]
    val = z_att[:, ATTN_WIDTH + KV_WIDTH:]
    k_bf = k.astype(BF16)
    lane = lax.broadcasted_iota(jnp.int32, (1, LANES), 1)
    low = lane < HEAD_DIM
    first = jnp.where(i % TILES_PER_SEQ == 0, 1, 0)
    c_i = lax.broadcasted_iota(jnp.int32, (WINDOW, WINDOW), 0)
    q_i = lax.broadcasted_iota(jnp.int32, (WINDOW, WINDOW), 1)
    from_prev = jnp.concatenate([c_i > q_i] * GQA_GROUP, axis=1)
    g_lane = lax.broadcasted_iota(jnp.int32, (1, GQA_GROUP * WINDOW), 1) // WINDOW
    vt_bf = val.T.astype(BF16)
    vt_all = jnp.concatenate([vprev_ref[...], vt_bf], axis=1)
    k_all = jnp.concatenate([kprev_ref[...], k_bf], axis=0)
    zero = jnp.zeros((), BF16)
    k_heads = [jnp.where(low, k_all, zero), jnp.where(low, zero, k_all)]
    sinks = []
    for kv in range(KV_HEADS):
        sink = jnp.full((1, GQA_GROUP * WINDOW), sink_ref[kv * GQA_GROUP], F32)
        for g in range(1, GQA_GROUP):
            sink = jnp.where(g_lane == g, sink_ref[kv * GQA_GROUP + g], sink)
        sinks.append(sink)

    def attention_block(c):
        rows = slice(c * WINDOW, (c + 1) * WINDOW)
        band = slice(c * WINDOW, (c + 2) * WINDOW)
        o_t = []
        q_rows = jnp.concatenate([q[rows, g * LANES:(g + 1) * LANES] for g in range(GQA_GROUP)], axis=0)
        keys = jnp.concatenate([k_heads[0][band], k_heads[1][band]], axis=0)
        s_both = _dot_nt(keys, q_rows)
        for kv in range(KV_HEADS):
            s2 = s_both[kv * 2 * WINDOW:(kv + 1) * 2 * WINDOW]
            bias = bias_ref[first, kv] if c == 0 else bias_ref[0, kv]
            s = jnp.where(from_prev, s2[:WINDOW], s2[WINDOW:]) + bias
            m = jnp.maximum(jnp.max(s, axis=0, keepdims=True), sinks[kv])
            p = jnp.exp(s - m)
            den = jnp.sum(p, axis=0, keepdims=True) + jnp.exp(sinks[kv] - m)
            p2 = jnp.concatenate([jnp.where(from_prev, p, 0.0), jnp.where(from_prev, 0.0, p)],
                                 axis=0).astype(BF16)
            o = jnp.dot(vt_all[kv * HEAD_DIM:(kv + 1) * HEAD_DIM, band], p2,
                        preferred_element_type=F32)
            o_t.append(o / den)
        bt = jnp.concatenate(o_t, axis=0)
        return jnp.concatenate(
            [bt[:, g * WINDOW:(g + 1) * WINDOW].T for g in range(GQA_GROUP)], axis=1)

    z_u = in_proj(0, GMLP_WIDTH)
    b_rows = [attention_block(0)]
    z_v = in_proj(GMLP_WIDTH, Q0)
    b_rows.append(attention_block(1))
    u = _gelu(z_u)
    b_rows.append(attention_block(2))
    v_heads = _gmlp_v(z_v, lng_ref, lnb_ref)
    row = lax.broadcasted_iota(jnp.int32, (CHUNK, CHUNK), 0)
    col = lax.broadcasted_iota(jnp.int32, (CHUNK, CHUNK), 1)
    a_cols = []
    for hh in range(GMLP_HEADS):
        w = jnp.where(row >= col, ws_ref[hh], 0.0).astype(BF16)
        v_cat = jnp.concatenate(
            [v_heads[hh][c * CHUNK:(c + 1) * CHUNK] for c in range(BLOCKS_PER_TILE)], axis=1)
        mixed = jnp.dot(w, v_cat.astype(BF16), preferred_element_type=F32)
        bias = bs_ref[:, hh * LANES:(hh + 1) * LANES]
        a_cols.append(jnp.concatenate(
            [mixed[:, c * LANES:(c + 1) * LANES] + bias for c in range(BLOCKS_PER_TILE)], axis=0))
    a = u * jnp.concatenate(a_cols, axis=1)
    a_n = _rms(a, ga_ref[...]).astype(BF16)
    b_rows.append(attention_block(3))
    out = x + jnp.dot(a_n, wout_ref[:GMLP_WIDTH, :], preferred_element_type=F32)
    b_n = _rms(jnp.concatenate(b_rows, axis=0), gb_ref[...]).astype(BF16)
    xo_ref[...] = out + jnp.dot(b_n, wout_ref[GMLP_WIDTH:, :], preferred_element_type=F32)

    kprev_ref[...] = k_bf[TM - WINDOW:]
    vprev_ref[...] = vt_bf[:, TM - WINDOW:]
    ks_ref[0] = k[TM - WINDOW:]
    vs_ref[0] = val[TM - WINDOW:]


def _layer_block(layer, *shape, **kwargs):
    return pl.BlockSpec((None,) + shape, lambda *_: (layer,) + (0,) * len(shape), **kwargs)


def _mixer_prompt(x, lw, layer, cast=None):
    const = lambda *shape: pl.BlockSpec(shape, lambda i: (0,) * len(shape))
    per_layer = functools.partial(_layer_block, layer)
    cast_in, cast_specs, cast_out_specs, cast_shapes = [], [], [], []
    if cast is not None:
        ffn_index, weights = cast
        for w in weights:
            rows, cols = w.shape[1:]
            n_slabs = NT_PROMPT if (rows // NT_PROMPT) % PIECE == 0 else NT_PROMPT // 2
            slab = lambda i, n=n_slabs: jnp.minimum(i, n - 1)
            cast_in.append(w)
            cast_specs.append(pl.BlockSpec((None, rows // n_slabs, cols),
                                           lambda i, slab=slab: (ffn_index, slab(i), 0)))
            cast_out_specs.append(pl.BlockSpec((rows // n_slabs, cols), lambda i, slab=slab: (slab(i), 0)))
            cast_shapes.append(jax.ShapeDtypeStruct((rows, cols), BF16))
    return pl.pallas_call(
        functools.partial(_mixer_prompt_kernel, n_cast=len(cast_in)),
        grid=(NT_ALL,),
        in_specs=[
            pl.BlockSpec((TM, D_MODEL), lambda i: (jnp.minimum(i, NT_PROMPT - 1), 0)),
            per_layer(1, D_MODEL), per_layer(D_MODEL, IN_WIDTH), per_layer(1, IN_WIDTH),
            per_layer(1, GMLP_WIDTH), per_layer(1, GMLP_WIDTH),
            per_layer(GMLP_HEADS, CHUNK, CHUNK), per_layer(CHUNK, GMLP_WIDTH),
            pl.BlockSpec(memory_space=pltpu.SMEM),
            const(2, KV_HEADS, WINDOW, GQA_GROUP * WINDOW),
            per_layer(1, GMLP_WIDTH), per_layer(1, ATTN_WIDTH), per_layer(D_MODEL, D_MODEL),
        ] + cast_specs,
        out_specs=[
            pl.BlockSpec((TM, D_MODEL), lambda i: (i, 0)),
            pl.BlockSpec((1, WINDOW, KV_WIDTH), lambda i: (jnp.minimum(i // TILES_PER_SEQ, BATCH - 1), 0, 0)),
            pl.BlockSpec((1, WINDOW, KV_WIDTH), lambda i: (jnp.minimum(i // TILES_PER_SEQ, BATCH - 1), 0, 0)),
        ] + cast_out_specs,
        out_shape=[
            jax.ShapeDtypeStruct((T_ALL, D_MODEL), F32),
            jax.ShapeDtypeStruct((BATCH, WINDOW, KV_WIDTH), F32),
            jax.ShapeDtypeStruct((BATCH, WINDOW, KV_WIDTH), F32),
        ] + cast_shapes,
        scratch_shapes=[pltpu.VMEM((WINDOW, KV_WIDTH), BF16), pltpu.VMEM((WINDOW, KV_WIDTH), BF16)],
        compiler_params=pltpu.CompilerParams(
            dimension_semantics=("arbitrary",), vmem_limit_bytes=VMEM_LIMIT),
        name="mixer_prompt",
    )(x, lw["norm_g"], lw["w_in"], lw["b_in"], lw["ln_g"], lw["ln_b"], lw["w_s"], lw["b_s_prompt"],
      lw["sinks"], lw["bias_prompt"], lw["g_a"], lw["g_b"], lw["w_out"], *cast_in)


def _mixer_sample_kernel(*refs, first_layer):
    n_alias = 1 if first_layer else 3
    _mixer_sample_body(refs[0], *refs[1 + n_alias:], first_layer=first_layer)


def _mixer_sample_body(x_ref, ck_ref, cv_ref, ng_ref, win_ref, bin_ref, lng_ref, lnb_ref,
                       wsd_ref, bsd_ref, sinkrow_ref, bias_ref, ga_ref, gb_ref, wout_ref,
                       xo_ref, ko_ref, vo_ref, cvo_ref,
                       q_s, k_s, v_s, a_s, b_s, kfull, vfull, *, first_layer):
    j = pl.program_id(0)

    @pl.when(j == 0)
    def _():
        h = _rms(x_ref[...], ng_ref[...]).astype(BF16)
        z = jnp.dot(h, win_ref[...], preferred_element_type=F32) + bin_ref[...]
        u = _gelu(z[:, :GMLP_WIDTH])
        v = jnp.concatenate(_gmlp_v(z[:, GMLP_WIDTH:Q0], lng_ref, lnb_ref), axis=1)
        cvo_ref[...] = v
        t_row = lax.broadcasted_iota(jnp.int32, (SUBLANES, 1), 0) % DEC_SEQ

        def tile_rows(pat):
            return jnp.broadcast_to(pat[None], (TM // SUBLANES,) + pat.shape).reshape(TM, pat.shape[-1])

        mixed = tile_rows(bsd_ref[...])
        for d in range(DEC_SEQ):
            vd = v if d == 0 else pltpu.roll(v, d, 0)
            mixed = mixed + tile_rows(jnp.where(t_row >= d, wsd_ref[d], 0.0)) * vd
        a_s[...] = _rms(u * mixed, ga_ref[...]).astype(BF16)
        q_s[...] = z[:, Q0:K0] * (HEAD_DIM ** -0.5)
        k_s[...] = z[:, K0:V0]
        v_s[...] = z[:, V0:]
        kfull[:, CACHE_W + DEC_SEQ:, :] = jnp.zeros((SEQ_GROUP, KEYS_PAD - CACHE_W - DEC_SEQ, KV_WIDTH), F32)
        vfull[:, CACHE_W + DEC_SEQ:, :] = jnp.zeros((SEQ_GROUP, KEYS_PAD - CACHE_W - DEC_SEQ, KV_WIDTH), F32)

    base = pl.multiple_of(j * (SEQ_GROUP * DEC_SEQ), SEQ_GROUP * DEC_SEQ)
    kfull[:, :CACHE_W, :] = jnp.swapaxes(ck_ref[0], 1, 2)
    vfull[:, :CACHE_W, :] = jnp.swapaxes(cv_ref[0], 1, 2)
    k_new = k_s[pl.ds(base, SEQ_GROUP * DEC_SEQ), :]
    v_new = v_s[pl.ds(base, SEQ_GROUP * DEC_SEQ), :]
    for s in range(SEQ_GROUP):
        kfull[s, CACHE_W:CACHE_W + DEC_SEQ, :] = k_new[s * DEC_SEQ:(s + 1) * DEC_SEQ]
        vfull[s, CACHE_W:CACHE_W + DEC_SEQ, :] = v_new[s * DEC_SEQ:(s + 1) * DEC_SEQ]
    ko_ref[0] = kfull[:, DEC_SEQ:DEC_SEQ + CACHE_W, :]
    vo_ref[0] = vfull[:, DEC_SEQ:DEC_SEQ + CACHE_W, :]
    if first_layer:
        ko_ref[1:] = jnp.zeros((DEPTH - 1,) + ko_ref.shape[1:], F32)
        vo_ref[1:] = jnp.zeros((DEPTH - 1,) + vo_ref.shape[1:], F32)

    lane = lax.broadcasted_iota(jnp.int32, (1, LANES), 1)
    low = lane < HEAD_DIM
    col_seq = (lane // DEC_SEQ) % SEQS_PER_TILE
    row_seq = (lax.broadcasted_iota(jnp.int32, (LANES, 1), 0) // DEC_SEQ) % SEQS_PER_TILE
    sink = sinkrow_ref[...]
    zero = jnp.zeros((), BF16)
    quad_rows = SEQS_PER_TILE * DEC_SEQ
    for m in range(SEQ_GROUP // SEQS_PER_TILE):
        r0 = pl.multiple_of(base + m * quad_rows, quad_rows)
        q16 = q_s[pl.ds(r0, quad_rows), :]
        qt = jnp.concatenate(
            [jnp.where(low if kv == 0 else jnp.logical_not(low), q16[:, g * LANES:(g + 1) * LANES], 0.0)
             for kv in range(KV_HEADS) for g in range(GQA_GROUP)], axis=0).astype(BF16)
        s = bias_ref[...]
        for n in range(SEQS_PER_TILE):
            kn = kfull[SEQS_PER_TILE * m + n].astype(BF16)
            s = s + _dot_nt(kn, jnp.where(row_seq == n, qt, zero))
        mx = jnp.maximum(jnp.max(s, axis=0, keepdims=True), sink)
        p = jnp.exp(s - mx)
        p = (p / (jnp.sum(p, axis=0, keepdims=True) + jnp.exp(sink - mx))).astype(BF16)
        o = jnp.zeros((LANES, KV_WIDTH), F32)
        for n in range(SEQS_PER_TILE):
            vn = vfull[SEQS_PER_TILE * m + n].astype(BF16)
            o = o + lax.dot_general(jnp.where(col_seq == n, p, zero), vn, (((0,), (0,)), ((), ())),
                                    preferred_element_type=F32)
        b_s[pl.ds(r0, quad_rows), :] = jnp.concatenate(
            [jnp.where(low, o[g * quad_rows:(g + 1) * quad_rows],
                       o[(GQA_GROUP + g) * quad_rows:(GQA_GROUP + g + 1) * quad_rows])
             for g in range(GQA_GROUP)], axis=1)

    @pl.when(j == N_SEQ_GROUPS - 1)
    def _():
        b_n = _rms(b_s[...], gb_ref[...]).astype(BF16)
        merged = jnp.concatenate([a_s[...], b_n], axis=1)
        xo_ref[...] = x_ref[...] + jnp.dot(merged, wout_ref[...], preferred_element_type=F32)


def _mixer_sample(x, x_block, x_buf, cache_k, cache_v, layer, lw, states=()):
    const = lambda *shape: pl.BlockSpec(shape, lambda j: (0,) * len(shape))
    per_layer = functools.partial(_layer_block, layer)
    first_layer = layer == 0
    cache_spec = pl.BlockSpec((1, SEQ_GROUP, KV_WIDTH, CACHE_W), lambda j: (layer, j, 0, 0))
    state_layers = DEPTH if first_layer else 1
    state_spec = pl.BlockSpec((state_layers, SEQ_GROUP, CACHE_W, KV_WIDTH), lambda j: (layer, j, 0, 0))
    state_shape = jax.ShapeDtypeStruct((DEPTH, DEC_BATCH, CACHE_W, KV_WIDTH), F32)
    aliased = (x_buf,) + tuple(states)
    return pl.pallas_call(
        functools.partial(_mixer_sample_kernel, first_layer=first_layer),
        grid=(N_SEQ_GROUPS,),
        in_specs=[
            pl.BlockSpec((TM, D_MODEL), lambda j: (x_block, 0)),
        ] + [pl.BlockSpec(memory_space=pl.ANY)] * len(aliased) + [
            cache_spec, cache_spec,
            per_layer(1, D_MODEL), per_layer(D_MODEL, IN_WIDTH), per_layer(1, IN_WIDTH),
            per_layer(1, GMLP_WIDTH), per_layer(1, GMLP_WIDTH),
            per_layer(DEC_SEQ, SUBLANES, GMLP_WIDTH), per_layer(SUBLANES, GMLP_WIDTH),
            per_layer(1, LANES), const(KEYS_PAD, LANES),
            per_layer(1, GMLP_WIDTH), per_layer(1, ATTN_WIDTH), per_layer(D_MODEL, D_MODEL),
        ],
        out_specs=[
            pl.BlockSpec((TM, D_MODEL), lambda j: (NT_PROMPT, 0)),
            state_spec, state_spec,
            const(TM, GMLP_WIDTH),
        ],
        out_shape=[
            jax.ShapeDtypeStruct((T_ALL, D_MODEL), F32),
            state_shape, state_shape,
            jax.ShapeDtypeStruct((TM, GMLP_WIDTH), F32),
        ],
        scratch_shapes=[
            pltpu.VMEM((TM, ATTN_WIDTH), F32), pltpu.VMEM((TM, KV_WIDTH), F32),
            pltpu.VMEM((TM, KV_WIDTH), F32), pltpu.VMEM((TM, GMLP_WIDTH), BF16),
            pltpu.VMEM((TM, ATTN_WIDTH), F32),
            pltpu.VMEM((SEQ_GROUP, KEYS_PAD, KV_WIDTH), F32),
            pltpu.VMEM((SEQ_GROUP, KEYS_PAD, KV_WIDTH), F32),
        ],
        input_output_aliases={1 + n: n for n in range(len(aliased))},
        compiler_params=pltpu.CompilerParams(
            dimension_semantics=("arbitrary",), vmem_limit_bytes=VMEM_LIMIT),
        name="mixer_sample",
    )(x, *aliased, cache_k, cache_v, lw["norm_g"], lw["w_in"], lw["b_in"], lw["ln_g"], lw["ln_b"],
      lw["w_s_sample"], lw["b_s_sample"], lw["sink_row"], lw["bias_sample"],
      lw["g_a"], lw["g_b"], lw["w_out"])


def _swiglu_kernel(x_ref, ng_ref, wg_ref, wu_ref, wd_ref, eg_ref, eu_ref, ed_ref,
                   xo_ref, ego_ref, euo_ref, edo_ref):
    x = x_ref[...]
    h = _rms(x, ng_ref[...]).astype(BF16)
    gate = jnp.dot(h, wg_ref[...], preferred_element_type=F32)
    ego_ref[...] = eg_ref[...].astype(BF16)
    up = jnp.dot(h, wu_ref[...], preferred_element_type=F32)
    euo_ref[...] = eu_ref[...].astype(BF16)
    act = (jax.nn.silu(gate) * up).astype(BF16)
    xo_ref[...] = x + jnp.dot(act, wd_ref[...], preferred_element_type=F32)
    edo_ref[...] = ed_ref[...].astype(BF16)


def _swiglu(x, norm_g, wg, wu, wd, layer, ffn_index, expert_f32, expert_index):
    weights = functools.partial(_layer_block, ffn_index, pipeline_mode=pl.Buffered(1))
    slab_steps = NT_ALL - 1
    up_rows = N_EXPERTS * D_MODEL // slab_steps
    down_rows = N_EXPERTS * D_FF_EXPERT // slab_steps
    slab = lambda i: jnp.minimum(i, slab_steps - 1)
    eg, eu, ed = expert_f32
    eg = eg.reshape(-1, N_EXPERTS * D_MODEL, D_FF_EXPERT)
    eu = eu.reshape(-1, N_EXPERTS * D_MODEL, D_FF_EXPERT)
    ed = ed.reshape(-1, N_EXPERTS * D_FF_EXPERT, D_MODEL)
    x_new, eg_bf, eu_bf, ed_bf = pl.pallas_call(
        _swiglu_kernel,
        grid=(NT_ALL,),
        in_specs=[
            pl.BlockSpec((TM, D_MODEL), lambda i: (i, 0)),
            _layer_block(layer, 1, D_MODEL),
            weights(D_MODEL, D_FF), weights(D_MODEL, D_FF), weights(D_FF, D_MODEL),
            pl.BlockSpec((None, up_rows, D_FF_EXPERT), lambda i: (expert_index, slab(i), 0)),
            pl.BlockSpec((None, up_rows, D_FF_EXPERT), lambda i: (expert_index, slab(i), 0)),
            pl.BlockSpec((None, down_rows, D_MODEL), lambda i: (expert_index, slab(i), 0)),
        ],
        out_specs=[
            pl.BlockSpec((TM, D_MODEL), lambda i: (i, 0)),
            pl.BlockSpec((up_rows, D_FF_EXPERT), lambda i: (slab(i), 0)),
            pl.BlockSpec((up_rows, D_FF_EXPERT), lambda i: (slab(i), 0)),
            pl.BlockSpec((down_rows, D_MODEL), lambda i: (slab(i), 0)),
        ],
        out_shape=[
            jax.ShapeDtypeStruct((T_ALL, D_MODEL), F32),
            jax.ShapeDtypeStruct((N_EXPERTS * D_MODEL, D_FF_EXPERT), BF16),
            jax.ShapeDtypeStruct((N_EXPERTS * D_MODEL, D_FF_EXPERT), BF16),
            jax.ShapeDtypeStruct((N_EXPERTS * D_FF_EXPERT, D_MODEL), BF16),
        ],
        compiler_params=pltpu.CompilerParams(
            dimension_semantics=("arbitrary",), vmem_limit_bytes=VMEM_LIMIT),
        name="swiglu",
    )(x, norm_g, wg, wu, wd, eg, eu, ed)
    return (x_new, eg_bf.reshape(1, N_EXPERTS, D_MODEL, D_FF_EXPERT),
            eu_bf.reshape(1, N_EXPERTS, D_MODEL, D_FF_EXPERT),
            ed_bf.reshape(1, N_EXPERTS, D_FF_EXPERT, D_MODEL))


def _moe_sort_kernel(x_ref, ng_ref, wrc_ref, hs_ref, meta_ref, off_ref, cnt_ref):
    n = SORT_TILES_PER_STEP
    logits, routes = {}, {}
    for step in range(n + 2):
        if step < n:
            logits[step] = _sort_logits(x_ref[step * TM:(step + 1) * TM, :], ng_ref, wrc_ref)
        j = step - 1
        if 0 <= j < n:
            h_hi, lg = logits.pop(j)
            meta, off, cnt_pad = _sort_route(lg)
            meta_ref[j * TM:(j + 1) * TM, :] = meta
            off_ref[j] = off.astype(jnp.int32)
            cnt_ref[j] = cnt_pad.astype(jnp.int32)
            routes[j] = (h_hi, meta)
        j = step - 2
        if 0 <= j < n:
            hs_ref[j * SORT_ROWS:(j + 1) * SORT_ROWS, :] = _sort_gather(*routes.pop(j))


def _sort_logits(x, ng_ref, wrc_ref):
    h = _rms(x, ng_ref[...])
    h_hi = h.astype(BF16)
    h_lo = (h - h_hi.astype(F32)).astype(BF16)
    both = jnp.dot(h_hi, wrc_ref[...], preferred_element_type=F32)
    logits = both[:, :LANES] + both[:, LANES:] + jnp.dot(h_lo, wrc_ref[:, :LANES], preferred_element_type=F32)
    return h_hi, logits


def _sort_route(logits):
    lane = lax.broadcasted_iota(jnp.int32, (1, LANES), 1)
    logits = jnp.where(lane < N_EXPERTS, logits, NEG_INF)
    m1 = jnp.max(logits, axis=-1, keepdims=True)
    i1 = jnp.min(jnp.where(logits == m1, lane, LANES), axis=-1, keepdims=True)
    rest = jnp.where(lane == i1, NEG_INF, logits)
    m2 = jnp.max(rest, axis=-1, keepdims=True)
    i2 = jnp.min(jnp.where(rest == m2, lane, LANES), axis=-1, keepdims=True)
    e2 = jnp.exp(m2 - m1)
    den = 1.0 + e2
    sel0 = lane == i1
    sel1 = lane == i2
    routed = jnp.where(jnp.logical_or(sel0, sel1), 1.0, 0.0)
    r_i = lax.broadcasted_iota(jnp.int32, (TM, TM), 0)
    c_i = lax.broadcasted_iota(jnp.int32, (TM, TM), 1)
    lower = jnp.where(r_i > c_i, 1.0, 0.0).astype(BF16)
    rank = jnp.dot(lower, routed.astype(BF16), preferred_element_type=F32)
    cnt = jnp.broadcast_to(jnp.sum(routed, axis=0, keepdims=True), (SUBLANES, LANES))
    cnt_pad = jnp.floor((cnt + (PIECE - 1)) * (1.0 / PIECE)) * PIECE
    incl = cnt_pad
    for sh in (1, 2, 4):
        incl = incl + jnp.where(lane >= sh, pltpu.roll(incl, sh, 1), 0.0)
    off = incl - cnt_pad
    pos = off[0:1] + rank
    pos0 = jnp.sum(jnp.where(sel0, pos, 0.0), axis=-1, keepdims=True)
    pos1 = jnp.sum(jnp.where(sel1, pos, 0.0), axis=-1, keepdims=True)
    meta = jnp.where(lane == 0, pos0, jnp.where(lane == 1, pos1,
                     jnp.where(lane == 2, 1.0 / den, jnp.where(lane == 3, e2 / den, 0.0))))
    return meta, off, cnt_pad


def _sort_gather(h_hi, meta):
    meta_t = meta.T
    p_row = lax.broadcasted_iota(jnp.int32, (SORT_ROWS, 1), 0).astype(F32)
    hit = jnp.logical_or(p_row == meta_t[0:1], p_row == meta_t[1:2])
    gather = jnp.where(hit, 1.0, 0.0).astype(BF16)
    return jnp.dot(gather, h_hi, preferred_element_type=F32).astype(BF16)


def _moe_sort(x, norm_g, wr_cat, layer, moe_index):
    return pl.pallas_call(
        _moe_sort_kernel,
        grid=(NT_ALL // SORT_TILES_PER_STEP,),
        in_specs=[
            pl.BlockSpec((SORT_TILES_PER_STEP * TM, D_MODEL), lambda i: (i, 0)),
            _layer_block(layer, 1, D_MODEL),
            _layer_block(moe_index, D_MODEL, 2 * LANES),
        ],
        out_specs=[
            pl.BlockSpec((SORT_TILES_PER_STEP * SORT_ROWS, D_MODEL), lambda i: (i, 0)),
            pl.BlockSpec((SORT_TILES_PER_STEP * TM, LANES), lambda i: (i, 0)),
            pl.BlockSpec((SORT_TILES_PER_STEP, SUBLANES, LANES), lambda i: (i, 0, 0)),
            pl.BlockSpec((SORT_TILES_PER_STEP, SUBLANES, LANES), lambda i: (i, 0, 0)),
        ],
        out_shape=[
            jax.ShapeDtypeStruct((NT_ALL * SORT_ROWS, D_MODEL), BF16),
            jax.ShapeDtypeStruct((T_ALL, LANES), F32),
            jax.ShapeDtypeStruct((NT_ALL, SUBLANES, LANES), jnp.int32),
            jax.ShapeDtypeStruct((NT_ALL, SUBLANES, LANES), jnp.int32),
        ],
        compiler_params=pltpu.CompilerParams(
            dimension_semantics=("arbitrary",), vmem_limit_bytes=VMEM_LIMIT),
        name="moe_sort",
    )(x, norm_g, wr_cat)


def _moe_experts_kernel(off_ref, cnt_ref, hs_hbm, wg_ref, wu_ref, wrem_ref, wd_ref, ys_hbm,
                        buf, lhs, sem_in, sem_out):
    s = pl.program_id(0)
    e = pl.program_id(1)
    row0 = pl.multiple_of(s * SUPER_ROWS, PIECE)
    is_first = e == 0
    is_last = e == N_EXPERTS - 1

    def copy_in(t):
        r = pl.multiple_of(t * SORT_ROWS, PIECE)
        return pltpu.make_async_copy(hs_hbm.at[pl.ds(row0 + r, SORT_ROWS)], buf.at[pl.ds(r, SORT_ROWS)],
                                     sem_in.at[t])

    def copy_out(t):
        r = pl.multiple_of(t * SORT_ROWS, PIECE)
        return pltpu.make_async_copy(buf.at[pl.ds(r, SORT_ROWS)], ys_hbm.at[pl.ds(row0 + r, SORT_ROWS)],
                                     sem_out.at[t])

    def for_tiles(lo, hi, fn):
        hi = jnp.maximum(lo, hi)

        def body(t, carry):
            fn(t)
            return carry

        lax.fori_loop(lo, hi, body, 0)
        return hi

    @pl.when(is_first)
    def _():
        for t in range(TILES_PER_SUPER):
            copy_in(t).start()

    starts, cums = [], [0]
    for i in range(TILES_PER_SUPER):
        idx = (s * TILES_PER_SUPER + i) * N_EXPERTS + e
        starts.append(i * SORT_ROWS + off_ref[idx])
        cums.append(cums[-1] + cnt_ref[idx] // PIECE)
    n_pieces = cums[-1]

    def locate(p):
        pc = jnp.minimum(p, n_pieces - 1)
        row = starts[0] + PIECE * pc
        tile = jnp.int32(0)
        for i in range(1, TILES_PER_SUPER):
            inside = pc >= cums[i]
            row = jnp.where(inside, starts[i] + PIECE * (pc - cums[i]), row)
            tile = jnp.where(inside, i, tile)
        return row, tile

    def piece_rows(q):
        src, dst, tiles = [], [], []
        for j in range(PIECES_PER_CHUNK):
            p = q * PIECES_PER_CHUNK + j
            row, tile = locate(p)
            src.append(row)
            dst.append(jnp.where(p < n_pieces, row, SUPER_ROWS))
            tiles.append(tile)
        return tuple(src), tuple(dst), tiles[0], tiles[-1]

    def expert_ffn(xs, between=lambda: None):
        gate = jnp.dot(xs, wg_ref[0, :, :FF_MAIN], preferred_element_type=F32)
        up = jnp.dot(xs, wu_ref[0, :, :FF_MAIN], preferred_element_type=F32)
        rem = jnp.dot(xs, wrem_ref[0], preferred_element_type=F32)
        extra = between()
        act = jnp.concatenate([jax.nn.silu(gate) * up,
                               jax.nn.silu(rem[:, :FF_REM]) * rem[:, FF_REM:]], axis=1).astype(BF16)
        return jnp.dot(act, wd_ref[0], preferred_element_type=F32).astype(BF16), extra

    def chunk_body(q, carry):
        (src, dst, _, last_tile), arrived, sent = carry
        arrived = for_tiles(arrived, jnp.where(is_first, last_tile + 1, 0), lambda t: copy_in(t).wait())
        for j in range(PIECES_PER_CHUNK):
            lhs[j * PIECE:(j + 1) * PIECE, :] = buf[pl.ds(pl.multiple_of(src[j], PIECE), PIECE), :]
        y, nxt = expert_ffn(lhs[...], lambda: piece_rows(q + 1))
        for j in range(PIECES_PER_CHUNK):
            buf[pl.ds(pl.multiple_of(dst[j], PIECE), PIECE), :] = y[j * PIECE:(j + 1) * PIECE]
        sent = for_tiles(sent, jnp.where(is_last, nxt[2], 0), lambda t: copy_out(t).start())
        return nxt, arrived, sent

    half = PIECES_PER_CHUNK // 2
    n_chunks = n_pieces // PIECES_PER_CHUNK + jnp.where(n_pieces % PIECES_PER_CHUNK > half, 1, 0)
    _, arrived, sent = lax.fori_loop(0, n_chunks, chunk_body, (piece_rows(0), jnp.int32(0), jnp.int32(0)))
    for_tiles(arrived, jnp.where(is_first, TILES_PER_SUPER, 0), lambda t: copy_in(t).wait())

    first_left = n_chunks * PIECES_PER_CHUNK

    @pl.when(first_left < n_pieces)
    def _():
        dst = []
        for j in range(half):
            row, _ = locate(first_left + j)
            dst.append(jnp.where(first_left + j < n_pieces, row, SUPER_ROWS))
            lhs[j * PIECE:(j + 1) * PIECE, :] = buf[pl.ds(pl.multiple_of(row, PIECE), PIECE), :]
        y, _ = expert_ffn(lhs[:half * PIECE, :])
        for j in range(half):
            buf[pl.ds(pl.multiple_of(dst[j], PIECE), PIECE), :] = y[j * PIECE:(j + 1) * PIECE]

    for_tiles(sent, jnp.where(is_last, TILES_PER_SUPER, 0), lambda t: copy_out(t).start())
    for_tiles(0, jnp.where(is_last, TILES_PER_SUPER, 0), lambda t: copy_out(t).wait())


def _moe_experts(off, cnt, hs, wg, wu, wrem, wd, moe_index):
    expert = lambda *shape: pl.BlockSpec((None, 1) + shape, lambda s, e, *_: (moe_index, e, 0, 0))
    grid_spec = pltpu.PrefetchScalarGridSpec(
        num_scalar_prefetch=2,
        grid=(N_SUPER, N_EXPERTS),
        in_specs=[
            pl.BlockSpec(memory_space=pl.ANY),
            expert(D_MODEL, D_FF_EXPERT), expert(D_MODEL, D_FF_EXPERT), expert(D_MODEL, 2 * FF_REM),
            expert(D_FF_EXPERT, D_MODEL),
        ],
        out_specs=pl.BlockSpec(memory_space=pl.ANY),
        scratch_shapes=[pltpu.VMEM((SUPER_ROWS + PIECE, D_MODEL), BF16),
                        pltpu.VMEM((CHUNK_ROWS, D_MODEL), BF16),
                        pltpu.SemaphoreType.DMA((TILES_PER_SUPER,)),
                        pltpu.SemaphoreType.DMA((TILES_PER_SUPER,))],
    )
    return pl.pallas_call(
        _moe_experts_kernel,
        grid_spec=grid_spec,
        out_shape=jax.ShapeDtypeStruct((NT_ALL * SORT_ROWS, D_MODEL), BF16),
        compiler_params=pltpu.CompilerParams(
            dimension_semantics=("arbitrary", "arbitrary"), vmem_limit_bytes=VMEM_LIMIT),
        name="moe_experts",
    )(off, cnt, hs, wg, wu, wrem, wd)


UNSORT_SLOTS = 3


def _unsorted_residual(x_hbm, ys_hbm, meta_ref, xbuf, ybuf, sem):
    i = pl.program_id(0)

    def fetch(t):
        slot = t % UNSORT_SLOTS
        return (pltpu.make_async_copy(x_hbm.at[pl.ds(pl.multiple_of(t * TM, TM), TM)],
                                      xbuf.at[slot], sem.at[0, slot]),
                pltpu.make_async_copy(ys_hbm.at[pl.ds(pl.multiple_of(t * SORT_ROWS, PIECE), SORT_ROWS)],
                                      ybuf.at[slot], sem.at[1, slot]))

    def start(t):
        for cp in fetch(t):
            cp.start()

    @pl.when(i == 0)
    def _():
        start(0)
        start(1)

    @pl.when(i + UNSORT_SLOTS - 1 < NT_ALL)
    def _():
        start(i + UNSORT_SLOTS - 1)

    meta = meta_ref[...]
    p_col = lax.broadcasted_iota(jnp.int32, (1, SORT_ROWS), 1).astype(F32)
    scatter = (jnp.where(p_col == meta[:, 0:1], meta[:, 2:3], 0.0)
               + jnp.where(p_col == meta[:, 1:2], meta[:, 3:4], 0.0)).astype(BF16)
    for cp in fetch(i):
        cp.wait()
    slot = i % UNSORT_SLOTS
    return xbuf[slot] + jnp.dot(scatter, ybuf[slot], preferred_element_type=F32)


def _moe_unsort_kernel(x_hbm, ys_hbm, meta_ref, xo_ref, xbuf, ybuf, sem):
    xo_ref[...] = _unsorted_residual(x_hbm, ys_hbm, meta_ref, xbuf, ybuf, sem)


def _moe_unsort_final_kernel(x_hbm, ys_hbm, meta_ref, g_ref, yp_ref, ysm_ref, xbuf, ybuf, sem):
    i = pl.program_id(0)
    y = _rms(_unsorted_residual(x_hbm, ys_hbm, meta_ref, xbuf, ybuf, sem), g_ref[...])

    @pl.when(i < NT_PROMPT)
    def _():
        yp_ref[...] = y

    @pl.when(i == NT_PROMPT)
    def _():
        ysm_ref[...] = y


def _moe_unsort(x, ys, meta, final_g=None):
    in_specs = [
        pl.BlockSpec(memory_space=pl.ANY),
        pl.BlockSpec(memory_space=pl.ANY),
        pl.BlockSpec((TM, LANES), lambda i: (i, 0)),
    ]
    scratch = [pltpu.VMEM((UNSORT_SLOTS, TM, D_MODEL), F32),
               pltpu.VMEM((UNSORT_SLOTS, SORT_ROWS, D_MODEL), BF16),
               pltpu.SemaphoreType.DMA((2, UNSORT_SLOTS))]
    params = pltpu.CompilerParams(dimension_semantics=("arbitrary",), vmem_limit_bytes=VMEM_LIMIT)
    if final_g is None:
        return pl.pallas_call(
            _moe_unsort_kernel,
            grid=(NT_ALL,),
            in_specs=in_specs,
            out_specs=pl.BlockSpec((TM, D_MODEL), lambda i: (i, 0)),
            out_shape=jax.ShapeDtypeStruct((T_ALL, D_MODEL), F32),
            scratch_shapes=scratch,
            compiler_params=params,
            name="moe_unsort",
        )(x, ys, meta)
    return pl.pallas_call(
        _moe_unsort_final_kernel,
        grid=(NT_ALL,),
        in_specs=in_specs + [pl.BlockSpec((1, D_MODEL), lambda i: (0, 0))],
        out_specs=[pl.BlockSpec((TM, D_MODEL), lambda i: (jnp.minimum(i, NT_PROMPT - 1), 0)),
                   pl.BlockSpec((TM, D_MODEL), lambda i: (0, 0))],
        out_shape=[jax.ShapeDtypeStruct((T_PROMPT, D_MODEL), F32),
                   jax.ShapeDtypeStruct((T_SAMPLE, D_MODEL), F32)],
        scratch_shapes=scratch,
        compiler_params=params,
        name="moe_unsort_final",
    )(x, ys, meta, final_g)


def _moe(x, norm_g, wr_cat, wg, wu, wrem, wd, layer, moe_index, final_g=None):
    hs, meta, off, cnt = _moe_sort(x, norm_g, wr_cat, layer, moe_index)
    off = off[:, 0, :N_EXPERTS].reshape(-1)
    cnt = cnt[:, 0, :N_EXPERTS].reshape(-1)
    ys = _moe_experts(off, cnt, hs, wg, wu, wrem, wd, moe_index)
    return _moe_unsort(x, ys, meta, final_g)


def kernel(x_prompt, x_sample, cache_win_k, cache_win_v, attn_norm_g, w_in, b_in, chunk_ln_g, chunk_ln_b, w_spatial, b_spatial, attn_sinks, mix_norm_a_g, mix_norm_b_g, w_out, ffn_norm_g, ffn_w_gate, ffn_w_up, ffn_w_down, router_w, expert_w_gate, expert_w_up, expert_w_down, final_norm_g):
    w_in_p = jnp.concatenate([w_in[..., :Q0], _g_major(w_in[..., Q0:K0], -1), w_in[..., K0:]], axis=-1).astype(BF16)
    b_in_p = jnp.concatenate([b_in[..., :Q0], _g_major(b_in[..., Q0:K0], -1), b_in[..., K0:]], axis=-1)
    w_out_p = jnp.concatenate([w_out[:, :GMLP_WIDTH], _g_major(w_out[:, GMLP_WIDTH:], 1)], axis=1).astype(BF16)
    g_b_p = _g_major(mix_norm_b_g, 1)
    bs_prompt = jnp.repeat(jnp.transpose(b_spatial, (0, 2, 1)), GMLP_HEAD_DIM, axis=2)
    bs_sample = jnp.tile(bs_prompt[:, :DEC_SEQ], (1, SUBLANES // DEC_SEQ, 1))
    ws_corner = w_spatial[:, :, :DEC_SEQ, :DEC_SEQ]
    ws_small = jnp.stack([jnp.stack([ws_corner[:, :, t, max(t - d, 0)] for t in range(DEC_SEQ)], axis=-1)
                          for d in range(DEC_SEQ)], axis=1)
    ws_sample = jnp.tile(jnp.repeat(jnp.transpose(ws_small, (0, 1, 3, 2)), GMLP_HEAD_DIM, axis=3),
                         (1, 1, SUBLANES // DEC_SEQ, 1))
    sink_row = jnp.repeat(attn_sinks, SEQS_PER_TILE * DEC_SEQ, axis=1)[:, None, :]
    bias_prompt = jnp.asarray(_prompt_bias())
    bias_sample = jnp.asarray(_sample_bias())
    wr_pad = jnp.pad(router_w, ((0, 0), (0, 0), (0, LANES - N_EXPERTS)))
    wr_hi = wr_pad.astype(BF16)
    wr_cat = jnp.concatenate([wr_hi, (wr_pad - wr_hi.astype(F32)).astype(BF16)], axis=-1)

    ck = jnp.swapaxes(cache_win_k.reshape(DEPTH, DEC_BATCH, CACHE_W, KV_WIDTH), 2, 3)
    cv = jnp.swapaxes(cache_win_v.reshape(DEPTH, DEC_BATCH, CACHE_W, KV_WIDTH), 2, 3)

    lw = dict(
        norm_g=attn_norm_g[:, None, :], w_in=w_in_p, b_in=b_in_p[:, None, :],
        ln_g=chunk_ln_g.reshape(DEPTH, 1, GMLP_WIDTH), ln_b=chunk_ln_b.reshape(DEPTH, 1, GMLP_WIDTH),
        w_s=w_spatial, b_s_prompt=bs_prompt, w_s_sample=ws_sample, b_s_sample=bs_sample,
        sink_row=sink_row, bias_prompt=bias_prompt, bias_sample=bias_sample,
        g_a=mix_norm_a_g[:, None, :], g_b=g_b_p[:, None, :], w_out=w_out_p)
    ffn_g = ffn_norm_g[:, None, :]

    x = None
    kp, vp, cvs = [], [], []
    states = ()
    for l in range(DEPTH):
        lw["sinks"] = attn_sinks[l]
        i = l // 2
        cast = (i, (ffn_w_gate, ffn_w_up, ffn_w_down)) if l % 2 == 0 else None
        x_in = x_prompt.reshape(T_PROMPT, D_MODEL) if l == 0 else x
        x_new, k_p, v_p, *ffn_bf = _mixer_prompt(x_in, lw, l, cast)
        if l == 0:
            x_new, *states, cv_s = _mixer_sample(x_sample.reshape(T_SAMPLE, D_MODEL), 0, x_new, ck, cv, l, lw)
        else:
            x_new, *states, cv_s = _mixer_sample(x, NT_PROMPT, x_new, ck, cv, l, lw, states)
        kp.append(k_p)
        vp.append(v_p)
        cvs.append(cv_s)
        if l % 2 == 0:
            ffn_wg, ffn_wu, ffn_wd = (w[None] for w in ffn_bf)
            x, ex_wg, ex_wu, ex_wd = _swiglu(x_new, ffn_g, ffn_wg, ffn_wu, ffn_wd, l, 0,
                                             (expert_w_gate, expert_w_up, expert_w_down), i)
            ex_wrem = jnp.concatenate([ex_wg[..., FF_MAIN:], ex_wu[..., FF_MAIN:]], axis=-1)
        else:
            final_g = final_norm_g[None] if l == DEPTH - 1 else None
            x = _moe(x_new, ffn_g, wr_cat[i:i + 1], ex_wg, ex_wu, ex_wrem, ex_wd, l, 0, final_g)
    y_p, y_s = x
    win_p = (DEPTH, BATCH, WINDOW, KV_HEADS, HEAD_DIM)
    win_s = (DEPTH, DEC_BATCH, CACHE_W, KV_HEADS, HEAD_DIM)
    return (y_p.reshape(BATCH, SEQ, D_MODEL), y_s.reshape(DEC_BATCH, DEC_SEQ, D_MODEL),
            jnp.stack(kp).reshape(win_p), jnp.stack(vp).reshape(win_p),
            states[0].reshape(win_s), states[1].reshape(win_s),
            jnp.stack(cvs).reshape(DEPTH, DEC_BATCH, DEC_SEQ, GMLP_HEADS, GMLP_HEAD_DIM))
```
